```python
import math
import jax
import jax.numpy as jnp
from jax import lax
import numpy as np

D_MODEL = 1024
BATCH = 4
SEQ = 4096
DEPTH = 2
DEC_BATCH = 128
DEC_SEQ = 1
PAST_LEN = 2048
PAGE_SIZE = 128

N_HEADS = 16
HEAD_DIM = D_MODEL // N_HEADS
N_KV = 4
HPG = N_HEADS // N_KV
N_BRANCH = 3
CMP_BLOCK = 32
CMP_HIDDEN = HEAD_DIM
SEL_BLOCK = 64
TOP_N = 16
WINDOW = 512
N_BUCKETS = 32
MAX_EXACT = N_BUCKETS // 2
MAX_DISTANCE = 1024
CONV_W = 3
D_FF = 256 * ((8 * D_MODEL // 3 + 255) // 256)
N_A = DEPTH // 2
N_B = DEPTH - N_A
Q_BLOCK = 32
EPS = 1e-6

kernel_name = 'yoco_shortconv_nsa_decoder_step'


def rmsnorm(x, g):
    xf = x.astype(jnp.float32)
    y = xf * lax.rsqrt(jnp.mean(xf * xf, axis=-1, keepdims=True) + EPS)
    return (y * g.astype(jnp.float32)).astype(x.dtype)


def modulated_norm(x, c, w_mod, b_mod, g):
    shift, scale, gate = jnp.split(c @ w_mod + b_mod, 3, axis=-1)
    h = rmsnorm(x, g) * (1 + scale[:, None, :]) + shift[:, None, :]
    return h, gate[:, None, :]


def causal_conv(x, prev, w, b):
    L = x.shape[1]
    xx = jnp.concatenate([prev.astype(x.dtype), x], axis=1)
    y = sum(w[j] * xx[:, j:j + L] for j in range(CONV_W)) + b
    return y, xx[:, L:]


def short_conv_mixer(h, prev, w_in, conv_w, conv_b, w_out):
    bg, cg, u = jnp.split(h @ w_in, 3, axis=-1)
    y, st = causal_conv(cg * u, prev, conv_w, conv_b)
    return (bg * y) @ w_out, st


def conv_ffn(h, prev, w_up, conv_w, conv_b, w_down):
    up, st = causal_conv(h @ w_up, prev, conv_w, conv_b)
    u, g = jnp.split(up, 2, axis=-1)
    return (jax.nn.silu(g) * u) @ w_down, st


def t5_bucket(n):
    n = jnp.maximum(n, 0)
    nf = jnp.maximum(n, 1).astype(jnp.float32)
    large = MAX_EXACT + (jnp.log(nf / MAX_EXACT) / math.log(MAX_DISTANCE / MAX_EXACT)
                         * (N_BUCKETS - MAX_EXACT)).astype(jnp.int32)
    return jnp.where(n < MAX_EXACT, n, jnp.minimum(large, N_BUCKETS - 1))


def masked_softmax(s, mask):
    s = jnp.where(mask, s, -jnp.inf)
    m = jnp.max(s, axis=-1, keepdims=True)
    m = jnp.where(jnp.isfinite(m), m, 0.0)
    p = jnp.exp(s - m)
    d = jnp.sum(p, axis=-1, keepdims=True)
    return p / jnp.where(d > 0, d, 1.0)


def compress_blocks(rows, pe, w1, w2):
    B, Lp = rows.shape[0], rows.shape[1]
    nb = Lp // CMP_BLOCK
    blk = rows.reshape(B, nb, CMP_BLOCK, N_KV, HEAD_DIM) + pe[None, None, :, None, :]
    flat = blk.transpose(0, 1, 3, 2, 4).reshape(B, nb, N_KV, CMP_BLOCK * HEAD_DIM)
    return jax.nn.gelu(flat @ w1) @ w2


def nsa_attend(q, gates, k_cb, v_cb, k_sel, v_sel, k_win, v_win, q_pos0, win_pos0, rel_bias):
    f32 = jnp.float32
    B, Q = q.shape[0], q.shape[1]
    n_cb = k_cb.shape[1]
    n_sb = k_sel.shape[1] // SEL_BLOCK
    n_sel = min(TOP_N, n_sb)
    qc_len = Q_BLOCK if Q % Q_BLOCK == 0 else Q
    n_chunks = Q // qc_len
    n_kw = WINDOW + qc_len - 1
    k_sb = k_sel.reshape(B, n_sb, SEL_BLOCK, N_KV, HEAD_DIM).transpose(0, 3, 1, 2, 4)
    v_sb = v_sel.reshape(B, n_sb, SEL_BLOCK, N_KV, HEAD_DIM).transpose(0, 3, 1, 2, 4)
    pad_w = ((0, 0), (WINDOW, 0), (0, 0), (0, 0))
    kw_pad = jnp.pad(k_win, pad_w)
    vw_pad = jnp.pad(v_win, pad_w)
    tab = rel_bias.astype(f32)
    tab_g = tab.T.reshape(N_KV, HPG, N_BUCKETS)
    cmp_end = jnp.arange(n_cb) * CMP_BLOCK + (CMP_BLOCK - 1)
    blk = jnp.arange(n_sb)

    def head_bias(d):
        return tab[t5_bucket(d)].reshape(d.shape[0], d.shape[1], N_KV, HPG).transpose(0, 2, 3, 1)

    def sel_group(q_g, ksb_g, vsb_g, idx_g, bkt_g, ok_g, tab_h):
        kg = jax.vmap(lambda kb, ix: kb[ix])(ksb_g, idx_g).reshape(B, qc_len, n_sel * SEL_BLOCK, HEAD_DIM)
        vg = jax.vmap(lambda vb, ix: vb[ix])(vsb_g, idx_g).reshape(B, qc_len, n_sel * SEL_BLOCK, HEAD_DIM)
        s = jnp.einsum('bqhd,bqkd->bqhk', q_g, kg.astype(f32)) + jnp.moveaxis(tab_h[:, bkt_g], 0, 2)
        p = masked_softmax(s, ok_g[:, :, None, :])
        return jnp.einsum('bqhk,bqkd->bqhd', p, vg.astype(f32))

    def chunk(ci):
        start = ci * qc_len
        qs = q_pos0 + start
        qc = lax.dynamic_slice_in_dim(q, start, qc_len, axis=1).astype(f32)
        gc = lax.dynamic_slice_in_dim(gates, start, qc_len, axis=1).astype(f32)
        t = qs + jnp.arange(qc_len)
        d_c = t[:, None] - cmp_end[None, :]
        s_c = jnp.einsum('bqghd,bngd->bqghn', qc, k_cb.astype(f32)) + head_bias(d_c)
        p_c = masked_softmax(s_c, (d_c >= 0)[:, None, None, :])
        o_c = jnp.einsum('bqghn,bngd->bqghd', p_c, v_cb.astype(f32))
        imp = p_c.sum(axis=3).reshape(B, qc_len, N_KV, n_sb, SEL_BLOCK // CMP_BLOCK).sum(axis=-1)
        cur = (t // SEL_BLOCK)[:, None]
        valid = blk[None, :] <= cur
        forced = (blk[None, :] == 0) | (blk[None, :] == cur) | (blk[None, :] == cur - 1)
        score = jnp.where((valid & forced)[None, :, None, :], jnp.inf,
                          jnp.where(valid[None, :, None, :], imp, -jnp.inf))
        top_val, top_idx = lax.top_k(score, n_sel)
        pos = top_idx[..., None] * SEL_BLOCK + jnp.arange(SEL_BLOCK)
        ok = (top_val > -jnp.inf)[..., None] & (pos <= t[None, :, None, None, None])
        pos = pos.reshape(B, qc_len, N_KV, n_sel * SEL_BLOCK)
        ok = ok.reshape(B, qc_len, N_KV, n_sel * SEL_BLOCK)
        bkt = t5_bucket(t[None, :, None, None] - pos)
        o_s = jax.vmap(sel_group, in_axes=(2, 1, 1, 2, 2, 2, 0), out_axes=2)(
            qc, k_sb, v_sb, top_idx, bkt, ok, tab_g)
        w0 = qs - win_pos0 + 1
        kw = lax.dynamic_slice_in_dim(kw_pad, w0, n_kw, axis=1).astype(f32)
        vw = lax.dynamic_slice_in_dim(vw_pad, w0, n_kw, axis=1).astype(f32)
        s_pos = qs - WINDOW + 1 + jnp.arange(n_kw)
        d_w = t[:, None] - s_pos[None, :]
        ok_w = (d_w >= 0) & (d_w < WINDOW) & (s_pos[None, :] >= win_pos0)
        s_w = jnp.einsum('bqghd,bkgd->bqghk', qc, kw) + head_bias(d_w)
        p_w = masked_softmax(s_w, ok_w[:, None, None, :])
        o_w = jnp.einsum('bqghk,bkgd->bqghd', p_w, vw)
        o = gc[..., 0:1] * o_c + gc[..., 1:2] * o_s + gc[..., 2:3] * o_w
        return o.astype(q.dtype)

    out = lax.map(chunk, jnp.arange(n_chunks))
    return jnp.moveaxis(out, 0, 1).reshape(B, Q, N_KV, HPG, HEAD_DIM)


def nsa_mixer(h, shared, pos0, w_qg, w_out, rel_bias):
    B, S = h.shape[0], h.shape[1]
    nq = N_HEADS * HEAD_DIM
    qg = h @ w_qg
    q = (qg[..., :nq] * HEAD_DIM ** -0.5).reshape(B, S, N_KV, HPG, HEAD_DIM)
    gates = jax.nn.sigmoid(qg[..., nq:].astype(jnp.float32)).reshape(B, S, N_KV, HPG, N_BRANCH)
    k_cb, v_cb, k_sel, v_sel, k_win, v_win, win_pos0 = shared
    o = nsa_attend(q, gates, k_cb, v_cb, k_sel, v_sel, k_win, v_win, pos0, win_pos0, rel_bias)
    return o.reshape(B, S, nq) @ w_out


def shared_kv(x, c, past_kv, past_win, page_table, kv_mod_w, kv_mod_b, kv_norm_g, w_kv,
              cmp_pe, cmp_w1, cmp_w2):
    Bn, S = x.shape[0], x.shape[1]
    shift, scale = jnp.split(c @ kv_mod_w + kv_mod_b, 2, axis=-1)
    hk = rmsnorm(x, kv_norm_g) * (1 + scale[:, None, :]) + shift[:, None, :]
    kv = (hk @ w_kv).reshape(Bn, S, 6, N_KV, HEAD_DIM)
    rows, win_rows = kv[:, :, :4], kv[:, :, 4:]
    if past_kv is None:
        full, wfull, win_pos0, keep = rows, win_rows, 0, min(WINDOW, S)
    else:
        past = past_kv[page_table].reshape(Bn, PAST_LEN, 4, N_KV, HEAD_DIM).astype(rows.dtype)
        full = jnp.concatenate([past, rows], axis=1)
        keep = past_win.shape[1]
        wfull = jnp.concatenate([past_win.astype(rows.dtype), win_rows], axis=1)
        win_pos0 = PAST_LEN - keep
    pad = (-full.shape[1]) % SEL_BLOCK
    full = jnp.pad(full, ((0, 0), (0, pad), (0, 0), (0, 0), (0, 0)))
    k_cb = compress_blocks(full[:, :, 0], cmp_pe[0], cmp_w1[0], cmp_w2[0])
    v_cb = compress_blocks(full[:, :, 1], cmp_pe[1], cmp_w1[1], cmp_w2[1])
    shared = (k_cb, v_cb, full[:, :, 2], full[:, :, 3], wfull[:, :, 0], wfull[:, :, 1], win_pos0)
    return shared, rows, wfull[:, wfull.shape[1] - keep:]


def setup_inputs(seed: int = 0) -> dict:
    key = jax.random.key(seed)
    ks = iter(jax.random.split(key, 40))
    D = D_MODEL

    def nrm(shape, scale):
        return jax.random.normal(next(ks), shape, jnp.float32) * scale

    n_pages = PAST_LEN // PAGE_SIZE
    n_used = DEC_BATCH * n_pages
    n_pool = n_used + n_used // 4
    win_buf = min(WINDOW, PAST_LEN)
    return {
        'x_prompt': nrm((BATCH, SEQ, D), 1.0),
        'x_sample': nrm((DEC_BATCH, DEC_SEQ, D), 1.0),
        'c_prompt': nrm((BATCH, D), 1.0),
        'c_sample': nrm((DEC_BATCH, D), 1.0),
        'cache_kv': nrm((n_pool, PAGE_SIZE, 4, N_KV, HEAD_DIM), 1.0),
        'cache_win': nrm((DEC_BATCH, win_buf, 2, N_KV, HEAD_DIM), 1.0),
        'state_conv_a': nrm((N_A, DEC_BATCH, CONV_W - 1, D), 1.0),
        'state_ffn_conv': nrm((DEPTH, DEC_BATCH, CONV_W - 1, 2 * D_FF), 1.0),
        'page_table': jax.random.permutation(next(ks), n_pool)[:n_used].reshape(DEC_BATCH, n_pages).astype(jnp.int32),
        'mod_w': nrm((DEPTH, 2, D, 3 * D), 0.5 * D ** -0.5),
        'mod_b': nrm((DEPTH, 2, 3 * D), 0.01),
        'norm_g': 1.0 + nrm((DEPTH, 4, D), 0.05),
        'a_w_in': nrm((N_A, D, 3 * D), D ** -0.5),
        'a_conv_w': nrm((N_A, CONV_W, D), CONV_W ** -0.5),
        'a_conv_b': nrm((N_A, D), 0.01),
        'a_w_out': nrm((N_A, D, D), D ** -0.5),
        'kv_mod_w': nrm((D, 2 * D), 0.5 * D ** -0.5),
        'kv_mod_b': nrm((2 * D,), 0.01),
        'kv_norm_g': 1.0 + nrm((D,), 0.05),
        'w_kv': nrm((D, 6 * N_KV * HEAD_DIM), D ** -0.5),
        'cmp_pe': nrm((2, CMP_BLOCK, HEAD_DIM), 0.1),
        'cmp_w1': nrm((2, CMP_BLOCK * HEAD_DIM, CMP_HIDDEN), (CMP_BLOCK * HEAD_DIM) ** -0.5),
        'cmp_w2': nrm((2, CMP_HIDDEN, HEAD_DIM), CMP_HIDDEN ** -0.5),
        'b_w_qg': nrm((N_B, D, N_HEADS * HEAD_DIM + N_BRANCH * N_HEADS), D ** -0.5),
        'b_w_out': nrm((N_B, N_HEADS * HEAD_DIM, D), (N_HEADS * HEAD_DIM) ** -0.5),
        'rel_bias': nrm((N_BUCKETS, N_HEADS), 0.5),
        'ffn_w_up': nrm((DEPTH, D, 2 * D_FF), D ** -0.5),
        'ffn_conv_w': nrm((DEPTH, CONV_W, 2 * D_FF), CONV_W ** -0.5),
        'ffn_conv_b': nrm((DEPTH, 2 * D_FF), 0.01),
        'ffn_w_down': nrm((DEPTH, D_FF, D), D_FF ** -0.5),
    }


def reference(x_prompt, x_sample, c_prompt, c_sample, cache_kv, cache_win, state_conv_a,
              state_ffn_conv, page_table, mod_w, mod_b, norm_g, a_w_in, a_conv_w, a_conv_b,
              a_w_out, kv_mod_w, kv_mod_b, kv_norm_g, w_kv, cmp_pe, cmp_w1, cmp_w2, b_w_qg,
              b_w_out, rel_bias, ffn_w_up, ffn_conv_w, ffn_conv_b, ffn_w_down):

    def trunk(x, c, conv_a_prev, ffn_prev, past_kv, past_win, pos0):
        conv_a_new, ffn_new = [], []
        shared, kv_rows, win_state = None, None, None
        for l in range(DEPTH):
            h, gate = modulated_norm(x, c, mod_w[l, 0], mod_b[l, 0], norm_g[l, 0])
            if l < N_A:
                o, st = short_conv_mixer(h, conv_a_prev[l], a_w_in[l], a_conv_w[l], a_conv_b[l], a_w_out[l])
                conv_a_new.append(st)
            else:
                if shared is None:
                    shared, kv_rows, win_state = shared_kv(
                        x, c, past_kv, past_win, page_table, kv_mod_w, kv_mod_b, kv_norm_g, w_kv,
                        cmp_pe, cmp_w1, cmp_w2)
                o = nsa_mixer(h, shared, pos0, b_w_qg[l - N_A], b_w_out[l - N_A], rel_bias)
            x = x + gate * rmsnorm(o, norm_g[l, 1])
            h, gate = modulated_norm(x, c, mod_w[l, 1], mod_b[l, 1], norm_g[l, 2])
            o, st = conv_ffn(h, ffn_prev[l], ffn_w_up[l], ffn_conv_w[l], ffn_conv_b[l], ffn_w_down[l])
            ffn_new.append(st)
            x = x + gate * rmsnorm(o, norm_g[l, 3])
        return x, kv_rows, win_state, jnp.stack(conv_a_new), jnp.stack(ffn_new)

    bp = x_prompt.shape[0]
    zeros_a = jnp.zeros((N_A, bp, CONV_W - 1, D_MODEL), x_prompt.dtype)
    zeros_f = jnp.zeros((DEPTH, bp, CONV_W - 1, 2 * D_FF), x_prompt.dtype)
    y_prompt, kv_p, win_p, conv_a_p, ffn_p = trunk(x_prompt, c_prompt, zeros_a, zeros_f, None, None, 0)
    y_sample, kv_s, win_s, conv_a_s, ffn_s = trunk(x_sample, c_sample, state_conv_a, state_ffn_conv,
                                                   cache_kv, cache_win, PAST_LEN)
    return (y_prompt, y_sample, kv_p, kv_s, win_p, win_s, conv_a_p, conv_a_s, ffn_p, ffn_s)
```

```python
import functools
import math

import numpy as np
import jax
import jax.numpy as jnp
from jax import lax
from jax.experimental import pallas as pl
from jax.experimental.pallas import tpu as pltpu

F32 = jnp.float32
BF16 = jnp.bfloat16

N_HEADS = 16
HEAD_DIM = 64
N_KV = 4
HPG = N_HEADS // N_KV
CMP_BLOCK = 32
SEL_BLOCK = 64
TOP_N = 16
WINDOW = 512
N_BUCKETS = 32
MAX_EXACT = N_BUCKETS // 2
MAX_DISTANCE = 1024
PAGE_SIZE = 128
EPS = 1e-6

LANES = 128
TQ = 128
NEG_MASK = -1e30
NEG_SEL = -1e9
BIG_SCORE = 1e30
VMEM_LIMIT_BYTES = 56 * 1024 * 1024


def _bucket_thresholds():
    d = np.arange(0, 4 * MAX_DISTANCE)
    nf = np.maximum(d, 1).astype(np.float32)
    large = MAX_EXACT + (np.log(nf / MAX_EXACT) / math.log(MAX_DISTANCE / MAX_EXACT)
                         * (N_BUCKETS - MAX_EXACT)).astype(np.int32)
    bucket = np.where(d < MAX_EXACT, d, np.minimum(large, N_BUCKETS - 1))
    assert np.all(np.diff(bucket) >= 0)
    return [int(np.argmax(bucket >= k)) for k in range(N_BUCKETS)]


BUCKET_THR = _bucket_thresholds()
N_TBL = -(-(BUCKET_THR[-1] + TQ - 1) // TQ)
N_WIN_TILES = WINDOW // TQ + 1


def _cparams(*sem):
    return pltpu.CompilerParams(dimension_semantics=sem, vmem_limit_bytes=VMEM_LIMIT_BYTES)


def _dot(a, b):
    return jnp.dot(a, b, preferred_element_type=F32)


def _dot_nt(a, b):
    return lax.dot_general(a, b, (((1,), (1,)), ((), ())), preferred_element_type=F32)


def _rms(x, g):
    return (x * lax.rsqrt(jnp.mean(x * x, axis=-1, keepdims=True) + EPS)) * g


def _sigmoid(x):
    return 1.0 / (1.0 + jnp.exp(-x))


def _shift_rows(v, carry):
    row = lax.broadcasted_iota(jnp.int32, v.shape, 0)
    r1 = jnp.where(row == 0, carry[7:8], pltpu.roll(v, 1, 0))
    r2 = jnp.where(row == 0, carry[6:7], jnp.where(row == 1, carry[7:8], pltpu.roll(v, 2, 0)))
    return r1, r2


def _conv3(v, r1, r2, cw, cb):
    return (cw[0:1] * r2 + cw[1:2] * r1) + cw[2:3] * v + cb


def _mod_kernel(c_ref, w_ref, b_ref, o_ref):
    o_ref[0] = _dot(c_ref[...].astype(BF16), w_ref[0].astype(BF16)) + b_ref[0]


def _mod_call(c_all, w, b):
    n, d, nn = w.shape
    r = c_all.shape[0]
    tn = 512
    return pl.pallas_call(
        _mod_kernel, grid=(n, nn // tn),
        in_specs=[pl.BlockSpec((r, d), lambda i, j: (0, 0)),
                  pl.BlockSpec((1, d, tn), lambda i, j: (i, 0, j)),
                  pl.BlockSpec((1, 1, tn), lambda i, j: (i, 0, j))],
        out_specs=pl.BlockSpec((1, r, tn), lambda i, j: (i, 0, j)),
        out_shape=jax.ShapeDtypeStruct((n, r, nn), F32),
        compiler_params=_cparams("arbitrary", "arbitrary"), name="mod")(c_all, w, b)


def _mixer_kernel(seq_mode, d, *refs):
    if seq_mode:
        x_ref, mod_ref, g0_ref, g1_ref, win_ref, cw_ref, cb_ref, wout_ref, xo_ref, st_ref, carry = refs
    else:
        x_ref, mod_ref, g0_ref, g1_ref, win_ref, cw_ref, cb_ref, wout_ref, p0_ref, p1_ref, xo_ref, st_ref = refs
    x = x_ref[0]
    m = mod_ref[0]
    h = _rms(x, g0_ref[...]) * (1.0 + m[:, d:2 * d]) + m[:, :d]
    z = _dot(h.astype(BF16), win_ref[...])
    bg, cg, u = z[:, :d], z[:, d:2 * d], z[:, 2 * d:]
    v = cg * u
    if seq_mode:
        @pl.when(pl.program_id(1) == 0)
        def _():
            carry[...] = jnp.zeros_like(carry)
        r1, r2 = _shift_rows(v, carry[...])
        carry[...] = v[-8:]
        st_ref[0] = v[-8:]
    else:
        r1, r2 = p1_ref[0], p0_ref[0]
        st_ref[0] = v
    y = _conv3(v, r1, r2, cw_ref[...], cb_ref[...])
    o = _dot((bg * y).astype(BF16), wout_ref[...])
    xo_ref[0] = x + m[:, 2 * d:] * _rms(o, g1_ref[...])


def _row_specs(bx, s, tm, mod):
    sm = mod.shape[1]
    if sm == 1:
        mod_spec = pl.BlockSpec((1, 1, mod.shape[2]), lambda b, i: (b, 0, 0))
    else:
        mod_spec = pl.BlockSpec((1, tm, mod.shape[2]), lambda b, i: (b, i, 0))
    return mod_spec


def _full(shape):
    nd = len(shape)
    return pl.BlockSpec(shape, lambda b, i, _nd=nd: (0,) * _nd)


def _mixer_call(x, mod, g0, g1, w_in, cw, cb, w_out, prev, tm):
    bx, s, d = x.shape
    seq_mode = prev is None
    row = pl.BlockSpec((1, tm, d), lambda b, i: (b, i, 0))
    in_specs = [row, _row_specs(bx, s, tm, mod), _full((1, d)), _full((1, d)), _full(w_in.shape),
                _full(cw.shape), _full((1, d)), _full(w_out.shape)]
    args = [x, mod, g0, g1, w_in, cw, cb, w_out]
    if seq_mode:
        st_shape, st_spec = (bx, 8, d), pl.BlockSpec((1, 8, d), lambda b, i: (b, 0, 0))
        scratch = [pltpu.VMEM((8, d), F32)]
    else:
        in_specs += [row, row]
        args += [prev[0], prev[1]]
        st_shape, st_spec = (bx, s, d), row
        scratch = []
    return pl.pallas_call(
        functools.partial(_mixer_kernel, seq_mode, d), grid=(bx, s // tm),
        in_specs=in_specs, out_specs=[row, st_spec],
        out_shape=[jax.ShapeDtypeStruct((bx, s, d), F32), jax.ShapeDtypeStruct(st_shape, F32)],
        scratch_shapes=scratch, compiler_params=_cparams("arbitrary", "arbitrary"), name="mixer_a")(*args)


def _ffn_kernel(seq_mode, d, dff, n_chunks, *refs):
    if seq_mode:
        x_ref, mod_ref, g2_ref, g3_ref, wup_ref, cw_ref, cb_ref, wdn_ref, xo_ref, st_ref, carry = refs
    else:
        x_ref, mod_ref, g2_ref, g3_ref, wup_ref, cw_ref, cb_ref, wdn_ref, p0_ref, p1_ref, xo_ref, st_ref = refs
    x = x_ref[0]
    m = mod_ref[0]
    h = (_rms(x, g2_ref[...]) * (1.0 + m[:, d:2 * d]) + m[:, :d]).astype(BF16)
    if seq_mode:
        @pl.when(pl.program_id(1) == 0)
        def _():
            carry[...] = jnp.zeros_like(carry)
    cwid = dff // n_chunks
    acc = jnp.zeros((x.shape[0], d), F32)
    for k in range(n_chunks):
        halves = []
        for c0 in (k * cwid, dff + k * cwid):
            up = _dot(h, wup_ref[:, c0:c0 + cwid])
            if seq_mode:
                r1, r2 = _shift_rows(up, carry[:, c0:c0 + cwid])
                carry[:, c0:c0 + cwid] = up[-8:]
                st_ref[0, :, c0:c0 + cwid] = up[-8:]
            else:
                r1, r2 = p1_ref[0, :, c0:c0 + cwid], p0_ref[0, :, c0:c0 + cwid]
                st_ref[0, :, c0:c0 + cwid] = up
            halves.append(_conv3(up, r1, r2, cw_ref[:, c0:c0 + cwid], cb_ref[:, c0:c0 + cwid]))
        u, g = halves
        act = (g * _sigmoid(g)) * u
        acc = acc + _dot(act.astype(BF16), wdn_ref[k * cwid:(k + 1) * cwid, :])
    xo_ref[0] = x + m[:, 2 * d:] * _rms(acc, g3_ref[...])


def _ffn_call(x, mod, g2, g3, w_up, cw, cb, w_dn, prev, tm, n_chunks):
    bx, s, d = x.shape
    dff = w_dn.shape[0]
    seq_mode = prev is None
    row = pl.BlockSpec((1, tm, d), lambda b, i: (b, i, 0))
    in_specs = [row, _row_specs(bx, s, tm, mod), _full((1, d)), _full((1, d)), _full(w_up.shape),
                _full(cw.shape), _full((1, 2 * dff)), _full(w_dn.shape)]
    args = [x, mod, g2, g3, w_up, cw, cb, w_dn]
    if seq_mode:
        st_shape, st_spec = (bx, 8, 2 * dff), pl.BlockSpec((1, 8, 2 * dff), lambda b, i: (b, 0, 0))
        scratch = [pltpu.VMEM((8, 2 * dff), F32)]
    else:
        prow = pl.BlockSpec((1, tm, 2 * dff), lambda b, i: (b, i, 0))
        in_specs += [prow, prow]
        args += [prev[0], prev[1]]
        st_shape, st_spec = (bx, s, 2 * dff), prow
        scratch = []
    return pl.pallas_call(
        functools.partial(_ffn_kernel, seq_mode, d, dff, n_chunks), grid=(bx, s // tm),
        in_specs=in_specs, out_specs=[row, st_spec],
        out_shape=[jax.ShapeDtypeStruct((bx, s, d), F32), jax.ShapeDtypeStruct(st_shape, F32)],
        scratch_shapes=scratch, compiler_params=_cparams("arbitrary", "arbitrary"), name="conv_ffn")(*args)


def _kvq_kernel(d, n_rows_cols, x_ref, modkv_ref, mod_ref, gkv_ref, g0_ref, wkv_ref, wqg_ref,
                rows_ref, win_ref, q_ref, gate_ref):
    x = x_ref[0]
    r = x * lax.rsqrt(jnp.mean(x * x, axis=-1, keepdims=True) + EPS)
    mk = modkv_ref[0]
    hk = (r * gkv_ref[...]) * (1.0 + mk[:, d:]) + mk[:, :d]
    kv = _dot(hk.astype(BF16), wkv_ref[...])
    rows_ref[0] = kv[:, :n_rows_cols]
    win_ref[0] = kv[:, n_rows_cols:]
    m = mod_ref[0]
    h1 = (r * g0_ref[...]) * (1.0 + m[:, d:2 * d]) + m[:, :d]
    qg = _dot(h1.astype(BF16), wqg_ref[...])
    nq = N_HEADS * HEAD_DIM
    q_ref[0] = qg[:, :nq] * (HEAD_DIM ** -0.5)
    gate_ref[0] = _sigmoid(qg[:, nq:])


def _kvq_call(x, modkv, mod, gkv, g0, w_kv, w_qg, tm):
    bx, s, d = x.shape
    nkv = w_kv.shape[1]
    n_rows_cols = 4 * N_KV * HEAD_DIM
    n_win_cols = nkv - n_rows_cols
    nq = N_HEADS * HEAD_DIM
    ng = w_qg.shape[1] - nq
    row = lambda w: pl.BlockSpec((1, tm, w), lambda b, i: (b, i, 0))
    return pl.pallas_call(
        functools.partial(_kvq_kernel, d, n_rows_cols), grid=(bx, s // tm),
        in_specs=[row(d), _row_specs(bx, s, tm, modkv), _row_specs(bx, s, tm, mod), _full((1, d)), _full((1, d)),
                  _full(w_kv.shape), _full(w_qg.shape)],
        out_specs=[row(n_rows_cols), row(n_win_cols), row(nq), row(ng)],
        out_shape=[jax.ShapeDtypeStruct((bx, s, n_rows_cols), F32), jax.ShapeDtypeStruct((bx, s, n_win_cols), F32),
                   jax.ShapeDtypeStruct((bx, s, nq), F32), jax.ShapeDtypeStruct((bx, s, ng), F32)],
        compiler_params=_cparams("arbitrary", "arbitrary"), name="kv_q_proj")(x, modkv, mod, gkv, g0, w_kv, w_qg)


def _outproj_kernel(d, n_branch, *refs):
    x_ref, mod_ref, g1_ref, w_ref = refs[:4]
    xo_ref = refs[-1]
    if n_branch == 1:
        o = refs[4][0]
    else:
        o = sum(refs[4 + 2 * i][0] * refs[5 + 2 * i][0] for i in range(n_branch))
    y = _dot(o.astype(BF16), w_ref[...])
    xo_ref[0] = x_ref[0] + mod_ref[0][:, 2 * d:] * _rms(y, g1_ref[...])


def _outproj_call(x, mod, g1, w, branches, tm):
    bx, s, d = x.shape
    row = pl.BlockSpec((1, tm, d), lambda b, i: (b, i, 0))
    n_branch = 1 if len(branches) == 1 else len(branches) // 2
    return pl.pallas_call(
        functools.partial(_outproj_kernel, d, n_branch), grid=(bx, s // tm),
        in_specs=[row, _row_specs(bx, s, tm, mod), _full((1, d)), _full(w.shape)] + [row] * len(branches),
        out_specs=row, out_shape=jax.ShapeDtypeStruct((bx, s, d), F32),
        compiler_params=_cparams("arbitrary", "arbitrary"), name="attn_out_proj")(x, mod, g1, w, *branches)


def _gelu_tanh(x):
    return x * (0.5 * (1.0 + jnp.tanh(math.sqrt(2.0 / math.pi) * (x + 0.044715 * (x * x * x)))))


def _compress_kernel(tmb, x_ref, pe_ref, w1_ref, w2_ref, o_ref):
    acc = jnp.zeros((tmb, LANES), F32)
    for r in range(CMP_BLOCK):
        xr = x_ref[pl.ds(r, tmb, stride=CMP_BLOCK), :] + pe_ref[0, r]
        acc = acc + _dot(xr.astype(BF16), w1_ref[0, r])
    o_ref[...] = _dot(_gelu_tanh(acc).astype(BF16), w2_ref[0])


def _compress_call(rows2d, pe2, w1bd, w2bd):
    m = rows2d.shape[0] // CMP_BLOCK
    tmb = max(t for t in range(8, min(256, m) + 1, 8) if m % t == 0)
    return pl.pallas_call(
        functools.partial(_compress_kernel, tmb), grid=(m // tmb, 4),
        in_specs=[pl.BlockSpec((tmb * CMP_BLOCK, LANES), lambda i, j: (i, j)),
                  pl.BlockSpec((1, CMP_BLOCK, 1, LANES), lambda i, j: (j // 2, 0, 0, 0)),
                  pl.BlockSpec((1, CMP_BLOCK, LANES, LANES), lambda i, j: (j // 2, 0, 0, 0)),
                  pl.BlockSpec((1, LANES, LANES), lambda i, j: (j // 2, 0, 0))],
        out_specs=pl.BlockSpec((tmb, LANES), lambda i, j: (i, j)),
        out_shape=jax.ShapeDtypeStruct((m, 4 * LANES), F32),
        compiler_params=_cparams("arbitrary", "arbitrary"), name="compress")(rows2d, pe2, w1bd, w2bd)


def _compress_weights(cmp_pe, cmp_w1, cmp_w2):
    z = jnp.zeros((2, CMP_BLOCK, HEAD_DIM, HEAD_DIM), F32)
    w1 = cmp_w1.reshape(2, CMP_BLOCK, HEAD_DIM, HEAD_DIM)
    w1bd = jnp.concatenate([jnp.concatenate([w1, z], -1), jnp.concatenate([z, w1], -1)], -2).astype(BF16)
    z2 = jnp.zeros((2, HEAD_DIM, HEAD_DIM), F32)
    w2bd = jnp.concatenate([jnp.concatenate([cmp_w2, z2], -1), jnp.concatenate([z2, cmp_w2], -1)], -2).astype(BF16)
    pe2 = jnp.concatenate([cmp_pe, cmp_pe], -1)[:, :, None, :]
    return pe2, w1bd, w2bd


def _bias_of_distance(dist, tab):
    val = jnp.zeros(dist.shape, F32) + tab(0)
    for k in range(1, N_BUCKETS):
        val = jnp.where(dist >= BUCKET_THR[k], tab(k), val)
    return val


def _toeplitz_kernel(window, tab_ref, o_ref):
    g = pl.program_id(0)
    dt = pl.program_id(1)
    i = lax.broadcasted_iota(jnp.int32, (TQ, LANES), 0)
    j = lax.broadcasted_iota(jnp.int32, (TQ, LANES), 1)
    dist = dt * TQ + i - j
    for hh in range(HPG):
        head = g * HPG + hh
        val = _bias_of_distance(dist, lambda k, head=head: tab_ref[k, head])
        val = jnp.where(dist < 0, NEG_MASK, val)
        if window is not None:
            val = jnp.where(dist >= window, NEG_MASK, val)
        o_ref[0, 0, hh * TQ:(hh + 1) * TQ, :] = val


def _toeplitz_call(rel_bias, n_chunks, window):
    return pl.pallas_call(
        functools.partial(_toeplitz_kernel, window), grid=(N_KV, n_chunks),
        in_specs=[pl.BlockSpec(memory_space=pltpu.SMEM)],
        out_specs=pl.BlockSpec((1, 1, HPG * TQ, LANES), lambda g, t: (g, t, 0, 0)),
        out_shape=jax.ShapeDtypeStruct((N_KV, n_chunks, HPG * TQ, LANES), F32),
        compiler_params=_cparams("arbitrary", "arbitrary"), name="bias_toeplitz")(rel_bias)


def _bias_rows_kernel(dist_ref, tab_ref, o_ref):
    dist = dist_ref[...]
    o_ref[...] = _bias_of_distance(dist, lambda k: tab_ref[k:k + 1, :])


def _bias_rows_call(dist, tab_lanes):
    r = dist.shape[0]
    return pl.pallas_call(
        _bias_rows_kernel, grid=(1,),
        in_specs=[pl.BlockSpec((r, LANES), lambda i: (0, 0)), pl.BlockSpec((N_BUCKETS, LANES), lambda i: (0, 0))],
        out_specs=pl.BlockSpec((r, LANES), lambda i: (0, 0)),
        out_shape=jax.ShapeDtypeStruct((r, LANES), F32),
        compiler_params=_cparams("arbitrary"), name="bias_rows")(dist, tab_lanes)


def _nsa_seq_kernel(n_qt, tab_ref, q_ref, gate_ref, ksel_ref, vsel_ref, kwin_ref, vwin_ref, kcb_ref, vcb_ref,
                    tsel_ref, twin_ref, o_ref, kaug, v1, kw, vw1, qaug, m_sc, acc_sc):
    g = pl.program_id(1)
    qi = pl.program_id(2)
    s_len = kaug.shape[0]
    lane = lax.broadcasted_iota(jnp.int32, (TQ, LANES), 1)
    low = lane < HEAD_DIM
    g_odd = (g % 2) == 1

    @pl.when(qi == 0)
    def _build_kv():
        ch = 512
        lane_c = lax.broadcasted_iota(jnp.int32, (ch, LANES), 1)
        row_c = lax.broadcasted_iota(jnp.int32, (ch, LANES), 0)
        low_c = lane_c < HEAD_DIM
        ones_c = jnp.where(lane_c == HEAD_DIM, 1.0, 0.0)

        def pick(ref, r0):
            x = ref[0, pl.ds(r0, ch), :]
            return jnp.where(g_odd, pltpu.roll(x, HEAD_DIM, 1), x)

        def body(c, _):
            r0 = pl.multiple_of(c * ch, ch)
            blk = (r0 + row_c) // SEL_BLOCK
            onehot = jnp.where(lane_c - HEAD_DIM == blk, 1.0, 0.0)
            kaug[pl.ds(r0, ch), :] = jnp.where(low_c, pick(ksel_ref, r0), onehot).astype(BF16)
            v1[pl.ds(r0, ch), :] = jnp.where(low_c, pick(vsel_ref, r0), ones_c).astype(BF16)
            kw[pl.ds(r0, ch), :] = jnp.where(low_c, pick(kwin_ref, r0), 0.0).astype(BF16)
            vw1[pl.ds(r0, ch), :] = jnp.where(low_c, pick(vwin_ref, r0), ones_c).astype(BF16)
            return 0
        lax.fori_loop(0, s_len // ch, body, 0)

    qh = []
    for hh in range(HPG):
        qv = q_ref[0, :, (hh // 2) * LANES:(hh // 2 + 1) * LANES]
        if hh % 2 == 1:
            qv = pltpu.roll(qv, HEAD_DIM, 1)
        qh.append(jnp.where(low, qv, 0.0))
    qw = jnp.concatenate(qh, axis=0).astype(BF16)

    t_row = qi * TQ + lax.broadcasted_iota(jnp.int32, (TQ, LANES), 0)
    cblk = 2 * (lane % HEAD_DIM) + lane // HEAD_DIM
    d_c = t_row - (cblk * CMP_BLOCK + (CMP_BLOCK - 1))
    ok_c = d_c >= 0
    s_c = _dot_nt(qw, kcb_ref[0, 0].astype(BF16))
    o_c = []
    vcb = vcb_ref[0, 0].astype(BF16)
    imp = jnp.zeros((TQ, LANES), F32)
    for hh in range(HPG):
        head = g * HPG + hh
        bias = _bias_of_distance(d_c, lambda k, head=head: tab_ref[k, head])
        s = jnp.where(ok_c, s_c[hh * TQ:(hh + 1) * TQ] + bias, NEG_MASK)
        mx = jnp.max(s, axis=1, keepdims=True)
        p = jnp.where(ok_c, jnp.exp(s - mx), 0.0)
        den = jnp.sum(p, axis=1, keepdims=True)
        pn = p / jnp.where(den > 0, den, 1.0)
        imp = imp + pn
        o_c.append(_dot(pn.astype(BF16), vcb))
    imp = imp + pltpu.roll(imp, HEAD_DIM, 1)

    blk = lane % HEAD_DIM
    cur = t_row // SEL_BLOCK
    valid = blk <= cur
    forced = (blk == 0) | (blk == cur) | (blk == cur - 1)
    score = jnp.where(valid & forced, BIG_SCORE, jnp.where(valid, imp, -BIG_SCORE))
    cnt = jnp.zeros((TQ, LANES), jnp.int32)
    for k in range(1, HEAD_DIM):
        r = pltpu.roll(score, k, 1)
        cnt = cnt + jnp.where((r > score) | ((r == score) & (blk >= k)), 1, 0)
    n_sel = min(TOP_N, s_len // SEL_BLOCK)
    selmask = jnp.where((cnt < n_sel) & valid, 0.0, NEG_SEL)

    for hh in range(HPG):
        qaug[hh * TQ:(hh + 1) * TQ, :] = jnp.where(low, qh[hh], selmask).astype(BF16)

    def flash(q_all, k_ref, v_ref, t_ref, n_tiles, n_tbl):
        m_sc[...] = jnp.full(m_sc.shape, NEG_MASK, F32)
        acc_sc[...] = jnp.zeros(acc_sc.shape, F32)

        def body(i, _):
            r0 = pl.multiple_of((qi - i) * TQ, TQ)
            s = _dot_nt(q_all, k_ref[pl.ds(r0, TQ), :]) + t_ref[0, jnp.minimum(i, n_tbl)]
            m_old = m_sc[...]
            m_new = jnp.maximum(m_old, jnp.max(s, axis=1, keepdims=True))
            p = jnp.exp(s - m_new)
            acc_sc[...] = jnp.exp(m_old - m_new) * acc_sc[...] + _dot(p.astype(BF16), v_ref[pl.ds(r0, TQ), :])
            m_sc[...] = m_new
            return 0
        lax.fori_loop(0, n_tiles, body, 0)
        acc = acc_sc[...]
        return acc / acc[:, HEAD_DIM:HEAD_DIM + 1]

    o_s = flash(qaug[...], kaug, v1, tsel_ref, qi + 1, N_TBL)
    o_w = flash(qw, kw, vw1, twin_ref, jnp.minimum(qi, N_WIN_TILES - 1) + 1, N_WIN_TILES - 1)

    outs = []
    for hh in range(HPG):
        gts = [gate_ref[0, :, hh * 3 + br:hh * 3 + br + 1] for br in range(3)]
        rs = slice(hh * TQ, (hh + 1) * TQ)
        outs.append(gts[0] * o_c[hh] + gts[1] * o_s[rs] + gts[2] * o_w[rs])
    for c in range(HPG // 2):
        o_ref[0, :, c * LANES:(c + 1) * LANES] = jnp.where(low, outs[2 * c], pltpu.roll(outs[2 * c + 1], HEAD_DIM, 1))


def _nsa_seq_call(rel_bias, q, gates, rows, win, kcb, vcb, tsel, twin):
    b, s, _ = q.shape
    n_qt = s // TQ
    kv_spec = lambda col0: pl.BlockSpec((1, s, LANES), lambda bb, g, i, _c=col0: (bb, 0, _c + g // 2))
    cb_spec = pl.BlockSpec((1, 1, kcb.shape[2], LANES), lambda bb, g, i: (bb, g, 0, 0))
    tbl_spec = lambda t: pl.BlockSpec((1,) + t.shape[1:], lambda bb, g, i: (g, 0, 0, 0))
    return pl.pallas_call(
        functools.partial(_nsa_seq_kernel, n_qt), grid=(b, N_KV, n_qt),
        in_specs=[pl.BlockSpec(memory_space=pltpu.SMEM),
                  pl.BlockSpec((1, TQ, HPG * HEAD_DIM), lambda bb, g, i: (bb, i, g)),
                  pl.BlockSpec((1, TQ, LANES), lambda bb, g, i: (bb, i, g)),
                  kv_spec(4), kv_spec(6), kv_spec(0), kv_spec(2), cb_spec, cb_spec, tbl_spec(tsel), tbl_spec(twin)],
        out_specs=pl.BlockSpec((1, TQ, HPG * HEAD_DIM), lambda bb, g, i: (bb, i, g)),
        out_shape=jax.ShapeDtypeStruct((b, s, N_HEADS * HEAD_DIM), F32),
        scratch_shapes=[pltpu.VMEM((s, LANES), BF16)] * 4 + [pltpu.VMEM((HPG * TQ, LANES), BF16),
                                                             pltpu.VMEM((HPG * TQ, 1), F32),
                                                             pltpu.VMEM((HPG * TQ, LANES), F32)],
        compiler_params=_cparams("arbitrary", "arbitrary", "arbitrary"), name="nsa_seq")(
            rel_bias, q, gates, rows, rows, win, win, kcb, vcb, tsel, twin)


def _softmax_rows(s):
    p = jnp.exp(s - jnp.max(s, axis=0, keepdims=True))
    return p / jnp.sum(p, axis=0, keepdims=True)


def _nsa_step_kernel(n_pages, pt_ref, *refs):
    del pt_ref
    q_ref = refs[0]
    cb_refs = refs[1:1 + n_pages]
    pg_refs = refs[1 + n_pages:1 + 2 * n_pages]
    kvnew_ref, winnew_ref, cwin_ref, bsel_ref, bnew_ref, bwin_ref, bcmp_ref = refs[1 + 2 * n_pages:8 + 2 * n_pages]
    oc_ref, os_ref, ow_ref, nwin_ref = refs[8 + 2 * n_pages:12 + 2 * n_pages]
    imp_sc, s_sc = refs[12 + 2 * n_pages:]
    gw = N_KV * HEAD_DIM

    q16 = q_ref[0].astype(BF16)
    qp = jnp.concatenate([q16, jnp.zeros((LANES - N_HEADS, gw), BF16)], axis=0)

    cb = jnp.concatenate([r[0] for r in cb_refs], axis=0)
    n_cb = cb.shape[0]
    s_c = _dot_nt(cb[:, :gw].astype(BF16), qp) + bcmp_ref[...]
    pn_c = _softmax_rows(s_c)
    assert n_cb <= LANES and n_cb % 8 == 0
    pnt = jnp.concatenate([pn_c, jnp.zeros((LANES - n_cb, LANES), F32)], axis=0).T
    vcb = jnp.concatenate([cb[:, gw:], jnp.zeros((LANES - n_cb, gw), F32)], axis=0).astype(BF16)
    oc_ref[0] = _dot(pnt.astype(BF16), vcb)[:N_HEADS]

    lane_c = lax.broadcasted_iota(jnp.int32, pn_c.shape, 1)
    y = pn_c + pltpu.roll(pn_c, 1, 1) + pltpu.roll(pn_c, 2, 1) + pltpu.roll(pn_c, 3, 1)
    ym = jnp.where(lane_c % HPG == HPG - 1, y, 0.0)
    imp_sc[...] = ym + pltpu.roll(ym, LANES - 1, 1) + pltpu.roll(ym, LANES - 2, 1) + pltpu.roll(ym, LANES - 3, 1)
    n_past_blk = n_cb // 2
    imp = imp_sc[pl.ds(0, n_past_blk, stride=2), :] + imp_sc[pl.ds(1, n_past_blk, stride=2), :]

    n_sb = n_past_blk + 1
    n_rows = -(-n_sb // 8) * 8
    imp_p = jnp.concatenate([imp, jnp.zeros((n_rows - n_past_blk, LANES), F32)], axis=0)
    rowb = lax.broadcasted_iota(jnp.int32, (n_rows, LANES), 0)
    cur = n_past_blk
    valid = rowb <= cur
    forced = (rowb == 0) | (rowb == cur) | (rowb == cur - 1)
    score = jnp.where(valid & forced, BIG_SCORE, jnp.where(valid, imp_p, -BIG_SCORE))
    cnt = jnp.zeros((n_rows, LANES), jnp.int32)
    for r in range(n_sb):
        sr = score[r:r + 1, :]
        cnt = cnt + jnp.where((sr > score) | ((sr == score) & (rowb > r)), 1, 0)
    selmask = jnp.where((cnt < min(TOP_N, n_sb)) & valid, 0.0, NEG_SEL)

    row_p = lax.broadcasted_iota(jnp.int32, (PAGE_SIZE, LANES), 0)
    blocks_per_page = PAGE_SIZE // SEL_BLOCK
    for p in range(n_pages):
        k = pg_refs[p][0, :, :gw].astype(BF16)
        s = _dot_nt(k, qp) + bsel_ref[p * PAGE_SIZE:(p + 1) * PAGE_SIZE, :]
        msk = selmask[blocks_per_page * p:blocks_per_page * p + 1, :]
        for j in range(1, blocks_per_page):
            msk = jnp.where(row_p >= j * SEL_BLOCK, selmask[blocks_per_page * p + j:blocks_per_page * p + j + 1, :], msk)
        s_sc[p * PAGE_SIZE:(p + 1) * PAGE_SIZE, :] = s + msk
    knew = jnp.broadcast_to(kvnew_ref[0][:, 2 * gw:3 * gw], (8, gw)).astype(BF16)
    row8 = lax.broadcasted_iota(jnp.int32, (8, LANES), 0)
    s_new = _dot_nt(knew, qp) + bnew_ref[...] + selmask[cur:cur + 1, :]
    n_past = n_pages * PAGE_SIZE
    s_sc[n_past:n_past + 8, :] = jnp.where(row8 == 0, s_new, NEG_MASK)
    s_all = s_sc[...]
    mx = jnp.max(s_all, axis=0, keepdims=True)
    den = jnp.sum(jnp.exp(s_all - mx), axis=0, keepdims=True)
    acc = jnp.zeros((LANES, gw), F32)
    for p in range(n_pages):
        pn = jnp.exp(s_sc[p * PAGE_SIZE:(p + 1) * PAGE_SIZE, :] - mx) / den
        acc = acc + _dot(pn.T.astype(BF16), pg_refs[p][0, :, gw:].astype(BF16))
    pn_new = jnp.exp(s_sc[n_past:n_past + 8, :] - mx) / den
    pn_new = jnp.concatenate([pn_new, jnp.zeros((LANES - 8, LANES), F32)], axis=0)
    vnew = jnp.broadcast_to(kvnew_ref[0][:, 3 * gw:4 * gw], (LANES, gw)).astype(BF16)
    acc = acc + _dot(pn_new.T.astype(BF16), vnew)
    os_ref[0] = acc[:N_HEADS]

    w_len = cwin_ref.shape[1]
    roww = lax.broadcasted_iota(jnp.int32, (w_len, 2 * gw), 0)
    nw = jnp.where(roww == w_len - 1, winnew_ref[0], pltpu.roll(cwin_ref[0], w_len - 1, 0))
    nwin_ref[0] = nw
    pn_w = _softmax_rows(_dot_nt(nw[:, :gw].astype(BF16), qp) + bwin_ref[...])
    acc = jnp.zeros((LANES, gw), F32)
    for c in range(w_len // LANES):
        rs = slice(c * LANES, (c + 1) * LANES)
        acc = acc + _dot(pn_w[rs].T.astype(BF16), nw[rs, gw:].astype(BF16))
    ow_ref[0] = acc[:N_HEADS]


def _nsa_step_call(page_table, qrows, cb_pool, cache_pages, kvnew, winnew, cache_win, bsel, bnew, bwin, bcmp):
    nb, n_pages = page_table.shape
    gw = N_KV * HEAD_DIM
    w_len = cache_win.shape[1]
    cb_specs = [pl.BlockSpec((1, PAGE_SIZE // CMP_BLOCK, 2 * gw), lambda b, pt, _p=p: (pt[b, _p], 0, 0))
                for p in range(n_pages)]
    pg_specs = [pl.BlockSpec((1, PAGE_SIZE, 2 * gw), lambda b, pt, _p=p: (pt[b, _p], 0, 1)) for p in range(n_pages)]
    const = lambda a: pl.BlockSpec(a.shape, lambda b, pt: (0, 0))
    o_spec = pl.BlockSpec((1, N_HEADS, gw), lambda b, pt: (b, 0, 0))
    n_cb = n_pages * (PAGE_SIZE // CMP_BLOCK)
    grid_spec = pltpu.PrefetchScalarGridSpec(
        num_scalar_prefetch=1, grid=(nb,),
        in_specs=[pl.BlockSpec((1, N_HEADS, gw), lambda b, pt: (b, 0, 0))] + cb_specs + pg_specs + [
            pl.BlockSpec((1, 1, 4 * gw), lambda b, pt: (b, 0, 0)),
            pl.BlockSpec((1, 1, 2 * gw), lambda b, pt: (b, 0, 0)),
            pl.BlockSpec((1, w_len, 2 * gw), lambda b, pt: (b, 0, 0)),
            const(bsel), const(bnew), const(bwin), const(bcmp)],
        out_specs=[o_spec, o_spec, o_spec, pl.BlockSpec((1, w_len, 2 * gw), lambda b, pt: (b, 0, 0))],
        scratch_shapes=[pltpu.VMEM((n_cb, LANES), F32), pltpu.VMEM((n_pages * PAGE_SIZE + 8, LANES), F32)])
    o_shape = jax.ShapeDtypeStruct((nb, N_HEADS, gw), F32)
    return pl.pallas_call(
        functools.partial(_nsa_step_kernel, n_pages), grid_spec=grid_spec,
        out_shape=[o_shape, o_shape, o_shape, jax.ShapeDtypeStruct((nb, w_len, 2 * gw), F32)],
        compiler_params=_cparams("arbitrary"), name="nsa_step")(
            page_table, qrows, *([cb_pool] * n_pages), *([cache_pages] * n_pages), kvnew, winnew, cache_win,
            bsel, bnew, bwin, bcmp)


def _head_diag(o):
    b = o.shape[0]
    o5 = o.reshape(b, N_KV, HPG, N_KV, HEAD_DIM)
    return jnp.stack([o5[:, g, :, g, :] for g in range(N_KV)], axis=1).reshape(b, N_HEADS * HEAD_DIM)


def kernel(x_prompt, x_sample, c_prompt, c_sample, cache_kv, cache_win, state_conv_a, state_ffn_conv, page_table, mod_w, mod_b, norm_g, a_w_in, a_conv_w, a_conv_b, a_w_out, kv_mod_w, kv_mod_b, kv_norm_g, w_kv, cmp_pe, cmp_w1, cmp_w2, b_w_qg, b_w_out, rel_bias, ffn_w_up, ffn_conv_w, ffn_conv_b, ffn_w_down):
    bp, s, d = x_prompt.shape
    bs = x_sample.shape[0]
    depth = mod_w.shape[0]
    n_a = a_w_in.shape[0]
    assert depth == 2 and n_a == 1 and x_sample.shape[1] == 1
    dff = ffn_w_down.shape[1]
    n_pool = cache_kv.shape[0]
    n_pages = page_table.shape[1]
    past_len = n_pages * PAGE_SIZE
    gw = N_KV * HEAD_DIM
    nq = N_HEADS * HEAD_DIM

    n_c = bp + bs
    n_cp = -(-n_c // 8) * 8
    c_all = jnp.pad(jnp.concatenate([c_prompt, c_sample], 0), ((0, n_cp - n_c), (0, 0)))
    mods = _mod_call(c_all, mod_w.reshape(depth * 2, d, 3 * d), mod_b.reshape(depth * 2, 1, 3 * d))
    modkv = _mod_call(c_all, kv_mod_w[None], kv_mod_b[None, None])[0]
    mod_p = lambda i: mods[i, :bp][:, None, :]
    mod_s = lambda i: mods[i, bp:n_c][None]

    w_in = a_w_in[0].astype(BF16)
    w_out_a = a_w_out[0].astype(BF16)
    w_up = ffn_w_up.astype(BF16)
    w_dn = ffn_w_down.astype(BF16)
    w_kv_b = w_kv.astype(BF16)
    ng_pad = -(-(b_w_qg.shape[2] - nq) // LANES) * LANES
    w_qg = jnp.pad(b_w_qg[0], ((0, 0), (0, nq + ng_pad - b_w_qg.shape[2]))).astype(BF16)
    w_out_b = b_w_out[0].astype(BF16)
    g = lambda l, i: norm_g[l, i][None]
    pe2, w1bd, w2bd = _compress_weights(cmp_pe, cmp_w1, cmp_w2)
    n_chunks = 2

    tm = min(256, s)
    x1, st_a = _mixer_call(x_prompt, mod_p(0), g(0, 0), g(0, 1), w_in, a_conv_w[0], a_conv_b[0][None], w_out_a, None, tm)
    x2, st_f0 = _ffn_call(x1, mod_p(1), g(0, 2), g(0, 3), w_up[0], ffn_conv_w[0], ffn_conv_b[0][None], w_dn[0], None, tm, n_chunks)
    rows, win, q, gates = _kvq_call(x2, modkv[:bp][:, None, :], mod_p(2), kv_norm_g[None], g(1, 0), w_kv_b, w_qg, tm)
    cb = _compress_call(rows.reshape(bp * s, 4 * gw), pe2, w1bd, w2bd)
    n_cbk = s // CMP_BLOCK
    assert n_cbk <= LANES
    cb = jnp.pad(cb.reshape(bp, n_cbk, 2, N_KV, HEAD_DIM), ((0, 0), (0, LANES - n_cbk), (0, 0), (0, 0), (0, 0)))
    cb = cb.reshape(bp, LANES // 2, 2, 2, N_KV, HEAD_DIM)
    cb = cb.transpose(3, 0, 4, 2, 1, 5).reshape(2, bp, N_KV, LANES, HEAD_DIM)
    cb = jnp.pad(cb, ((0, 0),) * 4 + ((0, LANES - HEAD_DIM),))
    gates_g = jnp.pad(gates[:, :, :N_HEADS * 3].reshape(bp, s, N_KV, HPG * 3),
                      ((0, 0), (0, 0), (0, 0), (0, LANES - HPG * 3))).reshape(bp, s, N_KV * LANES)
    tsel = _toeplitz_call(rel_bias, N_TBL + 1, None)
    twin = _toeplitz_call(rel_bias, N_WIN_TILES, WINDOW)
    o_att = _nsa_seq_call(rel_bias, q, gates_g, rows, win, cb[0], cb[1], tsel, twin)
    x3 = _outproj_call(x2, mod_p(2), g(1, 1), w_out_b, [o_att], tm)
    y_prompt, st_f1 = _ffn_call(x3, mod_p(3), g(1, 2), g(1, 3), w_up[1], ffn_conv_w[1], ffn_conv_b[1][None], w_dn[1], None, tm, n_chunks)
    kv_p = rows.reshape(bp, s, 4, N_KV, HEAD_DIM)
    keep = min(WINDOW, s)
    win_p = win[:, s - keep:].reshape(bp, keep, 2, N_KV, HEAD_DIM)
    conv_a_p = st_a[None, :, 6:8]
    ffn_p = jnp.stack([st_f0[:, 6:8], st_f1[:, 6:8]])

    xs = x_sample.reshape(1, bs, d)
    prev_a = (state_conv_a[0, :, 0][None], state_conv_a[0, :, 1][None])
    xs1, v_a = _mixer_call(xs, mod_s(0), g(0, 0), g(0, 1), w_in, a_conv_w[0], a_conv_b[0][None], w_out_a, prev_a, bs)
    prev_f = lambda l: (state_ffn_conv[l, :, 0][None], state_ffn_conv[l, :, 1][None])
    xs2, up0 = _ffn_call(xs1, mod_s(1), g(0, 2), g(0, 3), w_up[0], ffn_conv_w[0], ffn_conv_b[0][None], w_dn[0], prev_f(0), bs, n_chunks)
    rows_s, win_s, q_s, gates_s = _kvq_call(xs2, modkv[bp:n_c][None], mod_s(2), kv_norm_g[None], g(1, 0), w_kv_b, w_qg, bs)
    cb_pool = _compress_call(cache_kv.reshape(n_pool * PAGE_SIZE, 4 * gw), pe2, w1bd, w2bd)
    cb_pool = cb_pool.reshape(n_pool, PAGE_SIZE // CMP_BLOCK, 2 * gw)
    w_len = cache_win.shape[1]
    n_cb_s = past_len // CMP_BLOCK
    dist = np.concatenate([past_len - np.arange(past_len), np.zeros(8, np.int64),
                           w_len - 1 - np.arange(w_len),
                           past_len - (np.arange(n_cb_s) * CMP_BLOCK + CMP_BLOCK - 1)]).astype(np.int32)
    assert dist.min() >= 0
    tab_lanes = jnp.pad(rel_bias, ((0, 0), (0, LANES - N_HEADS)))
    bias_rows = _bias_rows_call(jnp.asarray(np.repeat(dist[:, None], LANES, 1)), tab_lanes)
    bsel, bnew = bias_rows[:past_len], bias_rows[past_len:past_len + 8]
    bwin = bias_rows[past_len + 8:past_len + 8 + w_len]
    bcmp = bias_rows[past_len + 8 + w_len:]
    head_group = (np.arange(N_HEADS)[:, None] // HPG == np.arange(N_KV)[None, :])[None, :, :, None]
    qrows = jnp.where(head_group, q_s.reshape(bs, N_HEADS, 1, HEAD_DIM), 0.0).reshape(bs, N_HEADS, gw)
    oc, os_, ow, nwin = _nsa_step_call(
        page_table, qrows, cb_pool, cache_kv.reshape(n_pool, PAGE_SIZE, 4 * gw), rows_s.reshape(bs, 1, 4 * gw),
        win_s.reshape(bs, 1, 2 * gw), cache_win.reshape(bs, w_len, 2 * gw), bsel, bnew, bwin, bcmp)
    gts = gates_s[0, :, :N_HEADS * 3].reshape(bs, N_HEADS, 3)
    branches = []
    for br, o in enumerate((oc, os_, ow)):
        branches += [jnp.repeat(gts[:, :, br], HEAD_DIM, axis=1)[None], _head_diag(o)[None]]
    xs3 = _outproj_call(xs2, mod_s(2), g(1, 1), w_out_b, branches, bs)
    ys, up1 = _ffn_call(xs3, mod_s(3), g(1, 2), g(1, 3), w_up[1], ffn_conv_w[1], ffn_conv_b[1][None], w_dn[1], prev_f(1), bs, n_chunks)
    y_sample = ys.reshape(bs, 1, d)
    kv_s = rows_s.reshape(bs, 1, 4, N_KV, HEAD_DIM)
    win_state_s = nwin.reshape(bs, w_len, 2, N_KV, HEAD_DIM)
    conv_a_s = jnp.stack([state_conv_a[0, :, 1], v_a[0]], axis=1)[None]
    ffn_s = jnp.stack([jnp.stack([state_ffn_conv[l, :, 1], u[0]], axis=1) for l, u in ((0, up0), (1, up1))])
    return (y_prompt, y_sample, kv_p, kv_s, win_p, win_state_s, conv_a_p, conv_a_s, ffn_p, ffn_s)
```

```python
import functools
import math

import numpy as np
import jax
import jax.numpy as jnp
from jax import lax
from jax.experimental import pallas as pl
from jax.experimental.pallas import tpu as pltpu

F32 = jnp.float32
BF16 = jnp.bfloat16

N_HEADS = 16
HEAD_DIM = 64
N_KV = 4
HPG = N_HEADS // N_KV
CMP_BLOCK = 32
SEL_BLOCK = 64
TOP_N = 16
WINDOW = 512
N_BUCKETS = 32
MAX_EXACT = N_BUCKETS // 2
MAX_DISTANCE = 1024
PAGE_SIZE = 128
EPS = 1e-6

LANES = 128
TQ = 256
TK = 256
NEG_MASK = -1e30
NEG_SEL = -1e9
BIG_SCORE = 1e30
VMEM_LIMIT_BYTES = 56 * 1024 * 1024


def _bucket_thresholds():
    d = np.arange(0, 4 * MAX_DISTANCE)
    nf = np.maximum(d, 1).astype(np.float32)
    large = MAX_EXACT + (np.log(nf / MAX_EXACT) / math.log(MAX_DISTANCE / MAX_EXACT)
                         * (N_BUCKETS - MAX_EXACT)).astype(np.int32)
    bucket = np.where(d < MAX_EXACT, d, np.minimum(large, N_BUCKETS - 1))
    assert np.all(np.diff(bucket) >= 0)
    return [int(np.argmax(bucket >= k)) for k in range(N_BUCKETS)]


BUCKET_THR = _bucket_thresholds()
assert TQ == TK
N_TBL = -(-(BUCKET_THR[-1] + TK - 1) // TQ)
N_WIN_TILES = WINDOW // TK + 1


def _cparams(*sem):
    return pltpu.CompilerParams(dimension_semantics=sem, vmem_limit_bytes=VMEM_LIMIT_BYTES)


def _dot(a, b):
    return jnp.dot(a, b, preferred_element_type=F32)


def _dot_nt(a, b):
    return lax.dot_general(a, b, (((1,), (1,)), ((), ())), preferred_element_type=F32)


def _rms(x, g):
    return (x * lax.rsqrt(jnp.mean(x * x, axis=-1, keepdims=True) + EPS)) * g


def _sigmoid(x):
    return 1.0 / (1.0 + jnp.exp(-x))


def _shift_rows(v, carry):
    row = lax.broadcasted_iota(jnp.int32, v.shape, 0)
    r1 = jnp.where(row == 0, carry[7:8], pltpu.roll(v, 1, 0))
    r2 = jnp.where(row == 0, carry[6:7], jnp.where(row == 1, carry[7:8], pltpu.roll(v, 2, 0)))
    return r1, r2


def _conv3(v, r1, r2, cw, cb):
    return (cw[0:1] * r2 + cw[1:2] * r1) + cw[2:3] * v + cb


def _mod_kernel(c_ref, w_ref, b_ref, o_ref):
    o_ref[0] = _dot(c_ref[...].astype(BF16), w_ref[0].astype(BF16)) + b_ref[0]


def _mod_call(c_all, w, b):
    n, d, nn = w.shape
    r = c_all.shape[0]
    tn = 512
    return pl.pallas_call(
        _mod_kernel, grid=(n, nn // tn),
        in_specs=[pl.BlockSpec((r, d), lambda i, j: (0, 0)),
                  pl.BlockSpec((1, d, tn), lambda i, j: (i, 0, j)),
                  pl.BlockSpec((1, 1, tn), lambda i, j: (i, 0, j))],
        out_specs=pl.BlockSpec((1, r, tn), lambda i, j: (i, 0, j)),
        out_shape=jax.ShapeDtypeStruct((n, r, nn), F32),
        compiler_params=_cparams("arbitrary", "arbitrary"), name="mod")(c_all, w, b)


def _mixer_kernel(seq_mode, d, *refs):
    if seq_mode:
        x_ref, mod_ref, g0_ref, g1_ref, win_ref, cw_ref, cb_ref, wout_ref, xo_ref, st_ref, carry = refs
    else:
        x_ref, mod_ref, g0_ref, g1_ref, win_ref, cw_ref, cb_ref, wout_ref, p0_ref, p1_ref, xo_ref, st_ref = refs
    x = x_ref[0]
    m = mod_ref[0]
    h = _rms(x, g0_ref[...]) * (1.0 + m[:, d:2 * d]) + m[:, :d]
    z = _dot(h.astype(BF16), win_ref[...])
    bg, cg, u = z[:, :d], z[:, d:2 * d], z[:, 2 * d:]
    v = cg * u
    if seq_mode:
        @pl.when(pl.program_id(1) == 0)
        def _():
            carry[...] = jnp.zeros_like(carry)
        r1, r2 = _shift_rows(v, carry[...])
        carry[...] = v[-8:]
        st_ref[0] = v[-8:]
    else:
        r1, r2 = p1_ref[0], p0_ref[0]
        st_ref[0] = v
    y = _conv3(v, r1, r2, cw_ref[...], cb_ref[...])
    o = _dot((bg * y).astype(BF16), wout_ref[...])
    xo_ref[0] = x + m[:, 2 * d:] * _rms(o, g1_ref[...])


def _row_specs(bx, s, tm, mod):
    sm = mod.shape[1]
    if sm == 1:
        mod_spec = pl.BlockSpec((1, 1, mod.shape[2]), lambda b, i: (b, 0, 0))
    else:
        mod_spec = pl.BlockSpec((1, tm, mod.shape[2]), lambda b, i: (b, i, 0))
    return mod_spec


def _full(shape):
    nd = len(shape)
    return pl.BlockSpec(shape, lambda b, i, _nd=nd: (0,) * _nd)


def _mixer_call(x, mod, g0, g1, w_in, cw, cb, w_out, prev, tm):
    bx, s, d = x.shape
    seq_mode = prev is None
    row = pl.BlockSpec((1, tm, d), lambda b, i: (b, i, 0))
    in_specs = [row, _row_specs(bx, s, tm, mod), _full((1, d)), _full((1, d)), _full(w_in.shape),
                _full(cw.shape), _full((1, d)), _full(w_out.shape)]
    args = [x, mod, g0, g1, w_in, cw, cb, w_out]
    if seq_mode:
        st_shape, st_spec = (bx, 8, d), pl.BlockSpec((1, 8, d), lambda b, i: (b, 0, 0))
        scratch = [pltpu.VMEM((8, d), F32)]
    else:
        in_specs += [row, row]
        args += [prev[0], prev[1]]
        st_shape, st_spec = (bx, s, d), row
        scratch = []
    return pl.pallas_call(
        functools.partial(_mixer_kernel, seq_mode, d), grid=(bx, s // tm),
        in_specs=in_specs, out_specs=[row, st_spec],
        out_shape=[jax.ShapeDtypeStruct((bx, s, d), F32), jax.ShapeDtypeStruct(st_shape, F32)],
        scratch_shapes=scratch, compiler_params=_cparams("arbitrary", "arbitrary"), name="mixer_a")(*args)


def _ffn_kernel(seq_mode, d, dff, n_chunks, *refs):
    if seq_mode:
        x_ref, mod_ref, g2_ref, g3_ref, wup_ref, cw_ref, cb_ref, wdn_ref, xo_ref, st_ref, carry = refs
    else:
        x_ref, mod_ref, g2_ref, g3_ref, wup_ref, cw_ref, cb_ref, wdn_ref, p0_ref, p1_ref, xo_ref, st_ref = refs
    x = x_ref[0]
    m = mod_ref[0]
    h = (_rms(x, g2_ref[...]) * (1.0 + m[:, d:2 * d]) + m[:, :d]).astype(BF16)
    if seq_mode:
        @pl.when(pl.program_id(1) == 0)
        def _():
            carry[...] = jnp.zeros_like(carry)
    cwid = dff // n_chunks
    acc = jnp.zeros((x.shape[0], d), F32)
    for k in range(n_chunks):
        halves = []
        for c0 in (k * cwid, dff + k * cwid):
            up = _dot(h, wup_ref[:, c0:c0 + cwid])
            if seq_mode:
                r1, r2 = _shift_rows(up, carry[:, c0:c0 + cwid])
                carry[:, c0:c0 + cwid] = up[-8:]
                st_ref[0, :, c0:c0 + cwid] = up[-8:]
            else:
                r1, r2 = p1_ref[0, :, c0:c0 + cwid], p0_ref[0, :, c0:c0 + cwid]
                st_ref[0, :, c0:c0 + cwid] = up
            halves.append(_conv3(up, r1, r2, cw_ref[:, c0:c0 + cwid], cb_ref[:, c0:c0 + cwid]))
        u, g = halves
        act = (g * _sigmoid(g)) * u
        acc = acc + _dot(act.astype(BF16), wdn_ref[k * cwid:(k + 1) * cwid, :])
    xo_ref[0] = x + m[:, 2 * d:] * _rms(acc, g3_ref[...])


def _ffn_call(x, mod, g2, g3, w_up, cw, cb, w_dn, prev, tm, n_chunks):
    bx, s, d = x.shape
    dff = w_dn.shape[0]
    seq_mode = prev is None
    row = pl.BlockSpec((1, tm, d), lambda b, i: (b, i, 0))
    in_specs = [row, _row_specs(bx, s, tm, mod), _full((1, d)), _full((1, d)), _full(w_up.shape),
                _full(cw.shape), _full((1, 2 * dff)), _full(w_dn.shape)]
    args = [x, mod, g2, g3, w_up, cw, cb, w_dn]
    if seq_mode:
        st_shape, st_spec = (bx, 8, 2 * dff), pl.BlockSpec((1, 8, 2 * dff), lambda b, i: (b, 0, 0))
        scratch = [pltpu.VMEM((8, 2 * dff), F32)]
    else:
        prow = pl.BlockSpec((1, tm, 2 * dff), lambda b, i: (b, i, 0))
        in_specs += [prow, prow]
        args += [prev[0], prev[1]]
        st_shape, st_spec = (bx, s, 2 * dff), prow
        scratch = []
    return pl.pallas_call(
        functools.partial(_ffn_kernel, seq_mode, d, dff, n_chunks), grid=(bx, s // tm),
        in_specs=in_specs, out_specs=[row, st_spec],
        out_shape=[jax.ShapeDtypeStruct((bx, s, d), F32), jax.ShapeDtypeStruct(st_shape, F32)],
        scratch_shapes=scratch, compiler_params=_cparams("arbitrary", "arbitrary"), name="conv_ffn")(*args)


def _kvq_kernel(d, n_rows_cols, x_ref, modkv_ref, mod_ref, gkv_ref, g0_ref, wkv_ref, wqg_ref,
                rows_ref, win_ref, q_ref, gate_ref):
    x = x_ref[0]
    r = x * lax.rsqrt(jnp.mean(x * x, axis=-1, keepdims=True) + EPS)
    mk = modkv_ref[0]
    hk = (r * gkv_ref[...]) * (1.0 + mk[:, d:]) + mk[:, :d]
    kv = _dot(hk.astype(BF16), wkv_ref[...])
    rows_ref[0] = kv[:, :n_rows_cols]
    win_ref[0] = kv[:, n_rows_cols:]
    m = mod_ref[0]
    h1 = (r * g0_ref[...]) * (1.0 + m[:, d:2 * d]) + m[:, :d]
    qg = _dot(h1.astype(BF16), wqg_ref[...])
    nq = N_HEADS * HEAD_DIM
    q_ref[0] = qg[:, :nq] * (HEAD_DIM ** -0.5)
    gate_ref[0] = _sigmoid(qg[:, nq:])


def _kvq_call(x, modkv, mod, gkv, g0, w_kv, w_qg, tm):
    bx, s, d = x.shape
    nkv = w_kv.shape[1]
    n_rows_cols = 4 * N_KV * HEAD_DIM
    n_win_cols = nkv - n_rows_cols
    nq = N_HEADS * HEAD_DIM
    ng = w_qg.shape[1] - nq
    row = lambda w: pl.BlockSpec((1, tm, w), lambda b, i: (b, i, 0))
    return pl.pallas_call(
        functools.partial(_kvq_kernel, d, n_rows_cols), grid=(bx, s // tm),
        in_specs=[row(d), _row_specs(bx, s, tm, modkv), _row_specs(bx, s, tm, mod), _full((1, d)), _full((1, d)),
                  _full(w_kv.shape), _full(w_qg.shape)],
        out_specs=[row(n_rows_cols), row(n_win_cols), row(nq), row(ng)],
        out_shape=[jax.ShapeDtypeStruct((bx, s, n_rows_cols), F32), jax.ShapeDtypeStruct((bx, s, n_win_cols), F32),
                   jax.ShapeDtypeStruct((bx, s, nq), F32), jax.ShapeDtypeStruct((bx, s, ng), F32)],
        compiler_params=_cparams("arbitrary", "arbitrary"), name="kv_q_proj")(x, modkv, mod, gkv, g0, w_kv, w_qg)


def _outproj_kernel(d, n_branch, *refs):
    x_ref, mod_ref, g1_ref, w_ref = refs[:4]
    xo_ref = refs[-1]
    if n_branch == 1:
        o = refs[4][0]
    else:
        o = sum(refs[4 + 2 * i][0] * refs[5 + 2 * i][0] for i in range(n_branch))
    y = _dot(o.astype(BF16), w_ref[...])
    xo_ref[0] = x_ref[0] + mod_ref[0][:, 2 * d:] * _rms(y, g1_ref[...])


def _outproj_call(x, mod, g1, w, branches, tm):
    bx, s, d = x.shape
    row = pl.BlockSpec((1, tm, d), lambda b, i: (b, i, 0))
    n_branch = 1 if len(branches) == 1 else len(branches) // 2
    return pl.pallas_call(
        functools.partial(_outproj_kernel, d, n_branch), grid=(bx, s // tm),
        in_specs=[row, _row_specs(bx, s, tm, mod), _full((1, d)), _full(w.shape)] + [row] * len(branches),
        out_specs=row, out_shape=jax.ShapeDtypeStruct((bx, s, d), F32),
        compiler_params=_cparams("arbitrary", "arbitrary"), name="attn_out_proj")(x, mod, g1, w, *branches)


def _gelu_tanh(x):
    return x * (0.5 * (1.0 + jnp.tanh(math.sqrt(2.0 / math.pi) * (x + 0.044715 * (x * x * x)))))


def _compress_kernel(tmb, x_ref, pe_ref, w1_ref, w2_ref, o_ref):
    acc = jnp.zeros((tmb, LANES), F32)
    for r in range(CMP_BLOCK):
        xr = x_ref[pl.ds(r, tmb, stride=CMP_BLOCK), :] + pe_ref[0, r]
        acc = acc + _dot(xr.astype(BF16), w1_ref[0, r])
    o_ref[...] = _dot(_gelu_tanh(acc).astype(BF16), w2_ref[0])


def _compress_call(rows2d, pe2, w1bd, w2bd):
    m = rows2d.shape[0] // CMP_BLOCK
    tmb = max(t for t in range(8, min(256, m) + 1, 8) if m % t == 0)
    return pl.pallas_call(
        functools.partial(_compress_kernel, tmb), grid=(m // tmb, 4),
        in_specs=[pl.BlockSpec((tmb * CMP_BLOCK, LANES), lambda i, j: (i, j)),
                  pl.BlockSpec((1, CMP_BLOCK, 1, LANES), lambda i, j: (j // 2, 0, 0, 0)),
                  pl.BlockSpec((1, CMP_BLOCK, LANES, LANES), lambda i, j: (j // 2, 0, 0, 0)),
                  pl.BlockSpec((1, LANES, LANES), lambda i, j: (j // 2, 0, 0))],
        out_specs=pl.BlockSpec((tmb, LANES), lambda i, j: (i, j)),
        out_shape=jax.ShapeDtypeStruct((m, 4 * LANES), F32),
        compiler_params=_cparams("arbitrary", "arbitrary"), name="compress")(rows2d, pe2, w1bd, w2bd)


def _compress_weights(cmp_pe, cmp_w1, cmp_w2):
    z = jnp.zeros((2, CMP_BLOCK, HEAD_DIM, HEAD_DIM), F32)
    w1 = cmp_w1.reshape(2, CMP_BLOCK, HEAD_DIM, HEAD_DIM)
    w1bd = jnp.concatenate([jnp.concatenate([w1, z], -1), jnp.concatenate([z, w1], -1)], -2).astype(BF16)
    z2 = jnp.zeros((2, HEAD_DIM, HEAD_DIM), F32)
    w2bd = jnp.concatenate([jnp.concatenate([cmp_w2, z2], -1), jnp.concatenate([z2, cmp_w2], -1)], -2).astype(BF16)
    pe2 = jnp.concatenate([cmp_pe, cmp_pe], -1)[:, :, None, :]
    return pe2, w1bd, w2bd


def _bias_of_distance(dist, tab):
    val = jnp.zeros(dist.shape, F32) + tab(0)
    for k in range(1, N_BUCKETS):
        val = jnp.where(dist >= BUCKET_THR[k], tab(k), val)
    return val


def _toeplitz_kernel(window, tab_ref, o_ref):
    g = pl.program_id(0)
    dt = pl.program_id(1)
    i = lax.broadcasted_iota(jnp.int32, (TQ, TK), 0)
    j = lax.broadcasted_iota(jnp.int32, (TQ, TK), 1)
    dist = dt * TQ + i - j
    for hh in range(HPG):
        head = g * HPG + hh
        val = _bias_of_distance(dist, lambda k, head=head: tab_ref[k, head])
        val = jnp.where(dist < 0, NEG_MASK, val)
        if window is not None:
            val = jnp.where(dist >= window, NEG_MASK, val)
        o_ref[0, 0, hh * TQ:(hh + 1) * TQ, :] = val


def _toeplitz_call(rel_bias, n_chunks, window):
    return pl.pallas_call(
        functools.partial(_toeplitz_kernel, window), grid=(N_KV, n_chunks),
        in_specs=[pl.BlockSpec(memory_space=pltpu.SMEM)],
        out_specs=pl.BlockSpec((1, 1, HPG * TQ, TK), lambda g, t: (g, t, 0, 0)),
        out_shape=jax.ShapeDtypeStruct((N_KV, n_chunks, HPG * TQ, TK), F32),
        compiler_params=_cparams("arbitrary", "arbitrary"), name="bias_toeplitz")(rel_bias)


def _cmp_bias_kernel(tab_ref, o_ref):
    g = pl.program_id(0)
    qi = pl.program_id(1)
    row = lax.broadcasted_iota(jnp.int32, (LANES, TQ), 0)
    t_q = qi * TQ + lax.broadcasted_iota(jnp.int32, (LANES, TQ), 1)
    cblk = 2 * (row % HEAD_DIM) + row // HEAD_DIM
    dist = t_q - (cblk * CMP_BLOCK + (CMP_BLOCK - 1))
    for hh in range(HPG):
        head = g * HPG + hh
        val = _bias_of_distance(dist, lambda k, head=head: tab_ref[k, head])
        o_ref[0, 0, :, hh * TQ:(hh + 1) * TQ] = jnp.where(dist < 0, NEG_MASK, val)


def _cmp_bias_call(rel_bias, n_qt):
    return pl.pallas_call(
        _cmp_bias_kernel, grid=(N_KV, n_qt),
        in_specs=[pl.BlockSpec(memory_space=pltpu.SMEM)],
        out_specs=pl.BlockSpec((1, 1, LANES, HPG * TQ), lambda g, t: (g, t, 0, 0)),
        out_shape=jax.ShapeDtypeStruct((N_KV, n_qt, LANES, HPG * TQ), F32),
        compiler_params=_cparams("arbitrary", "arbitrary"), name="bias_cmp")(rel_bias)


def _bias_rows_kernel(dist_ref, tab_ref, o_ref):
    dist = dist_ref[...]
    o_ref[...] = _bias_of_distance(dist, lambda k: tab_ref[k:k + 1, :])


def _bias_rows_call(dist, tab_lanes):
    r = dist.shape[0]
    return pl.pallas_call(
        _bias_rows_kernel, grid=(1,),
        in_specs=[pl.BlockSpec((r, LANES), lambda i: (0, 0)), pl.BlockSpec((N_BUCKETS, LANES), lambda i: (0, 0))],
        out_specs=pl.BlockSpec((r, LANES), lambda i: (0, 0)),
        out_shape=jax.ShapeDtypeStruct((r, LANES), F32),
        compiler_params=_cparams("arbitrary"), name="bias_rows")(dist, tab_lanes)


def _kv_prep_kernel(ksel_ref, vsel_ref, kwin_ref, vwin_ref, kaug_ref, v1_ref, kw_ref, vw1_ref):
    g_odd = (pl.program_id(1) % 2) == 1
    ch = ksel_ref.shape[1]
    lane = lax.broadcasted_iota(jnp.int32, (ch, LANES), 1)
    row = pl.program_id(2) * ch + lax.broadcasted_iota(jnp.int32, (ch, LANES), 0)
    low = lane < HEAD_DIM
    ones = jnp.where(lane == HEAD_DIM, 1.0, 0.0)
    onehot = jnp.where(lane - HEAD_DIM == row // SEL_BLOCK, 1.0, 0.0)

    def pick(ref):
        x = ref[0]
        return jnp.where(g_odd, pltpu.roll(x, HEAD_DIM, 1), x)

    kaug_ref[0, 0] = jnp.where(low, pick(ksel_ref), onehot).astype(BF16)
    v1_ref[0, 0] = jnp.where(low, pick(vsel_ref), ones).astype(BF16)
    kw_ref[0, 0] = jnp.where(low, pick(kwin_ref), 0.0).astype(BF16)
    vw1_ref[0, 0] = jnp.where(low, pick(vwin_ref), ones).astype(BF16)


def _kv_prep_call(rows, win):
    b, s, _ = rows.shape
    ch = min(1024, s)
    kv_spec = lambda col0: pl.BlockSpec((1, ch, LANES), lambda bb, g, c, _c=col0: (bb, c, _c + g // 2))
    o_spec = pl.BlockSpec((1, 1, ch, LANES), lambda bb, g, c: (bb, g, c, 0))
    o_shape = jax.ShapeDtypeStruct((b, N_KV, s, LANES), BF16)
    return pl.pallas_call(
        _kv_prep_kernel, grid=(b, N_KV, s // ch),
        in_specs=[kv_spec(4), kv_spec(6), kv_spec(0), kv_spec(2)],
        out_specs=[o_spec] * 4, out_shape=[o_shape] * 4,
        compiler_params=_cparams("arbitrary", "arbitrary", "arbitrary"), name="kv_prep")(rows, rows, win, win)


def _nsa_seq_kernel(n_sb, q_ref, gate_ref, kaug_ref, v1_ref, kw_ref, vw1_ref, kcb_ref, vcb_ref, tcmp_ref,
                    tsel_ref, twin_ref, o_ref, qaug, qw_sc, s_sc, mrun, mb, acc_sc, score_sc):
    qi = pl.program_id(2)
    rows = HPG * TQ
    lane = lax.broadcasted_iota(jnp.int32, (TQ, LANES), 1)
    low = lane < HEAD_DIM

    qh = []
    for hh in range(HPG):
        qv = q_ref[0, :, (hh // 2) * LANES:(hh // 2 + 1) * LANES]
        if hh % 2 == 1:
            qv = pltpu.roll(qv, HEAD_DIM, 1)
        qh.append(jnp.where(low, qv, 0.0))
    qw = jnp.concatenate(qh, axis=0).astype(BF16)
    qw_sc[...] = qw

    s_c = _dot_nt(kcb_ref[0, 0].astype(BF16), qw) + tcmp_ref[0, 0]
    ok_c = s_c > 0.5 * NEG_MASK
    mx = jnp.max(s_c, axis=0, keepdims=True)
    p = jnp.where(ok_c, jnp.exp(s_c - mx), 0.0)
    den = jnp.sum(p, axis=0, keepdims=True)
    pn_t = p / jnp.where(den > 0, den, 1.0)
    o_c = _dot(pn_t.T.astype(BF16), vcb_ref[0, 0].astype(BF16))
    imp = pn_t[:, 0:TQ]
    for hh in range(1, HPG):
        imp = imp + pn_t[:, hh * TQ:(hh + 1) * TQ]
    n_blk = LANES // 2
    imp = imp[:n_blk] + imp[n_blk:]

    blk = lax.broadcasted_iota(jnp.int32, (n_blk, TQ), 0)
    cur = (qi * TQ + lax.broadcasted_iota(jnp.int32, (n_blk, TQ), 1)) // SEL_BLOCK
    valid = blk <= cur
    forced = (blk == 0) | (blk == cur) | (blk == cur - 1)
    score_sc[...] = jnp.where(valid & forced, BIG_SCORE, jnp.where(valid, imp, -BIG_SCORE))
    sub = 8
    groups = [score_sc[r0:r0 + sub, :] for r0 in range(0, n_blk, sub)]
    rowg = lax.broadcasted_iota(jnp.int32, (sub, TQ), 0)
    cnts = [jnp.zeros((sub, TQ), jnp.int32) for _ in groups]
    for bp in range(n_blk):
        r = score_sc[bp:bp + 1, :]
        for gi, s_g in enumerate(groups):
            if gi * sub > bp:
                ahead = r >= s_g
            elif (gi + 1) * sub - 1 < bp:
                ahead = r > s_g
            else:
                ahead = (r > s_g) | ((r == s_g) & (rowg + gi * sub > bp))
            cnts[gi] = cnts[gi] + jnp.where(ahead, 1, 0)
    cnt = jnp.concatenate(cnts, axis=0)
    selmask_t = jnp.where((cnt < min(TOP_N, n_sb)) & valid, 0.0, NEG_SEL)
    selmask = jnp.concatenate([jnp.zeros((n_blk, TQ), F32), selmask_t], axis=0).T
    for hh in range(HPG):
        qaug[hh * TQ:(hh + 1) * TQ, :] = jnp.where(low, qh[hh], selmask).astype(BF16)

    def branch(q_sc, k_ref, v_ref, t_ref, n_tiles, n_tbl):
        mrun[...] = jnp.full(mrun.shape, NEG_MASK, F32)

        def scores(i, _):
            r0 = pl.multiple_of((qi - i) * TK, TK)
            s = _dot_nt(q_sc[...], k_ref[0, 0, pl.ds(r0, TK), :]) + t_ref[0, jnp.minimum(i, n_tbl)]
            s_sc[i] = s
            m = mrun[...]
            for c in range(TK // LANES):
                m = jnp.maximum(m, s[:, c * LANES:(c + 1) * LANES])
            mrun[...] = m
            return 0
        lax.fori_loop(0, n_tiles, scores, 0)
        mb[...] = jnp.broadcast_to(jnp.max(mrun[...], axis=1, keepdims=True), (rows, LANES))
        acc_sc[...] = jnp.zeros(acc_sc.shape, F32)

        def weigh(i, _):
            r0 = pl.multiple_of((qi - i) * TK, TK)
            mbv = mb[...]
            p = jnp.exp(s_sc[i] - jnp.concatenate([mbv] * (TK // LANES), axis=1))
            acc_sc[...] += _dot(p.astype(BF16), v_ref[0, 0, pl.ds(r0, TK), :])
            return 0
        lax.fori_loop(0, n_tiles, weigh, 0)
        acc = acc_sc[...]
        return acc / acc[:, HEAD_DIM:HEAD_DIM + 1]

    o_s = branch(qaug, kaug_ref, v1_ref, tsel_ref, qi + 1, N_TBL)
    o_w = branch(qw_sc, kw_ref, vw1_ref, twin_ref, jnp.minimum(qi, N_WIN_TILES - 1) + 1, N_WIN_TILES - 1)

    outs = []
    for hh in range(HPG):
        gts = [gate_ref[0, :, hh * 3 + br:hh * 3 + br + 1] for br in range(3)]
        rs = slice(hh * TQ, (hh + 1) * TQ)
        outs.append(gts[0] * o_c[rs] + gts[1] * o_s[rs] + gts[2] * o_w[rs])
    for c in range(HPG // 2):
        o_ref[0, :, c * LANES:(c + 1) * LANES] = jnp.where(low, outs[2 * c], pltpu.roll(outs[2 * c + 1], HEAD_DIM, 1))


def _nsa_seq_call(q, gates, kaug, v1, kw, vw1, kcb, vcb, tcmp, tsel, twin):
    b, s, _ = q.shape
    n_qt = s // TQ
    rows = HPG * TQ
    kv_spec = pl.BlockSpec((1, 1, s, LANES), lambda bb, g, i: (bb, g, 0, 0))
    cb_spec = pl.BlockSpec((1, 1, kcb.shape[2], LANES), lambda bb, g, i: (bb, g, 0, 0))
    tbl_spec = lambda t: pl.BlockSpec((1,) + t.shape[1:], lambda bb, g, i: (g, 0, 0, 0))
    return pl.pallas_call(
        functools.partial(_nsa_seq_kernel, s // SEL_BLOCK), grid=(b, N_KV, n_qt),
        in_specs=[pl.BlockSpec((1, TQ, HPG * HEAD_DIM), lambda bb, g, i: (bb, i, g)),
                  pl.BlockSpec((1, TQ, LANES), lambda bb, g, i: (bb, i, g)),
                  kv_spec, kv_spec, kv_spec, kv_spec, cb_spec, cb_spec,
                  pl.BlockSpec((1, 1, LANES, rows), lambda bb, g, i: (g, i, 0, 0)), tbl_spec(tsel), tbl_spec(twin)],
        out_specs=pl.BlockSpec((1, TQ, HPG * HEAD_DIM), lambda bb, g, i: (bb, i, g)),
        out_shape=jax.ShapeDtypeStruct((b, s, N_HEADS * HEAD_DIM), F32),
        scratch_shapes=[pltpu.VMEM((rows, LANES), BF16), pltpu.VMEM((rows, LANES), BF16),
                        pltpu.VMEM((s // TK, rows, TK), F32), pltpu.VMEM((rows, LANES), F32),
                        pltpu.VMEM((rows, LANES), F32), pltpu.VMEM((rows, LANES), F32),
                        pltpu.VMEM((LANES // 2, TQ), F32)],
        compiler_params=_cparams("arbitrary", "arbitrary", "arbitrary"), name="nsa_seq")(
            q, gates, kaug, v1, kw, vw1, kcb, vcb, tcmp, tsel, twin)


def _softmax_rows(s):
    p = jnp.exp(s - jnp.max(s, axis=0, keepdims=True))
    return p / jnp.sum(p, axis=0, keepdims=True)


def _nsa_step_kernel(n_pages, pt_ref, *refs):
    del pt_ref
    q_ref = refs[0]
    cb_refs = refs[1:1 + n_pages]
    pg_refs = refs[1 + n_pages:1 + 2 * n_pages]
    kvnew_ref, winnew_ref, cwin_ref, bsel_ref, bnew_ref, bwin_ref, bcmp_ref = refs[1 + 2 * n_pages:8 + 2 * n_pages]
    oc_ref, os_ref, ow_ref, nwin_ref = refs[8 + 2 * n_pages:12 + 2 * n_pages]
    imp_sc, s_sc = refs[12 + 2 * n_pages:]
    gw = N_KV * HEAD_DIM

    q16 = q_ref[0].astype(BF16)
    qp = jnp.concatenate([q16, jnp.zeros((LANES - N_HEADS, gw), BF16)], axis=0)

    cb = jnp.concatenate([r[0] for r in cb_refs], axis=0)
    n_cb = cb.shape[0]
    s_c = _dot_nt(cb[:, :gw].astype(BF16), qp) + bcmp_ref[...]
    pn_c = _softmax_rows(s_c)
    assert n_cb <= LANES and n_cb % 8 == 0
    pnt = jnp.concatenate([pn_c, jnp.zeros((LANES - n_cb, LANES), F32)], axis=0).T
    vcb = jnp.concatenate([cb[:, gw:], jnp.zeros((LANES - n_cb, gw), F32)], axis=0).astype(BF16)
    oc_ref[0] = _dot(pnt.astype(BF16), vcb)[:N_HEADS]

    lane_c = lax.broadcasted_iota(jnp.int32, pn_c.shape, 1)
    y = pn_c + pltpu.roll(pn_c, 1, 1) + pltpu.roll(pn_c, 2, 1) + pltpu.roll(pn_c, 3, 1)
    ym = jnp.where(lane_c % HPG == HPG - 1, y, 0.0)
    imp_sc[...] = ym + pltpu.roll(ym, LANES - 1, 1) + pltpu.roll(ym, LANES - 2, 1) + pltpu.roll(ym, LANES - 3, 1)
    n_past_blk = n_cb // 2
    imp = imp_sc[pl.ds(0, n_past_blk, stride=2), :] + imp_sc[pl.ds(1, n_past_blk, stride=2), :]

    n_sb = n_past_blk + 1
    n_rows = -(-n_sb // 8) * 8
    imp_p = jnp.concatenate([imp, jnp.zeros((n_rows - n_past_blk, LANES), F32)], axis=0)
    rowb = lax.broadcasted_iota(jnp.int32, (n_rows, LANES), 0)
    cur = n_past_blk
    valid = rowb <= cur
    forced = (rowb == 0) | (rowb == cur) | (rowb == cur - 1)
    score = jnp.where(valid & forced, BIG_SCORE, jnp.where(valid, imp_p, -BIG_SCORE))
    cnt = jnp.zeros((n_rows, LANES), jnp.int32)
    for r in range(n_sb):
        sr = score[r:r + 1, :]
        cnt = cnt + jnp.where((sr > score) | ((sr == score) & (rowb > r)), 1, 0)
    selmask = jnp.where((cnt < min(TOP_N, n_sb)) & valid, 0.0, NEG_SEL)

    row_p = lax.broadcasted_iota(jnp.int32, (PAGE_SIZE, LANES), 0)
    blocks_per_page = PAGE_SIZE // SEL_BLOCK
    for p in range(n_pages):
        k = pg_refs[p][0, :, :gw].astype(BF16)
        s = _dot_nt(k, qp) + bsel_ref[p * PAGE_SIZE:(p + 1) * PAGE_SIZE, :]
        msk = selmask[blocks_per_page * p:blocks_per_page * p + 1, :]
        for j in range(1, blocks_per_page):
            msk = jnp.where(row_p >= j * SEL_BLOCK, selmask[blocks_per_page * p + j:blocks_per_page * p + j + 1, :], msk)
        s_sc[p * PAGE_SIZE:(p + 1) * PAGE_SIZE, :] = s + msk
    knew = jnp.broadcast_to(kvnew_ref[0][:, 2 * gw:3 * gw], (8, gw)).astype(BF16)
    row8 = lax.broadcasted_iota(jnp.int32, (8, LANES), 0)
    s_new = _dot_nt(knew, qp) + bnew_ref[...] + selmask[cur:cur + 1, :]
    n_past = n_pages * PAGE_SIZE
    s_sc[n_past:n_past + 8, :] = jnp.where(row8 == 0, s_new, NEG_MASK)
    s_all = s_sc[...]
    mx = jnp.max(s_all, axis=0, keepdims=True)
    den = jnp.sum(jnp.exp(s_all - mx), axis=0, keepdims=True)
    acc = jnp.zeros((LANES, gw), F32)
    for p in range(n_pages):
        pn = jnp.exp(s_sc[p * PAGE_SIZE:(p + 1) * PAGE_SIZE, :] - mx) / den
        acc = acc + _dot(pn.T.astype(BF16), pg_refs[p][0, :, gw:].astype(BF16))
    pn_new = jnp.exp(s_sc[n_past:n_past + 8, :] - mx) / den
    pn_new = jnp.concatenate([pn_new, jnp.zeros((LANES - 8, LANES), F32)], axis=0)
    vnew = jnp.broadcast_to(kvnew_ref[0][:, 3 * gw:4 * gw], (LANES, gw)).astype(BF16)
    acc = acc + _dot(pn_new.T.astype(BF16), vnew)
    os_ref[0] = acc[:N_HEADS]

    w_len = cwin_ref.shape[1]
    roww = lax.broadcasted_iota(jnp.int32, (w_len, 2 * gw), 0)
    nw = jnp.where(roww == w_len - 1, winnew_ref[0], pltpu.roll(cwin_ref[0], w_len - 1, 0))
    nwin_ref[0] = nw
    pn_w = _softmax_rows(_dot_nt(nw[:, :gw].astype(BF16), qp) + bwin_ref[...])
    acc = jnp.zeros((LANES, gw), F32)
    for c in range(w_len // LANES):
        rs = slice(c * LANES, (c + 1) * LANES)
        acc = acc + _dot(pn_w[rs].T.astype(BF16), nw[rs, gw:].astype(BF16))
    ow_ref[0] = acc[:N_HEADS]


def _nsa_step_call(page_table, qrows, cb_pool, cache_pages, kvnew, winnew, cache_win, bsel, bnew, bwin, bcmp):
    nb, n_pages = page_table.shape
    gw = N_KV * HEAD_DIM
    w_len = cache_win.shape[1]
    cb_specs = [pl.BlockSpec((1, PAGE_SIZE // CMP_BLOCK, 2 * gw), lambda b, pt, _p=p: (pt[b, _p], 0, 0))
                for p in range(n_pages)]
    pg_specs = [pl.BlockSpec((1, PAGE_SIZE, 2 * gw), lambda b, pt, _p=p: (pt[b, _p], 0, 1)) for p in range(n_pages)]
    const = lambda a: pl.BlockSpec(a.shape, lambda b, pt: (0, 0))
    o_spec = pl.BlockSpec((1, N_HEADS, gw), lambda b, pt: (b, 0, 0))
    n_cb = n_pages * (PAGE_SIZE // CMP_BLOCK)
    grid_spec = pltpu.PrefetchScalarGridSpec(
        num_scalar_prefetch=1, grid=(nb,),
        in_specs=[pl.BlockSpec((1, N_HEADS, gw), lambda b, pt: (b, 0, 0))] + cb_specs + pg_specs + [
            pl.BlockSpec((1, 1, 4 * gw), lambda b, pt: (b, 0, 0)),
            pl.BlockSpec((1, 1, 2 * gw), lambda b, pt: (b, 0, 0)),
            pl.BlockSpec((1, w_len, 2 * gw), lambda b, pt: (b, 0, 0)),
            const(bsel), const(bnew), const(bwin), const(bcmp)],
        out_specs=[o_spec, o_spec, o_spec, pl.BlockSpec((1, w_len, 2 * gw), lambda b, pt: (b, 0, 0))],
        scratch_shapes=[pltpu.VMEM((n_cb, LANES), F32), pltpu.VMEM((n_pages * PAGE_SIZE + 8, LANES), F32)])
    o_shape = jax.ShapeDtypeStruct((nb, N_HEADS, gw), F32)
    return pl.pallas_call(
        functools.partial(_nsa_step_kernel, n_pages), grid_spec=grid_spec,
        out_shape=[o_shape, o_shape, o_shape, jax.ShapeDtypeStruct((nb, w_len, 2 * gw), F32)],
        compiler_params=_cparams("arbitrary"), name="nsa_step")(
            page_table, qrows, *([cb_pool] * n_pages), *([cache_pages] * n_pages), kvnew, winnew, cache_win,
            bsel, bnew, bwin, bcmp)


def _head_diag(o):
    b = o.shape[0]
    o5 = o.reshape(b, N_KV, HPG, N_KV, HEAD_DIM)
    return jnp.stack([o5[:, g, :, g, :] for g in range(N_KV)], axis=1).reshape(b, N_HEADS * HEAD_DIM)


def kernel(x_prompt, x_sample, c_prompt, c_sample, cache_kv, cache_win, state_conv_a, state_ffn_conv, page_table, mod_w, mod_b, norm_g, a_w_in, a_conv_w, a_conv_b, a_w_out, kv_mod_w, kv_mod_b, kv_norm_g, w_kv, cmp_pe, cmp_w1, cmp_w2, b_w_qg, b_w_out, rel_bias, ffn_w_up, ffn_conv_w, ffn_conv_b, ffn_w_down):
    bp, s, d = x_prompt.shape
    bs = x_sample.shape[0]
    depth = mod_w.shape[0]
    n_a = a_w_in.shape[0]
    assert depth == 2 and n_a == 1 and x_sample.shape[1] == 1
    dff = ffn_w_down.shape[1]
    n_pool = cache_kv.shape[0]
    n_pages = page_table.shape[1]
    past_len = n_pages * PAGE_SIZE
    gw = N_KV * HEAD_DIM
    nq = N_HEADS * HEAD_DIM

    n_c = bp + bs
    n_cp = -(-n_c // 8) * 8
    c_all = jnp.pad(jnp.concatenate([c_prompt, c_sample], 0), ((0, n_cp - n_c), (0, 0)))
    mods = _mod_call(c_all, mod_w.reshape(depth * 2, d, 3 * d), mod_b.reshape(depth * 2, 1, 3 * d))
    modkv = _mod_call(c_all, kv_mod_w[None], kv_mod_b[None, None])[0]
    mod_p = lambda i: mods[i, :bp][:, None, :]
    mod_s = lambda i: mods[i, bp:n_c][None]

    w_in = a_w_in[0].astype(BF16)
    w_out_a = a_w_out[0].astype(BF16)
    w_up = ffn_w_up.astype(BF16)
    w_dn = ffn_w_down.astype(BF16)
    w_kv_b = w_kv.astype(BF16)
    ng_pad = -(-(b_w_qg.shape[2] - nq) // LANES) * LANES
    w_qg = jnp.pad(b_w_qg[0], ((0, 0), (0, nq + ng_pad - b_w_qg.shape[2]))).astype(BF16)
    w_out_b = b_w_out[0].astype(BF16)
    g = lambda l, i: norm_g[l, i][None]
    pe2, w1bd, w2bd = _compress_weights(cmp_pe, cmp_w1, cmp_w2)
    n_chunks = 2

    tm = min(256, s)
    x1, st_a = _mixer_call(x_prompt, mod_p(0), g(0, 0), g(0, 1), w_in, a_conv_w[0], a_conv_b[0][None], w_out_a, None, tm)
    x2, st_f0 = _ffn_call(x1, mod_p(1), g(0, 2), g(0, 3), w_up[0], ffn_conv_w[0], ffn_conv_b[0][None], w_dn[0], None, tm, n_chunks)
    rows, win, q, gates = _kvq_call(x2, modkv[:bp][:, None, :], mod_p(2), kv_norm_g[None], g(1, 0), w_kv_b, w_qg, tm)
    cb = _compress_call(rows.reshape(bp * s, 4 * gw), pe2, w1bd, w2bd)
    n_cbk = s // CMP_BLOCK
    assert n_cbk <= LANES
    cb = jnp.pad(cb.reshape(bp, n_cbk, 2, N_KV, HEAD_DIM), ((0, 0), (0, LANES - n_cbk), (0, 0), (0, 0), (0, 0)))
    cb = cb.reshape(bp, LANES // 2, 2, 2, N_KV, HEAD_DIM)
    cb = cb.transpose(3, 0, 4, 2, 1, 5).reshape(2, bp, N_KV, LANES, HEAD_DIM)
    cb = jnp.pad(cb, ((0, 0),) * 4 + ((0, LANES - HEAD_DIM),))
    gates_g = jnp.pad(gates[:, :, :N_HEADS * 3].reshape(bp, s, N_KV, HPG * 3),
                      ((0, 0), (0, 0), (0, 0), (0, LANES - HPG * 3))).reshape(bp, s, N_KV * LANES)
    tsel = _toeplitz_call(rel_bias, N_TBL + 1, None)
    twin = _toeplitz_call(rel_bias, N_WIN_TILES, WINDOW)
    tcmp = _cmp_bias_call(rel_bias, s // TQ)
    kaug, v1, kw, vw1 = _kv_prep_call(rows, win)
    o_att = _nsa_seq_call(q, gates_g, kaug, v1, kw, vw1, cb[0], cb[1], tcmp, tsel, twin)
    x3 = _outproj_call(x2, mod_p(2), g(1, 1), w_out_b, [o_att], tm)
    y_prompt, st_f1 = _ffn_call(x3, mod_p(3), g(1, 2), g(1, 3), w_up[1], ffn_conv_w[1], ffn_conv_b[1][None], w_dn[1], None, tm, n_chunks)
    kv_p = rows.reshape(bp, s, 4, N_KV, HEAD_DIM)
    keep = min(WINDOW, s)
    win_p = win[:, s - keep:].reshape(bp, keep, 2, N_KV, HEAD_DIM)
    conv_a_p = st_a[None, :, 6:8]
    ffn_p = jnp.stack([st_f0[:, 6:8], st_f1[:, 6:8]])

    xs = x_sample.reshape(1, bs, d)
    prev_a = (state_conv_a[0, :, 0][None], state_conv_a[0, :, 1][None])
    xs1, v_a = _mixer_call(xs, mod_s(0), g(0, 0), g(0, 1), w_in, a_conv_w[0], a_conv_b[0][None], w_out_a, prev_a, bs)
    prev_f = lambda l: (state_ffn_conv[l, :, 0][None], state_ffn_conv[l, :, 1][None])
    xs2, up0 = _ffn_call(xs1, mod_s(1), g(0, 2), g(0, 3), w_up[0], ffn_conv_w[0], ffn_conv_b[0][None], w_dn[0], prev_f(0), bs, n_chunks)
    rows_s, win_s, q_s, gates_s = _kvq_call(xs2, modkv[bp:n_c][None], mod_s(2), kv_norm_g[None], g(1, 0), w_kv_b, w_qg, bs)
    cb_pool = _compress_call(cache_kv.reshape(n_pool * PAGE_SIZE, 4 * gw), pe2, w1bd, w2bd)
    cb_pool = cb_pool.reshape(n_pool, PAGE_SIZE // CMP_BLOCK, 2 * gw)
    w_len = cache_win.shape[1]
    n_cb_s = past_len // CMP_BLOCK
    dist = np.concatenate([past_len - np.arange(past_len), np.zeros(8, np.int64),
                           w_len - 1 - np.arange(w_len),
                           past_len - (np.arange(n_cb_s) * CMP_BLOCK + CMP_BLOCK - 1)]).astype(np.int32)
    assert dist.min() >= 0
    tab_lanes = jnp.pad(rel_bias, ((0, 0), (0, LANES - N_HEADS)))
    bias_rows = _bias_rows_call(jnp.asarray(np.repeat(dist[:, None], LANES, 1)), tab_lanes)
    bsel, bnew = bias_rows[:past_len], bias_rows[past_len:past_len + 8]
    bwin = bias_rows[past_len + 8:past_len + 8 + w_len]
    bcmp = bias_rows[past_len + 8 + w_len:]
    head_group = (np.arange(N_HEADS)[:, None] // HPG == np.arange(N_KV)[None, :])[None, :, :, None]
    qrows = jnp.where(head_group, q_s.reshape(bs, N_HEADS, 1, HEAD_DIM), 0.0).reshape(bs, N_HEADS, gw)
    oc, os_, ow, nwin = _nsa_step_call(
        page_table, qrows, cb_pool, cache_kv.reshape(n_pool, PAGE_SIZE, 4 * gw), rows_s.reshape(bs, 1, 4 * gw),
        win_s.reshape(bs, 1, 2 * gw), cache_win.reshape(bs, w_len, 2 * gw), bsel, bnew, bwin, bcmp)
    gts = gates_s[0, :, :N_HEADS * 3].reshape(bs, N_HEADS, 3)
    branches = []
    for br, o in enumerate((oc, os_, ow)):
        branches += [jnp.repeat(gts[:, :, br], HEAD_DIM, axis=1)[None], _head_diag(o)[None]]
    xs3 = _outproj_call(xs2, mod_s(2), g(1, 1), w_out_b, branches, bs)
    ys, up1 = _ffn_call(xs3, mod_s(3), g(1, 2), g(1, 3), w_up[1], ffn_conv_w[1], ffn_conv_b[1][None], w_dn[1], prev_f(1), bs, n_chunks)
    y_sample = ys.reshape(bs, 1, d)
    kv_s = rows_s.reshape(bs, 1, 4, N_KV, HEAD_DIM)
    win_state_s = nwin.reshape(bs, w_len, 2, N_KV, HEAD_DIM)
    conv_a_s = jnp.stack([state_conv_a[0, :, 1], v_a[0]], axis=1)[None]
    ffn_s = jnp.stack([jnp.stack([state_ffn_conv[l, :, 1], u[0]], axis=1) for l, u in ((0, up0), (1, up1))])
    return (y_prompt, y_sample, kv_p, kv_s, win_p, win_state_s, conv_a_p, conv_a_s, ffn_p, ffn_s)
```

```python
import functools
import math

import numpy as np
import jax
import jax.numpy as jnp
from jax import lax
from jax.experimental import pallas as pl
from jax.experimental.pallas import tpu as pltpu

F32 = jnp.float32
BF16 = jnp.bfloat16

N_HEADS = 16
HEAD_DIM = 64
N_KV = 4
HPG = N_HEADS // N_KV
CMP_BLOCK = 32
SEL_BLOCK = 64
TOP_N = 16
WINDOW = 512
N_BUCKETS = 32
MAX_EXACT = N_BUCKETS // 2
MAX_DISTANCE = 1024
PAGE_SIZE = 128
EPS = 1e-6

LANES = 128
TQ = 256
TK = 256
NEG_MASK = -1e30
NEG_SEL = -1e9
BIG_SCORE = 1e30
VMEM_LIMIT_BYTES = 56 * 1024 * 1024


def _bucket_thresholds():
    d = np.arange(0, 4 * MAX_DISTANCE)
    nf = np.maximum(d, 1).astype(np.float32)
    large = MAX_EXACT + (np.log(nf / MAX_EXACT) / math.log(MAX_DISTANCE / MAX_EXACT)
                         * (N_BUCKETS - MAX_EXACT)).astype(np.int32)
    bucket = np.where(d < MAX_EXACT, d, np.minimum(large, N_BUCKETS - 1))
    assert np.all(np.diff(bucket) >= 0)
    return [int(np.argmax(bucket >= k)) for k in range(N_BUCKETS)]


BUCKET_THR = _bucket_thresholds()
assert TQ == TK
N_TBL = -(-(BUCKET_THR[-1] + TK - 1) // TQ)
N_WIN_TILES = WINDOW // TK + 1


def _cparams(*sem):
    return pltpu.CompilerParams(dimension_semantics=sem, vmem_limit_bytes=VMEM_LIMIT_BYTES)


def _dot(a, b):
    return jnp.dot(a, b, preferred_element_type=F32)


def _dot_nt(a, b):
    return lax.dot_general(a, b, (((1,), (1,)), ((), ())), preferred_element_type=F32)


def _rms(x, g):
    return (x * lax.rsqrt(jnp.mean(x * x, axis=-1, keepdims=True) + EPS)) * g


def _sigmoid(x):
    return 1.0 / (1.0 + jnp.exp(-x))


def _shift_rows(v, carry):
    row = lax.broadcasted_iota(jnp.int32, v.shape, 0)
    r1 = jnp.where(row == 0, carry[7:8], pltpu.roll(v, 1, 0))
    r2 = jnp.where(row == 0, carry[6:7], jnp.where(row == 1, carry[7:8], pltpu.roll(v, 2, 0)))
    return r1, r2


def _conv3(v, r1, r2, cw, cb):
    return (cw[0:1] * r2 + cw[1:2] * r1) + cw[2:3] * v + cb


def _mod_kernel(c_ref, w_ref, b_ref, o_ref):
    o_ref[0] = _dot(c_ref[...].astype(BF16), w_ref[0].astype(BF16)) + b_ref[0]


def _mod_call(c_all, w, b):
    n, d, nn = w.shape
    r = c_all.shape[0]
    tn = 512
    return pl.pallas_call(
        _mod_kernel, grid=(n, nn // tn),
        in_specs=[pl.BlockSpec((r, d), lambda i, j: (0, 0)),
                  pl.BlockSpec((1, d, tn), lambda i, j: (i, 0, j)),
                  pl.BlockSpec((1, 1, tn), lambda i, j: (i, 0, j))],
        out_specs=pl.BlockSpec((1, r, tn), lambda i, j: (i, 0, j)),
        out_shape=jax.ShapeDtypeStruct((n, r, nn), F32),
        compiler_params=_cparams("arbitrary", "arbitrary"), name="mod")(c_all, w, b)


def _mixer_kernel(seq_mode, d, *refs):
    if seq_mode:
        x_ref, mod_ref, g0_ref, g1_ref, win_ref, cw_ref, cb_ref, wout_ref, xo_ref, st_ref, carry = refs
    else:
        x_ref, mod_ref, g0_ref, g1_ref, win_ref, cw_ref, cb_ref, wout_ref, p0_ref, p1_ref, xo_ref, st_ref = refs
    x = x_ref[0]
    m = mod_ref[0]
    h = _rms(x, g0_ref[...]) * (1.0 + m[:, d:2 * d]) + m[:, :d]
    z = _dot(h.astype(BF16), win_ref[...])
    bg, cg, u = z[:, :d], z[:, d:2 * d], z[:, 2 * d:]
    v = cg * u
    if seq_mode:
        @pl.when(pl.program_id(1) == 0)
        def _():
            carry[...] = jnp.zeros_like(carry)
        r1, r2 = _shift_rows(v, carry[...])
        carry[...] = v[-8:]
        st_ref[0] = v[-8:]
    else:
        r1, r2 = p1_ref[0], p0_ref[0]
        st_ref[0] = v
    y = _conv3(v, r1, r2, cw_ref[...], cb_ref[...])
    o = _dot((bg * y).astype(BF16), wout_ref[...])
    xo_ref[0] = x + m[:, 2 * d:] * _rms(o, g1_ref[...])


def _row_specs(bx, s, tm, mod):
    sm = mod.shape[1]
    if sm == 1:
        mod_spec = pl.BlockSpec((1, 1, mod.shape[2]), lambda b, i: (b, 0, 0))
    else:
        mod_spec = pl.BlockSpec((1, tm, mod.shape[2]), lambda b, i: (b, i, 0))
    return mod_spec


def _full(shape):
    nd = len(shape)
    return pl.BlockSpec(shape, lambda b, i, _nd=nd: (0,) * _nd)


def _mixer_call(x, mod, g0, g1, w_in, cw, cb, w_out, prev, tm):
    bx, s, d = x.shape
    seq_mode = prev is None
    row = pl.BlockSpec((1, tm, d), lambda b, i: (b, i, 0))
    in_specs = [row, _row_specs(bx, s, tm, mod), _full((1, d)), _full((1, d)), _full(w_in.shape),
                _full(cw.shape), _full((1, d)), _full(w_out.shape)]
    args = [x, mod, g0, g1, w_in, cw, cb, w_out]
    if seq_mode:
        st_shape, st_spec = (bx, 8, d), pl.BlockSpec((1, 8, d), lambda b, i: (b, 0, 0))
        scratch = [pltpu.VMEM((8, d), F32)]
    else:
        in_specs += [row, row]
        args += [prev[0], prev[1]]
        st_shape, st_spec = (bx, s, d), row
        scratch = []
    return pl.pallas_call(
        functools.partial(_mixer_kernel, seq_mode, d), grid=(bx, s // tm),
        in_specs=in_specs, out_specs=[row, st_spec],
        out_shape=[jax.ShapeDtypeStruct((bx, s, d), F32), jax.ShapeDtypeStruct(st_shape, F32)],
        scratch_shapes=scratch, compiler_params=_cparams("arbitrary", "arbitrary"), name="mixer_a")(*args)


def _ffn_kernel(seq_mode, d, dff, n_chunks, *refs):
    if seq_mode:
        x_ref, mod_ref, g2_ref, g3_ref, wup_ref, cw_ref, cb_ref, wdn_ref, xo_ref, st_ref, carry = refs
    else:
        x_ref, mod_ref, g2_ref, g3_ref, wup_ref, cw_ref, cb_ref, wdn_ref, p0_ref, p1_ref, xo_ref, st_ref = refs
    x = x_ref[0]
    m = mod_ref[0]
    h = (_rms(x, g2_ref[...]) * (1.0 + m[:, d:2 * d]) + m[:, :d]).astype(BF16)
    if seq_mode:
        @pl.when(pl.program_id(1) == 0)
        def _():
            carry[...] = jnp.zeros_like(carry)
    cwid = dff // n_chunks
    acc = jnp.zeros((x.shape[0], d), F32)
    for k in range(n_chunks):
        halves = []
        for c0 in (k * cwid, dff + k * cwid):
            up = _dot(h, wup_ref[:, c0:c0 + cwid])
            if seq_mode:
                r1, r2 = _shift_rows(up, carry[:, c0:c0 + cwid])
                carry[:, c0:c0 + cwid] = up[-8:]
                st_ref[0, :, c0:c0 + cwid] = up[-8:]
            else:
                r1, r2 = p1_ref[0, :, c0:c0 + cwid], p0_ref[0, :, c0:c0 + cwid]
                st_ref[0, :, c0:c0 + cwid] = up
            halves.append(_conv3(up, r1, r2, cw_ref[:, c0:c0 + cwid], cb_ref[:, c0:c0 + cwid]))
        u, g = halves
        act = (g * _sigmoid(g)) * u
        acc = acc + _dot(act.astype(BF16), wdn_ref[k * cwid:(k + 1) * cwid, :])
    xo_ref[0] = x + m[:, 2 * d:] * _rms(acc, g3_ref[...])


def _ffn_call(x, mod, g2, g3, w_up, cw, cb, w_dn, prev, tm, n_chunks):
    bx, s, d = x.shape
    dff = w_dn.shape[0]
    seq_mode = prev is None
    row = pl.BlockSpec((1, tm, d), lambda b, i: (b, i, 0))
    in_specs = [row, _row_specs(bx, s, tm, mod), _full((1, d)), _full((1, d)), _full(w_up.shape),
                _full(cw.shape), _full((1, 2 * dff)), _full(w_dn.shape)]
    args = [x, mod, g2, g3, w_up, cw, cb, w_dn]
    if seq_mode:
        st_shape, st_spec = (bx, 8, 2 * dff), pl.BlockSpec((1, 8, 2 * dff), lambda b, i: (b, 0, 0))
        scratch = [pltpu.VMEM((8, 2 * dff), F32)]
    else:
        prow = pl.BlockSpec((1, tm, 2 * dff), lambda b, i: (b, i, 0))
        in_specs += [prow, prow]
        args += [prev[0], prev[1]]
        st_shape, st_spec = (bx, s, 2 * dff), prow
        scratch = []
    return pl.pallas_call(
        functools.partial(_ffn_kernel, seq_mode, d, dff, n_chunks), grid=(bx, s // tm),
        in_specs=in_specs, out_specs=[row, st_spec],
        out_shape=[jax.ShapeDtypeStruct((bx, s, d), F32), jax.ShapeDtypeStruct(st_shape, F32)],
        scratch_shapes=scratch, compiler_params=_cparams("arbitrary", "arbitrary"), name="conv_ffn")(*args)


def _kvq_kernel(d, n_rows_cols, x_ref, modkv_ref, mod_ref, gkv_ref, g0_ref, wkv_ref, wqg_ref,
                rows_ref, win_ref, q_ref, gate_ref):
    x = x_ref[0]
    r = x * lax.rsqrt(jnp.mean(x * x, axis=-1, keepdims=True) + EPS)
    mk = modkv_ref[0]
    hk = (r * gkv_ref[...]) * (1.0 + mk[:, d:]) + mk[:, :d]
    kv = _dot(hk.astype(BF16), wkv_ref[...])
    rows_ref[0] = kv[:, :n_rows_cols]
    win_ref[0] = kv[:, n_rows_cols:]
    m = mod_ref[0]
    h1 = (r * g0_ref[...]) * (1.0 + m[:, d:2 * d]) + m[:, :d]
    qg = _dot(h1.astype(BF16), wqg_ref[...])
    nq = N_HEADS * HEAD_DIM
    q_ref[0] = qg[:, :nq] * (HEAD_DIM ** -0.5)
    gate_ref[0] = _sigmoid(qg[:, nq:])


def _kvq_call(x, modkv, mod, gkv, g0, w_kv, w_qg, tm):
    bx, s, d = x.shape
    nkv = w_kv.shape[1]
    n_rows_cols = 4 * N_KV * HEAD_DIM
    n_win_cols = nkv - n_rows_cols
    nq = N_HEADS * HEAD_DIM
    ng = w_qg.shape[1] - nq
    row = lambda w: pl.BlockSpec((1, tm, w), lambda b, i: (b, i, 0))
    return pl.pallas_call(
        functools.partial(_kvq_kernel, d, n_rows_cols), grid=(bx, s // tm),
        in_specs=[row(d), _row_specs(bx, s, tm, modkv), _row_specs(bx, s, tm, mod), _full((1, d)), _full((1, d)),
                  _full(w_kv.shape), _full(w_qg.shape)],
        out_specs=[row(n_rows_cols), row(n_win_cols), row(nq), row(ng)],
        out_shape=[jax.ShapeDtypeStruct((bx, s, n_rows_cols), F32), jax.ShapeDtypeStruct((bx, s, n_win_cols), F32),
                   jax.ShapeDtypeStruct((bx, s, nq), F32), jax.ShapeDtypeStruct((bx, s, ng), F32)],
        compiler_params=_cparams("arbitrary", "arbitrary"), name="kv_q_proj")(x, modkv, mod, gkv, g0, w_kv, w_qg)


def _outproj_kernel(d, n_branch, *refs):
    x_ref, mod_ref, g1_ref, w_ref = refs[:4]
    xo_ref = refs[-1]
    if n_branch == 1:
        o = refs[4][0]
    else:
        o = sum(refs[4 + 2 * i][0] * refs[5 + 2 * i][0] for i in range(n_branch))
    y = _dot(o.astype(BF16), w_ref[...])
    xo_ref[0] = x_ref[0] + mod_ref[0][:, 2 * d:] * _rms(y, g1_ref[...])


def _outproj_call(x, mod, g1, w, branches, tm):
    bx, s, d = x.shape
    row = pl.BlockSpec((1, tm, d), lambda b, i: (b, i, 0))
    n_branch = 1 if len(branches) == 1 else len(branches) // 2
    return pl.pallas_call(
        functools.partial(_outproj_kernel, d, n_branch), grid=(bx, s // tm),
        in_specs=[row, _row_specs(bx, s, tm, mod), _full((1, d)), _full(w.shape)] + [row] * len(branches),
        out_specs=row, out_shape=jax.ShapeDtypeStruct((bx, s, d), F32),
        compiler_params=_cparams("arbitrary", "arbitrary"), name="attn_out_proj")(x, mod, g1, w, *branches)


def _gelu_tanh(x):
    return x * (0.5 * (1.0 + jnp.tanh(math.sqrt(2.0 / math.pi) * (x + 0.044715 * (x * x * x)))))


def _compress_kernel(tmb, x_ref, pe_ref, w1_ref, w2_ref, o_ref):
    acc = jnp.zeros((tmb, LANES), F32)
    for r in range(CMP_BLOCK):
        xr = x_ref[pl.ds(r, tmb, stride=CMP_BLOCK), :] + pe_ref[0, r]
        acc = acc + _dot(xr.astype(BF16), w1_ref[0, r])
    o_ref[...] = _dot(_gelu_tanh(acc).astype(BF16), w2_ref[0])


def _compress_call(rows2d, pe2, w1bd, w2bd):
    m = rows2d.shape[0] // CMP_BLOCK
    tmb = max(t for t in range(8, min(256, m) + 1, 8) if m % t == 0)
    return pl.pallas_call(
        functools.partial(_compress_kernel, tmb), grid=(m // tmb, 4),
        in_specs=[pl.BlockSpec((tmb * CMP_BLOCK, LANES), lambda i, j: (i, j)),
                  pl.BlockSpec((1, CMP_BLOCK, 1, LANES), lambda i, j: (j // 2, 0, 0, 0)),
                  pl.BlockSpec((1, CMP_BLOCK, LANES, LANES), lambda i, j: (j // 2, 0, 0, 0)),
                  pl.BlockSpec((1, LANES, LANES), lambda i, j: (j // 2, 0, 0))],
        out_specs=pl.BlockSpec((tmb, LANES), lambda i, j: (i, j)),
        out_shape=jax.ShapeDtypeStruct((m, 4 * LANES), F32),
        compiler_params=_cparams("arbitrary", "arbitrary"), name="compress")(rows2d, pe2, w1bd, w2bd)


def _compress_weights(cmp_pe, cmp_w1, cmp_w2):
    z = jnp.zeros((2, CMP_BLOCK, HEAD_DIM, HEAD_DIM), F32)
    w1 = cmp_w1.reshape(2, CMP_BLOCK, HEAD_DIM, HEAD_DIM)
    w1bd = jnp.concatenate([jnp.concatenate([w1, z], -1), jnp.concatenate([z, w1], -1)], -2).astype(BF16)
    z2 = jnp.zeros((2, HEAD_DIM, HEAD_DIM), F32)
    w2bd = jnp.concatenate([jnp.concatenate([cmp_w2, z2], -1), jnp.concatenate([z2, cmp_w2], -1)], -2).astype(BF16)
    pe2 = jnp.concatenate([cmp_pe, cmp_pe], -1)[:, :, None, :]
    return pe2, w1bd, w2bd


N_PAGE_CB = PAGE_SIZE // CMP_BLOCK
CB_ROWS = 8


def _compress_pool_kernel(n_pg, x_ref, pe_ref, mw_ref, w2_ref, o_ref):
    gw = N_KV * HEAD_DIM
    row0 = pl.program_id(1) * gw
    page_rows = 4 * gw

    def rows_of(g, d):
        return x_ref[pl.ds(row0 + g * HEAD_DIM + d, n_pg, stride=page_rows), :]
    lhs = jnp.concatenate(
        [(jnp.concatenate([rows_of(g, d) for g in range(N_KV)], axis=0) + pe_ref[0, d]).astype(BF16)
         for d in range(HEAD_DIM)], axis=1)
    z = _dot(_gelu_tanh(_dot(lhs, mw_ref[0])).astype(BF16), w2_ref[0])
    lane = lax.broadcasted_iota(jnp.int32, (n_pg, LANES), 1)
    low = lane < HEAD_DIM
    o_ref[...] = jnp.zeros(o_ref.shape, F32)
    for n in range(N_PAGE_CB):
        for c in range(N_KV // 2):
            a = z[(2 * c) * n_pg:(2 * c + 1) * n_pg, (n // 2) * LANES:(n // 2 + 1) * LANES]
            b = z[(2 * c + 1) * n_pg:(2 * c + 2) * n_pg, (n // 2) * LANES:(n // 2 + 1) * LANES]
            if n % 2 == 0:
                b = pltpu.roll(b, HEAD_DIM, 1)
            else:
                a = pltpu.roll(a, HEAD_DIM, 1)
            o_ref[c, pl.ds(n, n_pg, stride=CB_ROWS), :] = jnp.where(low, a, b)


def _compress_pool_call(cache_t, pe_t, mw, w2bd4):
    gw = N_KV * HEAD_DIM
    n_pool = cache_t.shape[0] // (4 * gw)
    n_pg = max(t for t in range(8, min(32, n_pool) + 1, 8) if n_pool % t == 0)
    return pl.pallas_call(
        functools.partial(_compress_pool_kernel, n_pg), grid=(n_pool // n_pg, 2),
        in_specs=[pl.BlockSpec((n_pg * 4 * gw, PAGE_SIZE), lambda i, j: (i, 0)),
                  pl.BlockSpec((1, HEAD_DIM, 1, PAGE_SIZE), lambda i, j: (j, 0, 0, 0)),
                  pl.BlockSpec((1, HEAD_DIM * PAGE_SIZE, N_PAGE_CB * HEAD_DIM), lambda i, j: (j, 0, 0)),
                  pl.BlockSpec((1, N_PAGE_CB * HEAD_DIM, N_PAGE_CB * HEAD_DIM), lambda i, j: (j, 0, 0))],
        out_specs=pl.BlockSpec((N_KV // 2, n_pg * CB_ROWS, LANES), lambda i, j: (j, i, 0)),
        out_shape=jax.ShapeDtypeStruct((N_KV, n_pool * CB_ROWS, LANES), F32),
        compiler_params=_cparams("arbitrary", "arbitrary"), name="compress_pool")(cache_t, pe_t, mw, w2bd4)


def _compress_pool_weights(cmp_pe, cmp_w1, cmp_w2):
    eye = np.eye(N_PAGE_CB, dtype=bool)
    w1t = cmp_w1.reshape(2, CMP_BLOCK, HEAD_DIM, HEAD_DIM).transpose(0, 2, 1, 3)
    mw = jnp.where(eye[None, None, :, None, :, None], w1t[:, :, None, :, None, :], 0.0)
    mw = mw.reshape(2, HEAD_DIM * PAGE_SIZE, N_PAGE_CB * HEAD_DIM).astype(BF16)
    w2bd4 = jnp.where(eye[None, :, None, :, None], cmp_w2[:, None, :, None, :], 0.0)
    w2bd4 = w2bd4.reshape(2, N_PAGE_CB * HEAD_DIM, N_PAGE_CB * HEAD_DIM).astype(BF16)
    pe_t = jnp.tile(cmp_pe.transpose(0, 2, 1), (1, 1, N_PAGE_CB))[:, :, None, :]
    return pe_t, mw, w2bd4


def _bias_of_distance(dist, tab):
    val = jnp.zeros(dist.shape, F32) + tab(0)
    for k in range(1, N_BUCKETS):
        val = jnp.where(dist >= BUCKET_THR[k], tab(k), val)
    return val


def _toeplitz_kernel(window, tab_ref, o_ref):
    g = pl.program_id(0)
    dt = pl.program_id(1)
    i = lax.broadcasted_iota(jnp.int32, (TQ, TK), 0)
    j = lax.broadcasted_iota(jnp.int32, (TQ, TK), 1)
    dist = dt * TQ + i - j
    for hh in range(HPG):
        head = g * HPG + hh
        val = _bias_of_distance(dist, lambda k, head=head: tab_ref[k, head])
        val = jnp.where(dist < 0, NEG_MASK, val)
        if window is not None:
            val = jnp.where(dist >= window, NEG_MASK, val)
        o_ref[0, 0, hh * TQ:(hh + 1) * TQ, :] = val


def _toeplitz_call(rel_bias, n_chunks, window):
    return pl.pallas_call(
        functools.partial(_toeplitz_kernel, window), grid=(N_KV, n_chunks),
        in_specs=[pl.BlockSpec(memory_space=pltpu.SMEM)],
        out_specs=pl.BlockSpec((1, 1, HPG * TQ, TK), lambda g, t: (g, t, 0, 0)),
        out_shape=jax.ShapeDtypeStruct((N_KV, n_chunks, HPG * TQ, TK), F32),
        compiler_params=_cparams("arbitrary", "arbitrary"), name="bias_toeplitz")(rel_bias)


def _cmp_bias_kernel(tab_ref, o_ref):
    g = pl.program_id(0)
    qi = pl.program_id(1)
    row = lax.broadcasted_iota(jnp.int32, (LANES, TQ), 0)
    t_q = qi * TQ + lax.broadcasted_iota(jnp.int32, (LANES, TQ), 1)
    cblk = 2 * (row % HEAD_DIM) + row // HEAD_DIM
    dist = t_q - (cblk * CMP_BLOCK + (CMP_BLOCK - 1))
    for hh in range(HPG):
        head = g * HPG + hh
        val = _bias_of_distance(dist, lambda k, head=head: tab_ref[k, head])
        o_ref[0, 0, :, hh * TQ:(hh + 1) * TQ] = jnp.where(dist < 0, NEG_MASK, val)


def _cmp_bias_call(rel_bias, n_qt):
    return pl.pallas_call(
        _cmp_bias_kernel, grid=(N_KV, n_qt),
        in_specs=[pl.BlockSpec(memory_space=pltpu.SMEM)],
        out_specs=pl.BlockSpec((1, 1, LANES, HPG * TQ), lambda g, t: (g, t, 0, 0)),
        out_shape=jax.ShapeDtypeStruct((N_KV, n_qt, LANES, HPG * TQ), F32),
        compiler_params=_cparams("arbitrary", "arbitrary"), name="bias_cmp")(rel_bias)


def _bias_cols_kernel(dist_ref, tab_ref, o_ref):
    dist = dist_ref[...]
    val = _bias_of_distance(dist, lambda k: tab_ref[:, k:k + 1])
    o_ref[...] = jnp.where(dist < 0, NEG_MASK, val)


def _bias_cols_call(dist, tab_heads):
    r = dist.shape[1]
    return pl.pallas_call(
        _bias_cols_kernel, grid=(1,),
        in_specs=[pl.BlockSpec((N_HEADS, r), lambda i: (0, 0)), pl.BlockSpec((N_HEADS, N_BUCKETS), lambda i: (0, 0))],
        out_specs=pl.BlockSpec((N_HEADS, r), lambda i: (0, 0)),
        out_shape=jax.ShapeDtypeStruct((N_HEADS, r), F32),
        compiler_params=_cparams("arbitrary"), name="bias_cols")(dist, tab_heads)


def _kv_prep_kernel(ksel_ref, vsel_ref, kwin_ref, vwin_ref, kaug_ref, v1_ref, kw_ref, vw1_ref):
    g_odd = (pl.program_id(1) % 2) == 1
    ch = ksel_ref.shape[1]
    lane = lax.broadcasted_iota(jnp.int32, (ch, LANES), 1)
    row = pl.program_id(2) * ch + lax.broadcasted_iota(jnp.int32, (ch, LANES), 0)
    low = lane < HEAD_DIM
    ones = jnp.where(lane == HEAD_DIM, 1.0, 0.0)
    onehot = jnp.where(lane - HEAD_DIM == row // SEL_BLOCK, 1.0, 0.0)

    def pick(ref):
        x = ref[0]
        return jnp.where(g_odd, pltpu.roll(x, HEAD_DIM, 1), x)

    kaug_ref[0, 0] = jnp.where(low, pick(ksel_ref), onehot).astype(BF16)
    v1_ref[0, 0] = jnp.where(low, pick(vsel_ref), ones).astype(BF16)
    kw_ref[0, 0] = jnp.where(low, pick(kwin_ref), 0.0).astype(BF16)
    vw1_ref[0, 0] = jnp.where(low, pick(vwin_ref), ones).astype(BF16)


def _kv_prep_call(rows, win):
    b, s, _ = rows.shape
    ch = min(1024, s)
    kv_spec = lambda col0: pl.BlockSpec((1, ch, LANES), lambda bb, g, c, _c=col0: (bb, c, _c + g // 2))
    o_spec = pl.BlockSpec((1, 1, ch, LANES), lambda bb, g, c: (bb, g, c, 0))
    o_shape = jax.ShapeDtypeStruct((b, N_KV, s, LANES), BF16)
    return pl.pallas_call(
        _kv_prep_kernel, grid=(b, N_KV, s // ch),
        in_specs=[kv_spec(4), kv_spec(6), kv_spec(0), kv_spec(2)],
        out_specs=[o_spec] * 4, out_shape=[o_shape] * 4,
        compiler_params=_cparams("arbitrary", "arbitrary", "arbitrary"), name="kv_prep")(rows, rows, win, win)


def _nsa_seq_kernel(n_sb, q_ref, gate_ref, kaug_ref, v1_ref, kw_ref, vw1_ref, kcb_ref, vcb_ref, tcmp_ref,
                    tsel_ref, twin_ref, o_ref, qaug, qw_sc, s_sc, mrun, mb, acc_sc, score_sc):
    qi = pl.program_id(2)
    rows = HPG * TQ
    lane = lax.broadcasted_iota(jnp.int32, (TQ, LANES), 1)
    low = lane < HEAD_DIM

    qh = []
    for hh in range(HPG):
        qv = q_ref[0, :, (hh // 2) * LANES:(hh // 2 + 1) * LANES]
        if hh % 2 == 1:
            qv = pltpu.roll(qv, HEAD_DIM, 1)
        qh.append(jnp.where(low, qv, 0.0))
    qw = jnp.concatenate(qh, axis=0).astype(BF16)
    qw_sc[...] = qw

    s_c = _dot_nt(kcb_ref[0, 0].astype(BF16), qw) + tcmp_ref[0, 0]
    ok_c = s_c > 0.5 * NEG_MASK
    mx = jnp.max(s_c, axis=0, keepdims=True)
    p = jnp.where(ok_c, jnp.exp(s_c - mx), 0.0)
    den = jnp.sum(p, axis=0, keepdims=True)
    pn_t = p / jnp.where(den > 0, den, 1.0)
    o_c = _dot(pn_t.T.astype(BF16), vcb_ref[0, 0].astype(BF16))
    imp = pn_t[:, 0:TQ]
    for hh in range(1, HPG):
        imp = imp + pn_t[:, hh * TQ:(hh + 1) * TQ]
    n_blk = LANES // 2
    imp = imp[:n_blk] + imp[n_blk:]

    blk = lax.broadcasted_iota(jnp.int32, (n_blk, TQ), 0)
    cur = (qi * TQ + lax.broadcasted_iota(jnp.int32, (n_blk, TQ), 1)) // SEL_BLOCK
    valid = blk <= cur
    forced = (blk == 0) | (blk == cur) | (blk == cur - 1)
    score_sc[...] = jnp.where(valid & forced, BIG_SCORE, jnp.where(valid, imp, -BIG_SCORE))
    sub = 8
    groups = [score_sc[r0:r0 + sub, :] for r0 in range(0, n_blk, sub)]
    rowg = lax.broadcasted_iota(jnp.int32, (sub, TQ), 0)
    cnts = [jnp.zeros((sub, TQ), jnp.int32) for _ in groups]
    for bp in range(n_blk):
        r = score_sc[bp:bp + 1, :]
        for gi, s_g in enumerate(groups):
            if gi * sub > bp:
                ahead = r >= s_g
            elif (gi + 1) * sub - 1 < bp:
                ahead = r > s_g
            else:
                ahead = (r > s_g) | ((r == s_g) & (rowg + gi * sub > bp))
            cnts[gi] = cnts[gi] + jnp.where(ahead, 1, 0)
    cnt = jnp.concatenate(cnts, axis=0)
    selmask_t = jnp.where((cnt < min(TOP_N, n_sb)) & valid, 0.0, NEG_SEL)
    selmask = jnp.concatenate([jnp.zeros((n_blk, TQ), F32), selmask_t], axis=0).T
    for hh in range(HPG):
        qaug[hh * TQ:(hh + 1) * TQ, :] = jnp.where(low, qh[hh], selmask).astype(BF16)

    def branch(q_sc, k_ref, v_ref, t_ref, n_tiles, n_tbl):
        mrun[...] = jnp.full(mrun.shape, NEG_MASK, F32)

        def scores(i, _):
            r0 = pl.multiple_of((qi - i) * TK, TK)
            s = _dot_nt(q_sc[...], k_ref[0, 0, pl.ds(r0, TK), :]) + t_ref[0, jnp.minimum(i, n_tbl)]
            s_sc[i] = s
            m = mrun[...]
            for c in range(TK // LANES):
                m = jnp.maximum(m, s[:, c * LANES:(c + 1) * LANES])
            mrun[...] = m
            return 0
        lax.fori_loop(0, n_tiles, scores, 0)
        mb[...] = jnp.broadcast_to(jnp.max(mrun[...], axis=1, keepdims=True), (rows, LANES))
        acc_sc[...] = jnp.zeros(acc_sc.shape, F32)

        def weigh(i, _):
            r0 = pl.multiple_of((qi - i) * TK, TK)
            mbv = mb[...]
            p = jnp.exp(s_sc[i] - jnp.concatenate([mbv] * (TK // LANES), axis=1))
            acc_sc[...] += _dot(p.astype(BF16), v_ref[0, 0, pl.ds(r0, TK), :])
            return 0
        lax.fori_loop(0, n_tiles, weigh, 0)
        acc = acc_sc[...]
        return acc / acc[:, HEAD_DIM:HEAD_DIM + 1]

    o_s = branch(qaug, kaug_ref, v1_ref, tsel_ref, qi + 1, N_TBL)
    o_w = branch(qw_sc, kw_ref, vw1_ref, twin_ref, jnp.minimum(qi, N_WIN_TILES - 1) + 1, N_WIN_TILES - 1)

    outs = []
    for hh in range(HPG):
        gts = [gate_ref[0, :, hh * 3 + br:hh * 3 + br + 1] for br in range(3)]
        rs = slice(hh * TQ, (hh + 1) * TQ)
        outs.append(gts[0] * o_c[rs] + gts[1] * o_s[rs] + gts[2] * o_w[rs])
    for c in range(HPG // 2):
        o_ref[0, :, c * LANES:(c + 1) * LANES] = jnp.where(low, outs[2 * c], pltpu.roll(outs[2 * c + 1], HEAD_DIM, 1))


def _nsa_seq_call(q, gates, kaug, v1, kw, vw1, kcb, vcb, tcmp, tsel, twin):
    b, s, _ = q.shape
    n_qt = s // TQ
    rows = HPG * TQ
    kv_spec = pl.BlockSpec((1, 1, s, LANES), lambda bb, g, i: (bb, g, 0, 0))
    cb_spec = pl.BlockSpec((1, 1, kcb.shape[2], LANES), lambda bb, g, i: (bb, g, 0, 0))
    tbl_spec = lambda t: pl.BlockSpec((1,) + t.shape[1:], lambda bb, g, i: (g, 0, 0, 0))
    return pl.pallas_call(
        functools.partial(_nsa_seq_kernel, s // SEL_BLOCK), grid=(b, N_KV, n_qt),
        in_specs=[pl.BlockSpec((1, TQ, HPG * HEAD_DIM), lambda bb, g, i: (bb, i, g)),
                  pl.BlockSpec((1, TQ, LANES), lambda bb, g, i: (bb, i, g)),
                  kv_spec, kv_spec, kv_spec, kv_spec, cb_spec, cb_spec,
                  pl.BlockSpec((1, 1, LANES, rows), lambda bb, g, i: (g, i, 0, 0)), tbl_spec(tsel), tbl_spec(twin)],
        out_specs=pl.BlockSpec((1, TQ, HPG * HEAD_DIM), lambda bb, g, i: (bb, i, g)),
        out_shape=jax.ShapeDtypeStruct((b, s, N_HEADS * HEAD_DIM), F32),
        scratch_shapes=[pltpu.VMEM((rows, LANES), BF16), pltpu.VMEM((rows, LANES), BF16),
                        pltpu.VMEM((s // TK, rows, TK), F32), pltpu.VMEM((rows, LANES), F32),
                        pltpu.VMEM((rows, LANES), F32), pltpu.VMEM((rows, LANES), F32),
                        pltpu.VMEM((LANES // 2, TQ), F32)],
        compiler_params=_cparams("arbitrary", "arbitrary", "arbitrary"), name="nsa_seq")(
            q, gates, kaug, v1, kw, vw1, kcb, vcb, tcmp, tsel, twin)


def _softmax_lanes(s):
    p = jnp.exp(s - jnp.max(s, axis=1, keepdims=True))
    return p / jnp.sum(p, axis=1, keepdims=True)


def _rows_to_heads(rows):
    hg = lax.broadcasted_iota(jnp.int32, (N_HEADS, rows[0].shape[1]), 0) // HPG
    out = jnp.broadcast_to(rows[0], hg.shape)
    for g in range(1, N_KV):
        out = jnp.where(hg == g, rows[g], out)
    return out


def _nsa_stepT_kernel(n_pages, pt_ref, *refs):
    del pt_ref
    q_ref = refs[0]
    cb_refs = refs[1:1 + n_pages]
    pg_refs = refs[1 + n_pages:1 + 2 * n_pages]
    (kvnew_ref, winnew_ref, wcol_ref, cwin_ref, bsel_ref, bnew_ref, bwin_ref, bcmp_ref) = refs[1 + 2 * n_pages:9 + 2 * n_pages]
    oc_ref, os_ref, ow_ref, nwin_ref = refs[9 + 2 * n_pages:13 + 2 * n_pages]
    (s_sc,) = refs[13 + 2 * n_pages:]
    b = pl.program_id(0)
    gw = N_KV * HEAD_DIM
    n_past = n_pages * PAGE_SIZE
    q16 = q_ref[0].astype(BF16)
    qf = q16.astype(F32)
    lane = lax.broadcasted_iota(jnp.int32, (N_HEADS, LANES), 1)

    n_pad = LANES - CB_ROWS * n_pages
    cb = jnp.concatenate([jnp.concatenate([r[c] for c in range(N_KV)], axis=1) for r in cb_refs]
                         + ([jnp.zeros((n_pad, 2 * gw), F32)] if n_pad else []), axis=0)
    pn_c = _softmax_lanes(_dot_nt(q16, cb[:, :gw].astype(BF16)) + bcmp_ref[...])
    oc_ref[0] = _dot(pn_c.astype(BF16), cb[:, gw:].astype(BF16))

    grp = [pn_c[HPG * g:HPG * g + 1] + pn_c[HPG * g + 1:HPG * g + 2] + pn_c[HPG * g + 2:HPG * g + 3]
           + pn_c[HPG * g + 3:HPG * g + 4] for g in range(N_KV)]
    lane8 = lax.broadcasted_iota(jnp.int32, (8, LANES), 1)
    row8 = lax.broadcasted_iota(jnp.int32, (8, LANES), 0)
    imp = jnp.zeros((8, LANES), F32)
    for g in range(N_KV):
        imp = jnp.where(row8 == g, grp[g], imp)
    imp = imp + pltpu.roll(imp, LANES - 1, 1)
    n_past_blk = n_past // SEL_BLOCK
    cur_lane = LANES - 1
    is_blk = ((lane8 % CB_ROWS == 0) | (lane8 % CB_ROWS == 2))
    last_lane = ((n_past_blk - 1) // 2) * CB_ROWS + 2 * ((n_past_blk - 1) % 2)
    forced = (lane8 == 0) | (lane8 == last_lane) | (lane8 == cur_lane)
    valid = is_blk | (lane8 == cur_lane)
    score = jnp.where(valid & forced, BIG_SCORE, jnp.where(valid, imp, -BIG_SCORE))
    cnt = jnp.zeros((8, LANES), jnp.int32)
    for k in range(1, LANES):
        r = pltpu.roll(score, k, 1)
        cnt = cnt + jnp.where((r > score) | ((r == score) & (lane8 >= k)), 1, 0)
    selrows = jnp.where((cnt < min(TOP_N, n_past_blk + 1)) & valid, 0.0, NEG_SEL)
    selmask = _rows_to_heads([selrows[g:g + 1] for g in range(N_KV)])

    blocks_per_page = PAGE_SIZE // SEL_BLOCK
    for p in range(n_pages):
        kt = pg_refs[p][0, 0:gw, :].astype(BF16)
        msk = selmask[:, CB_ROWS * p:CB_ROWS * p + 1]
        for i in range(1, blocks_per_page):
            msk = jnp.where(lane >= i * SEL_BLOCK, selmask[:, CB_ROWS * p + 2 * i:CB_ROWS * p + 2 * i + 1], msk)
        s_sc[:, p * PAGE_SIZE:(p + 1) * PAGE_SIZE] = _dot(q16, kt) + bsel_ref[:, p * PAGE_SIZE:(p + 1) * PAGE_SIZE] + msk
    knew = kvnew_ref[0][:, 2 * gw:3 * gw].astype(BF16).astype(F32)
    s_new = jnp.sum(qf * knew, axis=1, keepdims=True) + bnew_ref[:, 0:1] + selmask[:, cur_lane:cur_lane + 1]
    s_sc[:, n_past:n_past + LANES] = jnp.where(lane == 0, s_new, NEG_MASK)
    s_all = s_sc[...]
    mx = jnp.max(s_all, axis=1, keepdims=True)
    den = jnp.sum(jnp.exp(s_all - mx), axis=1, keepdims=True)
    acc = jnp.zeros((N_HEADS, gw), F32)
    for p in range(n_pages):
        pn = jnp.exp(s_sc[:, p * PAGE_SIZE:(p + 1) * PAGE_SIZE] - mx) / den
        acc = acc + _dot_nt(pn.astype(BF16), pg_refs[p][0, gw:2 * gw, :].astype(BF16))
    pn_new = jnp.exp(s_new - mx) / den
    vnew = kvnew_ref[0][:, 3 * gw:4 * gw].astype(BF16).astype(F32)
    os_ref[0] = acc + pn_new.astype(BF16).astype(F32) * vnew

    cw = cwin_ref[0]
    w_len = cw.shape[1]
    s_w = _dot(q16, cw[0:gw].astype(BF16)) + bwin_ref[...]
    kwn = winnew_ref[0][:, 0:gw].astype(BF16).astype(F32)
    s_wn = jnp.sum(qf * kwn, axis=1, keepdims=True) + bnew_ref[:, 0:1]
    mxw = jnp.maximum(jnp.max(s_w, axis=1, keepdims=True), s_wn)
    pw = jnp.exp(s_w - mxw)
    pwn = jnp.exp(s_wn - mxw)
    denw = jnp.sum(pw, axis=1, keepdims=True) + pwn
    vwn = winnew_ref[0][:, gw:2 * gw].astype(BF16).astype(F32)
    ow_ref[0] = _dot_nt((pw / denw).astype(BF16), cw[gw:2 * gw].astype(BF16)) + (pwn / denw).astype(BF16).astype(F32) * vwn
    lane_w = lax.broadcasted_iota(jnp.int32, cw.shape, 1)
    nb_l = wcol_ref.shape[1]
    lane_b = lax.broadcasted_iota(jnp.int32, (cw.shape[0], nb_l), 1)
    col = jnp.sum(jnp.where(lane_b == b, wcol_ref[...], 0.0), axis=1, keepdims=True)
    nwin_ref[0] = jnp.where(lane_w == w_len - 1, col, pltpu.roll(cw, w_len - 1, 1))


def _nsa_step_call(page_table, qrows, cb_pool, cache_t, kvnew, winnew, wcol, cwin_t, bsel, bnew, bwin, bcmp):
    nb, n_pages = page_table.shape
    gw = N_KV * HEAD_DIM
    w_len = cwin_t.shape[2]
    cb_specs = [pl.BlockSpec((N_KV, None, CB_ROWS, LANES), lambda b, pt, _p=p: (0, pt[b, _p], 0, 0))
                for p in range(n_pages)]
    pg_specs = [pl.BlockSpec((1, 2 * gw, PAGE_SIZE), lambda b, pt, _p=p: (pt[b, _p], 1, 0)) for p in range(n_pages)]
    const = lambda a: pl.BlockSpec(a.shape, lambda b, pt: (0, 0))
    o_spec = pl.BlockSpec((1, N_HEADS, gw), lambda b, pt: (b, 0, 0))
    win_spec = pl.BlockSpec((1, 2 * gw, w_len), lambda b, pt: (b, 0, 0))
    grid_spec = pltpu.PrefetchScalarGridSpec(
        num_scalar_prefetch=1, grid=(nb,),
        in_specs=[pl.BlockSpec((1, N_HEADS, gw), lambda b, pt: (b, 0, 0))] + cb_specs + pg_specs + [
            pl.BlockSpec((1, 1, 4 * gw), lambda b, pt: (b, 0, 0)),
            pl.BlockSpec((1, 1, 2 * gw), lambda b, pt: (b, 0, 0)),
            const(wcol), win_spec, const(bsel), const(bnew), const(bwin), const(bcmp)],
        out_specs=[o_spec, o_spec, o_spec, win_spec],
        scratch_shapes=[pltpu.VMEM((N_HEADS, n_pages * PAGE_SIZE + LANES), F32)])
    o_shape = jax.ShapeDtypeStruct((nb, N_HEADS, gw), F32)
    return pl.pallas_call(
        functools.partial(_nsa_stepT_kernel, n_pages), grid_spec=grid_spec,
        out_shape=[o_shape, o_shape, o_shape, jax.ShapeDtypeStruct((nb, 2 * gw, w_len), F32)],
        compiler_params=_cparams("arbitrary"), name="nsa_step")(
            page_table, qrows, *([cb_pool] * n_pages), *([cache_t] * n_pages), kvnew, winnew, wcol, cwin_t,
            bsel, bnew, bwin, bcmp)


def _head_diag(o):
    b = o.shape[0]
    o5 = o.reshape(b, N_KV, HPG, N_KV, HEAD_DIM)
    return jnp.stack([o5[:, g, :, g, :] for g in range(N_KV)], axis=1).reshape(b, N_HEADS * HEAD_DIM)


def kernel(x_prompt, x_sample, c_prompt, c_sample, cache_kv, cache_win, state_conv_a, state_ffn_conv, page_table, mod_w, mod_b, norm_g, a_w_in, a_conv_w, a_conv_b, a_w_out, kv_mod_w, kv_mod_b, kv_norm_g, w_kv, cmp_pe, cmp_w1, cmp_w2, b_w_qg, b_w_out, rel_bias, ffn_w_up, ffn_conv_w, ffn_conv_b, ffn_w_down):
    bp, s, d = x_prompt.shape
    bs = x_sample.shape[0]
    depth = mod_w.shape[0]
    n_a = a_w_in.shape[0]
    assert depth == 2 and n_a == 1 and x_sample.shape[1] == 1
    dff = ffn_w_down.shape[1]
    n_pool = cache_kv.shape[0]
    n_pages = page_table.shape[1]
    past_len = n_pages * PAGE_SIZE
    gw = N_KV * HEAD_DIM
    nq = N_HEADS * HEAD_DIM

    n_c = bp + bs
    n_cp = -(-n_c // 8) * 8
    c_all = jnp.pad(jnp.concatenate([c_prompt, c_sample], 0), ((0, n_cp - n_c), (0, 0)))
    mods = _mod_call(c_all, mod_w.reshape(depth * 2, d, 3 * d), mod_b.reshape(depth * 2, 1, 3 * d))
    modkv = _mod_call(c_all, kv_mod_w[None], kv_mod_b[None, None])[0]
    mod_p = lambda i: mods[i, :bp][:, None, :]
    mod_s = lambda i: mods[i, bp:n_c][None]

    w_in = a_w_in[0].astype(BF16)
    w_out_a = a_w_out[0].astype(BF16)
    w_up = ffn_w_up.astype(BF16)
    w_dn = ffn_w_down.astype(BF16)
    w_kv_b = w_kv.astype(BF16)
    ng_pad = -(-(b_w_qg.shape[2] - nq) // LANES) * LANES
    w_qg = jnp.pad(b_w_qg[0], ((0, 0), (0, nq + ng_pad - b_w_qg.shape[2]))).astype(BF16)
    w_out_b = b_w_out[0].astype(BF16)
    g = lambda l, i: norm_g[l, i][None]
    pe2, w1bd, w2bd = _compress_weights(cmp_pe, cmp_w1, cmp_w2)
    n_chunks = 2

    tm = min(256, s)
    x1, st_a = _mixer_call(x_prompt, mod_p(0), g(0, 0), g(0, 1), w_in, a_conv_w[0], a_conv_b[0][None], w_out_a, None, tm)
    x2, st_f0 = _ffn_call(x1, mod_p(1), g(0, 2), g(0, 3), w_up[0], ffn_conv_w[0], ffn_conv_b[0][None], w_dn[0], None, tm, n_chunks)
    rows, win, q, gates = _kvq_call(x2, modkv[:bp][:, None, :], mod_p(2), kv_norm_g[None], g(1, 0), w_kv_b, w_qg, tm)
    cb = _compress_call(rows.reshape(bp * s, 4 * gw), pe2, w1bd, w2bd)
    n_cbk = s // CMP_BLOCK
    assert n_cbk <= LANES
    cb = jnp.pad(cb.reshape(bp, n_cbk, 2, N_KV, HEAD_DIM), ((0, 0), (0, LANES - n_cbk), (0, 0), (0, 0), (0, 0)))
    cb = cb.reshape(bp, LANES // 2, 2, 2, N_KV, HEAD_DIM)
    cb = cb.transpose(3, 0, 4, 2, 1, 5).reshape(2, bp, N_KV, LANES, HEAD_DIM)
    cb = jnp.pad(cb, ((0, 0),) * 4 + ((0, LANES - HEAD_DIM),))
    gates_g = jnp.pad(gates[:, :, :N_HEADS * 3].reshape(bp, s, N_KV, HPG * 3),
                      ((0, 0), (0, 0), (0, 0), (0, LANES - HPG * 3))).reshape(bp, s, N_KV * LANES)
    tsel = _toeplitz_call(rel_bias, N_TBL + 1, None)
    twin = _toeplitz_call(rel_bias, N_WIN_TILES, WINDOW)
    tcmp = _cmp_bias_call(rel_bias, s // TQ)
    kaug, v1, kw, vw1 = _kv_prep_call(rows, win)
    o_att = _nsa_seq_call(q, gates_g, kaug, v1, kw, vw1, cb[0], cb[1], tcmp, tsel, twin)
    x3 = _outproj_call(x2, mod_p(2), g(1, 1), w_out_b, [o_att], tm)
    y_prompt, st_f1 = _ffn_call(x3, mod_p(3), g(1, 2), g(1, 3), w_up[1], ffn_conv_w[1], ffn_conv_b[1][None], w_dn[1], None, tm, n_chunks)
    kv_p = rows.reshape(bp, s, 4, N_KV, HEAD_DIM)
    keep = min(WINDOW, s)
    win_p = win[:, s - keep:].reshape(bp, keep, 2, N_KV, HEAD_DIM)
    conv_a_p = st_a[None, :, 6:8]
    ffn_p = jnp.stack([st_f0[:, 6:8], st_f1[:, 6:8]])

    xs = x_sample.reshape(1, bs, d)
    prev_a = (state_conv_a[0, :, 0][None], state_conv_a[0, :, 1][None])
    xs1, v_a = _mixer_call(xs, mod_s(0), g(0, 0), g(0, 1), w_in, a_conv_w[0], a_conv_b[0][None], w_out_a, prev_a, bs)
    prev_f = lambda l: (state_ffn_conv[l, :, 0][None], state_ffn_conv[l, :, 1][None])
    xs2, up0 = _ffn_call(xs1, mod_s(1), g(0, 2), g(0, 3), w_up[0], ffn_conv_w[0], ffn_conv_b[0][None], w_dn[0], prev_f(0), bs, n_chunks)
    rows_s, win_s, q_s, gates_s = _kvq_call(xs2, modkv[bp:n_c][None], mod_s(2), kv_norm_g[None], g(1, 0), w_kv_b, w_qg, bs)
    cache_t = cache_kv.transpose(0, 2, 3, 4, 1).reshape(n_pool, 4 * gw, PAGE_SIZE)
    w_len = cache_win.shape[1]
    cwin_t = cache_win.transpose(0, 2, 3, 4, 1).reshape(bs, 2 * gw, w_len)
    cb_pool = _compress_pool_call(cache_t.reshape(n_pool * 4 * gw, PAGE_SIZE),
                                  *_compress_pool_weights(cmp_pe, cmp_w1, cmp_w2))
    cb_pool = cb_pool.reshape(N_KV, n_pool, CB_ROWS, LANES)
    assert n_pages * CB_ROWS <= LANES
    d_sel = past_len - np.arange(past_len)
    d_new = np.where(np.arange(LANES) == 0, 0, -1)
    d_win = w_len - np.arange(w_len)
    d_win = np.where(d_win < WINDOW, d_win, -1)
    cl = np.arange(LANES)
    d_cmp = past_len - (((cl // CB_ROWS) * N_PAGE_CB + cl % CB_ROWS) * CMP_BLOCK + CMP_BLOCK - 1)
    d_cmp = np.where((cl % CB_ROWS < N_PAGE_CB) & (cl // CB_ROWS < n_pages), d_cmp, -1)
    assert d_sel.min() >= 0 and d_cmp[d_cmp != -1].min() >= 0
    dist = np.concatenate([d_sel, d_new, d_win, d_cmp]).astype(np.int32)
    bias_cols = _bias_cols_call(jnp.asarray(np.repeat(dist[None, :], N_HEADS, 0)), rel_bias.T)
    bsel, bnew = bias_cols[:, :past_len], bias_cols[:, past_len:past_len + LANES]
    bwin = bias_cols[:, past_len + LANES:past_len + LANES + w_len]
    bcmp = bias_cols[:, past_len + LANES + w_len:]
    head_group = (np.arange(N_HEADS)[:, None] // HPG == np.arange(N_KV)[None, :])[None, :, :, None]
    qrows = jnp.where(head_group, q_s.reshape(bs, N_HEADS, 1, HEAD_DIM), 0.0).reshape(bs, N_HEADS, gw)
    oc, os_, ow, nwin_t = _nsa_step_call(
        page_table, qrows, cb_pool, cache_t, rows_s.reshape(bs, 1, 4 * gw),
        win_s.reshape(bs, 1, 2 * gw), win_s[0].T, cwin_t, bsel, bnew, bwin, bcmp)
    nwin = nwin_t.reshape(bs, 2, N_KV, HEAD_DIM, w_len).transpose(0, 4, 1, 2, 3)
    gts = gates_s[0, :, :N_HEADS * 3].reshape(bs, N_HEADS, 3)
    branches = []
    for br, o in enumerate((oc, os_, ow)):
        branches += [jnp.repeat(gts[:, :, br], HEAD_DIM, axis=1)[None], _head_diag(o)[None]]
    xs3 = _outproj_call(xs2, mod_s(2), g(1, 1), w_out_b, branches, bs)
    ys, up1 = _ffn_call(xs3, mod_s(3), g(1, 2), g(1, 3), w_up[1], ffn_conv_w[1], ffn_conv_b[1][None], w_dn[1], prev_f(1), bs, n_chunks)
    y_sample = ys.reshape(bs, 1, d)
    kv_s = rows_s.reshape(bs, 1, 4, N_KV, HEAD_DIM)
    win_state_s = nwin
    conv_a_s = jnp.stack([state_conv_a[0, :, 1], v_a[0]], axis=1)[None]
    ffn_s = jnp.stack([jnp.stack([state_ffn_conv[l, :, 1], u[0]], axis=1) for l, u in ((0, up0), (1, up1))])
    return (y_prompt, y_sample, kv_p, kv_s, win_p, win_state_s, conv_a_p, conv_a_s, ffn_p, ffn_s)
```

```python
import functools
import math

import numpy as np
import jax
import jax.numpy as jnp
from jax import lax
from jax.experimental import pallas as pl
from jax.experimental.pallas import tpu as pltpu

F32 = jnp.float32
BF16 = jnp.bfloat16

N_HEADS = 16
HEAD_DIM = 64
N_KV = 4
HPG = N_HEADS // N_KV
CMP_BLOCK = 32
SEL_BLOCK = 64
TOP_N = 16
WINDOW = 512
N_BUCKETS = 32
MAX_EXACT = N_BUCKETS // 2
MAX_DISTANCE = 1024
PAGE_SIZE = 128
EPS = 1e-6

LANES = 128
TQ = 256
TK = 256
NEG_MASK = -1e30
NEG_SEL = -1e9
BIG_SCORE = 1e30
VMEM_LIMIT_BYTES = 56 * 1024 * 1024


def _bucket_thresholds():
    d = np.arange(0, 4 * MAX_DISTANCE)
    nf = np.maximum(d, 1).astype(np.float32)
    large = MAX_EXACT + (np.log(nf / MAX_EXACT) / math.log(MAX_DISTANCE / MAX_EXACT)
                         * (N_BUCKETS - MAX_EXACT)).astype(np.int32)
    bucket = np.where(d < MAX_EXACT, d, np.minimum(large, N_BUCKETS - 1))
    assert np.all(np.diff(bucket) >= 0)
    return [int(np.argmax(bucket >= k)) for k in range(N_BUCKETS)]


BUCKET_THR = _bucket_thresholds()
assert TQ == TK
N_TBL = -(-(BUCKET_THR[-1] + TK - 1) // TQ)
N_WIN_TILES = WINDOW // TK + 1


def _cparams(*sem):
    return pltpu.CompilerParams(dimension_semantics=sem, vmem_limit_bytes=VMEM_LIMIT_BYTES)


def _dot(a, b):
    return jnp.dot(a, b, preferred_element_type=F32)


def _dot_nt(a, b):
    return lax.dot_general(a, b, (((1,), (1,)), ((), ())), preferred_element_type=F32)


def _rms(x, g):
    return (x * lax.rsqrt(jnp.mean(x * x, axis=-1, keepdims=True) + EPS)) * g


def _sigmoid(x):
    return 1.0 / (1.0 + jnp.exp(-x))


def _shift_rows(v, carry):
    row = lax.broadcasted_iota(jnp.int32, v.shape, 0)
    r1 = jnp.where(row == 0, carry[7:8], pltpu.roll(v, 1, 0))
    r2 = jnp.where(row == 0, carry[6:7], jnp.where(row == 1, carry[7:8], pltpu.roll(v, 2, 0)))
    return r1, r2


def _conv3(v, r1, r2, cw, cb):
    return (cw[0:1] * r2 + cw[1:2] * r1) + cw[2:3] * v + cb


def _mod_kernel(c_ref, w_ref, b_ref, o_ref):
    o_ref[0] = _dot(c_ref[...].astype(BF16), w_ref[0].astype(BF16)) + b_ref[0]


def _mod_call(c_all, w, b):
    n, d, nn = w.shape
    r = c_all.shape[0]
    tn = 512
    return pl.pallas_call(
        _mod_kernel, grid=(n, nn // tn),
        in_specs=[pl.BlockSpec((r, d), lambda i, j: (0, 0)),
                  pl.BlockSpec((1, d, tn), lambda i, j: (i, 0, j)),
                  pl.BlockSpec((1, 1, tn), lambda i, j: (i, 0, j))],
        out_specs=pl.BlockSpec((1, r, tn), lambda i, j: (i, 0, j)),
        out_shape=jax.ShapeDtypeStruct((n, r, nn), F32),
        compiler_params=_cparams("arbitrary", "arbitrary"), name="mod")(c_all, w, b)


def _mixer_kernel(seq_mode, d, *refs):
    if seq_mode:
        x_ref, mod_ref, g0_ref, g1_ref, win_ref, cw_ref, cb_ref, wout_ref, xo_ref, st_ref, carry = refs
    else:
        x_ref, mod_ref, g0_ref, g1_ref, win_ref, cw_ref, cb_ref, wout_ref, p0_ref, p1_ref, xo_ref, st_ref = refs
    x = x_ref[0]
    m = mod_ref[0]
    h = _rms(x, g0_ref[...]) * (1.0 + m[:, d:2 * d]) + m[:, :d]
    z = _dot(h.astype(BF16), win_ref[...])
    bg, cg, u = z[:, :d], z[:, d:2 * d], z[:, 2 * d:]
    v = cg * u
    if seq_mode:
        @pl.when(pl.program_id(1) == 0)
        def _():
            carry[...] = jnp.zeros_like(carry)
        r1, r2 = _shift_rows(v, carry[...])
        carry[...] = v[-8:]
        st_ref[0] = v[-8:]
    else:
        r1, r2 = p1_ref[0], p0_ref[0]
        st_ref[0] = v
    y = _conv3(v, r1, r2, cw_ref[...], cb_ref[...])
    o = _dot((bg * y).astype(BF16), wout_ref[...])
    xo_ref[0] = x + m[:, 2 * d:] * _rms(o, g1_ref[...])


def _row_specs(bx, s, tm, mod):
    sm = mod.shape[1]
    if sm == 1:
        mod_spec = pl.BlockSpec((1, 1, mod.shape[2]), lambda b, i: (b, 0, 0))
    else:
        mod_spec = pl.BlockSpec((1, tm, mod.shape[2]), lambda b, i: (b, i, 0))
    return mod_spec


def _full(shape):
    nd = len(shape)
    return pl.BlockSpec(shape, lambda b, i, _nd=nd: (0,) * _nd)


def _mixer_call(x, mod, g0, g1, w_in, cw, cb, w_out, prev, tm):
    bx, s, d = x.shape
    seq_mode = prev is None
    row = pl.BlockSpec((1, tm, d), lambda b, i: (b, i, 0))
    in_specs = [row, _row_specs(bx, s, tm, mod), _full((1, d)), _full((1, d)), _full(w_in.shape),
                _full(cw.shape), _full((1, d)), _full(w_out.shape)]
    args = [x, mod, g0, g1, w_in, cw, cb, w_out]
    if seq_mode:
        st_shape, st_spec = (bx, 8, d), pl.BlockSpec((1, 8, d), lambda b, i: (b, 0, 0))
        scratch = [pltpu.VMEM((8, d), F32)]
    else:
        in_specs += [row, row]
        args += [prev[0], prev[1]]
        st_shape, st_spec = (bx, s, d), row
        scratch = []
    return pl.pallas_call(
        functools.partial(_mixer_kernel, seq_mode, d), grid=(bx, s // tm),
        in_specs=in_specs, out_specs=[row, st_spec],
        out_shape=[jax.ShapeDtypeStruct((bx, s, d), F32), jax.ShapeDtypeStruct(st_shape, F32)],
        scratch_shapes=scratch, compiler_params=_cparams("arbitrary", "arbitrary"), name="mixer_a")(*args)


def _ffn_kernel(seq_mode, d, dff, n_chunks, *refs):
    if seq_mode:
        x_ref, mod_ref, g2_ref, g3_ref, wup_ref, cw_ref, cb_ref, wdn_ref, xo_ref, st_ref, carry = refs
    else:
        x_ref, mod_ref, g2_ref, g3_ref, wup_ref, cw_ref, cb_ref, wdn_ref, p0_ref, p1_ref, xo_ref, st_ref = refs
    x = x_ref[0]
    m = mod_ref[0]
    h = (_rms(x, g2_ref[...]) * (1.0 + m[:, d:2 * d]) + m[:, :d]).astype(BF16)
    if seq_mode:
        @pl.when(pl.program_id(1) == 0)
        def _():
            carry[...] = jnp.zeros_like(carry)
    cwid = dff // n_chunks
    acc = jnp.zeros((x.shape[0], d), F32)
    for k in range(n_chunks):
        halves = []
        for c0 in (k * cwid, dff + k * cwid):
            up = _dot(h, wup_ref[:, c0:c0 + cwid])
            if seq_mode:
                r1, r2 = _shift_rows(up, carry[:, c0:c0 + cwid])
                carry[:, c0:c0 + cwid] = up[-8:]
                st_ref[0, :, c0:c0 + cwid] = up[-8:]
            else:
                r1, r2 = p1_ref[0, :, c0:c0 + cwid], p0_ref[0, :, c0:c0 + cwid]
                st_ref[0, :, c0:c0 + cwid] = up
            halves.append(_conv3(up, r1, r2, cw_ref[:, c0:c0 + cwid], cb_ref[:, c0:c0 + cwid]))
        u, g = halves
        act = (g * _sigmoid(g)) * u
        acc = acc + _dot(act.astype(BF16), wdn_ref[k * cwid:(k + 1) * cwid, :])
    xo_ref[0] = x + m[:, 2 * d:] * _rms(acc, g3_ref[...])


def _ffn_call(x, mod, g2, g3, w_up, cw, cb, w_dn, prev, tm, n_chunks):
    bx, s, d = x.shape
    dff = w_dn.shape[0]
    seq_mode = prev is None
    row = pl.BlockSpec((1, tm, d), lambda b, i: (b, i, 0))
    in_specs = [row, _row_specs(bx, s, tm, mod), _full((1, d)), _full((1, d)), _full(w_up.shape),
                _full(cw.shape), _full((1, 2 * dff)), _full(w_dn.shape)]
    args = [x, mod, g2, g3, w_up, cw, cb, w_dn]
    if seq_mode:
        st_shape, st_spec = (bx, 8, 2 * dff), pl.BlockSpec((1, 8, 2 * dff), lambda b, i: (b, 0, 0))
        scratch = [pltpu.VMEM((8, 2 * dff), F32)]
    else:
        prow = pl.BlockSpec((1, tm, 2 * dff), lambda b, i: (b, i, 0))
        in_specs += [prow, prow]
        args += [prev[0], prev[1]]
        st_shape, st_spec = (bx, s, 2 * dff), prow
        scratch = []
    return pl.pallas_call(
        functools.partial(_ffn_kernel, seq_mode, d, dff, n_chunks), grid=(bx, s // tm),
        in_specs=in_specs, out_specs=[row, st_spec],
        out_shape=[jax.ShapeDtypeStruct((bx, s, d), F32), jax.ShapeDtypeStruct(st_shape, F32)],
        scratch_shapes=scratch, compiler_params=_cparams("arbitrary", "arbitrary"), name="conv_ffn")(*args)


def _kvq_kernel(d, n_rows_cols, x_ref, modkv_ref, mod_ref, gkv_ref, g0_ref, wkv_ref, wqg_ref,
                rows_ref, win_ref, q_ref, gate_ref):
    x = x_ref[0]
    r = x * lax.rsqrt(jnp.mean(x * x, axis=-1, keepdims=True) + EPS)
    mk = modkv_ref[0]
    hk = (r * gkv_ref[...]) * (1.0 + mk[:, d:]) + mk[:, :d]
    kv = _dot(hk.astype(BF16), wkv_ref[...])
    rows_ref[0] = kv[:, :n_rows_cols]
    win_ref[0] = kv[:, n_rows_cols:]
    m = mod_ref[0]
    h1 = (r * g0_ref[...]) * (1.0 + m[:, d:2 * d]) + m[:, :d]
    qg = _dot(h1.astype(BF16), wqg_ref[...])
    nq = N_HEADS * HEAD_DIM
    q_ref[0] = qg[:, :nq] * (HEAD_DIM ** -0.5)
    gate_ref[0] = _sigmoid(qg[:, nq:])


def _kvq_call(x, modkv, mod, gkv, g0, w_kv, w_qg, tm):
    bx, s, d = x.shape
    nkv = w_kv.shape[1]
    n_rows_cols = 4 * N_KV * HEAD_DIM
    n_win_cols = nkv - n_rows_cols
    nq = N_HEADS * HEAD_DIM
    ng = w_qg.shape[1] - nq
    row = lambda w: pl.BlockSpec((1, tm, w), lambda b, i: (b, i, 0))
    return pl.pallas_call(
        functools.partial(_kvq_kernel, d, n_rows_cols), grid=(bx, s // tm),
        in_specs=[row(d), _row_specs(bx, s, tm, modkv), _row_specs(bx, s, tm, mod), _full((1, d)), _full((1, d)),
                  _full(w_kv.shape), _full(w_qg.shape)],
        out_specs=[row(n_rows_cols), row(n_win_cols), row(nq), row(ng)],
        out_shape=[jax.ShapeDtypeStruct((bx, s, n_rows_cols), F32), jax.ShapeDtypeStruct((bx, s, n_win_cols), F32),
                   jax.ShapeDtypeStruct((bx, s, nq), F32), jax.ShapeDtypeStruct((bx, s, ng), F32)],
        compiler_params=_cparams("arbitrary", "arbitrary"), name="kv_q_proj")(x, modkv, mod, gkv, g0, w_kv, w_qg)


def _outproj_kernel(d, n_branch, *refs):
    x_ref, mod_ref, g1_ref, w_ref = refs[:4]
    xo_ref = refs[-1]
    if n_branch == 1:
        o = refs[4][0]
    else:
        o = sum(refs[4 + 2 * i][0] * refs[5 + 2 * i][0] for i in range(n_branch))
    y = _dot(o.astype(BF16), w_ref[...])
    xo_ref[0] = x_ref[0] + mod_ref[0][:, 2 * d:] * _rms(y, g1_ref[...])


def _outproj_call(x, mod, g1, w, branches, tm):
    bx, s, d = x.shape
    row = pl.BlockSpec((1, tm, d), lambda b, i: (b, i, 0))
    n_branch = 1 if len(branches) == 1 else len(branches) // 2
    return pl.pallas_call(
        functools.partial(_outproj_kernel, d, n_branch), grid=(bx, s // tm),
        in_specs=[row, _row_specs(bx, s, tm, mod), _full((1, d)), _full(w.shape)] + [row] * len(branches),
        out_specs=row, out_shape=jax.ShapeDtypeStruct((bx, s, d), F32),
        compiler_params=_cparams("arbitrary", "arbitrary"), name="attn_out_proj")(x, mod, g1, w, *branches)


def _gelu_tanh(x):
    return x * (0.5 * (1.0 + jnp.tanh(math.sqrt(2.0 / math.pi) * (x + 0.044715 * (x * x * x)))))


def _compress_kernel(tmb, x_ref, pe_ref, w1_ref, w2_ref, o_ref):
    acc = jnp.zeros((tmb, LANES), F32)
    for r in range(CMP_BLOCK):
        xr = x_ref[pl.ds(r, tmb, stride=CMP_BLOCK), :] + pe_ref[0, r]
        acc = acc + _dot(xr.astype(BF16), w1_ref[0, r])
    o_ref[...] = _dot(_gelu_tanh(acc).astype(BF16), w2_ref[0])


def _compress_call(rows2d, pe2, w1bd, w2bd):
    m = rows2d.shape[0] // CMP_BLOCK
    tmb = max(t for t in range(8, min(256, m) + 1, 8) if m % t == 0)
    return pl.pallas_call(
        functools.partial(_compress_kernel, tmb), grid=(m // tmb, 4),
        in_specs=[pl.BlockSpec((tmb * CMP_BLOCK, LANES), lambda i, j: (i, j)),
                  pl.BlockSpec((1, CMP_BLOCK, 1, LANES), lambda i, j: (j // 2, 0, 0, 0)),
                  pl.BlockSpec((1, CMP_BLOCK, LANES, LANES), lambda i, j: (j // 2, 0, 0, 0)),
                  pl.BlockSpec((1, LANES, LANES), lambda i, j: (j // 2, 0, 0))],
        out_specs=pl.BlockSpec((tmb, LANES), lambda i, j: (i, j)),
        out_shape=jax.ShapeDtypeStruct((m, 4 * LANES), F32),
        compiler_params=_cparams("arbitrary", "arbitrary"), name="compress")(rows2d, pe2, w1bd, w2bd)


def _compress_weights(cmp_pe, cmp_w1, cmp_w2):
    z = jnp.zeros((2, CMP_BLOCK, HEAD_DIM, HEAD_DIM), F32)
    w1 = cmp_w1.reshape(2, CMP_BLOCK, HEAD_DIM, HEAD_DIM)
    w1bd = jnp.concatenate([jnp.concatenate([w1, z], -1), jnp.concatenate([z, w1], -1)], -2).astype(BF16)
    z2 = jnp.zeros((2, HEAD_DIM, HEAD_DIM), F32)
    w2bd = jnp.concatenate([jnp.concatenate([cmp_w2, z2], -1), jnp.concatenate([z2, cmp_w2], -1)], -2).astype(BF16)
    pe2 = jnp.concatenate([cmp_pe, cmp_pe], -1)[:, :, None, :]
    return pe2, w1bd, w2bd


N_PAGE_CB = PAGE_SIZE // CMP_BLOCK
CB_ROWS = 8


PAGE_PITCH = N_KV * HEAD_DIM + 4


def _compress_pool_kernel(n_pg, n_i, x_hbm, pe_ref, mw_ref, w2_ref, o_ref, xbuf, sem):
    gw = N_KV * HEAD_DIM
    t = pl.program_id(0)
    n_t = pl.num_programs(0)
    slot_rows = n_pg * PAGE_PITCH

    def page_copy(step, slot, p):
        return pltpu.make_async_copy(x_hbm.at[(step % n_i) * n_pg + p, step // n_i],
                                     xbuf.at[pl.ds(slot * slot_rows + p * PAGE_PITCH, gw), :], sem.at[slot])

    def start_all(step, slot):
        for p in range(n_pg):
            page_copy(step, slot, p).start()

    slot = t % 2

    @pl.when(t == 0)
    def _():
        start_all(t, slot)

    @pl.when(t + 1 < n_t)
    def _():
        start_all(t + 1, 1 - slot)
    for p in range(n_pg):
        page_copy(t, slot, p).wait()

    def rows_of(g, d):
        return xbuf[pl.ds(slot * slot_rows + g * HEAD_DIM + d, n_pg, stride=PAGE_PITCH), :]
    lhs = jnp.concatenate(
        [(jnp.concatenate([rows_of(g, d) for g in range(N_KV)], axis=0) + pe_ref[0, d]).astype(BF16)
         for d in range(HEAD_DIM)], axis=1)
    z = _dot(_gelu_tanh(_dot(lhs, mw_ref[0])).astype(BF16), w2_ref[0])
    lane = lax.broadcasted_iota(jnp.int32, (n_pg, LANES), 1)
    low = lane < HEAD_DIM
    o_ref[...] = jnp.zeros(o_ref.shape, F32)
    for n in range(N_PAGE_CB):
        for c in range(N_KV // 2):
            a = z[(2 * c) * n_pg:(2 * c + 1) * n_pg, (n // 2) * LANES:(n // 2 + 1) * LANES]
            b = z[(2 * c + 1) * n_pg:(2 * c + 2) * n_pg, (n // 2) * LANES:(n // 2 + 1) * LANES]
            if n % 2 == 0:
                b = pltpu.roll(b, HEAD_DIM, 1)
            else:
                a = pltpu.roll(a, HEAD_DIM, 1)
            o_ref[c, pl.ds(n, n_pg, stride=CB_ROWS), :] = jnp.where(low, a, b)


def _compress_pool_call(cache_t, pe_t, mw, w2bd4):
    n_pool = cache_t.shape[0]
    n_pg = max(t for t in range(8, min(64, n_pool) + 1, 8) if n_pool % t == 0)
    n_i = n_pool // n_pg
    return pl.pallas_call(
        functools.partial(_compress_pool_kernel, n_pg, n_i), grid=(2 * n_i,),
        in_specs=[pl.BlockSpec(memory_space=pl.ANY),
                  pl.BlockSpec((1, HEAD_DIM, 1, PAGE_SIZE), lambda t: (t // n_i, 0, 0, 0)),
                  pl.BlockSpec((1, HEAD_DIM * PAGE_SIZE, N_PAGE_CB * HEAD_DIM), lambda t: (t // n_i, 0, 0)),
                  pl.BlockSpec((1, N_PAGE_CB * HEAD_DIM, N_PAGE_CB * HEAD_DIM), lambda t: (t // n_i, 0, 0))],
        out_specs=pl.BlockSpec((N_KV // 2, n_pg * CB_ROWS, LANES), lambda t: (t // n_i, t % n_i, 0)),
        out_shape=jax.ShapeDtypeStruct((N_KV, n_pool * CB_ROWS, LANES), F32),
        scratch_shapes=[pltpu.VMEM((2 * n_pg * PAGE_PITCH, PAGE_SIZE), F32), pltpu.SemaphoreType.DMA((2,))],
        compiler_params=_cparams("arbitrary"), name="compress_pool")(cache_t, pe_t, mw, w2bd4)


def _compress_pool_weights(cmp_pe, cmp_w1, cmp_w2):
    eye = np.eye(N_PAGE_CB, dtype=bool)
    w1t = cmp_w1.reshape(2, CMP_BLOCK, HEAD_DIM, HEAD_DIM).transpose(0, 2, 1, 3)
    mw = jnp.where(eye[None, None, :, None, :, None], w1t[:, :, None, :, None, :], 0.0)
    mw = mw.reshape(2, HEAD_DIM * PAGE_SIZE, N_PAGE_CB * HEAD_DIM).astype(BF16)
    w2bd4 = jnp.where(eye[None, :, None, :, None], cmp_w2[:, None, :, None, :], 0.0)
    w2bd4 = w2bd4.reshape(2, N_PAGE_CB * HEAD_DIM, N_PAGE_CB * HEAD_DIM).astype(BF16)
    pe_t = jnp.tile(cmp_pe.transpose(0, 2, 1), (1, 1, N_PAGE_CB))[:, :, None, :]
    return pe_t, mw, w2bd4


def _bias_of_distance(dist, tab):
    val = jnp.zeros(dist.shape, F32) + tab(0)
    for k in range(1, N_BUCKETS):
        val = jnp.where(dist >= BUCKET_THR[k], tab(k), val)
    return val


def _toeplitz_kernel(window, tab_ref, o_ref):
    g = pl.program_id(0)
    dt = pl.program_id(1)
    i = lax.broadcasted_iota(jnp.int32, (TQ, TK), 0)
    j = lax.broadcasted_iota(jnp.int32, (TQ, TK), 1)
    dist = dt * TQ + i - j
    for hh in range(HPG):
        head = g * HPG + hh
        val = _bias_of_distance(dist, lambda k, head=head: tab_ref[k, head])
        val = jnp.where(dist < 0, NEG_MASK, val)
        if window is not None:
            val = jnp.where(dist >= window, NEG_MASK, val)
        o_ref[0, 0, hh * TQ:(hh + 1) * TQ, :] = val


def _toeplitz_call(rel_bias, n_chunks, window):
    return pl.pallas_call(
        functools.partial(_toeplitz_kernel, window), grid=(N_KV, n_chunks),
        in_specs=[pl.BlockSpec(memory_space=pltpu.SMEM)],
        out_specs=pl.BlockSpec((1, 1, HPG * TQ, TK), lambda g, t: (g, t, 0, 0)),
        out_shape=jax.ShapeDtypeStruct((N_KV, n_chunks, HPG * TQ, TK), F32),
        compiler_params=_cparams("arbitrary", "arbitrary"), name="bias_toeplitz")(rel_bias)


def _cmp_bias_kernel(tab_ref, o_ref):
    g = pl.program_id(0)
    qi = pl.program_id(1)
    row = lax.broadcasted_iota(jnp.int32, (LANES, TQ), 0)
    t_q = qi * TQ + lax.broadcasted_iota(jnp.int32, (LANES, TQ), 1)
    cblk = 2 * (row % HEAD_DIM) + row // HEAD_DIM
    dist = t_q - (cblk * CMP_BLOCK + (CMP_BLOCK - 1))
    for hh in range(HPG):
        head = g * HPG + hh
        val = _bias_of_distance(dist, lambda k, head=head: tab_ref[k, head])
        o_ref[0, 0, :, hh * TQ:(hh + 1) * TQ] = jnp.where(dist < 0, NEG_MASK, val)


def _cmp_bias_call(rel_bias, n_qt):
    return pl.pallas_call(
        _cmp_bias_kernel, grid=(N_KV, n_qt),
        in_specs=[pl.BlockSpec(memory_space=pltpu.SMEM)],
        out_specs=pl.BlockSpec((1, 1, LANES, HPG * TQ), lambda g, t: (g, t, 0, 0)),
        out_shape=jax.ShapeDtypeStruct((N_KV, n_qt, LANES, HPG * TQ), F32),
        compiler_params=_cparams("arbitrary", "arbitrary"), name="bias_cmp")(rel_bias)


def _bias_cols_kernel(dist_ref, tab_ref, o_ref):
    dist = dist_ref[...]
    val = _bias_of_distance(dist, lambda k: tab_ref[:, k:k + 1])
    o_ref[...] = jnp.where(dist < 0, NEG_MASK, val)


def _bias_cols_call(dist, tab_heads):
    r = dist.shape[1]
    return pl.pallas_call(
        _bias_cols_kernel, grid=(1,),
        in_specs=[pl.BlockSpec((N_HEADS, r), lambda i: (0, 0)), pl.BlockSpec((N_HEADS, N_BUCKETS), lambda i: (0, 0))],
        out_specs=pl.BlockSpec((N_HEADS, r), lambda i: (0, 0)),
        out_shape=jax.ShapeDtypeStruct((N_HEADS, r), F32),
        compiler_params=_cparams("arbitrary"), name="bias_cols")(dist, tab_heads)


def _kv_prep_kernel(ksel_ref, vsel_ref, kwin_ref, vwin_ref, kaug_ref, v1_ref, kw_ref, vw1_ref):
    g_odd = (pl.program_id(1) % 2) == 1
    ch = ksel_ref.shape[1]
    lane = lax.broadcasted_iota(jnp.int32, (ch, LANES), 1)
    row = pl.program_id(2) * ch + lax.broadcasted_iota(jnp.int32, (ch, LANES), 0)
    low = lane < HEAD_DIM
    ones = jnp.where(lane == HEAD_DIM, 1.0, 0.0)
    onehot = jnp.where(lane - HEAD_DIM == row // SEL_BLOCK, 1.0, 0.0)

    def pick(ref):
        x = ref[0]
        return jnp.where(g_odd, pltpu.roll(x, HEAD_DIM, 1), x)

    kaug_ref[0, 0] = jnp.where(low, pick(ksel_ref), onehot).astype(BF16)
    v1_ref[0, 0] = jnp.where(low, pick(vsel_ref), ones).astype(BF16)
    kw_ref[0, 0] = jnp.where(low, pick(kwin_ref), 0.0).astype(BF16)
    vw1_ref[0, 0] = jnp.where(low, pick(vwin_ref), ones).astype(BF16)


def _kv_prep_call(rows, win):
    b, s, _ = rows.shape
    ch = min(1024, s)
    kv_spec = lambda col0: pl.BlockSpec((1, ch, LANES), lambda bb, g, c, _c=col0: (bb, c, _c + g // 2))
    o_spec = pl.BlockSpec((1, 1, ch, LANES), lambda bb, g, c: (bb, g, c, 0))
    o_shape = jax.ShapeDtypeStruct((b, N_KV, s, LANES), BF16)
    return pl.pallas_call(
        _kv_prep_kernel, grid=(b, N_KV, s // ch),
        in_specs=[kv_spec(4), kv_spec(6), kv_spec(0), kv_spec(2)],
        out_specs=[o_spec] * 4, out_shape=[o_shape] * 4,
        compiler_params=_cparams("arbitrary", "arbitrary", "arbitrary"), name="kv_prep")(rows, rows, win, win)


def _nsa_seq_kernel(n_sb, q_ref, gate_ref, kaug_ref, v1_ref, kw_ref, vw1_ref, kcb_ref, vcb_ref, tcmp_ref,
                    tsel_ref, twin_ref, o_ref, qaug, qw_sc, s_sc, mrun, mb, acc_sel, acc_win, score_sc):
    qi = pl.program_id(2)
    rows = HPG * TQ
    lane = lax.broadcasted_iota(jnp.int32, (TQ, LANES), 1)
    low = lane < HEAD_DIM

    qh = []
    for hh in range(HPG):
        qv = q_ref[0, :, (hh // 2) * LANES:(hh // 2 + 1) * LANES]
        if hh % 2 == 1:
            qv = pltpu.roll(qv, HEAD_DIM, 1)
        qh.append(jnp.where(low, qv, 0.0))
    qw = jnp.concatenate(qh, axis=0).astype(BF16)
    qw_sc[...] = qw

    s_c = _dot_nt(kcb_ref[0, 0].astype(BF16), qw) + tcmp_ref[0, 0]
    ok_c = s_c > 0.5 * NEG_MASK
    mx = jnp.max(s_c, axis=0, keepdims=True)
    p = jnp.where(ok_c, jnp.exp(s_c - mx), 0.0)
    den = jnp.sum(p, axis=0, keepdims=True)
    pn_t = p / jnp.where(den > 0, den, 1.0)
    o_c = _dot(pn_t.T.astype(BF16), vcb_ref[0, 0].astype(BF16))
    imp = pn_t[:, 0:TQ]
    for hh in range(1, HPG):
        imp = imp + pn_t[:, hh * TQ:(hh + 1) * TQ]
    n_blk = LANES // 2
    imp = imp[:n_blk] + imp[n_blk:]

    blk = lax.broadcasted_iota(jnp.int32, (n_blk, TQ), 0)
    cur = (qi * TQ + lax.broadcasted_iota(jnp.int32, (n_blk, TQ), 1)) // SEL_BLOCK
    valid = blk <= cur
    forced = (blk == 0) | (blk == cur) | (blk == cur - 1)
    score_sc[...] = jnp.where(valid & forced, BIG_SCORE, jnp.where(valid, imp, -BIG_SCORE))
    sub = 8
    groups = [score_sc[r0:r0 + sub, :] for r0 in range(0, n_blk, sub)]
    rowg = lax.broadcasted_iota(jnp.int32, (sub, TQ), 0)
    cnts = [jnp.zeros((sub, TQ), jnp.int32) for _ in groups]
    for bp in range(n_blk):
        r = score_sc[bp:bp + 1, :]
        for gi, s_g in enumerate(groups):
            if gi * sub > bp:
                ahead = r >= s_g
            elif (gi + 1) * sub - 1 < bp:
                ahead = r > s_g
            else:
                ahead = (r > s_g) | ((r == s_g) & (rowg + gi * sub > bp))
            cnts[gi] = cnts[gi] + jnp.where(ahead, 1, 0)
    cnt = jnp.concatenate(cnts, axis=0)
    selmask_t = jnp.where((cnt < min(TOP_N, n_sb)) & valid, 0.0, NEG_SEL)
    selmask = jnp.concatenate([jnp.zeros((n_blk, TQ), F32), selmask_t], axis=0).T
    for hh in range(HPG):
        qaug[hh * TQ:(hh + 1) * TQ, :] = jnp.where(low, qh[hh], selmask).astype(BF16)

    def branch(q_sc, k_ref, v_ref, t_ref, acc_sc, n_tiles, n_tbl):
        mrun[...] = jnp.full(mrun.shape, NEG_MASK, F32)

        def tile_loop(step):
            def pair(j, _):
                step((2 * j, 2 * j + 1))
                return 0
            lax.fori_loop(0, n_tiles // 2, pair, 0)

            def last(j, _):
                step((n_tiles - 1,))
                return 0
            lax.fori_loop(0, n_tiles % 2, last, 0)

        def scores(tiles):
            m = mrun[...]
            for i in tiles:
                r0 = pl.multiple_of((qi - i) * TK, TK)
                s = _dot_nt(q_sc[...], k_ref[0, 0, pl.ds(r0, TK), :]) + t_ref[0, jnp.minimum(i, n_tbl)]
                s_sc[i] = s
                for c in range(TK // LANES):
                    m = jnp.maximum(m, s[:, c * LANES:(c + 1) * LANES])
            mrun[...] = m
        tile_loop(scores)
        mb[...] = jnp.broadcast_to(jnp.max(mrun[...], axis=1, keepdims=True), (rows, LANES))
        acc_sc[...] = jnp.zeros(acc_sc.shape, F32)

        def weigh(tiles):
            mbv = mb[...]
            mb2 = jnp.concatenate([mbv] * (TK // LANES), axis=1)
            acc = acc_sc[...]
            for i in tiles:
                r0 = pl.multiple_of((qi - i) * TK, TK)
                acc = acc + _dot(jnp.exp(s_sc[i] - mb2).astype(BF16), v_ref[0, 0, pl.ds(r0, TK), :])
            acc_sc[...] = acc
        tile_loop(weigh)
        return acc_sc[...]

    acc_s = branch(qaug, kaug_ref, v1_ref, tsel_ref, acc_sel, qi + 1, N_TBL)
    acc_w = branch(qw_sc, kw_ref, vw1_ref, twin_ref, acc_win, jnp.minimum(qi, N_WIN_TILES - 1) + 1, N_WIN_TILES - 1)

    outs = []
    for hh in range(HPG):
        gts = [gate_ref[0, :, hh * 3 + br:hh * 3 + br + 1] for br in range(3)]
        rs = slice(hh * TQ, (hh + 1) * TQ)
        a_s, a_w = acc_s[rs], acc_w[rs]
        outs.append(gts[0] * o_c[rs] + (gts[1] / a_s[:, HEAD_DIM:HEAD_DIM + 1]) * a_s
                    + (gts[2] / a_w[:, HEAD_DIM:HEAD_DIM + 1]) * a_w)
    for c in range(HPG // 2):
        o_ref[0, :, c * LANES:(c + 1) * LANES] = jnp.where(low, outs[2 * c], pltpu.roll(outs[2 * c + 1], HEAD_DIM, 1))


def _nsa_seq_call(q, gates, kaug, v1, kw, vw1, kcb, vcb, tcmp, tsel, twin):
    b, s, _ = q.shape
    n_qt = s // TQ
    rows = HPG * TQ
    kv_spec = pl.BlockSpec((1, 1, s, LANES), lambda bb, g, i: (bb, g, 0, 0))
    cb_spec = pl.BlockSpec((1, 1, kcb.shape[2], LANES), lambda bb, g, i: (bb, g, 0, 0))
    tbl_spec = lambda t: pl.BlockSpec((1,) + t.shape[1:], lambda bb, g, i: (g, 0, 0, 0))
    return pl.pallas_call(
        functools.partial(_nsa_seq_kernel, s // SEL_BLOCK), grid=(b, N_KV, n_qt),
        in_specs=[pl.BlockSpec((1, TQ, HPG * HEAD_DIM), lambda bb, g, i: (bb, i, g)),
                  pl.BlockSpec((1, TQ, LANES), lambda bb, g, i: (bb, i, g)),
                  kv_spec, kv_spec, kv_spec, kv_spec, cb_spec, cb_spec,
                  pl.BlockSpec((1, 1, LANES, rows), lambda bb, g, i: (g, i, 0, 0)), tbl_spec(tsel), tbl_spec(twin)],
        out_specs=pl.BlockSpec((1, TQ, HPG * HEAD_DIM), lambda bb, g, i: (bb, i, g)),
        out_shape=jax.ShapeDtypeStruct((b, s, N_HEADS * HEAD_DIM), F32),
        scratch_shapes=[pltpu.VMEM((rows, LANES), BF16), pltpu.VMEM((rows, LANES), BF16),
                        pltpu.VMEM((s // TK, rows, TK), F32), pltpu.VMEM((rows, LANES), F32),
                        pltpu.VMEM((rows, LANES), F32), pltpu.VMEM((rows, LANES), F32),
                        pltpu.VMEM((rows, LANES), F32), pltpu.VMEM((LANES // 2, TQ), F32)],
        compiler_params=_cparams("arbitrary", "arbitrary", "arbitrary"), name="nsa_seq")(
            q, gates, kaug, v1, kw, vw1, kcb, vcb, tcmp, tsel, twin)


def _softmax_lanes(s):
    p = jnp.exp(s - jnp.max(s, axis=1, keepdims=True))
    return p / jnp.sum(p, axis=1, keepdims=True)


def _rows_to_heads(rows):
    hg = lax.broadcasted_iota(jnp.int32, (N_HEADS, rows[0].shape[1]), 0) // HPG
    out = jnp.broadcast_to(rows[0], hg.shape)
    for g in range(1, N_KV):
        out = jnp.where(hg == g, rows[g], out)
    return out


def _nsa_stepT_kernel(n_pages, pt_ref, *refs):
    del pt_ref
    q_ref = refs[0]
    cb_refs = refs[1:1 + n_pages]
    pg_refs = refs[1 + n_pages:1 + 2 * n_pages]
    (kvnew_ref, winnew_ref, wcol_ref, cwin_ref, bsel_ref, bnew_ref, bwin_ref, bcmp_ref) = refs[1 + 2 * n_pages:9 + 2 * n_pages]
    oc_ref, os_ref, ow_ref, nwin_ref = refs[9 + 2 * n_pages:13 + 2 * n_pages]
    (s_sc,) = refs[13 + 2 * n_pages:]
    b = pl.program_id(0)
    gw = N_KV * HEAD_DIM
    n_past = n_pages * PAGE_SIZE
    q16 = q_ref[0].astype(BF16)
    qf = q16.astype(F32)
    lane = lax.broadcasted_iota(jnp.int32, (N_HEADS, LANES), 1)

    n_pad = LANES - CB_ROWS * n_pages
    cb = jnp.concatenate([jnp.concatenate([r[c] for c in range(N_KV)], axis=1) for r in cb_refs]
                         + ([jnp.zeros((n_pad, 2 * gw), F32)] if n_pad else []), axis=0)
    pn_c = _softmax_lanes(_dot_nt(q16, cb[:, :gw].astype(BF16)) + bcmp_ref[...])
    oc_ref[0] = _dot(pn_c.astype(BF16), cb[:, gw:].astype(BF16))

    grp = [pn_c[HPG * g:HPG * g + 1] + pn_c[HPG * g + 1:HPG * g + 2] + pn_c[HPG * g + 2:HPG * g + 3]
           + pn_c[HPG * g + 3:HPG * g + 4] for g in range(N_KV)]
    lane8 = lax.broadcasted_iota(jnp.int32, (8, LANES), 1)
    row8 = lax.broadcasted_iota(jnp.int32, (8, LANES), 0)
    imp = jnp.zeros((8, LANES), F32)
    for g in range(N_KV):
        imp = jnp.where(row8 == g, grp[g], imp)
    imp = imp + pltpu.roll(imp, LANES - 1, 1)
    n_past_blk = n_past // SEL_BLOCK
    cur_lane = LANES - 1
    is_blk = ((lane8 % CB_ROWS == 0) | (lane8 % CB_ROWS == 2))
    last_lane = ((n_past_blk - 1) // 2) * CB_ROWS + 2 * ((n_past_blk - 1) % 2)
    forced = (lane8 == 0) | (lane8 == last_lane) | (lane8 == cur_lane)
    valid = is_blk | (lane8 == cur_lane)
    score = jnp.where(valid & forced, BIG_SCORE, jnp.where(valid, imp, -BIG_SCORE))
    cnt = jnp.zeros((8, LANES), jnp.int32)
    for k in range(1, LANES):
        r = pltpu.roll(score, k, 1)
        cnt = cnt + jnp.where((r > score) | ((r == score) & (lane8 >= k)), 1, 0)
    selrows = jnp.where((cnt < min(TOP_N, n_past_blk + 1)) & valid, 0.0, NEG_SEL)
    selmask = _rows_to_heads([selrows[g:g + 1] for g in range(N_KV)])

    blocks_per_page = PAGE_SIZE // SEL_BLOCK
    for p in range(n_pages):
        kt = pg_refs[p][0, 0:gw, :].astype(BF16)
        msk = selmask[:, CB_ROWS * p:CB_ROWS * p + 1]
        for i in range(1, blocks_per_page):
            msk = jnp.where(lane >= i * SEL_BLOCK, selmask[:, CB_ROWS * p + 2 * i:CB_ROWS * p + 2 * i + 1], msk)
        s_sc[:, p * PAGE_SIZE:(p + 1) * PAGE_SIZE] = _dot(q16, kt) + bsel_ref[:, p * PAGE_SIZE:(p + 1) * PAGE_SIZE] + msk
    knew = kvnew_ref[0][:, 2 * gw:3 * gw].astype(BF16).astype(F32)
    s_new = jnp.sum(qf * knew, axis=1, keepdims=True) + bnew_ref[:, 0:1] + selmask[:, cur_lane:cur_lane + 1]
    s_sc[:, n_past:n_past + LANES] = jnp.where(lane == 0, s_new, NEG_MASK)
    s_all = s_sc[...]
    mx = jnp.max(s_all, axis=1, keepdims=True)
    den = jnp.sum(jnp.exp(s_all - mx), axis=1, keepdims=True)
    acc = jnp.zeros((N_HEADS, gw), F32)
    for p in range(n_pages):
        pn = jnp.exp(s_sc[:, p * PAGE_SIZE:(p + 1) * PAGE_SIZE] - mx) / den
        acc = acc + _dot_nt(pn.astype(BF16), pg_refs[p][0, gw:2 * gw, :].astype(BF16))
    pn_new = jnp.exp(s_new - mx) / den
    vnew = kvnew_ref[0][:, 3 * gw:4 * gw].astype(BF16).astype(F32)
    os_ref[0] = acc + pn_new.astype(BF16).astype(F32) * vnew

    cw = cwin_ref[0]
    w_len = cw.shape[1]
    s_w = _dot(q16, cw[0:gw].astype(BF16)) + bwin_ref[...]
    kwn = winnew_ref[0][:, 0:gw].astype(BF16).astype(F32)
    s_wn = jnp.sum(qf * kwn, axis=1, keepdims=True) + bnew_ref[:, 0:1]
    mxw = jnp.maximum(jnp.max(s_w, axis=1, keepdims=True), s_wn)
    pw = jnp.exp(s_w - mxw)
    pwn = jnp.exp(s_wn - mxw)
    denw = jnp.sum(pw, axis=1, keepdims=True) + pwn
    vwn = winnew_ref[0][:, gw:2 * gw].astype(BF16).astype(F32)
    ow_ref[0] = _dot_nt((pw / denw).astype(BF16), cw[gw:2 * gw].astype(BF16)) + (pwn / denw).astype(BF16).astype(F32) * vwn
    lane_w = lax.broadcasted_iota(jnp.int32, cw.shape, 1)
    nb_l = wcol_ref.shape[1]
    lane_b = lax.broadcasted_iota(jnp.int32, (cw.shape[0], nb_l), 1)
    col = jnp.sum(jnp.where(lane_b == b, wcol_ref[...], 0.0), axis=1, keepdims=True)
    nwin_ref[0] = jnp.where(lane_w == w_len - 1, col, pltpu.roll(cw, w_len - 1, 1))


def _nsa_step_call(page_table, qrows, cb_pool, cache_t, kvnew, winnew, wcol, cwin_t, bsel, bnew, bwin, bcmp):
    nb, n_pages = page_table.shape
    gw = N_KV * HEAD_DIM
    w_len = cwin_t.shape[2]
    cb_specs = [pl.BlockSpec((N_KV, None, CB_ROWS, LANES), lambda b, pt, _p=p: (0, pt[b, _p], 0, 0))
                for p in range(n_pages)]
    pg_specs = [pl.BlockSpec((1, 2 * gw, PAGE_SIZE), lambda b, pt, _p=p: (pt[b, _p], 1, 0)) for p in range(n_pages)]
    const = lambda a: pl.BlockSpec(a.shape, lambda b, pt: (0, 0))
    o_spec = pl.BlockSpec((1, N_HEADS, gw), lambda b, pt: (b, 0, 0))
    win_spec = pl.BlockSpec((1, 2 * gw, w_len), lambda b, pt: (b, 0, 0))
    grid_spec = pltpu.PrefetchScalarGridSpec(
        num_scalar_prefetch=1, grid=(nb,),
        in_specs=[pl.BlockSpec((1, N_HEADS, gw), lambda b, pt: (b, 0, 0))] + cb_specs + pg_specs + [
            pl.BlockSpec((1, 1, 4 * gw), lambda b, pt: (b, 0, 0)),
            pl.BlockSpec((1, 1, 2 * gw), lambda b, pt: (b, 0, 0)),
            const(wcol), win_spec, const(bsel), const(bnew), const(bwin), const(bcmp)],
        out_specs=[o_spec, o_spec, o_spec, win_spec],
        scratch_shapes=[pltpu.VMEM((N_HEADS, n_pages * PAGE_SIZE + LANES), F32)])
    o_shape = jax.ShapeDtypeStruct((nb, N_HEADS, gw), F32)
    return pl.pallas_call(
        functools.partial(_nsa_stepT_kernel, n_pages), grid_spec=grid_spec,
        out_shape=[o_shape, o_shape, o_shape, jax.ShapeDtypeStruct((nb, 2 * gw, w_len), F32)],
        compiler_params=_cparams("arbitrary"), name="nsa_step")(
            page_table, qrows, *([cb_pool] * n_pages), *([cache_t] * n_pages), kvnew, winnew, wcol, cwin_t,
            bsel, bnew, bwin, bcmp)


def _head_diag(o):
    b = o.shape[0]
    o5 = o.reshape(b, N_KV, HPG, N_KV, HEAD_DIM)
    return jnp.stack([o5[:, g, :, g, :] for g in range(N_KV)], axis=1).reshape(b, N_HEADS * HEAD_DIM)


def kernel(x_prompt, x_sample, c_prompt, c_sample, cache_kv, cache_win, state_conv_a, state_ffn_conv, page_table, mod_w, mod_b, norm_g, a_w_in, a_conv_w, a_conv_b, a_w_out, kv_mod_w, kv_mod_b, kv_norm_g, w_kv, cmp_pe, cmp_w1, cmp_w2, b_w_qg, b_w_out, rel_bias, ffn_w_up, ffn_conv_w, ffn_conv_b, ffn_w_down):
    bp, s, d = x_prompt.shape
    bs = x_sample.shape[0]
    depth = mod_w.shape[0]
    n_a = a_w_in.shape[0]
    assert depth == 2 and n_a == 1 and x_sample.shape[1] == 1
    dff = ffn_w_down.shape[1]
    n_pool = cache_kv.shape[0]
    n_pages = page_table.shape[1]
    past_len = n_pages * PAGE_SIZE
    gw = N_KV * HEAD_DIM
    nq = N_HEADS * HEAD_DIM

    n_c = bp + bs
    n_cp = -(-n_c // 8) * 8
    c_all = jnp.pad(jnp.concatenate([c_prompt, c_sample], 0), ((0, n_cp - n_c), (0, 0)))
    mods = _mod_call(c_all, mod_w.reshape(depth * 2, d, 3 * d), mod_b.reshape(depth * 2, 1, 3 * d))
    modkv = _mod_call(c_all, kv_mod_w[None], kv_mod_b[None, None])[0]
    mod_p = lambda i: mods[i, :bp][:, None, :]
    mod_s = lambda i: mods[i, bp:n_c][None]

    w_in = a_w_in[0].astype(BF16)
    w_out_a = a_w_out[0].astype(BF16)
    w_up = ffn_w_up.astype(BF16)
    w_dn = ffn_w_down.astype(BF16)
    w_kv_b = w_kv.astype(BF16)
    ng_pad = -(-(b_w_qg.shape[2] - nq) // LANES) * LANES
    w_qg = jnp.pad(b_w_qg[0], ((0, 0), (0, nq + ng_pad - b_w_qg.shape[2]))).astype(BF16)
    w_out_b = b_w_out[0].astype(BF16)
    g = lambda l, i: norm_g[l, i][None]
    pe2, w1bd, w2bd = _compress_weights(cmp_pe, cmp_w1, cmp_w2)
    n_chunks = 2

    tm = min(256, s)
    x1, st_a = _mixer_call(x_prompt, mod_p(0), g(0, 0), g(0, 1), w_in, a_conv_w[0], a_conv_b[0][None], w_out_a, None, tm)
    x2, st_f0 = _ffn_call(x1, mod_p(1), g(0, 2), g(0, 3), w_up[0], ffn_conv_w[0], ffn_conv_b[0][None], w_dn[0], None, tm, n_chunks)
    rows, win, q, gates = _kvq_call(x2, modkv[:bp][:, None, :], mod_p(2), kv_norm_g[None], g(1, 0), w_kv_b, w_qg, tm)
    cb = _compress_call(rows.reshape(bp * s, 4 * gw), pe2, w1bd, w2bd)
    n_cbk = s // CMP_BLOCK
    assert n_cbk <= LANES
    cb = jnp.pad(cb.reshape(bp, n_cbk, 2, N_KV, HEAD_DIM), ((0, 0), (0, LANES - n_cbk), (0, 0), (0, 0), (0, 0)))
    cb = cb.reshape(bp, LANES // 2, 2, 2, N_KV, HEAD_DIM)
    cb = cb.transpose(3, 0, 4, 2, 1, 5).reshape(2, bp, N_KV, LANES, HEAD_DIM)
    cb = jnp.pad(cb, ((0, 0),) * 4 + ((0, LANES - HEAD_DIM),))
    gates_g = jnp.pad(gates[:, :, :N_HEADS * 3].reshape(bp, s, N_KV, HPG * 3),
                      ((0, 0), (0, 0), (0, 0), (0, LANES - HPG * 3))).reshape(bp, s, N_KV * LANES)
    tsel = _toeplitz_call(rel_bias, N_TBL + 1, None)
    twin = _toeplitz_call(rel_bias, N_WIN_TILES, WINDOW)
    tcmp = _cmp_bias_call(rel_bias, s // TQ)
    kaug, v1, kw, vw1 = _kv_prep_call(rows, win)
    o_att = _nsa_seq_call(q, gates_g, kaug, v1, kw, vw1, cb[0], cb[1], tcmp, tsel, twin)
    x3 = _outproj_call(x2, mod_p(2), g(1, 1), w_out_b, [o_att], tm)
    y_prompt, st_f1 = _ffn_call(x3, mod_p(3), g(1, 2), g(1, 3), w_up[1], ffn_conv_w[1], ffn_conv_b[1][None], w_dn[1], None, tm, n_chunks)
    kv_p = rows.reshape(bp, s, 4, N_KV, HEAD_DIM)
    keep = min(WINDOW, s)
    win_p = win[:, s - keep:].reshape(bp, keep, 2, N_KV, HEAD_DIM)
    conv_a_p = st_a[None, :, 6:8]
    ffn_p = jnp.stack([st_f0[:, 6:8], st_f1[:, 6:8]])

    xs = x_sample.reshape(1, bs, d)
    prev_a = (state_conv_a[0, :, 0][None], state_conv_a[0, :, 1][None])
    xs1, v_a = _mixer_call(xs, mod_s(0), g(0, 0), g(0, 1), w_in, a_conv_w[0], a_conv_b[0][None], w_out_a, prev_a, bs)
    prev_f = lambda l: (state_ffn_conv[l, :, 0][None], state_ffn_conv[l, :, 1][None])
    xs2, up0 = _ffn_call(xs1, mod_s(1), g(0, 2), g(0, 3), w_up[0], ffn_conv_w[0], ffn_conv_b[0][None], w_dn[0], prev_f(0), bs, n_chunks)
    rows_s, win_s, q_s, gates_s = _kvq_call(xs2, modkv[bp:n_c][None], mod_s(2), kv_norm_g[None], g(1, 0), w_kv_b, w_qg, bs)
    cache_t = cache_kv.transpose(0, 2, 3, 4, 1).reshape(n_pool, 4 * gw, PAGE_SIZE)
    w_len = cache_win.shape[1]
    cwin_t = cache_win.transpose(0, 2, 3, 4, 1).reshape(bs, 2 * gw, w_len)
    cb_pool = _compress_pool_call(cache_t.reshape(n_pool, 4, gw, PAGE_SIZE),
                                  *_compress_pool_weights(cmp_pe, cmp_w1, cmp_w2))
    cb_pool = cb_pool.reshape(N_KV, n_pool, CB_ROWS, LANES)
    assert n_pages * CB_ROWS <= LANES
    d_sel = past_len - np.arange(past_len)
    d_new = np.where(np.arange(LANES) == 0, 0, -1)
    d_win = w_len - np.arange(w_len)
    d_win = np.where(d_win < WINDOW, d_win, -1)
    cl = np.arange(LANES)
    d_cmp = past_len - (((cl // CB_ROWS) * N_PAGE_CB + cl % CB_ROWS) * CMP_BLOCK + CMP_BLOCK - 1)
    d_cmp = np.where((cl % CB_ROWS < N_PAGE_CB) & (cl // CB_ROWS < n_pages), d_cmp, -1)
    assert d_sel.min() >= 0 and d_cmp[d_cmp != -1].min() >= 0
    dist = np.concatenate([d_sel, d_new, d_win, d_cmp]).astype(np.int32)
    bias_cols = _bias_cols_call(jnp.asarray(np.repeat(dist[None, :], N_HEADS, 0)), rel_bias.T)
    bsel, bnew = bias_cols[:, :past_len], bias_cols[:, past_len:past_len + LANES]
    bwin = bias_cols[:, past_len + LANES:past_len + LANES + w_len]
    bcmp = bias_cols[:, past_len + LANES + w_len:]
    head_group = (np.arange(N_HEADS)[:, None] // HPG == np.arange(N_KV)[None, :])[None, :, :, None]
    qrows = jnp.where(head_group, q_s.reshape(bs, N_HEADS, 1, HEAD_DIM), 0.0).reshape(bs, N_HEADS, gw)
    oc, os_, ow, nwin_t = _nsa_step_call(
        page_table, qrows, cb_pool, cache_t, rows_s.reshape(bs, 1, 4 * gw),
        win_s.reshape(bs, 1, 2 * gw), win_s[0].T, cwin_t, bsel, bnew, bwin, bcmp)
    nwin = nwin_t.reshape(bs, 2, N_KV, HEAD_DIM, w_len).transpose(0, 4, 1, 2, 3)
    gts = gates_s[0, :, :N_HEADS * 3].reshape(bs, N_HEADS, 3)
    branches = []
    for br, o in enumerate((oc, os_, ow)):
        branches += [jnp.repeat(gts[:, :, br], HEAD_DIM, axis=1)[None], _head_diag(o)[None]]
    xs3 = _outproj_call(xs2, mod_s(2), g(1, 1), w_out_b, branches, bs)
    ys, up1 = _ffn_call(xs3, mod_s(3), g(1, 2), g(1, 3), w_up[1], ffn_conv_w[1], ffn_conv_b[1][None], w_dn[1], prev_f(1), bs, n_chunks)
    y_sample = ys.reshape(bs, 1, d)
    kv_s = rows_s.reshape(bs, 1, 4, N_KV, HEAD_DIM)
    win_state_s = nwin
    conv_a_s = jnp.stack([state_conv_a[0, :, 1], v_a[0]], axis=1)[None]
    ffn_s = jnp.stack([jnp.stack([state_ffn_conv[l, :, 1], u[0]], axis=1) for l, u in ((0, up0), (1, up1))])
    return (y_prompt, y_sample, kv_p, kv_s, win_p, win_state_s, conv_a_p, conv_a_s, ffn_p, ffn_s)
```

```python
import functools
import math

import numpy as np
import jax
import jax.numpy as jnp
from jax import lax
from jax.experimental import pallas as pl
from jax.experimental.pallas import tpu as pltpu

F32 = jnp.float32
BF16 = jnp.bfloat16

N_HEADS = 16
HEAD_DIM = 64
N_KV = 4
HPG = N_HEADS // N_KV
CMP_BLOCK = 32
SEL_BLOCK = 64
TOP_N = 16
WINDOW = 512
N_BUCKETS = 32
MAX_EXACT = N_BUCKETS // 2
MAX_DISTANCE = 1024
PAGE_SIZE = 128
EPS = 1e-6

LANES = 128
TQ = 256
TK = 256
NEG_MASK = -1e30
NEG_SEL = -1e9
BIG_SCORE = 1e30
VMEM_LIMIT_BYTES = 56 * 1024 * 1024


def _bucket_thresholds():
    d = np.arange(0, 4 * MAX_DISTANCE)
    nf = np.maximum(d, 1).astype(np.float32)
    large = MAX_EXACT + (np.log(nf / MAX_EXACT) / math.log(MAX_DISTANCE / MAX_EXACT)
                         * (N_BUCKETS - MAX_EXACT)).astype(np.int32)
    bucket = np.where(d < MAX_EXACT, d, np.minimum(large, N_BUCKETS - 1))
    assert np.all(np.diff(bucket) >= 0)
    return [int(np.argmax(bucket >= k)) for k in range(N_BUCKETS)]


BUCKET_THR = _bucket_thresholds()
assert TQ == TK
N_TBL = -(-(BUCKET_THR[-1] + TK - 1) // TQ)
N_WIN_TILES = WINDOW // TK + 1


def _cparams(*sem):
    return pltpu.CompilerParams(dimension_semantics=sem, vmem_limit_bytes=VMEM_LIMIT_BYTES)


def _dot(a, b):
    return jnp.dot(a, b, preferred_element_type=F32)


def _dot_nt(a, b):
    return lax.dot_general(a, b, (((1,), (1,)), ((), ())), preferred_element_type=F32)


def _rms(x, g):
    return (x * lax.rsqrt(jnp.mean(x * x, axis=-1, keepdims=True) + EPS)) * g


def _sigmoid(x):
    return 1.0 / (1.0 + jnp.exp(-x))


def _shift_rows(v, carry):
    row = lax.broadcasted_iota(jnp.int32, v.shape, 0)
    r1 = jnp.where(row == 0, carry[7:8], pltpu.roll(v, 1, 0))
    r2 = jnp.where(row == 0, carry[6:7], jnp.where(row == 1, carry[7:8], pltpu.roll(v, 2, 0)))
    return r1, r2


def _conv3(v, r1, r2, cw, cb):
    return (cw[0:1] * r2 + cw[1:2] * r1) + cw[2:3] * v + cb


def _mod_kernel(c_ref, w_ref, b_ref, o_ref):
    o_ref[0] = _dot(c_ref[...].astype(BF16), w_ref[0].astype(BF16)) + b_ref[0]


def _mod_call(c_all, w, b):
    n, d, nn = w.shape
    r = c_all.shape[0]
    tn = 512
    return pl.pallas_call(
        _mod_kernel, grid=(n, nn // tn),
        in_specs=[pl.BlockSpec((r, d), lambda i, j: (0, 0)),
                  pl.BlockSpec((1, d, tn), lambda i, j: (i, 0, j)),
                  pl.BlockSpec((1, 1, tn), lambda i, j: (i, 0, j))],
        out_specs=pl.BlockSpec((1, r, tn), lambda i, j: (i, 0, j)),
        out_shape=jax.ShapeDtypeStruct((n, r, nn), F32),
        compiler_params=_cparams("arbitrary", "arbitrary"), name="mod")(c_all, w, b)


def _mixer_kernel(seq_mode, d, *refs):
    if seq_mode:
        x_ref, mod_ref, g0_ref, g1_ref, win_ref, cw_ref, cb_ref, wout_ref, xo_ref, st_ref, carry = refs
    else:
        x_ref, mod_ref, g0_ref, g1_ref, win_ref, cw_ref, cb_ref, wout_ref, p0_ref, p1_ref, xo_ref, st_ref = refs
    x = x_ref[0]
    m = mod_ref[0]
    h = _rms(x, g0_ref[...]) * (1.0 + m[:, d:2 * d]) + m[:, :d]
    z = _dot(h.astype(BF16), win_ref[...])
    bg, cg, u = z[:, :d], z[:, d:2 * d], z[:, 2 * d:]
    v = cg * u
    if seq_mode:
        @pl.when(pl.program_id(1) == 0)
        def _():
            carry[...] = jnp.zeros_like(carry)
        r1, r2 = _shift_rows(v, carry[...])
        carry[...] = v[-8:]
        st_ref[0] = v[-8:]
    else:
        r1, r2 = p1_ref[0], p0_ref[0]
        st_ref[0] = v
    y = _conv3(v, r1, r2, cw_ref[...], cb_ref[...])
    o = _dot((bg * y).astype(BF16), wout_ref[...])
    xo_ref[0] = x + m[:, 2 * d:] * _rms(o, g1_ref[...])


def _row_specs(bx, s, tm, mod):
    sm = mod.shape[1]
    if sm == 1:
        mod_spec = pl.BlockSpec((1, 1, mod.shape[2]), lambda b, i: (b, 0, 0))
    else:
        mod_spec = pl.BlockSpec((1, tm, mod.shape[2]), lambda b, i: (b, i, 0))
    return mod_spec


def _full(shape):
    nd = len(shape)
    return pl.BlockSpec(shape, lambda b, i, _nd=nd: (0,) * _nd, pipeline_mode=pl.Buffered(1))


def _mixer_call(x, mod, g0, g1, w_in, cw, cb, w_out, prev, tm):
    bx, s, d = x.shape
    seq_mode = prev is None
    row = pl.BlockSpec((1, tm, d), lambda b, i: (b, i, 0))
    in_specs = [row, _row_specs(bx, s, tm, mod), _full((1, d)), _full((1, d)), _full(w_in.shape),
                _full(cw.shape), _full((1, d)), _full(w_out.shape)]
    args = [x, mod, g0, g1, w_in, cw, cb, w_out]
    if seq_mode:
        st_shape, st_spec = (bx, 8, d), pl.BlockSpec((1, 8, d), lambda b, i: (b, 0, 0))
        scratch = [pltpu.VMEM((8, d), F32)]
    else:
        in_specs += [row, row]
        args += [prev[0], prev[1]]
        st_shape, st_spec = (bx, s, d), row
        scratch = []
    return pl.pallas_call(
        functools.partial(_mixer_kernel, seq_mode, d), grid=(bx, s // tm),
        in_specs=in_specs, out_specs=[row, st_spec],
        out_shape=[jax.ShapeDtypeStruct((bx, s, d), F32), jax.ShapeDtypeStruct(st_shape, F32)],
        scratch_shapes=scratch, compiler_params=_cparams("arbitrary", "arbitrary"), name="mixer_a")(*args)


def _ffn_kernel(seq_mode, d, dff, n_chunks, *refs):
    if seq_mode:
        x_ref, mod_ref, g2_ref, g3_ref, wup_ref, cw_ref, cb_ref, wdn_ref, xo_ref, st_ref, carry = refs
    else:
        x_ref, mod_ref, g2_ref, g3_ref, wup_ref, cw_ref, cb_ref, wdn_ref, p0_ref, p1_ref, xo_ref, st_ref = refs
    x = x_ref[0]
    m = mod_ref[0]
    h = (_rms(x, g2_ref[...]) * (1.0 + m[:, d:2 * d]) + m[:, :d]).astype(BF16)
    if seq_mode:
        @pl.when(pl.program_id(1) == 0)
        def _():
            carry[...] = jnp.zeros_like(carry)
    cwid = dff // n_chunks
    acc = jnp.zeros((x.shape[0], d), F32)
    for k in range(n_chunks):
        halves = []
        for c0 in (k * cwid, dff + k * cwid):
            up = _dot(h, wup_ref[:, c0:c0 + cwid])
            if seq_mode:
                r1, r2 = _shift_rows(up, carry[:, c0:c0 + cwid])
                carry[:, c0:c0 + cwid] = up[-8:]
                st_ref[0, :, c0:c0 + cwid] = up[-8:]
            else:
                r1, r2 = p1_ref[0, :, c0:c0 + cwid], p0_ref[0, :, c0:c0 + cwid]
                st_ref[0, :, c0:c0 + cwid] = up
            halves.append(_conv3(up, r1, r2, cw_ref[:, c0:c0 + cwid], cb_ref[:, c0:c0 + cwid]))
        u, g = halves
        act = (g * _sigmoid(g)) * u
        acc = acc + _dot(act.astype(BF16), wdn_ref[k * cwid:(k + 1) * cwid, :])
    xo_ref[0] = x + m[:, 2 * d:] * _rms(acc, g3_ref[...])


def _ffn_call(x, mod, g2, g3, w_up, cw, cb, w_dn, prev, tm, n_chunks):
    bx, s, d = x.shape
    dff = w_dn.shape[0]
    seq_mode = prev is None
    row = pl.BlockSpec((1, tm, d), lambda b, i: (b, i, 0))
    in_specs = [row, _row_specs(bx, s, tm, mod), _full((1, d)), _full((1, d)), _full(w_up.shape),
                _full(cw.shape), _full((1, 2 * dff)), _full(w_dn.shape)]
    args = [x, mod, g2, g3, w_up, cw, cb, w_dn]
    if seq_mode:
        st_shape, st_spec = (bx, 8, 2 * dff), pl.BlockSpec((1, 8, 2 * dff), lambda b, i: (b, 0, 0))
        scratch = [pltpu.VMEM((8, 2 * dff), F32)]
    else:
        prow = pl.BlockSpec((1, tm, 2 * dff), lambda b, i: (b, i, 0))
        in_specs += [prow, prow]
        args += [prev[0], prev[1]]
        st_shape, st_spec = (bx, s, 2 * dff), prow
        scratch = []
    return pl.pallas_call(
        functools.partial(_ffn_kernel, seq_mode, d, dff, n_chunks), grid=(bx, s // tm),
        in_specs=in_specs, out_specs=[row, st_spec],
        out_shape=[jax.ShapeDtypeStruct((bx, s, d), F32), jax.ShapeDtypeStruct(st_shape, F32)],
        scratch_shapes=scratch, compiler_params=_cparams("arbitrary", "arbitrary"), name="conv_ffn")(*args)


def _kvq_kernel(d, n_rows_cols, x_ref, modkv_ref, mod_ref, gkv_ref, g0_ref, wkv_ref, wqg_ref,
                rows_ref, win_ref, q_ref, gate_ref):
    x = x_ref[0]
    r = x * lax.rsqrt(jnp.mean(x * x, axis=-1, keepdims=True) + EPS)
    mk = modkv_ref[0]
    hk = (r * gkv_ref[...]) * (1.0 + mk[:, d:]) + mk[:, :d]
    kv = _dot(hk.astype(BF16), wkv_ref[...])
    rows_ref[0] = kv[:, :n_rows_cols]
    win_ref[0] = kv[:, n_rows_cols:]
    m = mod_ref[0]
    h1 = (r * g0_ref[...]) * (1.0 + m[:, d:2 * d]) + m[:, :d]
    qg = _dot(h1.astype(BF16), wqg_ref[...])
    nq = N_HEADS * HEAD_DIM
    q_ref[0] = qg[:, :nq] * (HEAD_DIM ** -0.5)
    gate_ref[0] = _sigmoid(qg[:, nq:])


def _kvq_call(x, modkv, mod, gkv, g0, w_kv, w_qg, tm):
    bx, s, d = x.shape
    nkv = w_kv.shape[1]
    n_rows_cols = 4 * N_KV * HEAD_DIM
    n_win_cols = nkv - n_rows_cols
    nq = N_HEADS * HEAD_DIM
    ng = w_qg.shape[1] - nq
    row = lambda w: pl.BlockSpec((1, tm, w), lambda b, i: (b, i, 0))
    return pl.pallas_call(
        functools.partial(_kvq_kernel, d, n_rows_cols), grid=(bx, s // tm),
        in_specs=[row(d), _row_specs(bx, s, tm, modkv), _row_specs(bx, s, tm, mod), _full((1, d)), _full((1, d)),
                  _full(w_kv.shape), _full(w_qg.shape)],
        out_specs=[row(n_rows_cols), row(n_win_cols), row(nq), row(ng)],
        out_shape=[jax.ShapeDtypeStruct((bx, s, n_rows_cols), F32), jax.ShapeDtypeStruct((bx, s, n_win_cols), F32),
                   jax.ShapeDtypeStruct((bx, s, nq), F32), jax.ShapeDtypeStruct((bx, s, ng), F32)],
        compiler_params=_cparams("arbitrary", "arbitrary"), name="kv_q_proj")(x, modkv, mod, gkv, g0, w_kv, w_qg)


def _outproj_kernel(d, n_branch, *refs):
    x_ref, mod_ref, g1_ref, w_ref = refs[:4]
    xo_ref = refs[-1]
    if n_branch == 1:
        o = refs[4][0]
    else:
        o = sum(refs[4 + 2 * i][0] * refs[5 + 2 * i][0] for i in range(n_branch))
    y = _dot(o.astype(BF16), w_ref[...])
    xo_ref[0] = x_ref[0] + mod_ref[0][:, 2 * d:] * _rms(y, g1_ref[...])


def _outproj_call(x, mod, g1, w, branches, tm):
    bx, s, d = x.shape
    row = pl.BlockSpec((1, tm, d), lambda b, i: (b, i, 0))
    n_branch = 1 if len(branches) == 1 else len(branches) // 2
    return pl.pallas_call(
        functools.partial(_outproj_kernel, d, n_branch), grid=(bx, s // tm),
        in_specs=[row, _row_specs(bx, s, tm, mod), _full((1, d)), _full(w.shape)] + [row] * len(branches),
        out_specs=row, out_shape=jax.ShapeDtypeStruct((bx, s, d), F32),
        compiler_params=_cparams("arbitrary", "arbitrary"), name="attn_out_proj")(x, mod, g1, w, *branches)


def _gelu_tanh(x):
    return x * (0.5 * (1.0 + jnp.tanh(math.sqrt(2.0 / math.pi) * (x + 0.044715 * (x * x * x)))))


def _compress_kernel(tmb, x_ref, pe_ref, w1_ref, w2_ref, o_ref):
    acc = jnp.zeros((tmb, LANES), F32)
    for r in range(CMP_BLOCK):
        xr = x_ref[pl.ds(r, tmb, stride=CMP_BLOCK), :] + pe_ref[0, r]
        acc = acc + _dot(xr.astype(BF16), w1_ref[0, r])
    o_ref[...] = _dot(_gelu_tanh(acc).astype(BF16), w2_ref[0])


def _compress_call(rows2d, pe2, w1bd, w2bd):
    m = rows2d.shape[0] // CMP_BLOCK
    tmb = max(t for t in range(8, min(256, m) + 1, 8) if m % t == 0)
    return pl.pallas_call(
        functools.partial(_compress_kernel, tmb), grid=(m // tmb, 4),
        in_specs=[pl.BlockSpec((tmb * CMP_BLOCK, LANES), lambda i, j: (i, j)),
                  pl.BlockSpec((1, CMP_BLOCK, 1, LANES), lambda i, j: (j // 2, 0, 0, 0)),
                  pl.BlockSpec((1, CMP_BLOCK, LANES, LANES), lambda i, j: (j // 2, 0, 0, 0)),
                  pl.BlockSpec((1, LANES, LANES), lambda i, j: (j // 2, 0, 0))],
        out_specs=pl.BlockSpec((tmb, LANES), lambda i, j: (i, j)),
        out_shape=jax.ShapeDtypeStruct((m, 4 * LANES), F32),
        compiler_params=_cparams("arbitrary", "arbitrary"), name="compress")(rows2d, pe2, w1bd, w2bd)


def _compress_weights(cmp_pe, cmp_w1, cmp_w2):
    z = jnp.zeros((2, CMP_BLOCK, HEAD_DIM, HEAD_DIM), F32)
    w1 = cmp_w1.reshape(2, CMP_BLOCK, HEAD_DIM, HEAD_DIM)
    w1bd = jnp.concatenate([jnp.concatenate([w1, z], -1), jnp.concatenate([z, w1], -1)], -2).astype(BF16)
    z2 = jnp.zeros((2, HEAD_DIM, HEAD_DIM), F32)
    w2bd = jnp.concatenate([jnp.concatenate([cmp_w2, z2], -1), jnp.concatenate([z2, cmp_w2], -1)], -2).astype(BF16)
    pe2 = jnp.concatenate([cmp_pe, cmp_pe], -1)[:, :, None, :]
    return pe2, w1bd, w2bd


N_PAGE_CB = PAGE_SIZE // CMP_BLOCK
CB_ROWS = 8


PAGE_PITCH = N_KV * HEAD_DIM + 4


def _compress_pool_kernel(n_pg, n_i, x_hbm, pe_ref, mw_ref, w2_ref, o_ref, xbuf, sem):
    gw = N_KV * HEAD_DIM
    t = pl.program_id(0)
    n_t = pl.num_programs(0)
    slot_rows = n_pg * PAGE_PITCH

    def page_copy(step, slot, p):
        return pltpu.make_async_copy(x_hbm.at[(step % n_i) * n_pg + p, step // n_i],
                                     xbuf.at[pl.ds(slot * slot_rows + p * PAGE_PITCH, gw), :], sem.at[slot])

    def start_all(step, slot):
        for p in range(n_pg):
            page_copy(step, slot, p).start()

    slot = t % 2

    @pl.when(t == 0)
    def _():
        start_all(t, slot)

    @pl.when(t + 1 < n_t)
    def _():
        start_all(t + 1, 1 - slot)
    for p in range(n_pg):
        page_copy(t, slot, p).wait()

    def rows_of(g, d):
        return xbuf[pl.ds(slot * slot_rows + g * HEAD_DIM + d, n_pg, stride=PAGE_PITCH), :]
    lhs = jnp.concatenate(
        [(jnp.concatenate([rows_of(g, d) for g in range(N_KV)], axis=0) + pe_ref[0, d]).astype(BF16)
         for d in range(HEAD_DIM)], axis=1)
    z = _dot(_gelu_tanh(_dot(lhs, mw_ref[0])).astype(BF16), w2_ref[0])
    lane = lax.broadcasted_iota(jnp.int32, (n_pg, LANES), 1)
    low = lane < HEAD_DIM
    o_ref[...] = jnp.zeros(o_ref.shape, F32)
    for n in range(N_PAGE_CB):
        for c in range(N_KV // 2):
            a = z[(2 * c) * n_pg:(2 * c + 1) * n_pg, (n // 2) * LANES:(n // 2 + 1) * LANES]
            b = z[(2 * c + 1) * n_pg:(2 * c + 2) * n_pg, (n // 2) * LANES:(n // 2 + 1) * LANES]
            if n % 2 == 0:
                b = pltpu.roll(b, HEAD_DIM, 1)
            else:
                a = pltpu.roll(a, HEAD_DIM, 1)
            o_ref[c, pl.ds(n, n_pg, stride=CB_ROWS), :] = jnp.where(low, a, b)


def _compress_pool_call(cache_t, pe_t, mw, w2bd4):
    n_pool = cache_t.shape[0]
    n_pg = max(t for t in range(8, min(64, n_pool) + 1, 8) if n_pool % t == 0)
    n_i = n_pool // n_pg
    return pl.pallas_call(
        functools.partial(_compress_pool_kernel, n_pg, n_i), grid=(2 * n_i,),
        in_specs=[pl.BlockSpec(memory_space=pl.ANY),
                  pl.BlockSpec((1, HEAD_DIM, 1, PAGE_SIZE), lambda t: (t // n_i, 0, 0, 0)),
                  pl.BlockSpec((1, HEAD_DIM * PAGE_SIZE, N_PAGE_CB * HEAD_DIM), lambda t: (t // n_i, 0, 0)),
                  pl.BlockSpec((1, N_PAGE_CB * HEAD_DIM, N_PAGE_CB * HEAD_DIM), lambda t: (t // n_i, 0, 0))],
        out_specs=pl.BlockSpec((N_KV // 2, n_pg * CB_ROWS, LANES), lambda t: (t // n_i, t % n_i, 0)),
        out_shape=jax.ShapeDtypeStruct((N_KV, n_pool * CB_ROWS, LANES), F32),
        scratch_shapes=[pltpu.VMEM((2 * n_pg * PAGE_PITCH, PAGE_SIZE), F32), pltpu.SemaphoreType.DMA((2,))],
        compiler_params=_cparams("arbitrary"), name="compress_pool")(cache_t, pe_t, mw, w2bd4)


def _compress_pool_weights(cmp_pe, cmp_w1, cmp_w2):
    eye = np.eye(N_PAGE_CB, dtype=bool)
    w1t = cmp_w1.reshape(2, CMP_BLOCK, HEAD_DIM, HEAD_DIM).transpose(0, 2, 1, 3)
    mw = jnp.where(eye[None, None, :, None, :, None], w1t[:, :, None, :, None, :], 0.0)
    mw = mw.reshape(2, HEAD_DIM * PAGE_SIZE, N_PAGE_CB * HEAD_DIM).astype(BF16)
    w2bd4 = jnp.where(eye[None, :, None, :, None], cmp_w2[:, None, :, None, :], 0.0)
    w2bd4 = w2bd4.reshape(2, N_PAGE_CB * HEAD_DIM, N_PAGE_CB * HEAD_DIM).astype(BF16)
    pe_t = jnp.tile(cmp_pe.transpose(0, 2, 1), (1, 1, N_PAGE_CB))[:, :, None, :]
    return pe_t, mw, w2bd4


def _bias_of_distance(dist, tab):
    val = jnp.zeros(dist.shape, F32) + tab(0)
    for k in range(1, N_BUCKETS):
        val = jnp.where(dist >= BUCKET_THR[k], tab(k), val)
    return val


def _toeplitz_kernel(window, tab_ref, o_ref):
    g = pl.program_id(0)
    dt = pl.program_id(1)
    x = lax.broadcasted_iota(jnp.int32, (8, 2 * TK), 1)
    dist = dt * TQ - jnp.where(x < TK, x, x - 2 * TK)
    for hh in range(HPG):
        head = g * HPG + hh
        val = _bias_of_distance(dist, lambda k, head=head: tab_ref[k, head])
        val = jnp.where(dist < 0, NEG_MASK, val)
        if window is not None:
            val = jnp.where(dist >= window, NEG_MASK, val)
        tile = pltpu.roll(jnp.broadcast_to(val[0:1], (TQ, 2 * TK)), 0, 1, stride=1, stride_axis=0)
        o_ref[0, 0, hh * TQ:(hh + 1) * TQ, :] = tile[:, :TK]


def _toeplitz_call(rel_bias, n_chunks, window):
    return pl.pallas_call(
        functools.partial(_toeplitz_kernel, window), grid=(N_KV, n_chunks),
        in_specs=[pl.BlockSpec(memory_space=pltpu.SMEM)],
        out_specs=pl.BlockSpec((1, 1, HPG * TQ, TK), lambda g, t: (g, t, 0, 0)),
        out_shape=jax.ShapeDtypeStruct((N_KV, n_chunks, HPG * TQ, TK), F32),
        compiler_params=_cparams("arbitrary", "arbitrary"), name="bias_toeplitz")(rel_bias)


def _cmp_bias_kernel(tab_ref, o_ref):
    g = pl.program_id(0)
    qi = pl.program_id(1)
    row = lax.broadcasted_iota(jnp.int32, (LANES, TQ), 0)
    t_q = qi * TQ + lax.broadcasted_iota(jnp.int32, (LANES, TQ), 1)
    cblk = 2 * (row % HEAD_DIM) + row // HEAD_DIM
    dist = t_q - (cblk * CMP_BLOCK + (CMP_BLOCK - 1))
    for hh in range(HPG):
        head = g * HPG + hh
        val = _bias_of_distance(dist, lambda k, head=head: tab_ref[k, head])
        o_ref[0, 0, :, hh * TQ:(hh + 1) * TQ] = jnp.where(dist < 0, NEG_MASK, val)


def _cmp_bias_call(rel_bias, n_qt):
    return pl.pallas_call(
        _cmp_bias_kernel, grid=(N_KV, n_qt),
        in_specs=[pl.BlockSpec(memory_space=pltpu.SMEM)],
        out_specs=pl.BlockSpec((1, 1, LANES, HPG * TQ), lambda g, t: (g, t, 0, 0)),
        out_shape=jax.ShapeDtypeStruct((N_KV, n_qt, LANES, HPG * TQ), F32),
        compiler_params=_cparams("arbitrary", "arbitrary"), name="bias_cmp")(rel_bias)


def _bias_cols_kernel(dist_ref, tab_ref, o_ref):
    dist = dist_ref[...]
    val = _bias_of_distance(dist, lambda k: tab_ref[:, k:k + 1])
    o_ref[...] = jnp.where(dist < 0, NEG_MASK, val)


def _bias_cols_call(dist, tab_heads):
    r = dist.shape[1]
    return pl.pallas_call(
        _bias_cols_kernel, grid=(1,),
        in_specs=[pl.BlockSpec((N_HEADS, r), lambda i: (0, 0)), pl.BlockSpec((N_HEADS, N_BUCKETS), lambda i: (0, 0))],
        out_specs=pl.BlockSpec((N_HEADS, r), lambda i: (0, 0)),
        out_shape=jax.ShapeDtypeStruct((N_HEADS, r), F32),
        compiler_params=_cparams("arbitrary"), name="bias_cols")(dist, tab_heads)


def _kv_prep_kernel(ksel_ref, vsel_ref, kwin_ref, vwin_ref, kaug_ref, v1_ref, kw_ref, vw1_ref):
    g_odd = (pl.program_id(1) % 2) == 1
    ch = ksel_ref.shape[1]
    lane = lax.broadcasted_iota(jnp.int32, (ch, LANES), 1)
    row = pl.program_id(2) * ch + lax.broadcasted_iota(jnp.int32, (ch, LANES), 0)
    low = lane < HEAD_DIM
    ones = jnp.where(lane == HEAD_DIM, 1.0, 0.0)
    onehot = jnp.where(lane - HEAD_DIM == row // SEL_BLOCK, 1.0, 0.0)

    def pick(ref):
        x = ref[0]
        return jnp.where(g_odd, pltpu.roll(x, HEAD_DIM, 1), x)

    kaug_ref[0, 0] = jnp.where(low, pick(ksel_ref), onehot).astype(BF16)
    v1_ref[0, 0] = jnp.where(low, pick(vsel_ref), ones).astype(BF16)
    kw_ref[0, 0] = jnp.where(low, pick(kwin_ref), 0.0).astype(BF16)
    vw1_ref[0, 0] = jnp.where(low, pick(vwin_ref), ones).astype(BF16)


def _kv_prep_call(rows, win):
    b, s, _ = rows.shape
    ch = min(1024, s)
    kv_spec = lambda col0: pl.BlockSpec((1, ch, LANES), lambda bb, g, c, _c=col0: (bb, c, _c + g // 2))
    o_spec = pl.BlockSpec((1, 1, ch, LANES), lambda bb, g, c: (bb, g, c, 0))
    o_shape = jax.ShapeDtypeStruct((b, N_KV, s, LANES), BF16)
    return pl.pallas_call(
        _kv_prep_kernel, grid=(b, N_KV, s // ch),
        in_specs=[kv_spec(4), kv_spec(6), kv_spec(0), kv_spec(2)],
        out_specs=[o_spec] * 4, out_shape=[o_shape] * 4,
        compiler_params=_cparams("arbitrary", "arbitrary", "arbitrary"), name="kv_prep")(rows, rows, win, win)


def _nsa_seq_kernel(n_sb, q_ref, gate_ref, kaug_ref, v1_ref, kw_ref, vw1_ref, kcb_ref, vcb_ref, tcmp_ref,
                    tsel_ref, twin_ref, o_ref, qaug, qw_sc, s_sc, mrun, mb, acc_sel, acc_win, score_sc):
    qi = pl.program_id(2)
    rows = HPG * TQ
    lane = lax.broadcasted_iota(jnp.int32, (TQ, LANES), 1)
    low = lane < HEAD_DIM

    qh = []
    for hh in range(HPG):
        qv = q_ref[0, :, (hh // 2) * LANES:(hh // 2 + 1) * LANES]
        if hh % 2 == 1:
            qv = pltpu.roll(qv, HEAD_DIM, 1)
        qh.append(jnp.where(low, qv, 0.0))
    qw = jnp.concatenate(qh, axis=0).astype(BF16)
    qw_sc[...] = qw

    s_c = _dot_nt(kcb_ref[0, 0].astype(BF16), qw) + tcmp_ref[0, 0]
    ok_c = s_c > 0.5 * NEG_MASK
    mx = jnp.max(s_c, axis=0, keepdims=True)
    p = jnp.where(ok_c, jnp.exp(s_c - mx), 0.0)
    den = jnp.sum(p, axis=0, keepdims=True)
    pn_t = p / jnp.where(den > 0, den, 1.0)
    o_c = _dot(pn_t.T.astype(BF16), vcb_ref[0, 0].astype(BF16))
    imp = pn_t[:, 0:TQ]
    for hh in range(1, HPG):
        imp = imp + pn_t[:, hh * TQ:(hh + 1) * TQ]
    n_blk = LANES // 2
    imp = imp[:n_blk] + imp[n_blk:]

    blk = lax.broadcasted_iota(jnp.int32, (n_blk, TQ), 0)
    cur = (qi * TQ + lax.broadcasted_iota(jnp.int32, (n_blk, TQ), 1)) // SEL_BLOCK
    valid = blk <= cur
    forced = (blk == 0) | (blk == cur) | (blk == cur - 1)
    score_sc[...] = jnp.where(valid & forced, BIG_SCORE, jnp.where(valid, imp, -BIG_SCORE))
    sub = 8
    groups = [score_sc[r0:r0 + sub, :] for r0 in range(0, n_blk, sub)]
    rowg = lax.broadcasted_iota(jnp.int32, (sub, TQ), 0)
    cnts = [jnp.zeros((sub, TQ), jnp.int32) for _ in groups]
    for bp in range(n_blk):
        r = score_sc[bp:bp + 1, :]
        for gi, s_g in enumerate(groups):
            if gi * sub > bp:
                ahead = r >= s_g
            elif (gi + 1) * sub - 1 < bp:
                ahead = r > s_g
            else:
                ahead = (r > s_g) | ((r == s_g) & (rowg + gi * sub > bp))
            cnts[gi] = cnts[gi] + jnp.where(ahead, 1, 0)
    cnt = jnp.concatenate(cnts, axis=0)
    selmask_t = jnp.where((cnt < min(TOP_N, n_sb)) & valid, 0.0, NEG_SEL)
    selmask = jnp.concatenate([jnp.zeros((n_blk, TQ), F32), selmask_t], axis=0).T
    for hh in range(HPG):
        qaug[hh * TQ:(hh + 1) * TQ, :] = jnp.where(low, qh[hh], selmask).astype(BF16)

    def branch(q_sc, k_ref, v_ref, t_ref, acc_sc, n_tiles, n_tbl, unroll):
        mrun[...] = jnp.full(mrun.shape, NEG_MASK, F32)

        def tile_loop(step):
            done = 0
            width = unroll
            while width >= 1:
                def group(j, _, done=done, width=width):
                    step(tuple(done + width * j + u for u in range(width)))
                    return 0
                n_groups = (n_tiles - done) // width
                lax.fori_loop(0, n_groups, group, 0)
                done = done + n_groups * width
                width //= 2

        def scores(tiles):
            m = mrun[...]
            for i in tiles:
                r0 = pl.multiple_of((qi - i) * TK, TK)
                s = _dot_nt(q_sc[...], k_ref[0, 0, pl.ds(r0, TK), :]) + t_ref[0, jnp.minimum(i, n_tbl)]
                s_sc[i] = s
                for c in range(TK // LANES):
                    m = jnp.maximum(m, s[:, c * LANES:(c + 1) * LANES])
            mrun[...] = m
        tile_loop(scores)
        mb[...] = jnp.broadcast_to(jnp.max(mrun[...], axis=1, keepdims=True), (rows, LANES))
        acc_sc[...] = jnp.zeros(acc_sc.shape, F32)

        def weigh(tiles):
            mbv = mb[...]
            mb2 = jnp.concatenate([mbv] * (TK // LANES), axis=1)
            acc = acc_sc[...]
            for i in tiles:
                r0 = pl.multiple_of((qi - i) * TK, TK)
                acc = acc + _dot(jnp.exp(s_sc[i] - mb2).astype(BF16), v_ref[0, 0, pl.ds(r0, TK), :])
            acc_sc[...] = acc
        tile_loop(weigh)
        return acc_sc[...]

    acc_s = branch(qaug, kaug_ref, v1_ref, tsel_ref, acc_sel, qi + 1, N_TBL, 4)
    acc_w = branch(qw_sc, kw_ref, vw1_ref, twin_ref, acc_win, jnp.minimum(qi, N_WIN_TILES - 1) + 1, N_WIN_TILES - 1, 2)

    outs = []
    for hh in range(HPG):
        gts = [gate_ref[0, :, hh * 3 + br:hh * 3 + br + 1] for br in range(3)]
        rs = slice(hh * TQ, (hh + 1) * TQ)
        a_s, a_w = acc_s[rs], acc_w[rs]
        outs.append(gts[0] * o_c[rs] + (gts[1] / a_s[:, HEAD_DIM:HEAD_DIM + 1]) * a_s
                    + (gts[2] / a_w[:, HEAD_DIM:HEAD_DIM + 1]) * a_w)
    for c in range(HPG // 2):
        o_ref[0, :, c * LANES:(c + 1) * LANES] = jnp.where(low, outs[2 * c], pltpu.roll(outs[2 * c + 1], HEAD_DIM, 1))


def _nsa_seq_call(q, gates, kaug, v1, kw, vw1, kcb, vcb, tcmp, tsel, twin):
    b, s, _ = q.shape
    n_qt = s // TQ
    rows = HPG * TQ
    kv_spec = pl.BlockSpec((1, 1, s, LANES), lambda bb, g, i: (bb, g, 0, 0))
    cb_spec = pl.BlockSpec((1, 1, kcb.shape[2], LANES), lambda bb, g, i: (bb, g, 0, 0))
    tbl_spec = lambda t: pl.BlockSpec((1,) + t.shape[1:], lambda bb, g, i: (g, 0, 0, 0))
    return pl.pallas_call(
        functools.partial(_nsa_seq_kernel, s // SEL_BLOCK), grid=(b, N_KV, n_qt),
        in_specs=[pl.BlockSpec((1, TQ, HPG * HEAD_DIM), lambda bb, g, i: (bb, i, g)),
                  pl.BlockSpec((1, TQ, LANES), lambda bb, g, i: (bb, i, g)),
                  kv_spec, kv_spec, kv_spec, kv_spec, cb_spec, cb_spec,
                  pl.BlockSpec((1, 1, LANES, rows), lambda bb, g, i: (g, i, 0, 0)), tbl_spec(tsel), tbl_spec(twin)],
        out_specs=pl.BlockSpec((1, TQ, HPG * HEAD_DIM), lambda bb, g, i: (bb, i, g)),
        out_shape=jax.ShapeDtypeStruct((b, s, N_HEADS * HEAD_DIM), F32),
        scratch_shapes=[pltpu.VMEM((rows, LANES), BF16), pltpu.VMEM((rows, LANES), BF16),
                        pltpu.VMEM((s // TK, rows, TK), F32), pltpu.VMEM((rows, LANES), F32),
                        pltpu.VMEM((rows, LANES), F32), pltpu.VMEM((rows, LANES), F32),
                        pltpu.VMEM((rows, LANES), F32), pltpu.VMEM((LANES // 2, TQ), F32)],
        compiler_params=_cparams("arbitrary", "arbitrary", "arbitrary"), name="nsa_seq")(
            q, gates, kaug, v1, kw, vw1, kcb, vcb, tcmp, tsel, twin)


def _softmax_lanes(s):
    p = jnp.exp(s - jnp.max(s, axis=1, keepdims=True))
    return p / jnp.sum(p, axis=1, keepdims=True)


def _rows_to_heads(rows):
    hg = lax.broadcasted_iota(jnp.int32, (N_HEADS, rows[0].shape[1]), 0) // HPG
    out = jnp.broadcast_to(rows[0], hg.shape)
    for g in range(1, N_KV):
        out = jnp.where(hg == g, rows[g], out)
    return out


def _nsa_stepT_kernel(n_pages, pt_ref, *refs):
    del pt_ref
    q_ref = refs[0]
    cb_refs = refs[1:1 + n_pages]
    pg_refs = refs[1 + n_pages:1 + 2 * n_pages]
    (kvnew_ref, winnew_ref, wcol_ref, cwin_ref, bsel_ref, bnew_ref, bwin_ref, bcmp_ref) = refs[1 + 2 * n_pages:9 + 2 * n_pages]
    oc_ref, os_ref, ow_ref, nwin_ref = refs[9 + 2 * n_pages:13 + 2 * n_pages]
    (s_sc,) = refs[13 + 2 * n_pages:]
    b = pl.program_id(0)
    gw = N_KV * HEAD_DIM
    n_past = n_pages * PAGE_SIZE
    q16 = q_ref[0].astype(BF16)
    qf = q16.astype(F32)
    lane = lax.broadcasted_iota(jnp.int32, (N_HEADS, LANES), 1)

    n_pad = LANES - CB_ROWS * n_pages
    cb = jnp.concatenate([jnp.concatenate([r[c] for c in range(N_KV)], axis=1) for r in cb_refs]
                         + ([jnp.zeros((n_pad, 2 * gw), F32)] if n_pad else []), axis=0)
    pn_c = _softmax_lanes(_dot_nt(q16, cb[:, :gw].astype(BF16)) + bcmp_ref[...])
    oc_ref[0] = _dot(pn_c.astype(BF16), cb[:, gw:].astype(BF16))

    grp = [pn_c[HPG * g:HPG * g + 1] + pn_c[HPG * g + 1:HPG * g + 2] + pn_c[HPG * g + 2:HPG * g + 3]
           + pn_c[HPG * g + 3:HPG * g + 4] for g in range(N_KV)]
    lane8 = lax.broadcasted_iota(jnp.int32, (8, LANES), 1)
    row8 = lax.broadcasted_iota(jnp.int32, (8, LANES), 0)
    imp = jnp.zeros((8, LANES), F32)
    for g in range(N_KV):
        imp = jnp.where(row8 == g, grp[g], imp)
    imp = imp + pltpu.roll(imp, LANES - 1, 1)
    n_past_blk = n_past // SEL_BLOCK
    cur_lane = LANES - 1
    is_blk = ((lane8 % CB_ROWS == 0) | (lane8 % CB_ROWS == 2))
    last_lane = ((n_past_blk - 1) // 2) * CB_ROWS + 2 * ((n_past_blk - 1) % 2)
    forced = (lane8 == 0) | (lane8 == last_lane) | (lane8 == cur_lane)
    valid = is_blk | (lane8 == cur_lane)
    score = jnp.where(valid & forced, BIG_SCORE, jnp.where(valid, imp, -BIG_SCORE))
    cnt = jnp.zeros((8, LANES), jnp.int32)
    for k in range(1, LANES):
        r = pltpu.roll(score, k, 1)
        cnt = cnt + jnp.where((r > score) | ((r == score) & (lane8 >= k)), 1, 0)
    selrows = jnp.where((cnt < min(TOP_N, n_past_blk + 1)) & valid, 0.0, NEG_SEL)
    selmask = _rows_to_heads([selrows[g:g + 1] for g in range(N_KV)])

    blocks_per_page = PAGE_SIZE // SEL_BLOCK
    for p in range(n_pages):
        kt = pg_refs[p][0, 0:gw, :].astype(BF16)
        msk = selmask[:, CB_ROWS * p:CB_ROWS * p + 1]
        for i in range(1, blocks_per_page):
            msk = jnp.where(lane >= i * SEL_BLOCK, selmask[:, CB_ROWS * p + 2 * i:CB_ROWS * p + 2 * i + 1], msk)
        s_sc[:, p * PAGE_SIZE:(p + 1) * PAGE_SIZE] = _dot(q16, kt) + bsel_ref[:, p * PAGE_SIZE:(p + 1) * PAGE_SIZE] + msk
    knew = kvnew_ref[0][:, 2 * gw:3 * gw].astype(BF16).astype(F32)
    s_new = jnp.sum(qf * knew, axis=1, keepdims=True) + bnew_ref[:, 0:1] + selmask[:, cur_lane:cur_lane + 1]
    s_sc[:, n_past:n_past + LANES] = jnp.where(lane == 0, s_new, NEG_MASK)
    s_all = s_sc[...]
    mx = jnp.max(s_all, axis=1, keepdims=True)
    den = jnp.sum(jnp.exp(s_all - mx), axis=1, keepdims=True)
    acc = jnp.zeros((N_HEADS, gw), F32)
    for p in range(n_pages):
        pn = jnp.exp(s_sc[:, p * PAGE_SIZE:(p + 1) * PAGE_SIZE] - mx) / den
        acc = acc + _dot_nt(pn.astype(BF16), pg_refs[p][0, gw:2 * gw, :].astype(BF16))
    pn_new = jnp.exp(s_new - mx) / den
    vnew = kvnew_ref[0][:, 3 * gw:4 * gw].astype(BF16).astype(F32)
    os_ref[0] = acc + pn_new.astype(BF16).astype(F32) * vnew

    cw = cwin_ref[0]
    w_len = cw.shape[1]
    s_w = _dot(q16, cw[0:gw].astype(BF16)) + bwin_ref[...]
    kwn = winnew_ref[0][:, 0:gw].astype(BF16).astype(F32)
    s_wn = jnp.sum(qf * kwn, axis=1, keepdims=True) + bnew_ref[:, 0:1]
    mxw = jnp.maximum(jnp.max(s_w, axis=1, keepdims=True), s_wn)
    pw = jnp.exp(s_w - mxw)
    pwn = jnp.exp(s_wn - mxw)
    denw = jnp.sum(pw, axis=1, keepdims=True) + pwn
    vwn = winnew_ref[0][:, gw:2 * gw].astype(BF16).astype(F32)
    ow_ref[0] = _dot_nt((pw / denw).astype(BF16), cw[gw:2 * gw].astype(BF16)) + (pwn / denw).astype(BF16).astype(F32) * vwn
    lane_w = lax.broadcasted_iota(jnp.int32, cw.shape, 1)
    nb_l = wcol_ref.shape[1]
    lane_b = lax.broadcasted_iota(jnp.int32, (cw.shape[0], nb_l), 1)
    col = jnp.sum(jnp.where(lane_b == b, wcol_ref[...], 0.0), axis=1, keepdims=True)
    nwin_ref[0] = jnp.where(lane_w == w_len - 1, col, pltpu.roll(cw, w_len - 1, 1))


def _nsa_step_call(page_table, qrows, cb_pool, cache_t, kvnew, winnew, wcol, cwin_t, bsel, bnew, bwin, bcmp):
    nb, n_pages = page_table.shape
    gw = N_KV * HEAD_DIM
    w_len = cwin_t.shape[2]
    cb_specs = [pl.BlockSpec((N_KV, None, CB_ROWS, LANES), lambda b, pt, _p=p: (0, pt[b, _p], 0, 0))
                for p in range(n_pages)]
    pg_specs = [pl.BlockSpec((1, 2 * gw, PAGE_SIZE), lambda b, pt, _p=p: (pt[b, _p], 1, 0)) for p in range(n_pages)]
    const = lambda a: pl.BlockSpec(a.shape, lambda b, pt: (0, 0))
    o_spec = pl.BlockSpec((1, N_HEADS, gw), lambda b, pt: (b, 0, 0))
    win_spec = pl.BlockSpec((1, 2 * gw, w_len), lambda b, pt: (b, 0, 0))
    grid_spec = pltpu.PrefetchScalarGridSpec(
        num_scalar_prefetch=1, grid=(nb,),
        in_specs=[pl.BlockSpec((1, N_HEADS, gw), lambda b, pt: (b, 0, 0))] + cb_specs + pg_specs + [
            pl.BlockSpec((1, 1, 4 * gw), lambda b, pt: (b, 0, 0)),
            pl.BlockSpec((1, 1, 2 * gw), lambda b, pt: (b, 0, 0)),
            const(wcol), win_spec, const(bsel), const(bnew), const(bwin), const(bcmp)],
        out_specs=[o_spec, o_spec, o_spec, win_spec],
        scratch_shapes=[pltpu.VMEM((N_HEADS, n_pages * PAGE_SIZE + LANES), F32)])
    o_shape = jax.ShapeDtypeStruct((nb, N_HEADS, gw), F32)
    return pl.pallas_call(
        functools.partial(_nsa_stepT_kernel, n_pages), grid_spec=grid_spec,
        out_shape=[o_shape, o_shape, o_shape, jax.ShapeDtypeStruct((nb, 2 * gw, w_len), F32)],
        compiler_params=_cparams("arbitrary"), name="nsa_step")(
            page_table, qrows, *([cb_pool] * n_pages), *([cache_t] * n_pages), kvnew, winnew, wcol, cwin_t,
            bsel, bnew, bwin, bcmp)


def _head_diag(o):
    b = o.shape[0]
    o5 = o.reshape(b, N_KV, HPG, N_KV, HEAD_DIM)
    return jnp.stack([o5[:, g, :, g, :] for g in range(N_KV)], axis=1).reshape(b, N_HEADS * HEAD_DIM)


def kernel(x_prompt, x_sample, c_prompt, c_sample, cache_kv, cache_win, state_conv_a, state_ffn_conv, page_table, mod_w, mod_b, norm_g, a_w_in, a_conv_w, a_conv_b, a_w_out, kv_mod_w, kv_mod_b, kv_norm_g, w_kv, cmp_pe, cmp_w1, cmp_w2, b_w_qg, b_w_out, rel_bias, ffn_w_up, ffn_conv_w, ffn_conv_b, ffn_w_down):
    bp, s, d = x_prompt.shape
    bs = x_sample.shape[0]
    depth = mod_w.shape[0]
    n_a = a_w_in.shape[0]
    assert depth == 2 and n_a == 1 and x_sample.shape[1] == 1
    dff = ffn_w_down.shape[1]
    n_pool = cache_kv.shape[0]
    n_pages = page_table.shape[1]
    past_len = n_pages * PAGE_SIZE
    gw = N_KV * HEAD_DIM
    nq = N_HEADS * HEAD_DIM

    n_c = bp + bs
    n_cp = -(-n_c // 8) * 8
    c_all = jnp.pad(jnp.concatenate([c_prompt, c_sample], 0), ((0, n_cp - n_c), (0, 0)))
    mods = _mod_call(c_all, mod_w.reshape(depth * 2, d, 3 * d), mod_b.reshape(depth * 2, 1, 3 * d))
    modkv = _mod_call(c_all, kv_mod_w[None], kv_mod_b[None, None])[0]
    mod_p = lambda i: mods[i, :bp][:, None, :]
    mod_s = lambda i: mods[i, bp:n_c][None]

    w_in = a_w_in[0].astype(BF16)
    w_out_a = a_w_out[0].astype(BF16)
    w_up = ffn_w_up.astype(BF16)
    w_dn = ffn_w_down.astype(BF16)
    w_kv_b = w_kv.astype(BF16)
    ng_pad = -(-(b_w_qg.shape[2] - nq) // LANES) * LANES
    w_qg = jnp.pad(b_w_qg[0], ((0, 0), (0, nq + ng_pad - b_w_qg.shape[2]))).astype(BF16)
    w_out_b = b_w_out[0].astype(BF16)
    g = lambda l, i: norm_g[l, i][None]
    pe2, w1bd, w2bd = _compress_weights(cmp_pe, cmp_w1, cmp_w2)
    n_chunks = 2

    tm = min(512, s)
    tm_ffn = min(512, s)
    x1, st_a = _mixer_call(x_prompt, mod_p(0), g(0, 0), g(0, 1), w_in, a_conv_w[0], a_conv_b[0][None], w_out_a, None, tm)
    x2, st_f0 = _ffn_call(x1, mod_p(1), g(0, 2), g(0, 3), w_up[0], ffn_conv_w[0], ffn_conv_b[0][None], w_dn[0], None, tm_ffn, n_chunks)
    rows, win, q, gates = _kvq_call(x2, modkv[:bp][:, None, :], mod_p(2), kv_norm_g[None], g(1, 0), w_kv_b, w_qg, tm)
    cb = _compress_call(rows.reshape(bp * s, 4 * gw), pe2, w1bd, w2bd)
    n_cbk = s // CMP_BLOCK
    assert n_cbk <= LANES
    cb = jnp.pad(cb.reshape(bp, n_cbk, 2, N_KV, HEAD_DIM), ((0, 0), (0, LANES - n_cbk), (0, 0), (0, 0), (0, 0)))
    cb = cb.reshape(bp, LANES // 2, 2, 2, N_KV, HEAD_DIM)
    cb = cb.transpose(3, 0, 4, 2, 1, 5).reshape(2, bp, N_KV, LANES, HEAD_DIM)
    cb = jnp.pad(cb, ((0, 0),) * 4 + ((0, LANES - HEAD_DIM),))
    gates_g = jnp.pad(gates[:, :, :N_HEADS * 3].reshape(bp, s, N_KV, HPG * 3),
                      ((0, 0), (0, 0), (0, 0), (0, LANES - HPG * 3))).reshape(bp, s, N_KV * LANES)
    tsel = _toeplitz_call(rel_bias, N_TBL + 1, None)
    twin = _toeplitz_call(rel_bias, N_WIN_TILES, WINDOW)
    tcmp = _cmp_bias_call(rel_bias, s // TQ)
    kaug, v1, kw, vw1 = _kv_prep_call(rows, win)
    o_att = _nsa_seq_call(q, gates_g, kaug, v1, kw, vw1, cb[0], cb[1], tcmp, tsel, twin)
    x3 = _outproj_call(x2, mod_p(2), g(1, 1), w_out_b, [o_att], tm)
    y_prompt, st_f1 = _ffn_call(x3, mod_p(3), g(1, 2), g(1, 3), w_up[1], ffn_conv_w[1], ffn_conv_b[1][None], w_dn[1], None, tm_ffn, n_chunks)
    kv_p = rows.reshape(bp, s, 4, N_KV, HEAD_DIM)
    keep = min(WINDOW, s)
    win_p = win[:, s - keep:].reshape(bp, keep, 2, N_KV, HEAD_DIM)
    conv_a_p = st_a[None, :, 6:8]
    ffn_p = jnp.stack([st_f0[:, 6:8], st_f1[:, 6:8]])

    xs = x_sample.reshape(1, bs, d)
    prev_a = (state_conv_a[0, :, 0][None], state_conv_a[0, :, 1][None])
    xs1, v_a = _mixer_call(xs, mod_s(0), g(0, 0), g(0, 1), w_in, a_conv_w[0], a_conv_b[0][None], w_out_a, prev_a, bs)
    prev_f = lambda l: (state_ffn_conv[l, :, 0][None], state_ffn_conv[l, :, 1][None])
    xs2, up0 = _ffn_call(xs1, mod_s(1), g(0, 2), g(0, 3), w_up[0], ffn_conv_w[0], ffn_conv_b[0][None], w_dn[0], prev_f(0), bs, n_chunks)
    rows_s, win_s, q_s, gates_s = _kvq_call(xs2, modkv[bp:n_c][None], mod_s(2), kv_norm_g[None], g(1, 0), w_kv_b, w_qg, bs)
    cache_t = cache_kv.transpose(0, 2, 3, 4, 1).reshape(n_pool, 4 * gw, PAGE_SIZE)
    w_len = cache_win.shape[1]
    cwin_t = cache_win.transpose(0, 2, 3, 4, 1).reshape(bs, 2 * gw, w_len)
    cb_pool = _compress_pool_call(cache_t.reshape(n_pool, 4, gw, PAGE_SIZE),
                                  *_compress_pool_weights(cmp_pe, cmp_w1, cmp_w2))
    cb_pool = cb_pool.reshape(N_KV, n_pool, CB_ROWS, LANES)
    assert n_pages * CB_ROWS <= LANES
    d_sel = past_len - np.arange(past_len)
    d_new = np.where(np.arange(LANES) == 0, 0, -1)
    d_win = w_len - np.arange(w_len)
    d_win = np.where(d_win < WINDOW, d_win, -1)
    cl = np.arange(LANES)
    d_cmp = past_len - (((cl // CB_ROWS) * N_PAGE_CB + cl % CB_ROWS) * CMP_BLOCK + CMP_BLOCK - 1)
    d_cmp = np.where((cl % CB_ROWS < N_PAGE_CB) & (cl // CB_ROWS < n_pages), d_cmp, -1)
    assert d_sel.min() >= 0 and d_cmp[d_cmp != -1].min() >= 0
    dist = np.concatenate([d_sel, d_new, d_win, d_cmp]).astype(np.int32)
    bias_cols = _bias_cols_call(jnp.asarray(np.repeat(dist[None, :], N_HEADS, 0)), rel_bias.T)
    bsel, bnew = bias_cols[:, :past_len], bias_cols[:, past_len:past_len + LANES]
    bwin = bias_cols[:, past_len + LANES:past_len + LANES + w_len]
    bcmp = bias_cols[:, past_len + LANES + w_len:]
    head_group = (np.arange(N_HEADS)[:, None] // HPG == np.arange(N_KV)[None, :])[None, :, :, None]
    qrows = jnp.where(head_group, q_s.reshape(bs, N_HEADS, 1, HEAD_DIM), 0.0).reshape(bs, N_HEADS, gw)
    oc, os_, ow, nwin_t = _nsa_step_call(
        page_table, qrows, cb_pool, cache_t, rows_s.reshape(bs, 1, 4 * gw),
        win_s.reshape(bs, 1, 2 * gw), win_s[0].T, cwin_t, bsel, bnew, bwin, bcmp)
    nwin = nwin_t.reshape(bs, 2, N_KV, HEAD_DIM, w_len).transpose(0, 4, 1, 2, 3)
    gts = gates_s[0, :, :N_HEADS * 3].reshape(bs, N_HEADS, 3)
    branches = []
    for br, o in enumerate((oc, os_, ow)):
        branches += [jnp.repeat(gts[:, :, br], HEAD_DIM, axis=1)[None], _head_diag(o)[None]]
    xs3 = _outproj_call(xs2, mod_s(2), g(1, 1), w_out_b, branches, bs)
    ys, up1 = _ffn_call(xs3, mod_s(3), g(1, 2), g(1, 3), w_up[1], ffn_conv_w[1], ffn_conv_b[1][None], w_dn[1], prev_f(1), bs, n_chunks)
    y_sample = ys.reshape(bs, 1, d)
    kv_s = rows_s.reshape(bs, 1, 4, N_KV, HEAD_DIM)
    win_state_s = nwin
    conv_a_s = jnp.stack([state_conv_a[0, :, 1], v_a[0]], axis=1)[None]
    ffn_s = jnp.stack([jnp.stack([state_ffn_conv[l, :, 1], u[0]], axis=1) for l, u in ((0, up0), (1, up1))])
    return (y_prompt, y_sample, kv_p, kv_s, win_p, win_state_s, conv_a_p, conv_a_s, ffn_p, ffn_s)
```

```python
import functools
import math

import numpy as np
import jax
import jax.numpy as jnp
from jax import lax
from jax.experimental import pallas as pl
from jax.experimental.pallas import tpu as pltpu

F32 = jnp.float32
BF16 = jnp.bfloat16

N_HEADS = 16
HEAD_DIM = 64
N_KV = 4
HPG = N_HEADS // N_KV
CMP_BLOCK = 32
SEL_BLOCK = 64
TOP_N = 16
WINDOW = 512
N_BUCKETS = 32
MAX_EXACT = N_BUCKETS // 2
MAX_DISTANCE = 1024
PAGE_SIZE = 128
EPS = 1e-6

LANES = 128
TQ = 256
TK = 256
NEG_MASK = -1e30
NEG_SEL = -1e9
BIG_SCORE = 1e30
VMEM_LIMIT_BYTES = 56 * 1024 * 1024


def _bucket_thresholds():
    d = np.arange(0, 4 * MAX_DISTANCE)
    nf = np.maximum(d, 1).astype(np.float32)
    large = MAX_EXACT + (np.log(nf / MAX_EXACT) / math.log(MAX_DISTANCE / MAX_EXACT)
                         * (N_BUCKETS - MAX_EXACT)).astype(np.int32)
    bucket = np.where(d < MAX_EXACT, d, np.minimum(large, N_BUCKETS - 1))
    assert np.all(np.diff(bucket) >= 0)
    return [int(np.argmax(bucket >= k)) for k in range(N_BUCKETS)]


BUCKET_THR = _bucket_thresholds()
assert TQ == TK
N_TBL = -(-(BUCKET_THR[-1] + TK - 1) // TQ)
N_WIN_TILES = WINDOW // TK + 1


def _cparams(*sem):
    return pltpu.CompilerParams(dimension_semantics=sem, vmem_limit_bytes=VMEM_LIMIT_BYTES)


def _dot(a, b):
    return jnp.dot(a, b, preferred_element_type=F32)


def _dot_nt(a, b):
    return lax.dot_general(a, b, (((1,), (1,)), ((), ())), preferred_element_type=F32)


def _rms(x, g):
    return (x * lax.rsqrt(jnp.mean(x * x, axis=-1, keepdims=True) + EPS)) * g


def _sigmoid(x):
    return 1.0 / (1.0 + jnp.exp(-x))


def _shift_rows(v, carry):
    r1 = pltpu.roll(v, 1, 0)
    r2 = pltpu.roll(v, 2, 0)
    if v.shape[0] > 8:
        row = lax.broadcasted_iota(jnp.int32, (8, v.shape[1]), 0)
        h1 = jnp.where(row == 0, carry[7:8], r1[:8])
        h2 = jnp.where(row == 0, carry[6:7], jnp.where(row == 1, carry[7:8], r2[:8]))
        return jnp.concatenate([h1, r1[8:]], axis=0), jnp.concatenate([h2, r2[8:]], axis=0)
    row = lax.broadcasted_iota(jnp.int32, v.shape, 0)
    return (jnp.where(row == 0, carry[7:8], r1),
            jnp.where(row == 0, carry[6:7], jnp.where(row == 1, carry[7:8], r2)))


def _conv3(v, r1, r2, cw, cb):
    return (cw[0:1] * r2 + cw[1:2] * r1) + cw[2:3] * v + cb


def _mod_kernel(c_ref, w_ref, b_ref, o_ref):
    o_ref[0] = _dot(c_ref[...].astype(BF16), w_ref[0].astype(BF16)) + b_ref[0]


def _mod_call(c_all, w, b):
    n, d, nn = w.shape
    r = c_all.shape[0]
    tn = 512
    return pl.pallas_call(
        _mod_kernel, grid=(n, nn // tn),
        in_specs=[pl.BlockSpec((r, d), lambda i, j: (0, 0)),
                  pl.BlockSpec((1, d, tn), lambda i, j: (i, 0, j)),
                  pl.BlockSpec((1, 1, tn), lambda i, j: (i, 0, j))],
        out_specs=pl.BlockSpec((1, r, tn), lambda i, j: (i, 0, j)),
        out_shape=jax.ShapeDtypeStruct((n, r, nn), F32),
        compiler_params=_cparams("arbitrary", "arbitrary"), name="mod")(c_all, w, b)


def _mixer_kernel(seq_mode, d, *refs):
    if seq_mode:
        x_ref, mod_ref, g0_ref, g1_ref, win_ref, cw_ref, cb_ref, wout_ref, xo_ref, st_ref, carry = refs
    else:
        x_ref, mod_ref, g0_ref, g1_ref, win_ref, cw_ref, cb_ref, wout_ref, p0_ref, p1_ref, xo_ref, st_ref = refs
    x = x_ref[0]
    m = mod_ref[0]
    h = _rms(x, g0_ref[...]) * (1.0 + m[:, d:2 * d]) + m[:, :d]
    z = _dot(h.astype(BF16), win_ref[...])
    bg, cg, u = z[:, :d], z[:, d:2 * d], z[:, 2 * d:]
    v = cg * u
    if seq_mode:
        @pl.when(pl.program_id(1) == 0)
        def _():
            carry[...] = jnp.zeros_like(carry)
        r1, r2 = _shift_rows(v, carry[...])
        carry[...] = v[-8:]
        st_ref[0] = v[-8:]
    else:
        r1, r2 = p1_ref[0], p0_ref[0]
        st_ref[0] = v
    y = _conv3(v, r1, r2, cw_ref[...], cb_ref[...])
    o = _dot((bg * y).astype(BF16), wout_ref[...])
    xo_ref[0] = x + m[:, 2 * d:] * _rms(o, g1_ref[...])


def _row_specs(bx, s, tm, mod):
    sm = mod.shape[1]
    if sm == 1:
        mod_spec = pl.BlockSpec((1, 1, mod.shape[2]), lambda b, i: (b, 0, 0))
    else:
        mod_spec = pl.BlockSpec((1, tm, mod.shape[2]), lambda b, i: (b, i, 0))
    return mod_spec


def _full(shape):
    nd = len(shape)
    return pl.BlockSpec(shape, lambda b, i, _nd=nd: (0,) * _nd, pipeline_mode=pl.Buffered(1))


def _mixer_call(x, mod, g0, g1, w_in, cw, cb, w_out, prev, tm):
    bx, s, d = x.shape
    seq_mode = prev is None
    row = pl.BlockSpec((1, tm, d), lambda b, i: (b, i, 0))
    in_specs = [row, _row_specs(bx, s, tm, mod), _full((1, d)), _full((1, d)), _full(w_in.shape),
                _full(cw.shape), _full((1, d)), _full(w_out.shape)]
    args = [x, mod, g0, g1, w_in, cw, cb, w_out]
    if seq_mode:
        st_shape, st_spec = (bx, 8, d), pl.BlockSpec((1, 8, d), lambda b, i: (b, 0, 0))
        scratch = [pltpu.VMEM((8, d), F32)]
    else:
        in_specs += [row, row]
        args += [prev[0], prev[1]]
        st_shape, st_spec = (bx, s, d), row
        scratch = []
    return pl.pallas_call(
        functools.partial(_mixer_kernel, seq_mode, d), grid=(bx, s // tm),
        in_specs=in_specs, out_specs=[row, st_spec],
        out_shape=[jax.ShapeDtypeStruct((bx, s, d), F32), jax.ShapeDtypeStruct(st_shape, F32)],
        scratch_shapes=scratch, compiler_params=_cparams("arbitrary", "arbitrary"), name="mixer_a")(*args)


def _ffn_kernel(seq_mode, d, dff, n_chunks, *refs):
    if seq_mode:
        x_ref, mod_ref, g2_ref, g3_ref, wup_ref, cw_ref, cb_ref, wdn_ref, xo_ref, st_ref, carry = refs
    else:
        x_ref, mod_ref, g2_ref, g3_ref, wup_ref, cw_ref, cb_ref, wdn_ref, p0_ref, p1_ref, xo_ref, st_ref = refs
    x = x_ref[0]
    m = mod_ref[0]
    h = (_rms(x, g2_ref[...]) * (1.0 + m[:, d:2 * d]) + m[:, :d]).astype(BF16)
    if seq_mode:
        @pl.when(pl.program_id(1) == 0)
        def _():
            carry[...] = jnp.zeros_like(carry)
    cwid = dff // n_chunks
    acc = jnp.zeros((x.shape[0], d), F32)
    for k in range(n_chunks):
        halves = []
        for c0 in (k * cwid, dff + k * cwid):
            up = _dot(h, wup_ref[:, c0:c0 + cwid])
            if seq_mode:
                r1, r2 = _shift_rows(up, carry[:, c0:c0 + cwid])
                carry[:, c0:c0 + cwid] = up[-8:]
                st_ref[0, :, c0:c0 + cwid] = up[-8:]
            else:
                r1, r2 = p1_ref[0, :, c0:c0 + cwid], p0_ref[0, :, c0:c0 + cwid]
                st_ref[0, :, c0:c0 + cwid] = up
            halves.append(_conv3(up, r1, r2, cw_ref[:, c0:c0 + cwid], cb_ref[:, c0:c0 + cwid]))
        u, g = halves
        act = (g * _sigmoid(g)) * u
        acc = acc + _dot(act.astype(BF16), wdn_ref[k * cwid:(k + 1) * cwid, :])
    xo_ref[0] = x + m[:, 2 * d:] * _rms(acc, g3_ref[...])


def _ffn_call(x, mod, g2, g3, w_up, cw, cb, w_dn, prev, tm, n_chunks):
    bx, s, d = x.shape
    dff = w_dn.shape[0]
    seq_mode = prev is None
    row = pl.BlockSpec((1, tm, d), lambda b, i: (b, i, 0))
    in_specs = [row, _row_specs(bx, s, tm, mod), _full((1, d)), _full((1, d)), _full(w_up.shape),
                _full(cw.shape), _full((1, 2 * dff)), _full(w_dn.shape)]
    args = [x, mod, g2, g3, w_up, cw, cb, w_dn]
    if seq_mode:
        st_shape, st_spec = (bx, 8, 2 * dff), pl.BlockSpec((1, 8, 2 * dff), lambda b, i: (b, 0, 0))
        scratch = [pltpu.VMEM((8, 2 * dff), F32)]
    else:
        prow = pl.BlockSpec((1, tm, 2 * dff), lambda b, i: (b, i, 0))
        in_specs += [prow, prow]
        args += [prev[0], prev[1]]
        st_shape, st_spec = (bx, s, 2 * dff), prow
        scratch = []
    return pl.pallas_call(
        functools.partial(_ffn_kernel, seq_mode, d, dff, n_chunks), grid=(bx, s // tm),
        in_specs=in_specs, out_specs=[row, st_spec],
        out_shape=[jax.ShapeDtypeStruct((bx, s, d), F32), jax.ShapeDtypeStruct(st_shape, F32)],
        scratch_shapes=scratch, compiler_params=_cparams("arbitrary", "arbitrary"), name="conv_ffn")(*args)


def _kvq_kernel(d, n_rows_cols, x_ref, modkv_ref, mod_ref, gkv_ref, g0_ref, wkv_ref, wqg_ref,
                rows_ref, win_ref, q_ref, gate_ref):
    x = x_ref[0]
    r = x * lax.rsqrt(jnp.mean(x * x, axis=-1, keepdims=True) + EPS)
    mk = modkv_ref[0]
    hk = (r * gkv_ref[...]) * (1.0 + mk[:, d:]) + mk[:, :d]
    kv = _dot(hk.astype(BF16), wkv_ref[...])
    rows_ref[0] = kv[:, :n_rows_cols]
    win_ref[0] = kv[:, n_rows_cols:]
    m = mod_ref[0]
    h1 = (r * g0_ref[...]) * (1.0 + m[:, d:2 * d]) + m[:, :d]
    qg = _dot(h1.astype(BF16), wqg_ref[...])
    nq = N_HEADS * HEAD_DIM
    q_ref[0] = qg[:, :nq] * (HEAD_DIM ** -0.5)
    gate_ref[0] = _sigmoid(qg[:, nq:])


def _kvq_call(x, modkv, mod, gkv, g0, w_kv, w_qg, tm):
    bx, s, d = x.shape
    nkv = w_kv.shape[1]
    n_rows_cols = 4 * N_KV * HEAD_DIM
    n_win_cols = nkv - n_rows_cols
    nq = N_HEADS * HEAD_DIM
    ng = w_qg.shape[1] - nq
    row = lambda w: pl.BlockSpec((1, tm, w), lambda b, i: (b, i, 0))
    return pl.pallas_call(
        functools.partial(_kvq_kernel, d, n_rows_cols), grid=(bx, s // tm),
        in_specs=[row(d), _row_specs(bx, s, tm, modkv), _row_specs(bx, s, tm, mod), _full((1, d)), _full((1, d)),
                  _full(w_kv.shape), _full(w_qg.shape)],
        out_specs=[row(n_rows_cols), row(n_win_cols), row(nq), row(ng)],
        out_shape=[jax.ShapeDtypeStruct((bx, s, n_rows_cols), F32), jax.ShapeDtypeStruct((bx, s, n_win_cols), F32),
                   jax.ShapeDtypeStruct((bx, s, nq), F32), jax.ShapeDtypeStruct((bx, s, ng), F32)],
        compiler_params=_cparams("arbitrary", "arbitrary"), name="kv_q_proj")(x, modkv, mod, gkv, g0, w_kv, w_qg)


def _outproj_kernel(d, n_branch, *refs):
    x_ref, mod_ref, g1_ref, w_ref = refs[:4]
    xo_ref = refs[-1]
    if n_branch == 1:
        o = refs[4][0]
    else:
        o = sum(refs[4 + 2 * i][0] * refs[5 + 2 * i][0] for i in range(n_branch))
    y = _dot(o.astype(BF16), w_ref[...])
    xo_ref[0] = x_ref[0] + mod_ref[0][:, 2 * d:] * _rms(y, g1_ref[...])


def _outproj_call(x, mod, g1, w, branches, tm):
    bx, s, d = x.shape
    row = pl.BlockSpec((1, tm, d), lambda b, i: (b, i, 0))
    n_branch = 1 if len(branches) == 1 else len(branches) // 2
    return pl.pallas_call(
        functools.partial(_outproj_kernel, d, n_branch), grid=(bx, s // tm),
        in_specs=[row, _row_specs(bx, s, tm, mod), _full((1, d)), _full(w.shape)] + [row] * len(branches),
        out_specs=row, out_shape=jax.ShapeDtypeStruct((bx, s, d), F32),
        compiler_params=_cparams("arbitrary", "arbitrary"), name="attn_out_proj")(x, mod, g1, w, *branches)


def _gelu_tanh(x):
    return x * (0.5 * (1.0 + jnp.tanh(math.sqrt(2.0 / math.pi) * (x + 0.044715 * (x * x * x)))))


def _compress_kernel(tmb, x_ref, pe_ref, w1_ref, w2_ref, o_ref):
    acc = jnp.zeros((tmb, LANES), F32)
    for r in range(CMP_BLOCK):
        xr = x_ref[pl.ds(r, tmb, stride=CMP_BLOCK), :] + pe_ref[0, r]
        acc = acc + _dot(xr.astype(BF16), w1_ref[0, r])
    o_ref[...] = _dot(_gelu_tanh(acc).astype(BF16), w2_ref[0])


def _compress_call(rows2d, pe2, w1bd, w2bd):
    m = rows2d.shape[0] // CMP_BLOCK
    tmb = max(t for t in range(8, min(256, m) + 1, 8) if m % t == 0)
    return pl.pallas_call(
        functools.partial(_compress_kernel, tmb), grid=(m // tmb, 4),
        in_specs=[pl.BlockSpec((tmb * CMP_BLOCK, LANES), lambda i, j: (i, j)),
                  pl.BlockSpec((1, CMP_BLOCK, 1, LANES), lambda i, j: (j // 2, 0, 0, 0)),
                  pl.BlockSpec((1, CMP_BLOCK, LANES, LANES), lambda i, j: (j // 2, 0, 0, 0)),
                  pl.BlockSpec((1, LANES, LANES), lambda i, j: (j // 2, 0, 0))],
        out_specs=pl.BlockSpec((tmb, LANES), lambda i, j: (i, j)),
        out_shape=jax.ShapeDtypeStruct((m, 4 * LANES), F32),
        compiler_params=_cparams("arbitrary", "arbitrary"), name="compress")(rows2d, pe2, w1bd, w2bd)


def _compress_weights(cmp_pe, cmp_w1, cmp_w2):
    z = jnp.zeros((2, CMP_BLOCK, HEAD_DIM, HEAD_DIM), F32)
    w1 = cmp_w1.reshape(2, CMP_BLOCK, HEAD_DIM, HEAD_DIM)
    w1bd = jnp.concatenate([jnp.concatenate([w1, z], -1), jnp.concatenate([z, w1], -1)], -2).astype(BF16)
    z2 = jnp.zeros((2, HEAD_DIM, HEAD_DIM), F32)
    w2bd = jnp.concatenate([jnp.concatenate([cmp_w2, z2], -1), jnp.concatenate([z2, cmp_w2], -1)], -2).astype(BF16)
    pe2 = jnp.concatenate([cmp_pe, cmp_pe], -1)[:, :, None, :]
    return pe2, w1bd, w2bd


N_PAGE_CB = PAGE_SIZE // CMP_BLOCK
CB_ROWS = 8


PAGE_PITCH = N_KV * HEAD_DIM + 4


def _compress_pool_kernel(n_pg, n_i, x_hbm, pe_ref, mw_ref, w2_ref, o_ref, xbuf, sem):
    gw = N_KV * HEAD_DIM
    t = pl.program_id(0)
    n_t = pl.num_programs(0)
    slot_rows = n_pg * PAGE_PITCH

    def page_copy(step, slot, p):
        return pltpu.make_async_copy(x_hbm.at[(step % n_i) * n_pg + p, step // n_i],
                                     xbuf.at[pl.ds(slot * slot_rows + p * PAGE_PITCH, gw), :], sem.at[slot])

    def start_all(step, slot):
        for p in range(n_pg):
            page_copy(step, slot, p).start()

    slot = t % 2

    @pl.when(t == 0)
    def _():
        start_all(t, slot)

    @pl.when(t + 1 < n_t)
    def _():
        start_all(t + 1, 1 - slot)
    for p in range(n_pg):
        page_copy(t, slot, p).wait()

    def rows_of(g, d):
        return xbuf[pl.ds(slot * slot_rows + g * HEAD_DIM + d, n_pg, stride=PAGE_PITCH), :]
    lhs = jnp.concatenate(
        [(jnp.concatenate([rows_of(g, d) for g in range(N_KV)], axis=0) + pe_ref[0, d]).astype(BF16)
         for d in range(HEAD_DIM)], axis=1)
    z = _dot(_gelu_tanh(_dot(lhs, mw_ref[0])).astype(BF16), w2_ref[0])
    lane = lax.broadcasted_iota(jnp.int32, (n_pg, LANES), 1)
    low = lane < HEAD_DIM
    o_ref[...] = jnp.zeros(o_ref.shape, F32)
    for n in range(N_PAGE_CB):
        for c in range(N_KV // 2):
            a = z[(2 * c) * n_pg:(2 * c + 1) * n_pg, (n // 2) * LANES:(n // 2 + 1) * LANES]
            b = z[(2 * c + 1) * n_pg:(2 * c + 2) * n_pg, (n // 2) * LANES:(n // 2 + 1) * LANES]
            if n % 2 == 0:
                b = pltpu.roll(b, HEAD_DIM, 1)
            else:
                a = pltpu.roll(a, HEAD_DIM, 1)
            o_ref[c, pl.ds(n, n_pg, stride=CB_ROWS), :] = jnp.where(low, a, b)


def _compress_pool_call(cache_t, pe_t, mw, w2bd4):
    n_pool = cache_t.shape[0]
    n_pg = max(t for t in range(8, min(64, n_pool) + 1, 8) if n_pool % t == 0)
    n_i = n_pool // n_pg
    return pl.pallas_call(
        functools.partial(_compress_pool_kernel, n_pg, n_i), grid=(2 * n_i,),
        in_specs=[pl.BlockSpec(memory_space=pl.ANY),
                  pl.BlockSpec((1, HEAD_DIM, 1, PAGE_SIZE), lambda t: (t // n_i, 0, 0, 0)),
                  pl.BlockSpec((1, HEAD_DIM * PAGE_SIZE, N_PAGE_CB * HEAD_DIM), lambda t: (t // n_i, 0, 0)),
                  pl.BlockSpec((1, N_PAGE_CB * HEAD_DIM, N_PAGE_CB * HEAD_DIM), lambda t: (t // n_i, 0, 0))],
        out_specs=pl.BlockSpec((N_KV // 2, n_pg * CB_ROWS, LANES), lambda t: (t // n_i, t % n_i, 0)),
        out_shape=jax.ShapeDtypeStruct((N_KV, n_pool * CB_ROWS, LANES), F32),
        scratch_shapes=[pltpu.VMEM((2 * n_pg * PAGE_PITCH, PAGE_SIZE), F32), pltpu.SemaphoreType.DMA((2,))],
        compiler_params=_cparams("arbitrary"), name="compress_pool")(cache_t, pe_t, mw, w2bd4)


def _compress_pool_weights(cmp_pe, cmp_w1, cmp_w2):
    eye = np.eye(N_PAGE_CB, dtype=bool)
    w1t = cmp_w1.reshape(2, CMP_BLOCK, HEAD_DIM, HEAD_DIM).transpose(0, 2, 1, 3)
    mw = jnp.where(eye[None, None, :, None, :, None], w1t[:, :, None, :, None, :], 0.0)
    mw = mw.reshape(2, HEAD_DIM * PAGE_SIZE, N_PAGE_CB * HEAD_DIM).astype(BF16)
    w2bd4 = jnp.where(eye[None, :, None, :, None], cmp_w2[:, None, :, None, :], 0.0)
    w2bd4 = w2bd4.reshape(2, N_PAGE_CB * HEAD_DIM, N_PAGE_CB * HEAD_DIM).astype(BF16)
    pe_t = jnp.tile(cmp_pe.transpose(0, 2, 1), (1, 1, N_PAGE_CB))[:, :, None, :]
    return pe_t, mw, w2bd4


def _bias_of_distance(dist, tab):
    val = jnp.zeros(dist.shape, F32) + tab(0)
    for k in range(1, N_BUCKETS):
        val = jnp.where(dist >= BUCKET_THR[k], tab(k), val)
    return val


def _toeplitz_kernel(window, tab_ref, o_ref):
    g = pl.program_id(0)
    dt = pl.program_id(1)
    x = lax.broadcasted_iota(jnp.int32, (8, 2 * TK), 1)
    dist = dt * TQ - jnp.where(x < TK, x, x - 2 * TK)
    for hh in range(HPG):
        head = g * HPG + hh
        val = _bias_of_distance(dist, lambda k, head=head: tab_ref[k, head])
        val = jnp.where(dist < 0, NEG_MASK, val)
        if window is not None:
            val = jnp.where(dist >= window, NEG_MASK, val)
        tile = pltpu.roll(jnp.broadcast_to(val[0:1], (TQ, 2 * TK)), 0, 1, stride=1, stride_axis=0)
        o_ref[0, 0, hh * TQ:(hh + 1) * TQ, :] = tile[:, :TK]


def _toeplitz_call(rel_bias, n_chunks, window):
    return pl.pallas_call(
        functools.partial(_toeplitz_kernel, window), grid=(N_KV, n_chunks),
        in_specs=[pl.BlockSpec(memory_space=pltpu.SMEM)],
        out_specs=pl.BlockSpec((1, 1, HPG * TQ, TK), lambda g, t: (g, t, 0, 0)),
        out_shape=jax.ShapeDtypeStruct((N_KV, n_chunks, HPG * TQ, TK), F32),
        compiler_params=_cparams("arbitrary", "arbitrary"), name="bias_toeplitz")(rel_bias)


def _cmp_bias_kernel(tab_ref, o_ref):
    g = pl.program_id(0)
    qi = pl.program_id(1)
    row = lax.broadcasted_iota(jnp.int32, (LANES, TQ), 0)
    t_q = qi * TQ + lax.broadcasted_iota(jnp.int32, (LANES, TQ), 1)
    cblk = 2 * (row % HEAD_DIM) + row // HEAD_DIM
    dist = t_q - (cblk * CMP_BLOCK + (CMP_BLOCK - 1))
    for hh in range(HPG):
        head = g * HPG + hh
        val = _bias_of_distance(dist, lambda k, head=head: tab_ref[k, head])
        o_ref[0, 0, :, hh * TQ:(hh + 1) * TQ] = jnp.where(dist < 0, NEG_MASK, val)


def _cmp_bias_call(rel_bias, n_qt):
    return pl.pallas_call(
        _cmp_bias_kernel, grid=(N_KV, n_qt),
        in_specs=[pl.BlockSpec(memory_space=pltpu.SMEM)],
        out_specs=pl.BlockSpec((1, 1, LANES, HPG * TQ), lambda g, t: (g, t, 0, 0)),
        out_shape=jax.ShapeDtypeStruct((N_KV, n_qt, LANES, HPG * TQ), F32),
        compiler_params=_cparams("arbitrary", "arbitrary"), name="bias_cmp")(rel_bias)


def _bias_cols_kernel(dist_ref, tab_ref, o_ref):
    dist = dist_ref[...]
    val = _bias_of_distance(dist, lambda k: tab_ref[:, k:k + 1])
    o_ref[...] = jnp.where(dist < 0, NEG_MASK, val)


def _bias_cols_call(dist, tab_heads):
    r = dist.shape[1]
    return pl.pallas_call(
        _bias_cols_kernel, grid=(1,),
        in_specs=[pl.BlockSpec((N_HEADS, r), lambda i: (0, 0)), pl.BlockSpec((N_HEADS, N_BUCKETS), lambda i: (0, 0))],
        out_specs=pl.BlockSpec((N_HEADS, r), lambda i: (0, 0)),
        out_shape=jax.ShapeDtypeStruct((N_HEADS, r), F32),
        compiler_params=_cparams("arbitrary"), name="bias_cols")(dist, tab_heads)


def _kv_prep_kernel(ksel_ref, vsel_ref, kwin_ref, vwin_ref, kaug_ref, v1_ref, kw_ref, vw1_ref):
    g_odd = (pl.program_id(1) % 2) == 1
    ch = ksel_ref.shape[1]
    lane = lax.broadcasted_iota(jnp.int32, (ch, LANES), 1)
    row = pl.program_id(2) * ch + lax.broadcasted_iota(jnp.int32, (ch, LANES), 0)
    low = lane < HEAD_DIM
    ones = jnp.where(lane >= HEAD_DIM, 1.0, 0.0)
    onehot = jnp.where(lane - HEAD_DIM == row // SEL_BLOCK, 1.0, 0.0)

    def pick(ref):
        x = ref[0]
        return jnp.where(g_odd, pltpu.roll(x, HEAD_DIM, 1), x)

    kaug_ref[0, 0] = jnp.where(low, pick(ksel_ref), onehot).astype(BF16)
    v1_ref[0, 0] = jnp.where(low, pick(vsel_ref), ones).astype(BF16)
    kw_ref[0, 0] = jnp.where(low, pick(kwin_ref), 0.0).astype(BF16)
    vw1_ref[0, 0] = jnp.where(low, pick(vwin_ref), ones).astype(BF16)


def _kv_prep_call(rows, win):
    b, s, _ = rows.shape
    ch = min(1024, s)
    kv_spec = lambda col0: pl.BlockSpec((1, ch, LANES), lambda bb, g, c, _c=col0: (bb, c, _c + g // 2))
    o_spec = pl.BlockSpec((1, 1, ch, LANES), lambda bb, g, c: (bb, g, c, 0))
    o_shape = jax.ShapeDtypeStruct((b, N_KV, s, LANES), BF16)
    return pl.pallas_call(
        _kv_prep_kernel, grid=(b, N_KV, s // ch),
        in_specs=[kv_spec(4), kv_spec(6), kv_spec(0), kv_spec(2)],
        out_specs=[o_spec] * 4, out_shape=[o_shape] * 4,
        compiler_params=_cparams("arbitrary", "arbitrary", "arbitrary"), name="kv_prep")(rows, rows, win, win)


def _nsa_seq_kernel(n_sb, q_ref, gate_ref, kaug_ref, v1_ref, kw_ref, vw1_ref, kcb_ref, vcb_ref, tcmp_ref,
                    tsel_ref, twin_ref, gexp_ref, o_ref, qaug, qw_sc, s_sc, mrun, mb, acc_sel, acc_win, score_sc):
    qi = pl.program_id(2)
    rows = HPG * TQ
    lane = lax.broadcasted_iota(jnp.int32, (TQ, LANES), 1)
    low = lane < HEAD_DIM

    qh = []
    for hh in range(HPG):
        qv = q_ref[0, :, (hh // 2) * LANES:(hh // 2 + 1) * LANES]
        if hh % 2 == 1:
            qv = pltpu.roll(qv, HEAD_DIM, 1)
        qh.append(jnp.where(low, qv, 0.0))
    qw = jnp.concatenate(qh, axis=0).astype(BF16)
    qw_sc[...] = qw

    s_c = _dot_nt(kcb_ref[0, 0].astype(BF16), qw) + tcmp_ref[0, 0]
    ok_c = s_c > 0.5 * NEG_MASK
    mx = jnp.max(s_c, axis=0, keepdims=True)
    p = jnp.where(ok_c, jnp.exp(s_c - mx), 0.0)
    den = jnp.sum(p, axis=0, keepdims=True)
    pn_t = p / jnp.where(den > 0, den, 1.0)
    o_c = _dot(pn_t.T.astype(BF16), vcb_ref[0, 0].astype(BF16))
    imp = pn_t[:, 0:TQ]
    for hh in range(1, HPG):
        imp = imp + pn_t[:, hh * TQ:(hh + 1) * TQ]
    n_blk = LANES // 2
    imp = imp[:n_blk] + imp[n_blk:]

    blk = lax.broadcasted_iota(jnp.int32, (n_blk, TQ), 0)
    cur = (qi * TQ + lax.broadcasted_iota(jnp.int32, (n_blk, TQ), 1)) // SEL_BLOCK
    valid = blk <= cur
    forced = (blk == 0) | (blk == cur) | (blk == cur - 1)
    score_sc[...] = jnp.where(valid & forced, BIG_SCORE, jnp.where(valid, imp, -BIG_SCORE))
    sub = 8
    groups = [score_sc[r0:r0 + sub, :] for r0 in range(0, n_blk, sub)]
    rowg = lax.broadcasted_iota(jnp.int32, (sub, TQ), 0)
    cnts = [jnp.zeros((sub, TQ), jnp.int32) for _ in groups]
    for bp in range(n_blk):
        r = score_sc[bp:bp + 1, :]
        for gi, s_g in enumerate(groups):
            if gi * sub > bp:
                ahead = r >= s_g
            elif (gi + 1) * sub - 1 < bp:
                ahead = r > s_g
            else:
                ahead = (r > s_g) | ((r == s_g) & (rowg + gi * sub > bp))
            cnts[gi] = cnts[gi] + jnp.where(ahead, 1, 0)
    cnt = jnp.concatenate(cnts, axis=0)
    selmask_t = jnp.where((cnt < min(TOP_N, n_sb)) & valid, 0.0, NEG_SEL)
    selmask = jnp.concatenate([jnp.zeros((n_blk, TQ), F32), selmask_t], axis=0).T
    for hh in range(HPG):
        qaug[hh * TQ:(hh + 1) * TQ, :] = jnp.where(low, qh[hh], selmask).astype(BF16)

    def branch(q_sc, k_ref, v_ref, t_ref, acc_sc, n_tiles, n_tbl, widths):
        mrun[...] = jnp.full(mrun.shape, NEG_MASK, F32)

        def tile_loop(step):
            done = 0
            for width in widths:
                def group(j, _, done=done, width=width):
                    step(tuple(done + width * j + u for u in range(width)))
                    return 0
                n_groups = (n_tiles - done) // width
                lax.fori_loop(0, n_groups, group, 0)
                done = done + n_groups * width

        def scores(tiles):
            m = mrun[...]
            for i in tiles:
                r0 = pl.multiple_of((qi - i) * TK, TK)
                s = _dot_nt(q_sc[...], k_ref[0, 0, pl.ds(r0, TK), :]) + t_ref[0, jnp.minimum(i, n_tbl)]
                s_sc[i] = s
                for c in range(TK // LANES):
                    m = jnp.maximum(m, s[:, c * LANES:(c + 1) * LANES])
            mrun[...] = m
        tile_loop(scores)
        mb[...] = jnp.broadcast_to(jnp.max(mrun[...], axis=1, keepdims=True), (rows, LANES))
        acc_sc[...] = jnp.zeros(acc_sc.shape, F32)

        def weigh(tiles):
            mbv = mb[...]
            mb2 = jnp.concatenate([mbv] * (TK // LANES), axis=1)
            acc = acc_sc[...]
            for i in tiles:
                r0 = pl.multiple_of((qi - i) * TK, TK)
                acc = acc + _dot(jnp.exp(s_sc[i] - mb2).astype(BF16), v_ref[0, 0, pl.ds(r0, TK), :])
            acc_sc[...] = acc
        tile_loop(weigh)
        return acc_sc[...]

    acc_s = branch(qaug, kaug_ref, v1_ref, tsel_ref, acc_sel, qi + 1, N_TBL, (4, 2, 1))
    acc_w = branch(qw_sc, kw_ref, vw1_ref, twin_ref, acc_win, jnp.minimum(qi, N_WIN_TILES - 1) + 1, N_WIN_TILES - 1,
                   (N_WIN_TILES, 2, 1))

    gates = gate_ref[0]
    g_hi = gates.astype(BF16)
    g_lo = (gates - g_hi.astype(F32)).astype(BF16)
    g_all = _dot(jnp.concatenate([g_hi, g_lo], axis=1), gexp_ref[...])

    def normalized(acc, odd):
        swapped = pltpu.roll(acc, HEAD_DIM, 1)
        return swapped / acc if odd else acc / swapped

    outs = []
    for hh in range(HPG):
        gb = [g_all[:, (hh * 3 + br) * LANES:(hh * 3 + br + 1) * LANES] for br in range(3)]
        rs = slice(hh * TQ, (hh + 1) * TQ)
        outs.append(gb[0] * o_c[rs] + gb[1] * normalized(acc_s[rs], hh % 2) + gb[2] * normalized(acc_w[rs], hh % 2))
    for c in range(HPG // 2):
        o_ref[0, :, c * LANES:(c + 1) * LANES] = jnp.where(low, outs[2 * c], outs[2 * c + 1]).astype(o_ref.dtype)


def _nsa_seq_call(q, gates, kaug, v1, kw, vw1, kcb, vcb, tcmp, tsel, twin):
    b, s, _ = q.shape
    n_qt = s // TQ
    rows = HPG * TQ
    kv_spec = pl.BlockSpec((1, 1, s, LANES), lambda bb, g, i: (bb, g, 0, 0))
    cb_spec = pl.BlockSpec((1, 1, kcb.shape[2], LANES), lambda bb, g, i: (bb, g, 0, 0))
    tbl_spec = lambda t: pl.BlockSpec((1,) + t.shape[1:], lambda bb, g, i: (g, 0, 0, 0))
    n_gate = HPG * 3
    gexp = np.zeros((2 * LANES, n_gate * LANES), np.float32)
    for c in range(n_gate):
        gexp[c, c * LANES:(c + 1) * LANES] = 1.0
        gexp[LANES + c, c * LANES:(c + 1) * LANES] = 1.0
    gexp = jnp.asarray(gexp, BF16)
    return pl.pallas_call(
        functools.partial(_nsa_seq_kernel, s // SEL_BLOCK), grid=(b, N_KV, n_qt),
        in_specs=[pl.BlockSpec((1, TQ, HPG * HEAD_DIM), lambda bb, g, i: (bb, i, g)),
                  pl.BlockSpec((1, TQ, LANES), lambda bb, g, i: (bb, i, g)),
                  kv_spec, kv_spec, kv_spec, kv_spec, cb_spec, cb_spec,
                  pl.BlockSpec((1, 1, LANES, rows), lambda bb, g, i: (g, i, 0, 0)), tbl_spec(tsel), tbl_spec(twin),
                  pl.BlockSpec(gexp.shape, lambda bb, g, i: (0, 0))],
        out_specs=pl.BlockSpec((1, TQ, HPG * HEAD_DIM), lambda bb, g, i: (bb, i, g)),
        out_shape=jax.ShapeDtypeStruct((b, s, N_HEADS * HEAD_DIM), BF16),
        scratch_shapes=[pltpu.VMEM((rows, LANES), BF16), pltpu.VMEM((rows, LANES), BF16),
                        pltpu.VMEM((s // TK, rows, TK), F32), pltpu.VMEM((rows, LANES), F32),
                        pltpu.VMEM((rows, LANES), F32), pltpu.VMEM((rows, LANES), F32),
                        pltpu.VMEM((rows, LANES), F32), pltpu.VMEM((LANES // 2, TQ), F32)],
        compiler_params=_cparams("arbitrary", "arbitrary", "arbitrary"), name="nsa_seq")(
            q, gates, kaug, v1, kw, vw1, kcb, vcb, tcmp, tsel, twin, gexp)


def _softmax_lanes(s):
    p = jnp.exp(s - jnp.max(s, axis=1, keepdims=True))
    return p / jnp.sum(p, axis=1, keepdims=True)


def _rows_to_heads(rows):
    hg = lax.broadcasted_iota(jnp.int32, (N_HEADS, rows[0].shape[1]), 0) // HPG
    out = jnp.broadcast_to(rows[0], hg.shape)
    for g in range(1, N_KV):
        out = jnp.where(hg == g, rows[g], out)
    return out


def _nsa_stepT_kernel(n_pages, pt_ref, *refs):
    del pt_ref
    q_ref = refs[0]
    cb_refs = refs[1:1 + n_pages]
    pg_refs = refs[1 + n_pages:1 + 2 * n_pages]
    (kvnew_ref, winnew_ref, wcol_ref, cwin_ref, bsel_ref, bnew_ref, bwin_ref, bcmp_ref) = refs[1 + 2 * n_pages:9 + 2 * n_pages]
    oc_ref, os_ref, ow_ref, nwin_ref = refs[9 + 2 * n_pages:13 + 2 * n_pages]
    (s_sc,) = refs[13 + 2 * n_pages:]
    b = pl.program_id(0)
    gw = N_KV * HEAD_DIM
    n_past = n_pages * PAGE_SIZE
    q16 = q_ref[0].astype(BF16)
    qf = q16.astype(F32)
    lane = lax.broadcasted_iota(jnp.int32, (N_HEADS, LANES), 1)

    n_pad = LANES - CB_ROWS * n_pages
    cb = jnp.concatenate([jnp.concatenate([r[c] for c in range(N_KV)], axis=1) for r in cb_refs]
                         + ([jnp.zeros((n_pad, 2 * gw), F32)] if n_pad else []), axis=0)
    pn_c = _softmax_lanes(_dot_nt(q16, cb[:, :gw].astype(BF16)) + bcmp_ref[...])
    oc_ref[0] = _dot(pn_c.astype(BF16), cb[:, gw:].astype(BF16))

    grp = [pn_c[HPG * g:HPG * g + 1] + pn_c[HPG * g + 1:HPG * g + 2] + pn_c[HPG * g + 2:HPG * g + 3]
           + pn_c[HPG * g + 3:HPG * g + 4] for g in range(N_KV)]
    lane8 = lax.broadcasted_iota(jnp.int32, (8, LANES), 1)
    row8 = lax.broadcasted_iota(jnp.int32, (8, LANES), 0)
    imp = jnp.zeros((8, LANES), F32)
    for g in range(N_KV):
        imp = jnp.where(row8 == g, grp[g], imp)
    imp = imp + pltpu.roll(imp, LANES - 1, 1)
    n_past_blk = n_past // SEL_BLOCK
    cur_lane = LANES - 2
    is_blk = ((lane8 % CB_ROWS == 0) | (lane8 % CB_ROWS == 2)) & (lane8 < CB_ROWS * n_pages)
    last_lane = ((n_past_blk - 1) // 2) * CB_ROWS + 2 * ((n_past_blk - 1) % 2)
    forced = (lane8 == 0) | (lane8 == last_lane) | (lane8 == cur_lane)
    valid = is_blk | (lane8 == cur_lane)
    score = jnp.where(valid & forced, BIG_SCORE, jnp.where(valid, imp, -BIG_SCORE))
    cnt = jnp.zeros((8, LANES), jnp.int32)
    for k in range(2, LANES, 2):
        r = pltpu.roll(score, k, 1)
        cnt = cnt + jnp.where((r > score) | ((r == score) & (lane8 >= k)), 1, 0)
    selrows = jnp.where((cnt < min(TOP_N, n_past_blk + 1)) & valid, 0.0, NEG_SEL)
    selmask = _rows_to_heads([selrows[g:g + 1] for g in range(N_KV)])

    blocks_per_page = PAGE_SIZE // SEL_BLOCK
    for p in range(n_pages):
        kt = pg_refs[p][0, 0:gw, :].astype(BF16)
        msk = selmask[:, CB_ROWS * p:CB_ROWS * p + 1]
        for i in range(1, blocks_per_page):
            msk = jnp.where(lane >= i * SEL_BLOCK, selmask[:, CB_ROWS * p + 2 * i:CB_ROWS * p + 2 * i + 1], msk)
        s_sc[:, p * PAGE_SIZE:(p + 1) * PAGE_SIZE] = _dot(q16, kt) + bsel_ref[:, p * PAGE_SIZE:(p + 1) * PAGE_SIZE] + msk
    knew = kvnew_ref[0][:, 2 * gw:3 * gw].astype(BF16).astype(F32)
    s_new = jnp.sum(qf * knew, axis=1, keepdims=True) + bnew_ref[:, 0:1] + selmask[:, cur_lane:cur_lane + 1]
    s_sc[:, n_past:n_past + LANES] = jnp.where(lane == 0, s_new, NEG_MASK)
    s_all = s_sc[...]
    mx = jnp.max(s_all, axis=1, keepdims=True)
    den = jnp.sum(jnp.exp(s_all - mx), axis=1, keepdims=True)
    acc = jnp.zeros((N_HEADS, gw), F32)
    for p in range(n_pages):
        pn = jnp.exp(s_sc[:, p * PAGE_SIZE:(p + 1) * PAGE_SIZE] - mx) / den
        acc = acc + _dot_nt(pn.astype(BF16), pg_refs[p][0, gw:2 * gw, :].astype(BF16))
    pn_new = jnp.exp(s_new - mx) / den
    vnew = kvnew_ref[0][:, 3 * gw:4 * gw].astype(BF16).astype(F32)
    os_ref[0] = acc + pn_new.astype(BF16).astype(F32) * vnew

    cw = cwin_ref[0]
    w_len = cw.shape[1]
    s_w = _dot(q16, cw[0:gw].astype(BF16)) + bwin_ref[...]
    kwn = winnew_ref[0][:, 0:gw].astype(BF16).astype(F32)
    s_wn = jnp.sum(qf * kwn, axis=1, keepdims=True) + bnew_ref[:, 0:1]
    mxw = jnp.maximum(jnp.max(s_w, axis=1, keepdims=True), s_wn)
    pw = jnp.exp(s_w - mxw)
    pwn = jnp.exp(s_wn - mxw)
    denw = jnp.sum(pw, axis=1, keepdims=True) + pwn
    vwn = winnew_ref[0][:, gw:2 * gw].astype(BF16).astype(F32)
    ow_ref[0] = _dot_nt((pw / denw).astype(BF16), cw[gw:2 * gw].astype(BF16)) + (pwn / denw).astype(BF16).astype(F32) * vwn
    lane_w = lax.broadcasted_iota(jnp.int32, cw.shape, 1)
    nb_l = wcol_ref.shape[1]
    lane_b = lax.broadcasted_iota(jnp.int32, (cw.shape[0], nb_l), 1)
    col = jnp.sum(jnp.where(lane_b == b, wcol_ref[...], 0.0), axis=1, keepdims=True)
    nwin_ref[0] = jnp.where(lane_w == w_len - 1, col, pltpu.roll(cw, w_len - 1, 1))


def _nsa_step_call(page_table, qrows, cb_pool, cache_t, kvnew, winnew, wcol, cwin_t, bsel, bnew, bwin, bcmp):
    nb, n_pages = page_table.shape
    gw = N_KV * HEAD_DIM
    w_len = cwin_t.shape[2]
    cb_specs = [pl.BlockSpec((N_KV, None, CB_ROWS, LANES), lambda b, pt, _p=p: (0, pt[b, _p], 0, 0))
                for p in range(n_pages)]
    pg_specs = [pl.BlockSpec((1, 2 * gw, PAGE_SIZE), lambda b, pt, _p=p: (pt[b, _p], 1, 0)) for p in range(n_pages)]
    const = lambda a: pl.BlockSpec(a.shape, lambda b, pt: (0, 0))
    o_spec = pl.BlockSpec((1, N_HEADS, gw), lambda b, pt: (b, 0, 0))
    win_spec = pl.BlockSpec((1, 2 * gw, w_len), lambda b, pt: (b, 0, 0))
    grid_spec = pltpu.PrefetchScalarGridSpec(
        num_scalar_prefetch=1, grid=(nb,),
        in_specs=[pl.BlockSpec((1, N_HEADS, gw), lambda b, pt: (b, 0, 0))] + cb_specs + pg_specs + [
            pl.BlockSpec((1, 1, 4 * gw), lambda b, pt: (b, 0, 0)),
            pl.BlockSpec((1, 1, 2 * gw), lambda b, pt: (b, 0, 0)),
            const(wcol), win_spec, const(bsel), const(bnew), const(bwin), const(bcmp)],
        out_specs=[o_spec, o_spec, o_spec, win_spec],
        scratch_shapes=[pltpu.VMEM((N_HEADS, n_pages * PAGE_SIZE + LANES), F32)])
    o_shape = jax.ShapeDtypeStruct((nb, N_HEADS, gw), F32)
    return pl.pallas_call(
        functools.partial(_nsa_stepT_kernel, n_pages), grid_spec=grid_spec,
        out_shape=[o_shape, o_shape, o_shape, jax.ShapeDtypeStruct((nb, 2 * gw, w_len), F32)],
        compiler_params=_cparams("arbitrary"), name="nsa_step")(
            page_table, qrows, *([cb_pool] * n_pages), *([cache_t] * n_pages), kvnew, winnew, wcol, cwin_t,
            bsel, bnew, bwin, bcmp)


def _head_diag(o):
    b = o.shape[0]
    o5 = o.reshape(b, N_KV, HPG, N_KV, HEAD_DIM)
    return jnp.stack([o5[:, g, :, g, :] for g in range(N_KV)], axis=1).reshape(b, N_HEADS * HEAD_DIM)


def kernel(x_prompt, x_sample, c_prompt, c_sample, cache_kv, cache_win, state_conv_a, state_ffn_conv, page_table, mod_w, mod_b, norm_g, a_w_in, a_conv_w, a_conv_b, a_w_out, kv_mod_w, kv_mod_b, kv_norm_g, w_kv, cmp_pe, cmp_w1, cmp_w2, b_w_qg, b_w_out, rel_bias, ffn_w_up, ffn_conv_w, ffn_conv_b, ffn_w_down):
    bp, s, d = x_prompt.shape
    bs = x_sample.shape[0]
    depth = mod_w.shape[0]
    n_a = a_w_in.shape[0]
    assert depth == 2 and n_a == 1 and x_sample.shape[1] == 1
    dff = ffn_w_down.shape[1]
    n_pool = cache_kv.shape[0]
    n_pages = page_table.shape[1]
    past_len = n_pages * PAGE_SIZE
    gw = N_KV * HEAD_DIM
    nq = N_HEADS * HEAD_DIM

    n_c = bp + bs
    n_cp = -(-n_c // 8) * 8
    c_all = jnp.pad(jnp.concatenate([c_prompt, c_sample], 0), ((0, n_cp - n_c), (0, 0)))
    mods = _mod_call(c_all, mod_w.reshape(depth * 2, d, 3 * d), mod_b.reshape(depth * 2, 1, 3 * d))
    modkv = _mod_call(c_all, kv_mod_w[None], kv_mod_b[None, None])[0]
    mod_p = lambda i: mods[i, :bp][:, None, :]
    mod_s = lambda i: mods[i, bp:n_c][None]

    w_in = a_w_in[0].astype(BF16)
    w_out_a = a_w_out[0].astype(BF16)
    w_up = ffn_w_up.astype(BF16)
    w_dn = ffn_w_down.astype(BF16)
    w_kv_b = w_kv.astype(BF16)
    ng_pad = -(-(b_w_qg.shape[2] - nq) // LANES) * LANES
    w_qg = jnp.pad(b_w_qg[0], ((0, 0), (0, nq + ng_pad - b_w_qg.shape[2]))).astype(BF16)
    w_out_b = b_w_out[0].astype(BF16)
    g = lambda l, i: norm_g[l, i][None]
    pe2, w1bd, w2bd = _compress_weights(cmp_pe, cmp_w1, cmp_w2)
    n_chunks = 2

    tm = min(512, s)
    tm_ffn = min(512, s)
    x1, st_a = _mixer_call(x_prompt, mod_p(0), g(0, 0), g(0, 1), w_in, a_conv_w[0], a_conv_b[0][None], w_out_a, None, tm)
    x2, st_f0 = _ffn_call(x1, mod_p(1), g(0, 2), g(0, 3), w_up[0], ffn_conv_w[0], ffn_conv_b[0][None], w_dn[0], None, tm_ffn, n_chunks)
    rows, win, q, gates = _kvq_call(x2, modkv[:bp][:, None, :], mod_p(2), kv_norm_g[None], g(1, 0), w_kv_b, w_qg, tm)
    cb = _compress_call(rows.reshape(bp * s, 4 * gw), pe2, w1bd, w2bd)
    n_cbk = s // CMP_BLOCK
    assert n_cbk <= LANES
    cb = jnp.pad(cb.reshape(bp, n_cbk, 2, N_KV, HEAD_DIM), ((0, 0), (0, LANES - n_cbk), (0, 0), (0, 0), (0, 0)))
    cb = cb.reshape(bp, LANES // 2, 2, 2, N_KV, HEAD_DIM)
    cb = cb.transpose(3, 0, 4, 2, 1, 5).reshape(2, bp, N_KV, LANES, HEAD_DIM)
    kcb = jnp.pad(cb[0], ((0, 0),) * 3 + ((0, LANES - HEAD_DIM),))
    vcb = jnp.concatenate([cb[1], cb[1]], axis=-1)
    gates_g = jnp.pad(gates[:, :, :N_HEADS * 3].reshape(bp, s, N_KV, HPG * 3),
                      ((0, 0), (0, 0), (0, 0), (0, LANES - HPG * 3))).reshape(bp, s, N_KV * LANES)
    tsel = _toeplitz_call(rel_bias, N_TBL + 1, None)
    twin = _toeplitz_call(rel_bias, N_WIN_TILES, WINDOW)
    tcmp = _cmp_bias_call(rel_bias, s // TQ)
    kaug, v1, kw, vw1 = _kv_prep_call(rows, win)
    o_att = _nsa_seq_call(q, gates_g, kaug, v1, kw, vw1, kcb, vcb, tcmp, tsel, twin)
    x3 = _outproj_call(x2, mod_p(2), g(1, 1), w_out_b, [o_att], tm)
    y_prompt, st_f1 = _ffn_call(x3, mod_p(3), g(1, 2), g(1, 3), w_up[1], ffn_conv_w[1], ffn_conv_b[1][None], w_dn[1], None, tm_ffn, n_chunks)
    kv_p = rows.reshape(bp, s, 4, N_KV, HEAD_DIM)
    keep = min(WINDOW, s)
    win_p = win[:, s - keep:].reshape(bp, keep, 2, N_KV, HEAD_DIM)
    conv_a_p = st_a[None, :, 6:8]
    ffn_p = jnp.stack([st_f0[:, 6:8], st_f1[:, 6:8]])

    xs = x_sample.reshape(1, bs, d)
    prev_a = (state_conv_a[0, :, 0][None], state_conv_a[0, :, 1][None])
    xs1, v_a = _mixer_call(xs, mod_s(0), g(0, 0), g(0, 1), w_in, a_conv_w[0], a_conv_b[0][None], w_out_a, prev_a, bs)
    prev_f = lambda l: (state_ffn_conv[l, :, 0][None], state_ffn_conv[l, :, 1][None])
    xs2, up0 = _ffn_call(xs1, mod_s(1), g(0, 2), g(0, 3), w_up[0], ffn_conv_w[0], ffn_conv_b[0][None], w_dn[0], prev_f(0), bs, n_chunks)
    rows_s, win_s, q_s, gates_s = _kvq_call(xs2, modkv[bp:n_c][None], mod_s(2), kv_norm_g[None], g(1, 0), w_kv_b, w_qg, bs)
    cache_t = cache_kv.transpose(0, 2, 3, 4, 1).reshape(n_pool, 4 * gw, PAGE_SIZE)
    w_len = cache_win.shape[1]
    cwin_t = cache_win.transpose(0, 2, 3, 4, 1).reshape(bs, 2 * gw, w_len)
    cb_pool = _compress_pool_call(cache_t.reshape(n_pool, 4, gw, PAGE_SIZE),
                                  *_compress_pool_weights(cmp_pe, cmp_w1, cmp_w2))
    cb_pool = cb_pool.reshape(N_KV, n_pool, CB_ROWS, LANES)
    assert n_pages * CB_ROWS <= LANES
    d_sel = past_len - np.arange(past_len)
    d_new = np.where(np.arange(LANES) == 0, 0, -1)
    d_win = w_len - np.arange(w_len)
    d_win = np.where(d_win < WINDOW, d_win, -1)
    cl = np.arange(LANES)
    d_cmp = past_len - (((cl // CB_ROWS) * N_PAGE_CB + cl % CB_ROWS) * CMP_BLOCK + CMP_BLOCK - 1)
    d_cmp = np.where((cl % CB_ROWS < N_PAGE_CB) & (cl // CB_ROWS < n_pages), d_cmp, -1)
    assert d_sel.min() >= 0 and d_cmp[d_cmp != -1].min() >= 0
    dist = np.concatenate([d_sel, d_new, d_win, d_cmp]).astype(np.int32)
    bias_cols = _bias_cols_call(jnp.asarray(np.repeat(dist[None, :], N_HEADS, 0)), rel_bias.T)
    bsel, bnew = bias_cols[:, :past_len], bias_cols[:, past_len:past_len + LANES]
    bwin = bias_cols[:, past_len + LANES:past_len + LANES + w_len]
    bcmp = bias_cols[:, past_len + LANES + w_len:]
    head_group = (np.arange(N_HEADS)[:, None] // HPG == np.arange(N_KV)[None, :])[None, :, :, None]
    qrows = jnp.where(head_group, q_s.reshape(bs, N_HEADS, 1, HEAD_DIM), 0.0).reshape(bs, N_HEADS, gw)
    oc, os_, ow, nwin_t = _nsa_step_call(
        page_table, qrows, cb_pool, cache_t, rows_s.reshape(bs, 1, 4 * gw),
        win_s.reshape(bs, 1, 2 * gw), win_s[0].T, cwin_t, bsel, bnew, bwin, bcmp)
    nwin = nwin_t.reshape(bs, 2, N_KV, HEAD_DIM, w_len).transpose(0, 4, 1, 2, 3)
    gts = gates_s[0, :, :N_HEADS * 3].reshape(bs, N_HEADS, 3)
    branches = []
    for br, o in enumerate((oc, os_, ow)):
        branches += [jnp.repeat(gts[:, :, br], HEAD_DIM, axis=1)[None], _head_diag(o)[None]]
    xs3 = _outproj_call(xs2, mod_s(2), g(1, 1), w_out_b, branches, bs)
    ys, up1 = _ffn_call(xs3, mod_s(3), g(1, 2), g(1, 3), w_up[1], ffn_conv_w[1], ffn_conv_b[1][None], w_dn[1], prev_f(1), bs, n_chunks)
    y_sample = ys.reshape(bs, 1, d)
    kv_s = rows_s.reshape(bs, 1, 4, N_KV, HEAD_DIM)
    win_state_s = nwin
    conv_a_s = jnp.stack([state_conv_a[0, :, 1], v_a[0]], axis=1)[None]
    ffn_s = jnp.stack([jnp.stack([state_ffn_conv[l, :, 1], u[0]], axis=1) for l, u in ((0, up0), (1, up1))])
    return (y_prompt, y_sample, kv_p, kv_s, win_p, win_state_s, conv_a_p, conv_a_s, ffn_p, ffn_s)
```

```python
import functools
import math

import numpy as np
import jax
import jax.numpy as jnp
from jax import lax
from jax.experimental import pallas as pl
from jax.experimental.pallas import tpu as pltpu

F32 = jnp.float32
BF16 = jnp.bfloat16

N_HEADS = 16
HEAD_DIM = 64
N_KV = 4
HPG = N_HEADS // N_KV
CMP_BLOCK = 32
SEL_BLOCK = 64
TOP_N = 16
WINDOW = 512
N_BUCKETS = 32
MAX_EXACT = N_BUCKETS // 2
MAX_DISTANCE = 1024
PAGE_SIZE = 128
EPS = 1e-6

LANES = 128
TQ = 256
TK = 256
NEG_MASK = -1e30
NEG_SEL = -1e9
BIG_SCORE = 1e30
VMEM_LIMIT_BYTES = 56 * 1024 * 1024


def _bucket_thresholds():
    d = np.arange(0, 4 * MAX_DISTANCE)
    nf = np.maximum(d, 1).astype(np.float32)
    large = MAX_EXACT + (np.log(nf / MAX_EXACT) / math.log(MAX_DISTANCE / MAX_EXACT)
                         * (N_BUCKETS - MAX_EXACT)).astype(np.int32)
    bucket = np.where(d < MAX_EXACT, d, np.minimum(large, N_BUCKETS - 1))
    assert np.all(np.diff(bucket) >= 0)
    return [int(np.argmax(bucket >= k)) for k in range(N_BUCKETS)]


BUCKET_THR = _bucket_thresholds()
assert TQ == TK
N_TBL = -(-(BUCKET_THR[-1] + TK - 1) // TQ)
N_WIN_TILES = WINDOW // TK + 1


def _cparams(*sem):
    return pltpu.CompilerParams(dimension_semantics=sem, vmem_limit_bytes=VMEM_LIMIT_BYTES)


def _dot(a, b):
    return jnp.dot(a, b, preferred_element_type=F32)


def _dot_nt(a, b):
    return lax.dot_general(a, b, (((1,), (1,)), ((), ())), preferred_element_type=F32)


def _rms(x, g):
    return (x * lax.rsqrt(jnp.mean(x * x, axis=-1, keepdims=True) + EPS)) * g


def _sigmoid(x):
    return 1.0 / (1.0 + jnp.exp(-x))


def _shift_rows(v, carry):
    r1 = pltpu.roll(v, 1, 0)
    r2 = pltpu.roll(v, 2, 0)
    if v.shape[0] > 8:
        row = lax.broadcasted_iota(jnp.int32, (8, v.shape[1]), 0)
        h1 = jnp.where(row == 0, carry[7:8], r1[:8])
        h2 = jnp.where(row == 0, carry[6:7], jnp.where(row == 1, carry[7:8], r2[:8]))
        return jnp.concatenate([h1, r1[8:]], axis=0), jnp.concatenate([h2, r2[8:]], axis=0)
    row = lax.broadcasted_iota(jnp.int32, v.shape, 0)
    return (jnp.where(row == 0, carry[7:8], r1),
            jnp.where(row == 0, carry[6:7], jnp.where(row == 1, carry[7:8], r2)))


def _conv3(v, r1, r2, cw, cb):
    return (cw[0:1] * r2 + cw[1:2] * r1) + cw[2:3] * v + cb


def _mod_kernel(c_ref, w_ref, b_ref, o_ref):
    o_ref[0] = _dot(c_ref[...].astype(BF16), w_ref[0].astype(BF16)) + b_ref[0]


def _mod_call(c_all, w, b):
    n, d, nn = w.shape
    r = c_all.shape[0]
    tn = 512
    return pl.pallas_call(
        _mod_kernel, grid=(n, nn // tn),
        in_specs=[pl.BlockSpec((r, d), lambda i, j: (0, 0)),
                  pl.BlockSpec((1, d, tn), lambda i, j: (i, 0, j)),
                  pl.BlockSpec((1, 1, tn), lambda i, j: (i, 0, j))],
        out_specs=pl.BlockSpec((1, r, tn), lambda i, j: (i, 0, j)),
        out_shape=jax.ShapeDtypeStruct((n, r, nn), F32),
        compiler_params=_cparams("arbitrary", "arbitrary"), name="mod")(c_all, w, b)


def _mixer_kernel(seq_mode, d, *refs):
    if seq_mode:
        x_ref, mod_ref, g0_ref, g1_ref, win_ref, cw_ref, cb_ref, wout_ref, xo_ref, st_ref, carry = refs
    else:
        x_ref, mod_ref, g0_ref, g1_ref, win_ref, cw_ref, cb_ref, wout_ref, p0_ref, p1_ref, xo_ref, st_ref = refs
    x = x_ref[0]
    m = mod_ref[0]
    h = _rms(x, g0_ref[...]) * (1.0 + m[:, d:2 * d]) + m[:, :d]
    z = _dot(h.astype(BF16), win_ref[...])
    bg, cg, u = z[:, :d], z[:, d:2 * d], z[:, 2 * d:]
    v = cg * u
    if seq_mode:
        @pl.when(pl.program_id(1) == 0)
        def _():
            carry[...] = jnp.zeros_like(carry)
        r1, r2 = _shift_rows(v, carry[...])
        carry[...] = v[-8:]
        st_ref[0] = v[-8:]
    else:
        r1, r2 = p1_ref[0], p0_ref[0]
        st_ref[0] = v
    y = _conv3(v, r1, r2, cw_ref[...], cb_ref[...])
    o = _dot((bg * y).astype(BF16), wout_ref[...])
    xo_ref[0] = x + m[:, 2 * d:] * _rms(o, g1_ref[...])


def _row_specs(bx, s, tm, mod):
    sm = mod.shape[1]
    if sm == 1:
        mod_spec = pl.BlockSpec((1, 1, mod.shape[2]), lambda b, i: (b, 0, 0))
    else:
        mod_spec = pl.BlockSpec((1, tm, mod.shape[2]), lambda b, i: (b, i, 0))
    return mod_spec


def _full(shape):
    nd = len(shape)
    return pl.BlockSpec(shape, lambda b, i, _nd=nd: (0,) * _nd, pipeline_mode=pl.Buffered(1))


def _mixer_call(x, mod, g0, g1, w_in, cw, cb, w_out, prev, tm):
    bx, s, d = x.shape
    seq_mode = prev is None
    row = pl.BlockSpec((1, tm, d), lambda b, i: (b, i, 0))
    in_specs = [row, _row_specs(bx, s, tm, mod), _full((1, d)), _full((1, d)), _full(w_in.shape),
                _full(cw.shape), _full((1, d)), _full(w_out.shape)]
    args = [x, mod, g0, g1, w_in, cw, cb, w_out]
    if seq_mode:
        st_shape, st_spec = (bx, 8, d), pl.BlockSpec((1, 8, d), lambda b, i: (b, 0, 0))
        scratch = [pltpu.VMEM((8, d), F32)]
    else:
        in_specs += [row, row]
        args += [prev[0], prev[1]]
        st_shape, st_spec = (bx, s, d), row
        scratch = []
    return pl.pallas_call(
        functools.partial(_mixer_kernel, seq_mode, d), grid=(bx, s // tm),
        in_specs=in_specs, out_specs=[row, st_spec],
        out_shape=[jax.ShapeDtypeStruct((bx, s, d), F32), jax.ShapeDtypeStruct(st_shape, F32)],
        scratch_shapes=scratch, compiler_params=_cparams("arbitrary", "arbitrary"), name="mixer_a")(*args)


def _ffn_kernel(seq_mode, d, dff, n_chunks, *refs):
    if seq_mode:
        x_ref, mod_ref, g2_ref, g3_ref, wup_ref, cw_ref, cb_ref, wdn_ref, xo_ref, st_ref, carry = refs
    else:
        x_ref, mod_ref, g2_ref, g3_ref, wup_ref, cw_ref, cb_ref, wdn_ref, p0_ref, p1_ref, xo_ref, st_ref = refs
    x = x_ref[0]
    m = mod_ref[0]
    h = (_rms(x, g2_ref[...]) * (1.0 + m[:, d:2 * d]) + m[:, :d]).astype(BF16)
    if seq_mode:
        @pl.when(pl.program_id(1) == 0)
        def _():
            carry[...] = jnp.zeros_like(carry)
    cwid = dff // n_chunks
    acc = jnp.zeros((x.shape[0], d), F32)
    for k in range(n_chunks):
        halves = []
        for c0 in (k * cwid, dff + k * cwid):
            up = _dot(h, wup_ref[:, c0:c0 + cwid])
            if seq_mode:
                r1, r2 = _shift_rows(up, carry[:, c0:c0 + cwid])
                carry[:, c0:c0 + cwid] = up[-8:]
                st_ref[0, :, c0:c0 + cwid] = up[-8:]
            else:
                r1, r2 = p1_ref[0, :, c0:c0 + cwid], p0_ref[0, :, c0:c0 + cwid]
                st_ref[0, :, c0:c0 + cwid] = up
            halves.append(_conv3(up, r1, r2, cw_ref[:, c0:c0 + cwid], cb_ref[:, c0:c0 + cwid]))
        u, g = halves
        act = (g * _sigmoid(g)) * u
        acc = acc + _dot(act.astype(BF16), wdn_ref[k * cwid:(k + 1) * cwid, :])
    xo_ref[0] = x + m[:, 2 * d:] * _rms(acc, g3_ref[...])


def _ffn_call(x, mod, g2, g3, w_up, cw, cb, w_dn, prev, tm, n_chunks):
    bx, s, d = x.shape
    dff = w_dn.shape[0]
    seq_mode = prev is None
    row = pl.BlockSpec((1, tm, d), lambda b, i: (b, i, 0))
    in_specs = [row, _row_specs(bx, s, tm, mod), _full((1, d)), _full((1, d)), _full(w_up.shape),
                _full(cw.shape), _full((1, 2 * dff)), _full(w_dn.shape)]
    args = [x, mod, g2, g3, w_up, cw, cb, w_dn]
    if seq_mode:
        st_shape, st_spec = (bx, 8, 2 * dff), pl.BlockSpec((1, 8, 2 * dff), lambda b, i: (b, 0, 0))
        scratch = [pltpu.VMEM((8, 2 * dff), F32)]
    else:
        prow = pl.BlockSpec((1, tm, 2 * dff), lambda b, i: (b, i, 0))
        in_specs += [prow, prow]
        args += [prev[0], prev[1]]
        st_shape, st_spec = (bx, s, 2 * dff), prow
        scratch = []
    return pl.pallas_call(
        functools.partial(_ffn_kernel, seq_mode, d, dff, n_chunks), grid=(bx, s // tm),
        in_specs=in_specs, out_specs=[row, st_spec],
        out_shape=[jax.ShapeDtypeStruct((bx, s, d), F32), jax.ShapeDtypeStruct(st_shape, F32)],
        scratch_shapes=scratch, compiler_params=_cparams("arbitrary", "arbitrary"), name="conv_ffn")(*args)


def _kvq_kernel(d, n_rows_cols, emit_attn_kv, x_ref, modkv_ref, mod_ref, gkv_ref, g0_ref, wkv_ref, wqg_ref,
                rows_ref, win_ref, q_ref, gate_ref, *attn_refs):
    x = x_ref[0]
    r = x * lax.rsqrt(jnp.mean(x * x, axis=-1, keepdims=True) + EPS)
    mk = modkv_ref[0]
    hk = (r * gkv_ref[...]) * (1.0 + mk[:, d:]) + mk[:, :d]
    kv = _dot(hk.astype(BF16), wkv_ref[...])
    rows_ref[0] = kv[:, :n_rows_cols]
    win_ref[0] = kv[:, n_rows_cols:]
    if emit_attn_kv:
        kaug_ref, v1_ref, kw_ref, vw1_ref = attn_refs
        tm = x.shape[0]
        gw = N_KV * HEAD_DIM
        lane = lax.broadcasted_iota(jnp.int32, (tm, LANES), 1)
        pos = pl.program_id(1) * tm + lax.broadcasted_iota(jnp.int32, (tm, LANES), 0)
        low = lane < HEAD_DIM
        ones = jnp.where(low, 0.0, 1.0)
        onehot = jnp.where(lane - HEAD_DIM == pos // SEL_BLOCK, 1.0, 0.0)
        for ref, plane, fill in ((kaug_ref, 2, onehot), (v1_ref, 3, ones), (kw_ref, 4, 0.0), (vw1_ref, 5, ones)):
            for g in range(N_KV):
                c0 = plane * gw + (g // 2) * LANES
                blk = kv[:, c0:c0 + LANES]
                if g % 2 == 1:
                    blk = pltpu.roll(blk, HEAD_DIM, 1)
                ref[0, g] = jnp.where(low, blk, fill).astype(BF16)
    m = mod_ref[0]
    h1 = (r * g0_ref[...]) * (1.0 + m[:, d:2 * d]) + m[:, :d]
    qg = _dot(h1.astype(BF16), wqg_ref[...])
    nq = N_HEADS * HEAD_DIM
    q_ref[0] = qg[:, :nq] * (HEAD_DIM ** -0.5)
    gate_ref[0] = _sigmoid(qg[:, nq:])


def _kvq_call(x, modkv, mod, gkv, g0, w_kv, w_qg, tm, emit_attn_kv):
    bx, s, d = x.shape
    nkv = w_kv.shape[1]
    n_rows_cols = 4 * N_KV * HEAD_DIM
    n_win_cols = nkv - n_rows_cols
    nq = N_HEADS * HEAD_DIM
    ng = w_qg.shape[1] - nq
    row = lambda w: pl.BlockSpec((1, tm, w), lambda b, i: (b, i, 0))
    out_specs = [row(n_rows_cols), row(n_win_cols), row(nq), row(ng)]
    out_shape = [jax.ShapeDtypeStruct((bx, s, n_rows_cols), F32), jax.ShapeDtypeStruct((bx, s, n_win_cols), F32),
                 jax.ShapeDtypeStruct((bx, s, nq), F32), jax.ShapeDtypeStruct((bx, s, ng), F32)]
    if emit_attn_kv:
        out_specs += [pl.BlockSpec((1, N_KV, tm, LANES), lambda b, i: (b, 0, i, 0))] * 4
        out_shape += [jax.ShapeDtypeStruct((bx, N_KV, s, LANES), BF16)] * 4
    return pl.pallas_call(
        functools.partial(_kvq_kernel, d, n_rows_cols, emit_attn_kv), grid=(bx, s // tm),
        in_specs=[row(d), _row_specs(bx, s, tm, modkv), _row_specs(bx, s, tm, mod), _full((1, d)), _full((1, d)),
                  _full(w_kv.shape), _full(w_qg.shape)],
        out_specs=out_specs, out_shape=out_shape,
        compiler_params=_cparams("arbitrary", "arbitrary"), name="kv_q_proj")(x, modkv, mod, gkv, g0, w_kv, w_qg)


def _outproj_kernel(d, n_branch, *refs):
    x_ref, mod_ref, g1_ref, w_ref = refs[:4]
    xo_ref = refs[-1]
    if n_branch == 1:
        o = refs[4][0]
    else:
        o = sum(refs[4 + 2 * i][0] * refs[5 + 2 * i][0] for i in range(n_branch))
    y = _dot(o.astype(BF16), w_ref[...])
    xo_ref[0] = x_ref[0] + mod_ref[0][:, 2 * d:] * _rms(y, g1_ref[...])


def _outproj_call(x, mod, g1, w, branches, tm):
    bx, s, d = x.shape
    row = pl.BlockSpec((1, tm, d), lambda b, i: (b, i, 0))
    n_branch = 1 if len(branches) == 1 else len(branches) // 2
    return pl.pallas_call(
        functools.partial(_outproj_kernel, d, n_branch), grid=(bx, s // tm),
        in_specs=[row, _row_specs(bx, s, tm, mod), _full((1, d)), _full(w.shape)] + [row] * len(branches),
        out_specs=row, out_shape=jax.ShapeDtypeStruct((bx, s, d), F32),
        compiler_params=_cparams("arbitrary", "arbitrary"), name="attn_out_proj")(x, mod, g1, w, *branches)


def _gelu_tanh(x):
    return x * (0.5 * (1.0 + jnp.tanh(math.sqrt(2.0 / math.pi) * (x + 0.044715 * (x * x * x)))))


def _compress_kernel(tmb, x_ref, pe_ref, w1_ref, w2_ref, o_ref):
    acc = jnp.zeros((tmb, LANES), F32)
    for r in range(CMP_BLOCK):
        xr = x_ref[pl.ds(r, tmb, stride=CMP_BLOCK), :] + pe_ref[0, r]
        acc = acc + _dot(xr.astype(BF16), w1_ref[0, r])
    o_ref[...] = _dot(_gelu_tanh(acc).astype(BF16), w2_ref[0])


def _compress_call(rows2d, pe2, w1bd, w2bd):
    m = rows2d.shape[0] // CMP_BLOCK
    tmb = max(t for t in range(8, min(256, m) + 1, 8) if m % t == 0)
    return pl.pallas_call(
        functools.partial(_compress_kernel, tmb), grid=(m // tmb, 4),
        in_specs=[pl.BlockSpec((tmb * CMP_BLOCK, LANES), lambda i, j: (i, j)),
                  pl.BlockSpec((1, CMP_BLOCK, 1, LANES), lambda i, j: (j // 2, 0, 0, 0)),
                  pl.BlockSpec((1, CMP_BLOCK, LANES, LANES), lambda i, j: (j // 2, 0, 0, 0)),
                  pl.BlockSpec((1, LANES, LANES), lambda i, j: (j // 2, 0, 0))],
        out_specs=pl.BlockSpec((tmb, LANES), lambda i, j: (i, j)),
        out_shape=jax.ShapeDtypeStruct((m, 4 * LANES), F32),
        compiler_params=_cparams("arbitrary", "arbitrary"), name="compress")(rows2d, pe2, w1bd, w2bd)


def _compress_weights(cmp_pe, cmp_w1, cmp_w2):
    z = jnp.zeros((2, CMP_BLOCK, HEAD_DIM, HEAD_DIM), F32)
    w1 = cmp_w1.reshape(2, CMP_BLOCK, HEAD_DIM, HEAD_DIM)
    w1bd = jnp.concatenate([jnp.concatenate([w1, z], -1), jnp.concatenate([z, w1], -1)], -2).astype(BF16)
    z2 = jnp.zeros((2, HEAD_DIM, HEAD_DIM), F32)
    w2bd = jnp.concatenate([jnp.concatenate([cmp_w2, z2], -1), jnp.concatenate([z2, cmp_w2], -1)], -2).astype(BF16)
    pe2 = jnp.concatenate([cmp_pe, cmp_pe], -1)[:, :, None, :]
    return pe2, w1bd, w2bd


N_PAGE_CB = PAGE_SIZE // CMP_BLOCK
CB_ROWS = 8


PAGE_PITCH = N_KV * HEAD_DIM + 4


def _compress_pool_kernel(n_pg, n_i, x_hbm, pe_ref, mw_ref, w2_ref, o_ref, xbuf, sem):
    gw = N_KV * HEAD_DIM
    t = pl.program_id(0)
    n_t = pl.num_programs(0)
    slot_rows = n_pg * PAGE_PITCH

    def page_copy(step, slot, p):
        return pltpu.make_async_copy(x_hbm.at[(step % n_i) * n_pg + p, step // n_i],
                                     xbuf.at[pl.ds(slot * slot_rows + p * PAGE_PITCH, gw), :], sem.at[slot])

    def start_all(step, slot):
        for p in range(n_pg):
            page_copy(step, slot, p).start()

    slot = t % 2

    @pl.when(t == 0)
    def _():
        start_all(t, slot)

    @pl.when(t + 1 < n_t)
    def _():
        start_all(t + 1, 1 - slot)
    for p in range(n_pg):
        page_copy(t, slot, p).wait()

    def rows_of(g, d):
        return xbuf[pl.ds(slot * slot_rows + g * HEAD_DIM + d, n_pg, stride=PAGE_PITCH), :]
    lhs = jnp.concatenate(
        [(jnp.concatenate([rows_of(g, d) for g in range(N_KV)], axis=0) + pe_ref[0, d]).astype(BF16)
         for d in range(HEAD_DIM)], axis=1)
    z = _dot(_gelu_tanh(_dot(lhs, mw_ref[0])).astype(BF16), w2_ref[0])
    lane = lax.broadcasted_iota(jnp.int32, (n_pg, LANES), 1)
    low = lane < HEAD_DIM
    o_ref[...] = jnp.zeros(o_ref.shape, F32)
    for n in range(N_PAGE_CB):
        for c in range(N_KV // 2):
            a = z[(2 * c) * n_pg:(2 * c + 1) * n_pg, (n // 2) * LANES:(n // 2 + 1) * LANES]
            b = z[(2 * c + 1) * n_pg:(2 * c + 2) * n_pg, (n // 2) * LANES:(n // 2 + 1) * LANES]
            if n % 2 == 0:
                b = pltpu.roll(b, HEAD_DIM, 1)
            else:
                a = pltpu.roll(a, HEAD_DIM, 1)
            o_ref[c, pl.ds(n, n_pg, stride=CB_ROWS), :] = jnp.where(low, a, b)


def _compress_pool_call(cache_t, pe_t, mw, w2bd4):
    n_pool = cache_t.shape[0]
    n_pg = max(t for t in range(8, min(64, n_pool) + 1, 8) if n_pool % t == 0)
    n_i = n_pool // n_pg
    return pl.pallas_call(
        functools.partial(_compress_pool_kernel, n_pg, n_i), grid=(2 * n_i,),
        in_specs=[pl.BlockSpec(memory_space=pl.ANY),
                  pl.BlockSpec((1, HEAD_DIM, 1, PAGE_SIZE), lambda t: (t // n_i, 0, 0, 0)),
                  pl.BlockSpec((1, HEAD_DIM * PAGE_SIZE, N_PAGE_CB * HEAD_DIM), lambda t: (t // n_i, 0, 0)),
                  pl.BlockSpec((1, N_PAGE_CB * HEAD_DIM, N_PAGE_CB * HEAD_DIM), lambda t: (t // n_i, 0, 0))],
        out_specs=pl.BlockSpec((N_KV // 2, n_pg * CB_ROWS, LANES), lambda t: (t // n_i, t % n_i, 0)),
        out_shape=jax.ShapeDtypeStruct((N_KV, n_pool * CB_ROWS, LANES), F32),
        scratch_shapes=[pltpu.VMEM((2 * n_pg * PAGE_PITCH, PAGE_SIZE), F32), pltpu.SemaphoreType.DMA((2,))],
        compiler_params=_cparams("arbitrary"), name="compress_pool")(cache_t, pe_t, mw, w2bd4)


def _compress_pool_weights(cmp_pe, cmp_w1, cmp_w2):
    eye = np.eye(N_PAGE_CB, dtype=bool)
    w1t = cmp_w1.reshape(2, CMP_BLOCK, HEAD_DIM, HEAD_DIM).transpose(0, 2, 1, 3)
    mw = jnp.where(eye[None, None, :, None, :, None], w1t[:, :, None, :, None, :], 0.0)
    mw = mw.reshape(2, HEAD_DIM * PAGE_SIZE, N_PAGE_CB * HEAD_DIM).astype(BF16)
    w2bd4 = jnp.where(eye[None, :, None, :, None], cmp_w2[:, None, :, None, :], 0.0)
    w2bd4 = w2bd4.reshape(2, N_PAGE_CB * HEAD_DIM, N_PAGE_CB * HEAD_DIM).astype(BF16)
    pe_t = jnp.tile(cmp_pe.transpose(0, 2, 1), (1, 1, N_PAGE_CB))[:, :, None, :]
    return pe_t, mw, w2bd4


def _bias_of_distance(dist, tab):
    val = jnp.zeros(dist.shape, F32) + tab(0)
    for k in range(1, N_BUCKETS):
        val = jnp.where(dist >= BUCKET_THR[k], tab(k), val)
    return val


def _toeplitz_kernel(window, tab_ref, o_ref):
    g = pl.program_id(0)
    dt = pl.program_id(1)
    x = lax.broadcasted_iota(jnp.int32, (8, 2 * TK), 1)
    dist = dt * TQ - jnp.where(x < TK, x, x - 2 * TK)
    for hh in range(HPG):
        head = g * HPG + hh
        val = _bias_of_distance(dist, lambda k, head=head: tab_ref[k, head])
        val = jnp.where(dist < 0, NEG_MASK, val)
        if window is not None:
            val = jnp.where(dist >= window, NEG_MASK, val)
        tile = pltpu.roll(jnp.broadcast_to(val[0:1], (TQ, 2 * TK)), 0, 1, stride=1, stride_axis=0)
        o_ref[0, 0, hh * TQ:(hh + 1) * TQ, :] = tile[:, :TK]


def _toeplitz_call(rel_bias, n_chunks, window):
    return pl.pallas_call(
        functools.partial(_toeplitz_kernel, window), grid=(N_KV, n_chunks),
        in_specs=[pl.BlockSpec(memory_space=pltpu.SMEM)],
        out_specs=pl.BlockSpec((1, 1, HPG * TQ, TK), lambda g, t: (g, t, 0, 0)),
        out_shape=jax.ShapeDtypeStruct((N_KV, n_chunks, HPG * TQ, TK), F32),
        compiler_params=_cparams("arbitrary", "arbitrary"), name="bias_toeplitz")(rel_bias)


def _cmp_bias_kernel(tab_ref, o_ref):
    g = pl.program_id(0)
    qi = pl.program_id(1)
    row = lax.broadcasted_iota(jnp.int32, (LANES, TQ), 0)
    t_q = qi * TQ + lax.broadcasted_iota(jnp.int32, (LANES, TQ), 1)
    cblk = 2 * (row % HEAD_DIM) + row // HEAD_DIM
    dist = t_q - (cblk * CMP_BLOCK + (CMP_BLOCK - 1))
    for hh in range(HPG):
        head = g * HPG + hh
        val = _bias_of_distance(dist, lambda k, head=head: tab_ref[k, head])
        o_ref[0, 0, :, hh * TQ:(hh + 1) * TQ] = jnp.where(dist < 0, NEG_MASK, val)


def _cmp_bias_call(rel_bias, n_qt):
    return pl.pallas_call(
        _cmp_bias_kernel, grid=(N_KV, n_qt),
        in_specs=[pl.BlockSpec(memory_space=pltpu.SMEM)],
        out_specs=pl.BlockSpec((1, 1, LANES, HPG * TQ), lambda g, t: (g, t, 0, 0)),
        out_shape=jax.ShapeDtypeStruct((N_KV, n_qt, LANES, HPG * TQ), F32),
        compiler_params=_cparams("arbitrary", "arbitrary"), name="bias_cmp")(rel_bias)


def _bias_cols_kernel(dist_ref, tab_ref, o_ref):
    dist = dist_ref[...]
    val = _bias_of_distance(dist, lambda k: tab_ref[:, k:k + 1])
    o_ref[...] = jnp.where(dist < 0, NEG_MASK, val)


def _bias_cols_call(dist, tab_heads):
    r = dist.shape[1]
    return pl.pallas_call(
        _bias_cols_kernel, grid=(1,),
        in_specs=[pl.BlockSpec((N_HEADS, r), lambda i: (0, 0)), pl.BlockSpec((N_HEADS, N_BUCKETS), lambda i: (0, 0))],
        out_specs=pl.BlockSpec((N_HEADS, r), lambda i: (0, 0)),
        out_shape=jax.ShapeDtypeStruct((N_HEADS, r), F32),
        compiler_params=_cparams("arbitrary"), name="bias_cols")(dist, tab_heads)


def _nsa_seq_kernel(n_sb, q_ref, gate_ref, kaug_ref, v1_ref, kw_ref, vw1_ref, kcb_ref, vcb_ref, tcmp_ref,
                    tsel_ref, twin_ref, gexp_ref, o_ref, qaug, qw_sc, s_sc, mrun, mb, acc_sel, acc_win, score_sc):
    qi = pl.program_id(2)
    rows = HPG * TQ
    lane = lax.broadcasted_iota(jnp.int32, (TQ, LANES), 1)
    low = lane < HEAD_DIM

    qh = []
    for hh in range(HPG):
        qv = q_ref[0, :, (hh // 2) * LANES:(hh // 2 + 1) * LANES]
        if hh % 2 == 1:
            qv = pltpu.roll(qv, HEAD_DIM, 1)
        qh.append(jnp.where(low, qv, 0.0))
    qw = jnp.concatenate(qh, axis=0).astype(BF16)
    qw_sc[...] = qw

    s_c = _dot_nt(kcb_ref[0, 0].astype(BF16), qw) + tcmp_ref[0, 0]
    ok_c = s_c > 0.5 * NEG_MASK
    mx = jnp.max(s_c, axis=0, keepdims=True)
    p = jnp.where(ok_c, jnp.exp(s_c - mx), 0.0)
    den = jnp.sum(p, axis=0, keepdims=True)
    pn_t = p / jnp.where(den > 0, den, 1.0)
    o_c = _dot(pn_t.T.astype(BF16), vcb_ref[0, 0].astype(BF16))
    imp = pn_t[:, 0:TQ]
    for hh in range(1, HPG):
        imp = imp + pn_t[:, hh * TQ:(hh + 1) * TQ]
    n_blk = LANES // 2
    imp = imp[:n_blk] + imp[n_blk:]

    blk = lax.broadcasted_iota(jnp.int32, (n_blk, TQ), 0)
    cur = (qi * TQ + lax.broadcasted_iota(jnp.int32, (n_blk, TQ), 1)) // SEL_BLOCK
    valid = blk <= cur
    forced = (blk == 0) | (blk == cur) | (blk == cur - 1)
    score_sc[...] = jnp.where(valid & forced, BIG_SCORE, jnp.where(valid, imp, -BIG_SCORE))
    sub = 8
    groups = [score_sc[r0:r0 + sub, :] for r0 in range(0, n_blk, sub)]
    rowg = lax.broadcasted_iota(jnp.int32, (sub, TQ), 0)
    cnts = [jnp.zeros((sub, TQ), jnp.int32) for _ in groups]
    for bp in range(n_blk):
        r = score_sc[bp:bp + 1, :]
        for gi, s_g in enumerate(groups):
            if gi * sub > bp:
                ahead = r >= s_g
            elif (gi + 1) * sub - 1 < bp:
                ahead = r > s_g
            else:
                ahead = (r > s_g) | ((r == s_g) & (rowg + gi * sub > bp))
            cnts[gi] = cnts[gi] + jnp.where(ahead, 1, 0)
    cnt = jnp.concatenate(cnts, axis=0)
    selmask_t = jnp.where((cnt < min(TOP_N, n_sb)) & valid, 0.0, NEG_SEL)
    selmask = jnp.concatenate([jnp.zeros((n_blk, TQ), F32), selmask_t], axis=0).T
    for hh in range(HPG):
        qaug[hh * TQ:(hh + 1) * TQ, :] = jnp.where(low, qh[hh], selmask).astype(BF16)

    def branch(q_sc, k_ref, v_ref, t_ref, acc_sc, n_tiles, n_tbl, widths):
        mrun[...] = jnp.full(mrun.shape, NEG_MASK, F32)

        def tile_loop(step):
            done = 0
            for width in widths:
                def group(j, _, done=done, width=width):
                    step(tuple(done + width * j + u for u in range(width)))
                    return 0
                n_groups = (n_tiles - done) // width
                lax.fori_loop(0, n_groups, group, 0)
                done = done + n_groups * width

        def scores(tiles):
            m = mrun[...]
            for i in tiles:
                r0 = pl.multiple_of((qi - i) * TK, TK)
                s = _dot_nt(q_sc[...], k_ref[0, 0, pl.ds(r0, TK), :]) + t_ref[0, jnp.minimum(i, n_tbl)]
                s_sc[i] = s
                for c in range(TK // LANES):
                    m = jnp.maximum(m, s[:, c * LANES:(c + 1) * LANES])
            mrun[...] = m
        tile_loop(scores)
        mb[...] = jnp.broadcast_to(jnp.max(mrun[...], axis=1, keepdims=True), (rows, LANES))
        acc_sc[...] = jnp.zeros(acc_sc.shape, F32)

        def weigh(tiles):
            mbv = mb[...]
            mb2 = jnp.concatenate([mbv] * (TK // LANES), axis=1)
            acc = acc_sc[...]
            for i in tiles:
                r0 = pl.multiple_of((qi - i) * TK, TK)
                acc = acc + _dot(jnp.exp(s_sc[i] - mb2).astype(BF16), v_ref[0, 0, pl.ds(r0, TK), :])
            acc_sc[...] = acc
        tile_loop(weigh)
        return acc_sc[...]

    acc_s = branch(qaug, kaug_ref, v1_ref, tsel_ref, acc_sel, qi + 1, N_TBL, (8, 4, 2, 1))
    acc_w = branch(qw_sc, kw_ref, vw1_ref, twin_ref, acc_win, jnp.minimum(qi, N_WIN_TILES - 1) + 1, N_WIN_TILES - 1,
                   (N_WIN_TILES, 2, 1))

    gates = gate_ref[0]
    g_hi = gates.astype(BF16)
    g_lo = (gates - g_hi.astype(F32)).astype(BF16)
    g_all = _dot(jnp.concatenate([g_hi, g_lo], axis=1), gexp_ref[...])

    def normalized(acc, odd):
        swapped = pltpu.roll(acc, HEAD_DIM, 1)
        return swapped / acc if odd else acc / swapped

    outs = []
    for hh in range(HPG):
        gb = [g_all[:, (hh * 3 + br) * LANES:(hh * 3 + br + 1) * LANES] for br in range(3)]
        rs = slice(hh * TQ, (hh + 1) * TQ)
        outs.append(gb[0] * o_c[rs] + gb[1] * normalized(acc_s[rs], hh % 2) + gb[2] * normalized(acc_w[rs], hh % 2))
    for c in range(HPG // 2):
        o_ref[0, :, c * LANES:(c + 1) * LANES] = jnp.where(low, outs[2 * c], outs[2 * c + 1]).astype(o_ref.dtype)


def _nsa_seq_call(q, gates, kaug, v1, kw, vw1, kcb, vcb, tcmp, tsel, twin):
    b, s, _ = q.shape
    n_qt = s // TQ
    rows = HPG * TQ
    kv_spec = pl.BlockSpec((1, 1, s, LANES), lambda bb, g, i: (bb, g, 0, 0))
    cb_spec = pl.BlockSpec((1, 1, kcb.shape[2], LANES), lambda bb, g, i: (bb, g, 0, 0))
    tbl_spec = lambda t: pl.BlockSpec((1,) + t.shape[1:], lambda bb, g, i: (g, 0, 0, 0))
    n_gate = HPG * 3
    gexp = np.zeros((2 * LANES, n_gate * LANES), np.float32)
    for c in range(n_gate):
        gexp[c, c * LANES:(c + 1) * LANES] = 1.0
        gexp[LANES + c, c * LANES:(c + 1) * LANES] = 1.0
    gexp = jnp.asarray(gexp, BF16)
    return pl.pallas_call(
        functools.partial(_nsa_seq_kernel, s // SEL_BLOCK), grid=(b, N_KV, n_qt),
        in_specs=[pl.BlockSpec((1, TQ, HPG * HEAD_DIM), lambda bb, g, i: (bb, i, g)),
                  pl.BlockSpec((1, TQ, LANES), lambda bb, g, i: (bb, i, g)),
                  kv_spec, kv_spec, kv_spec, kv_spec, cb_spec, cb_spec,
                  pl.BlockSpec((1, 1, LANES, rows), lambda bb, g, i: (g, i, 0, 0)), tbl_spec(tsel), tbl_spec(twin),
                  pl.BlockSpec(gexp.shape, lambda bb, g, i: (0, 0))],
        out_specs=pl.BlockSpec((1, TQ, HPG * HEAD_DIM), lambda bb, g, i: (bb, i, g)),
        out_shape=jax.ShapeDtypeStruct((b, s, N_HEADS * HEAD_DIM), BF16),
        scratch_shapes=[pltpu.VMEM((rows, LANES), BF16), pltpu.VMEM((rows, LANES), BF16),
                        pltpu.VMEM((s // TK, rows, TK), F32), pltpu.VMEM((rows, LANES), F32),
                        pltpu.VMEM((rows, LANES), F32), pltpu.VMEM((rows, LANES), F32),
                        pltpu.VMEM((rows, LANES), F32), pltpu.VMEM((LANES // 2, TQ), F32)],
        compiler_params=_cparams("arbitrary", "arbitrary", "arbitrary"), name="nsa_seq")(
            q, gates, kaug, v1, kw, vw1, kcb, vcb, tcmp, tsel, twin, gexp)


def _softmax_lanes(s):
    p = jnp.exp(s - jnp.max(s, axis=1, keepdims=True))
    return p / jnp.sum(p, axis=1, keepdims=True)


def _rows_to_heads(rows):
    hg = lax.broadcasted_iota(jnp.int32, (N_HEADS, rows[0].shape[1]), 0) // HPG
    out = jnp.broadcast_to(rows[0], hg.shape)
    for g in range(1, N_KV):
        out = jnp.where(hg == g, rows[g], out)
    return out


def _nsa_stepT_kernel(n_pages, pt_ref, *refs):
    del pt_ref
    q_ref = refs[0]
    cb_refs = refs[1:1 + n_pages]
    pg_refs = refs[1 + n_pages:1 + 2 * n_pages]
    (kvnew_ref, winnew_ref, wcol_ref, cwin_ref, bsel_ref, bnew_ref, bwin_ref, bcmp_ref) = refs[1 + 2 * n_pages:9 + 2 * n_pages]
    oc_ref, os_ref, ow_ref, nwin_ref = refs[9 + 2 * n_pages:13 + 2 * n_pages]
    (s_sc,) = refs[13 + 2 * n_pages:]
    b = pl.program_id(0)
    gw = N_KV * HEAD_DIM
    n_past = n_pages * PAGE_SIZE
    q16 = q_ref[0].astype(BF16)
    qf = q16.astype(F32)
    lane = lax.broadcasted_iota(jnp.int32, (N_HEADS, LANES), 1)

    n_pad = LANES - CB_ROWS * n_pages
    cb = jnp.concatenate([jnp.concatenate([r[c] for c in range(N_KV)], axis=1) for r in cb_refs]
                         + ([jnp.zeros((n_pad, 2 * gw), F32)] if n_pad else []), axis=0)
    pn_c = _softmax_lanes(_dot_nt(q16, cb[:, :gw].astype(BF16)) + bcmp_ref[...])
    oc_ref[0] = _dot(pn_c.astype(BF16), cb[:, gw:].astype(BF16))

    grp = [pn_c[HPG * g:HPG * g + 1] + pn_c[HPG * g + 1:HPG * g + 2] + pn_c[HPG * g + 2:HPG * g + 3]
           + pn_c[HPG * g + 3:HPG * g + 4] for g in range(N_KV)]
    lane8 = lax.broadcasted_iota(jnp.int32, (8, LANES), 1)
    row8 = lax.broadcasted_iota(jnp.int32, (8, LANES), 0)
    imp = jnp.zeros((8, LANES), F32)
    for g in range(N_KV):
        imp = jnp.where(row8 == g, grp[g], imp)
    imp = imp + pltpu.roll(imp, LANES - 1, 1)
    n_past_blk = n_past // SEL_BLOCK
    cur_lane = LANES - 2
    is_blk = ((lane8 % CB_ROWS == 0) | (lane8 % CB_ROWS == 2)) & (lane8 < CB_ROWS * n_pages)
    last_lane = ((n_past_blk - 1) // 2) * CB_ROWS + 2 * ((n_past_blk - 1) % 2)
    forced = (lane8 == 0) | (lane8 == last_lane) | (lane8 == cur_lane)
    valid = is_blk | (lane8 == cur_lane)
    score = jnp.where(valid & forced, BIG_SCORE, jnp.where(valid, imp, -BIG_SCORE))
    cnt = jnp.zeros((8, LANES), jnp.int32)
    for k in range(2, LANES, 2):
        r = pltpu.roll(score, k, 1)
        cnt = cnt + jnp.where((r > score) | ((r == score) & (lane8 >= k)), 1, 0)
    selrows = jnp.where((cnt < min(TOP_N, n_past_blk + 1)) & valid, 0.0, NEG_SEL)
    selmask = _rows_to_heads([selrows[g:g + 1] for g in range(N_KV)])

    blocks_per_page = PAGE_SIZE // SEL_BLOCK
    for p in range(n_pages):
        kt = pg_refs[p][0, 0:gw, :].astype(BF16)
        msk = selmask[:, CB_ROWS * p:CB_ROWS * p + 1]
        for i in range(1, blocks_per_page):
            msk = jnp.where(lane >= i * SEL_BLOCK, selmask[:, CB_ROWS * p + 2 * i:CB_ROWS * p + 2 * i + 1], msk)
        s_sc[:, p * PAGE_SIZE:(p + 1) * PAGE_SIZE] = _dot(q16, kt) + bsel_ref[:, p * PAGE_SIZE:(p + 1) * PAGE_SIZE] + msk
    knew = kvnew_ref[0][:, 2 * gw:3 * gw].astype(BF16).astype(F32)
    s_new = jnp.sum(qf * knew, axis=1, keepdims=True) + bnew_ref[:, 0:1] + selmask[:, cur_lane:cur_lane + 1]
    s_sc[:, n_past:n_past + LANES] = jnp.where(lane == 0, s_new, NEG_MASK)
    s_all = s_sc[...]
    mx = jnp.max(s_all, axis=1, keepdims=True)
    den = jnp.sum(jnp.exp(s_all - mx), axis=1, keepdims=True)
    acc = jnp.zeros((N_HEADS, gw), F32)
    for p in range(n_pages):
        pn = jnp.exp(s_sc[:, p * PAGE_SIZE:(p + 1) * PAGE_SIZE] - mx) / den
        acc = acc + _dot_nt(pn.astype(BF16), pg_refs[p][0, gw:2 * gw, :].astype(BF16))
    pn_new = jnp.exp(s_new - mx) / den
    vnew = kvnew_ref[0][:, 3 * gw:4 * gw].astype(BF16).astype(F32)
    os_ref[0] = acc + pn_new.astype(BF16).astype(F32) * vnew

    cw = cwin_ref[0]
    w_len = cw.shape[1]
    s_w = _dot(q16, cw[0:gw].astype(BF16)) + bwin_ref[...]
    kwn = winnew_ref[0][:, 0:gw].astype(BF16).astype(F32)
    s_wn = jnp.sum(qf * kwn, axis=1, keepdims=True) + bnew_ref[:, 0:1]
    mxw = jnp.maximum(jnp.max(s_w, axis=1, keepdims=True), s_wn)
    pw = jnp.exp(s_w - mxw)
    pwn = jnp.exp(s_wn - mxw)
    denw = jnp.sum(pw, axis=1, keepdims=True) + pwn
    vwn = winnew_ref[0][:, gw:2 * gw].astype(BF16).astype(F32)
    ow_ref[0] = _dot_nt((pw / denw).astype(BF16), cw[gw:2 * gw].astype(BF16)) + (pwn / denw).astype(BF16).astype(F32) * vwn
    lane_w = lax.broadcasted_iota(jnp.int32, cw.shape, 1)
    nb_l = wcol_ref.shape[1]
    lane_b = lax.broadcasted_iota(jnp.int32, (cw.shape[0], nb_l), 1)
    col = jnp.sum(jnp.where(lane_b == b, wcol_ref[...], 0.0), axis=1, keepdims=True)
    nwin_ref[0] = jnp.where(lane_w == w_len - 1, col, pltpu.roll(cw, w_len - 1, 1))


def _nsa_step_call(page_table, qrows, cb_pool, cache_t, kvnew, winnew, wcol, cwin_t, bsel, bnew, bwin, bcmp):
    nb, n_pages = page_table.shape
    gw = N_KV * HEAD_DIM
    w_len = cwin_t.shape[2]
    cb_specs = [pl.BlockSpec((N_KV, None, CB_ROWS, LANES), lambda b, pt, _p=p: (0, pt[b, _p], 0, 0))
                for p in range(n_pages)]
    pg_specs = [pl.BlockSpec((1, 2 * gw, PAGE_SIZE), lambda b, pt, _p=p: (pt[b, _p], 1, 0)) for p in range(n_pages)]
    const = lambda a: pl.BlockSpec(a.shape, lambda b, pt: (0, 0))
    o_spec = pl.BlockSpec((1, N_HEADS, gw), lambda b, pt: (b, 0, 0))
    win_spec = pl.BlockSpec((1, 2 * gw, w_len), lambda b, pt: (b, 0, 0))
    grid_spec = pltpu.PrefetchScalarGridSpec(
        num_scalar_prefetch=1, grid=(nb,),
        in_specs=[pl.BlockSpec((1, N_HEADS, gw), lambda b, pt: (b, 0, 0))] + cb_specs + pg_specs + [
            pl.BlockSpec((1, 1, 4 * gw), lambda b, pt: (b, 0, 0)),
            pl.BlockSpec((1, 1, 2 * gw), lambda b, pt: (b, 0, 0)),
            const(wcol), win_spec, const(bsel), const(bnew), const(bwin), const(bcmp)],
        out_specs=[o_spec, o_spec, o_spec, win_spec],
        scratch_shapes=[pltpu.VMEM((N_HEADS, n_pages * PAGE_SIZE + LANES), F32)])
    o_shape = jax.ShapeDtypeStruct((nb, N_HEADS, gw), F32)
    return pl.pallas_call(
        functools.partial(_nsa_stepT_kernel, n_pages), grid_spec=grid_spec,
        out_shape=[o_shape, o_shape, o_shape, jax.ShapeDtypeStruct((nb, 2 * gw, w_len), F32)],
        compiler_params=_cparams("arbitrary"), name="nsa_step")(
            page_table, qrows, *([cb_pool] * n_pages), *([cache_t] * n_pages), kvnew, winnew, wcol, cwin_t,
            bsel, bnew, bwin, bcmp)


def _head_diag(o):
    b = o.shape[0]
    o5 = o.reshape(b, N_KV, HPG, N_KV, HEAD_DIM)
    return jnp.stack([o5[:, g, :, g, :] for g in range(N_KV)], axis=1).reshape(b, N_HEADS * HEAD_DIM)


def kernel(x_prompt, x_sample, c_prompt, c_sample, cache_kv, cache_win, state_conv_a, state_ffn_conv, page_table, mod_w, mod_b, norm_g, a_w_in, a_conv_w, a_conv_b, a_w_out, kv_mod_w, kv_mod_b, kv_norm_g, w_kv, cmp_pe, cmp_w1, cmp_w2, b_w_qg, b_w_out, rel_bias, ffn_w_up, ffn_conv_w, ffn_conv_b, ffn_w_down):
    bp, s, d = x_prompt.shape
    bs = x_sample.shape[0]
    depth = mod_w.shape[0]
    n_a = a_w_in.shape[0]
    assert depth == 2 and n_a == 1 and x_sample.shape[1] == 1
    dff = ffn_w_down.shape[1]
    n_pool = cache_kv.shape[0]
    n_pages = page_table.shape[1]
    past_len = n_pages * PAGE_SIZE
    gw = N_KV * HEAD_DIM
    nq = N_HEADS * HEAD_DIM

    n_c = bp + bs
    n_cp = -(-n_c // 8) * 8
    c_all = jnp.pad(jnp.concatenate([c_prompt, c_sample], 0), ((0, n_cp - n_c), (0, 0)))
    mods = _mod_call(c_all, mod_w.reshape(depth * 2, d, 3 * d), mod_b.reshape(depth * 2, 1, 3 * d))
    modkv = _mod_call(c_all, kv_mod_w[None], kv_mod_b[None, None])[0]
    mod_p = lambda i: mods[i, :bp][:, None, :]
    mod_s = lambda i: mods[i, bp:n_c][None]

    w_in = a_w_in[0].astype(BF16)
    w_out_a = a_w_out[0].astype(BF16)
    w_up = ffn_w_up.astype(BF16)
    w_dn = ffn_w_down.astype(BF16)
    w_kv_b = w_kv.astype(BF16)
    ng_pad = -(-(b_w_qg.shape[2] - nq) // LANES) * LANES
    w_qg = jnp.pad(b_w_qg[0], ((0, 0), (0, nq + ng_pad - b_w_qg.shape[2]))).astype(BF16)
    w_out_b = b_w_out[0].astype(BF16)
    g = lambda l, i: norm_g[l, i][None]
    pe2, w1bd, w2bd = _compress_weights(cmp_pe, cmp_w1, cmp_w2)
    n_chunks = 2

    tm = min(512, s)
    tm_ffn = min(512, s)
    x1, st_a = _mixer_call(x_prompt, mod_p(0), g(0, 0), g(0, 1), w_in, a_conv_w[0], a_conv_b[0][None], w_out_a, None, tm)
    x2, st_f0 = _ffn_call(x1, mod_p(1), g(0, 2), g(0, 3), w_up[0], ffn_conv_w[0], ffn_conv_b[0][None], w_dn[0], None, tm_ffn, n_chunks)
    rows, win, q, gates, kaug, v1, kw, vw1 = _kvq_call(
        x2, modkv[:bp][:, None, :], mod_p(2), kv_norm_g[None], g(1, 0), w_kv_b, w_qg, tm, True)
    cb = _compress_call(rows.reshape(bp * s, 4 * gw), pe2, w1bd, w2bd)
    n_cbk = s // CMP_BLOCK
    assert n_cbk <= LANES
    cb = jnp.pad(cb.reshape(bp, n_cbk, 2, N_KV, HEAD_DIM), ((0, 0), (0, LANES - n_cbk), (0, 0), (0, 0), (0, 0)))
    cb = cb.reshape(bp, LANES // 2, 2, 2, N_KV, HEAD_DIM)
    cb = cb.transpose(3, 0, 4, 2, 1, 5).reshape(2, bp, N_KV, LANES, HEAD_DIM)
    kcb = jnp.pad(cb[0], ((0, 0),) * 3 + ((0, LANES - HEAD_DIM),))
    vcb = jnp.concatenate([cb[1], cb[1]], axis=-1)
    gates_g = jnp.pad(gates[:, :, :N_HEADS * 3].reshape(bp, s, N_KV, HPG * 3),
                      ((0, 0), (0, 0), (0, 0), (0, LANES - HPG * 3))).reshape(bp, s, N_KV * LANES)
    tsel = _toeplitz_call(rel_bias, N_TBL + 1, None)
    twin = _toeplitz_call(rel_bias, N_WIN_TILES, WINDOW)
    tcmp = _cmp_bias_call(rel_bias, s // TQ)
    o_att = _nsa_seq_call(q, gates_g, kaug, v1, kw, vw1, kcb, vcb, tcmp, tsel, twin)
    x3 = _outproj_call(x2, mod_p(2), g(1, 1), w_out_b, [o_att], tm)
    y_prompt, st_f1 = _ffn_call(x3, mod_p(3), g(1, 2), g(1, 3), w_up[1], ffn_conv_w[1], ffn_conv_b[1][None], w_dn[1], None, tm_ffn, n_chunks)
    kv_p = rows.reshape(bp, s, 4, N_KV, HEAD_DIM)
    keep = min(WINDOW, s)
    win_p = win[:, s - keep:].reshape(bp, keep, 2, N_KV, HEAD_DIM)
    conv_a_p = st_a[None, :, 6:8]
    ffn_p = jnp.stack([st_f0[:, 6:8], st_f1[:, 6:8]])

    xs = x_sample.reshape(1, bs, d)
    prev_a = (state_conv_a[0, :, 0][None], state_conv_a[0, :, 1][None])
    xs1, v_a = _mixer_call(xs, mod_s(0), g(0, 0), g(0, 1), w_in, a_conv_w[0], a_conv_b[0][None], w_out_a, prev_a, bs)
    prev_f = lambda l: (state_ffn_conv[l, :, 0][None], state_ffn_conv[l, :, 1][None])
    xs2, up0 = _ffn_call(xs1, mod_s(1), g(0, 2), g(0, 3), w_up[0], ffn_conv_w[0], ffn_conv_b[0][None], w_dn[0], prev_f(0), bs, n_chunks)
    rows_s, win_s, q_s, gates_s = _kvq_call(xs2, modkv[bp:n_c][None], mod_s(2), kv_norm_g[None], g(1, 0), w_kv_b, w_qg, bs, False)
    cache_t = cache_kv.transpose(0, 2, 3, 4, 1).reshape(n_pool, 4 * gw, PAGE_SIZE)
    w_len = cache_win.shape[1]
    cwin_t = cache_win.transpose(0, 2, 3, 4, 1).reshape(bs, 2 * gw, w_len)
    cb_pool = _compress_pool_call(cache_t.reshape(n_pool, 4, gw, PAGE_SIZE),
                                  *_compress_pool_weights(cmp_pe, cmp_w1, cmp_w2))
    cb_pool = cb_pool.reshape(N_KV, n_pool, CB_ROWS, LANES)
    assert n_pages * CB_ROWS <= LANES
    d_sel = past_len - np.arange(past_len)
    d_new = np.where(np.arange(LANES) == 0, 0, -1)
    d_win = w_len - np.arange(w_len)
    d_win = np.where(d_win < WINDOW, d_win, -1)
    cl = np.arange(LANES)
    d_cmp = past_len - (((cl // CB_ROWS) * N_PAGE_CB + cl % CB_ROWS) * CMP_BLOCK + CMP_BLOCK - 1)
    d_cmp = np.where((cl % CB_ROWS < N_PAGE_CB) & (cl // CB_ROWS < n_pages), d_cmp, -1)
    assert d_sel.min() >= 0 and d_cmp[d_cmp != -1].min() >= 0
    dist = np.concatenate([d_sel, d_new, d_win, d_cmp]).astype(np.int32)
    bias_cols = _bias_cols_call(jnp.asarray(np.repeat(dist[None, :], N_HEADS, 0)), rel_bias.T)
    bsel, bnew = bias_cols[:, :past_len], bias_cols[:, past_len:past_len + LANES]
    bwin = bias_cols[:, past_len + LANES:past_len + LANES + w_len]
    bcmp = bias_cols[:, past_len + LANES + w_len:]
    head_group = (np.arange(N_HEADS)[:, None] // HPG == np.arange(N_KV)[None, :])[None, :, :, None]
    qrows = jnp.where(head_group, q_s.reshape(bs, N_HEADS, 1, HEAD_DIM), 0.0).reshape(bs, N_HEADS, gw)
    oc, os_, ow, nwin_t = _nsa_step_call(
        page_table, qrows, cb_pool, cache_t, rows_s.reshape(bs, 1, 4 * gw),
        win_s.reshape(bs, 1, 2 * gw), win_s[0].T, cwin_t, bsel, bnew, bwin, bcmp)
    nwin = nwin_t.reshape(bs, 2, N_KV, HEAD_DIM, w_len).transpose(0, 4, 1, 2, 3)
    gts = gates_s[0, :, :N_HEADS * 3].reshape(bs, N_HEADS, 3)
    branches = []
    for br, o in enumerate((oc, os_, ow)):
        branches += [jnp.repeat(gts[:, :, br], HEAD_DIM, axis=1)[None], _head_diag(o)[None]]
    xs3 = _outproj_call(xs2, mod_s(2), g(1, 1), w_out_b, branches, bs)
    ys, up1 = _ffn_call(xs3, mod_s(3), g(1, 2), g(1, 3), w_up[1], ffn_conv_w[1], ffn_conv_b[1][None], w_dn[1], prev_f(1), bs, n_chunks)
    y_sample = ys.reshape(bs, 1, d)
    kv_s = rows_s.reshape(bs, 1, 4, N_KV, HEAD_DIM)
    win_state_s = nwin
    conv_a_s = jnp.stack([state_conv_a[0, :, 1], v_a[0]], axis=1)[None]
    ffn_s = jnp.stack([jnp.stack([state_ffn_conv[l, :, 1], u[0]], axis=1) for l, u in ((0, up0), (1, up1))])
    return (y_prompt, y_sample, kv_p, kv_s, win_p, win_state_s, conv_a_p, conv_a_s, ffn_p, ffn_s)
```

```python
import functools
import math

import numpy as np
import jax
import jax.numpy as jnp
from jax import lax
from jax.experimental import pallas as pl
from jax.experimental.pallas import tpu as pltpu

F32 = jnp.float32
BF16 = jnp.bfloat16

N_HEADS = 16
HEAD_DIM = 64
N_KV = 4
HPG = N_HEADS // N_KV
CMP_BLOCK = 32
SEL_BLOCK = 64
TOP_N = 16
WINDOW = 512
N_BUCKETS = 32
MAX_EXACT = N_BUCKETS // 2
MAX_DISTANCE = 1024
PAGE_SIZE = 128
EPS = 1e-6

LANES = 128
TQ = 256
TK = 256
NEG_MASK = -1e30
NEG_SEL = -1e9
BIG_SCORE = 1e30
VMEM_LIMIT_BYTES = 56 * 1024 * 1024


def _bucket_thresholds():
    d = np.arange(0, 4 * MAX_DISTANCE)
    nf = np.maximum(d, 1).astype(np.float32)
    large = MAX_EXACT + (np.log(nf / MAX_EXACT) / math.log(MAX_DISTANCE / MAX_EXACT)
                         * (N_BUCKETS - MAX_EXACT)).astype(np.int32)
    bucket = np.where(d < MAX_EXACT, d, np.minimum(large, N_BUCKETS - 1))
    assert np.all(np.diff(bucket) >= 0)
    return [int(np.argmax(bucket >= k)) for k in range(N_BUCKETS)]


BUCKET_THR = _bucket_thresholds()
assert TQ == TK
N_TBL = -(-(BUCKET_THR[-1] + TK - 1) // TQ)
N_WIN_TILES = WINDOW // TK + 1


def _cparams(*sem):
    return pltpu.CompilerParams(dimension_semantics=sem, vmem_limit_bytes=VMEM_LIMIT_BYTES)


def _dot(a, b):
    return jnp.dot(a, b, preferred_element_type=F32)


def _dot_nt(a, b):
    return lax.dot_general(a, b, (((1,), (1,)), ((), ())), preferred_element_type=F32)


def _rms(x, g):
    return (x * lax.rsqrt(jnp.mean(x * x, axis=-1, keepdims=True) + EPS)) * g


def _sigmoid(x):
    return 1.0 / (1.0 + jnp.exp(-x))


def _shift_rows(v, carry):
    r1 = pltpu.roll(v, 1, 0)
    r2 = pltpu.roll(v, 2, 0)
    if v.shape[0] > 8:
        row = lax.broadcasted_iota(jnp.int32, (8, v.shape[1]), 0)
        h1 = jnp.where(row == 0, carry[7:8], r1[:8])
        h2 = jnp.where(row == 0, carry[6:7], jnp.where(row == 1, carry[7:8], r2[:8]))
        return jnp.concatenate([h1, r1[8:]], axis=0), jnp.concatenate([h2, r2[8:]], axis=0)
    row = lax.broadcasted_iota(jnp.int32, v.shape, 0)
    return (jnp.where(row == 0, carry[7:8], r1),
            jnp.where(row == 0, carry[6:7], jnp.where(row == 1, carry[7:8], r2)))


def _conv3(v, r1, r2, cw, cb):
    return (cw[0:1] * r2 + cw[1:2] * r1) + cw[2:3] * v + cb


def _mod_kernel(c_ref, w_ref, b_ref, o_ref):
    o_ref[0] = _dot(c_ref[...].astype(BF16), w_ref[0].astype(BF16)) + b_ref[0]


def _mod_call(c_all, w, b):
    n, d, nn = w.shape
    r = c_all.shape[0]
    tn = 512
    return pl.pallas_call(
        _mod_kernel, grid=(n, nn // tn),
        in_specs=[pl.BlockSpec((r, d), lambda i, j: (0, 0)),
                  pl.BlockSpec((1, d, tn), lambda i, j: (i, 0, j)),
                  pl.BlockSpec((1, 1, tn), lambda i, j: (i, 0, j))],
        out_specs=pl.BlockSpec((1, r, tn), lambda i, j: (i, 0, j)),
        out_shape=jax.ShapeDtypeStruct((n, r, nn), F32),
        compiler_params=_cparams("arbitrary", "arbitrary"), name="mod")(c_all, w, b)


def _mixer_kernel(seq_mode, d, *refs):
    if seq_mode:
        x_ref, mod_ref, g0_ref, g1_ref, win_ref, cw_ref, cb_ref, wout_ref, xo_ref, st_ref, carry = refs
    else:
        x_ref, mod_ref, g0_ref, g1_ref, win_ref, cw_ref, cb_ref, wout_ref, p0_ref, p1_ref, xo_ref, st_ref = refs
    x = x_ref[0]
    m = mod_ref[0]
    h = _rms(x, g0_ref[...]) * (1.0 + m[:, d:2 * d]) + m[:, :d]
    z = _dot(h.astype(BF16), win_ref[...])
    bg, cg, u = z[:, :d], z[:, d:2 * d], z[:, 2 * d:]
    v = cg * u
    if seq_mode:
        @pl.when(pl.program_id(1) == 0)
        def _():
            carry[...] = jnp.zeros_like(carry)
        r1, r2 = _shift_rows(v, carry[...])
        carry[...] = v[-8:]
        st_ref[0] = v[-8:]
    else:
        r1, r2 = p1_ref[0], p0_ref[0]
        st_ref[0] = v
    y = _conv3(v, r1, r2, cw_ref[...], cb_ref[...])
    o = _dot((bg * y).astype(BF16), wout_ref[...])
    xo_ref[0] = x + m[:, 2 * d:] * _rms(o, g1_ref[...])


def _row_specs(bx, s, tm, mod):
    sm = mod.shape[1]
    if sm == 1:
        mod_spec = pl.BlockSpec((1, 1, mod.shape[2]), lambda b, i: (b, 0, 0))
    else:
        mod_spec = pl.BlockSpec((1, tm, mod.shape[2]), lambda b, i: (b, i, 0))
    return mod_spec


def _full(shape):
    nd = len(shape)
    return pl.BlockSpec(shape, lambda b, i, _nd=nd: (0,) * _nd, pipeline_mode=pl.Buffered(1))


def _mixer_call(x, mod, g0, g1, w_in, cw, cb, w_out, prev, tm):
    bx, s, d = x.shape
    seq_mode = prev is None
    row = pl.BlockSpec((1, tm, d), lambda b, i: (b, i, 0))
    in_specs = [row, _row_specs(bx, s, tm, mod), _full((1, d)), _full((1, d)), _full(w_in.shape),
                _full(cw.shape), _full((1, d)), _full(w_out.shape)]
    args = [x, mod, g0, g1, w_in, cw, cb, w_out]
    if seq_mode:
        st_shape, st_spec = (bx, 8, d), pl.BlockSpec((1, 8, d), lambda b, i: (b, 0, 0))
        scratch = [pltpu.VMEM((8, d), F32)]
    else:
        in_specs += [row, row]
        args += [prev[0], prev[1]]
        st_shape, st_spec = (bx, s, d), row
        scratch = []
    return pl.pallas_call(
        functools.partial(_mixer_kernel, seq_mode, d), grid=(bx, s // tm),
        in_specs=in_specs, out_specs=[row, st_spec],
        out_shape=[jax.ShapeDtypeStruct((bx, s, d), F32), jax.ShapeDtypeStruct(st_shape, F32)],
        scratch_shapes=scratch, compiler_params=_cparams("arbitrary", "arbitrary"), name="mixer_a")(*args)


def _ffn_kernel(seq_mode, d, dff, n_chunks, *refs):
    if seq_mode:
        x_ref, mod_ref, g2_ref, g3_ref, wup_ref, cw_ref, cb_ref, wdn_ref, xo_ref, st_ref, carry = refs
    else:
        x_ref, mod_ref, g2_ref, g3_ref, wup_ref, cw_ref, cb_ref, wdn_ref, p0_ref, p1_ref, xo_ref, st_ref = refs
    x = x_ref[0]
    m = mod_ref[0]
    h = (_rms(x, g2_ref[...]) * (1.0 + m[:, d:2 * d]) + m[:, :d]).astype(BF16)
    if seq_mode:
        @pl.when(pl.program_id(1) == 0)
        def _():
            carry[...] = jnp.zeros_like(carry)
    cwid = dff // n_chunks
    acc = jnp.zeros((x.shape[0], d), F32)
    for k in range(n_chunks):
        halves = []
        for c0 in (k * cwid, dff + k * cwid):
            up = _dot(h, wup_ref[:, c0:c0 + cwid])
            if seq_mode:
                r1, r2 = _shift_rows(up, carry[:, c0:c0 + cwid])
                carry[:, c0:c0 + cwid] = up[-8:]
                st_ref[0, :, c0:c0 + cwid] = up[-8:]
            else:
                r1, r2 = p1_ref[0, :, c0:c0 + cwid], p0_ref[0, :, c0:c0 + cwid]
                st_ref[0, :, c0:c0 + cwid] = up
            halves.append(_conv3(up, r1, r2, cw_ref[:, c0:c0 + cwid], cb_ref[:, c0:c0 + cwid]))
        u, g = halves
        act = (g * _sigmoid(g)) * u
        acc = acc + _dot(act.astype(BF16), wdn_ref[k * cwid:(k + 1) * cwid, :])
    xo_ref[0] = x + m[:, 2 * d:] * _rms(acc, g3_ref[...])


def _ffn_call(x, mod, g2, g3, w_up, cw, cb, w_dn, prev, tm, n_chunks):
    bx, s, d = x.shape
    dff = w_dn.shape[0]
    seq_mode = prev is None
    row = pl.BlockSpec((1, tm, d), lambda b, i: (b, i, 0))
    in_specs = [row, _row_specs(bx, s, tm, mod), _full((1, d)), _full((1, d)), _full(w_up.shape),
                _full(cw.shape), _full((1, 2 * dff)), _full(w_dn.shape)]
    args = [x, mod, g2, g3, w_up, cw, cb, w_dn]
    if seq_mode:
        st_shape, st_spec = (bx, 8, 2 * dff), pl.BlockSpec((1, 8, 2 * dff), lambda b, i: (b, 0, 0))
        scratch = [pltpu.VMEM((8, 2 * dff), F32)]
    else:
        prow = pl.BlockSpec((1, tm, 2 * dff), lambda b, i: (b, i, 0))
        in_specs += [prow, prow]
        args += [prev[0], prev[1]]
        st_shape, st_spec = (bx, s, 2 * dff), prow
        scratch = []
    return pl.pallas_call(
        functools.partial(_ffn_kernel, seq_mode, d, dff, n_chunks), grid=(bx, s // tm),
        in_specs=in_specs, out_specs=[row, st_spec],
        out_shape=[jax.ShapeDtypeStruct((bx, s, d), F32), jax.ShapeDtypeStruct(st_shape, F32)],
        scratch_shapes=scratch, compiler_params=_cparams("arbitrary", "arbitrary"), name="conv_ffn")(*args)


def _kvq_kernel(d, n_rows_cols, emit_attn_kv, x_ref, modkv_ref, mod_ref, gkv_ref, g0_ref, wkv_ref, wqg_ref,
                rows_ref, win_ref, q_ref, gate_ref, *attn_refs):
    x = x_ref[0]
    r = x * lax.rsqrt(jnp.mean(x * x, axis=-1, keepdims=True) + EPS)
    mk = modkv_ref[0]
    hk = (r * gkv_ref[...]) * (1.0 + mk[:, d:]) + mk[:, :d]
    kv = _dot(hk.astype(BF16), wkv_ref[...])
    rows_ref[0] = kv[:, :n_rows_cols]
    win_ref[0] = kv[:, n_rows_cols:]
    if emit_attn_kv:
        kaug_ref, v1_ref, kw_ref, vw1_ref = attn_refs
        tm = x.shape[0]
        gw = N_KV * HEAD_DIM
        lane = lax.broadcasted_iota(jnp.int32, (tm, LANES), 1)
        pos = pl.program_id(1) * tm + lax.broadcasted_iota(jnp.int32, (tm, LANES), 0)
        low = lane < HEAD_DIM
        ones = jnp.where(low, 0.0, 1.0)
        onehot = jnp.where(lane - HEAD_DIM == pos // SEL_BLOCK, 1.0, 0.0)
        for ref, plane, fill in ((kaug_ref, 2, onehot), (v1_ref, 3, ones), (kw_ref, 4, 0.0), (vw1_ref, 5, ones)):
            for g in range(N_KV):
                c0 = plane * gw + (g // 2) * LANES
                blk = kv[:, c0:c0 + LANES]
                if g % 2 == 1:
                    blk = pltpu.roll(blk, HEAD_DIM, 1)
                ref[0, g] = jnp.where(low, blk, fill).astype(BF16)
    m = mod_ref[0]
    h1 = (r * g0_ref[...]) * (1.0 + m[:, d:2 * d]) + m[:, :d]
    qg = _dot(h1.astype(BF16), wqg_ref[...])
    nq = N_HEADS * HEAD_DIM
    q_ref[0] = qg[:, :nq] * (HEAD_DIM ** -0.5)
    gate_ref[0] = _sigmoid(qg[:, nq:])


def _kvq_call(x, modkv, mod, gkv, g0, w_kv, w_qg, tm, emit_attn_kv):
    bx, s, d = x.shape
    nkv = w_kv.shape[1]
    n_rows_cols = 4 * N_KV * HEAD_DIM
    n_win_cols = nkv - n_rows_cols
    nq = N_HEADS * HEAD_DIM
    ng = w_qg.shape[1] - nq
    row = lambda w: pl.BlockSpec((1, tm, w), lambda b, i: (b, i, 0))
    out_specs = [row(n_rows_cols), row(n_win_cols), row(nq), row(ng)]
    out_shape = [jax.ShapeDtypeStruct((bx, s, n_rows_cols), F32), jax.ShapeDtypeStruct((bx, s, n_win_cols), F32),
                 jax.ShapeDtypeStruct((bx, s, nq), F32), jax.ShapeDtypeStruct((bx, s, ng), F32)]
    if emit_attn_kv:
        out_specs += [pl.BlockSpec((1, N_KV, tm, LANES), lambda b, i: (b, 0, i, 0))] * 4
        out_shape += [jax.ShapeDtypeStruct((bx, N_KV, s, LANES), BF16)] * 4
    return pl.pallas_call(
        functools.partial(_kvq_kernel, d, n_rows_cols, emit_attn_kv), grid=(bx, s // tm),
        in_specs=[row(d), _row_specs(bx, s, tm, modkv), _row_specs(bx, s, tm, mod), _full((1, d)), _full((1, d)),
                  _full(w_kv.shape), _full(w_qg.shape)],
        out_specs=out_specs, out_shape=out_shape,
        compiler_params=_cparams("arbitrary", "arbitrary"), name="kv_q_proj")(x, modkv, mod, gkv, g0, w_kv, w_qg)


def _outproj_kernel(d, n_branch, *refs):
    x_ref, mod_ref, g1_ref, w_ref = refs[:4]
    xo_ref = refs[-1]
    if n_branch == 1:
        o = refs[4][0]
    else:
        o = sum(refs[4 + 2 * i][0] * refs[5 + 2 * i][0] for i in range(n_branch))
    y = _dot(o.astype(BF16), w_ref[...])
    xo_ref[0] = x_ref[0] + mod_ref[0][:, 2 * d:] * _rms(y, g1_ref[...])


def _outproj_call(x, mod, g1, w, branches, tm):
    bx, s, d = x.shape
    row = pl.BlockSpec((1, tm, d), lambda b, i: (b, i, 0))
    n_branch = 1 if len(branches) == 1 else len(branches) // 2
    return pl.pallas_call(
        functools.partial(_outproj_kernel, d, n_branch), grid=(bx, s // tm),
        in_specs=[row, _row_specs(bx, s, tm, mod), _full((1, d)), _full(w.shape)] + [row] * len(branches),
        out_specs=row, out_shape=jax.ShapeDtypeStruct((bx, s, d), F32),
        compiler_params=_cparams("arbitrary", "arbitrary"), name="attn_out_proj")(x, mod, g1, w, *branches)


def _gelu_tanh(x):
    return x * (0.5 * (1.0 + jnp.tanh(math.sqrt(2.0 / math.pi) * (x + 0.044715 * (x * x * x)))))


def _compress_kernel(tmb, x_ref, pe_ref, w1_ref, w2_ref, o_ref):
    acc = jnp.zeros((tmb, LANES), F32)
    for r in range(CMP_BLOCK):
        xr = x_ref[pl.ds(r, tmb, stride=CMP_BLOCK), :] + pe_ref[0, r]
        acc = acc + _dot(xr.astype(BF16), w1_ref[0, r])
    o_ref[...] = _dot(_gelu_tanh(acc).astype(BF16), w2_ref[0])


def _compress_call(rows2d, pe2, w1bd, w2bd):
    m = rows2d.shape[0] // CMP_BLOCK
    tmb = max(t for t in range(8, min(256, m) + 1, 8) if m % t == 0)
    return pl.pallas_call(
        functools.partial(_compress_kernel, tmb), grid=(m // tmb, 4),
        in_specs=[pl.BlockSpec((tmb * CMP_BLOCK, LANES), lambda i, j: (i, j)),
                  pl.BlockSpec((1, CMP_BLOCK, 1, LANES), lambda i, j: (j // 2, 0, 0, 0)),
                  pl.BlockSpec((1, CMP_BLOCK, LANES, LANES), lambda i, j: (j // 2, 0, 0, 0)),
                  pl.BlockSpec((1, LANES, LANES), lambda i, j: (j // 2, 0, 0))],
        out_specs=pl.BlockSpec((tmb, LANES), lambda i, j: (i, j)),
        out_shape=jax.ShapeDtypeStruct((m, 4 * LANES), F32),
        compiler_params=_cparams("arbitrary", "arbitrary"), name="compress")(rows2d, pe2, w1bd, w2bd)


def _compress_weights(cmp_pe, cmp_w1, cmp_w2):
    z = jnp.zeros((2, CMP_BLOCK, HEAD_DIM, HEAD_DIM), F32)
    w1 = cmp_w1.reshape(2, CMP_BLOCK, HEAD_DIM, HEAD_DIM)
    w1bd = jnp.concatenate([jnp.concatenate([w1, z], -1), jnp.concatenate([z, w1], -1)], -2).astype(BF16)
    z2 = jnp.zeros((2, HEAD_DIM, HEAD_DIM), F32)
    w2bd = jnp.concatenate([jnp.concatenate([cmp_w2, z2], -1), jnp.concatenate([z2, cmp_w2], -1)], -2).astype(BF16)
    pe2 = jnp.concatenate([cmp_pe, cmp_pe], -1)[:, :, None, :]
    return pe2, w1bd, w2bd


N_PAGE_CB = PAGE_SIZE // CMP_BLOCK
CB_ROWS = 8


PAGE_PITCH = N_KV * HEAD_DIM + 4


def _compress_pool_kernel(n_pg, n_i, x_hbm, pe_ref, mw_ref, w2_ref, o_ref, xbuf, sem):
    gw = N_KV * HEAD_DIM
    t = pl.program_id(0)
    n_t = pl.num_programs(0)
    slot_rows = n_pg * PAGE_PITCH

    def page_copy(step, slot, p):
        return pltpu.make_async_copy(x_hbm.at[(step % n_i) * n_pg + p, step // n_i],
                                     xbuf.at[pl.ds(slot * slot_rows + p * PAGE_PITCH, gw), :], sem.at[slot])

    def start_all(step, slot):
        for p in range(n_pg):
            page_copy(step, slot, p).start()

    slot = t % 2

    @pl.when(t == 0)
    def _():
        start_all(t, slot)

    @pl.when(t + 1 < n_t)
    def _():
        start_all(t + 1, 1 - slot)
    for p in range(n_pg):
        page_copy(t, slot, p).wait()

    def rows_of(g, d):
        return xbuf[pl.ds(slot * slot_rows + g * HEAD_DIM + d, n_pg, stride=PAGE_PITCH), :]
    lhs = jnp.concatenate(
        [(jnp.concatenate([rows_of(g, d) for g in range(N_KV)], axis=0) + pe_ref[0, d]).astype(BF16)
         for d in range(HEAD_DIM)], axis=1)
    z = _dot(_gelu_tanh(_dot(lhs, mw_ref[0])).astype(BF16), w2_ref[0])
    lane = lax.broadcasted_iota(jnp.int32, (n_pg, LANES), 1)
    low = lane < HEAD_DIM
    o_ref[...] = jnp.zeros(o_ref.shape, F32)
    for n in range(N_PAGE_CB):
        for c in range(N_KV // 2):
            a = z[(2 * c) * n_pg:(2 * c + 1) * n_pg, (n // 2) * LANES:(n // 2 + 1) * LANES]
            b = z[(2 * c + 1) * n_pg:(2 * c + 2) * n_pg, (n // 2) * LANES:(n // 2 + 1) * LANES]
            if n % 2 == 0:
                b = pltpu.roll(b, HEAD_DIM, 1)
            else:
                a = pltpu.roll(a, HEAD_DIM, 1)
            o_ref[c, pl.ds(n, n_pg, stride=CB_ROWS), :] = jnp.where(low, a, b)


def _compress_pool_call(cache_t, pe_t, mw, w2bd4):
    n_pool = cache_t.shape[0]
    n_pg = max(t for t in range(8, min(64, n_pool) + 1, 8) if n_pool % t == 0)
    n_i = n_pool // n_pg
    return pl.pallas_call(
        functools.partial(_compress_pool_kernel, n_pg, n_i), grid=(2 * n_i,),
        in_specs=[pl.BlockSpec(memory_space=pl.ANY),
                  pl.BlockSpec((1, HEAD_DIM, 1, PAGE_SIZE), lambda t: (t // n_i, 0, 0, 0)),
                  pl.BlockSpec((1, HEAD_DIM * PAGE_SIZE, N_PAGE_CB * HEAD_DIM), lambda t: (t // n_i, 0, 0)),
                  pl.BlockSpec((1, N_PAGE_CB * HEAD_DIM, N_PAGE_CB * HEAD_DIM), lambda t: (t // n_i, 0, 0))],
        out_specs=pl.BlockSpec((N_KV // 2, n_pg * CB_ROWS, LANES), lambda t: (t // n_i, t % n_i, 0)),
        out_shape=jax.ShapeDtypeStruct((N_KV, n_pool * CB_ROWS, LANES), F32),
        scratch_shapes=[pltpu.VMEM((2 * n_pg * PAGE_PITCH, PAGE_SIZE), F32), pltpu.SemaphoreType.DMA((2,))],
        compiler_params=_cparams("arbitrary"), name="compress_pool")(cache_t, pe_t, mw, w2bd4)


def _compress_pool_weights(cmp_pe, cmp_w1, cmp_w2):
    eye = np.eye(N_PAGE_CB, dtype=bool)
    w1t = cmp_w1.reshape(2, CMP_BLOCK, HEAD_DIM, HEAD_DIM).transpose(0, 2, 1, 3).astype(BF16)
    rows = jnp.broadcast_to(w1t[:, :, None], (2, HEAD_DIM, N_PAGE_CB, CMP_BLOCK, HEAD_DIM))
    rows = jnp.tile(rows.reshape(2, HEAD_DIM * PAGE_SIZE, HEAD_DIM), (1, 1, N_PAGE_CB))
    same_block = ((np.arange(HEAD_DIM * PAGE_SIZE) // CMP_BLOCK) % N_PAGE_CB)[:, None] == (
        np.arange(N_PAGE_CB * HEAD_DIM) // HEAD_DIM)[None, :]
    mw = jnp.where(same_block[None], rows, jnp.zeros((), BF16))
    w2bd4 = jnp.where(eye[None, :, None, :, None], cmp_w2[:, None, :, None, :], 0.0)
    w2bd4 = w2bd4.reshape(2, N_PAGE_CB * HEAD_DIM, N_PAGE_CB * HEAD_DIM).astype(BF16)
    pe_t = jnp.tile(cmp_pe.transpose(0, 2, 1), (1, 1, N_PAGE_CB))[:, :, None, :]
    return pe_t, mw, w2bd4


def _bias_of_distance(dist, tab):
    val = jnp.zeros(dist.shape, F32) + tab(0)
    for k in range(1, N_BUCKETS):
        val = jnp.where(dist >= BUCKET_THR[k], tab(k), val)
    return val


def _toeplitz_kernel(window, tab_ref, o_ref):
    g = pl.program_id(0)
    dt = pl.program_id(1)
    x = lax.broadcasted_iota(jnp.int32, (8, 2 * TK), 1)
    dist = dt * TQ - jnp.where(x < TK, x, x - 2 * TK)
    for hh in range(HPG):
        head = g * HPG + hh
        val = _bias_of_distance(dist, lambda k, head=head: tab_ref[k, head])
        val = jnp.where(dist < 0, NEG_MASK, val)
        if window is not None:
            val = jnp.where(dist >= window, NEG_MASK, val)
        tile = pltpu.roll(jnp.broadcast_to(val[0:1], (TQ, 2 * TK)), 0, 1, stride=1, stride_axis=0)
        o_ref[0, 0, hh * TQ:(hh + 1) * TQ, :] = tile[:, :TK]


def _toeplitz_call(rel_bias, n_chunks, window):
    return pl.pallas_call(
        functools.partial(_toeplitz_kernel, window), grid=(N_KV, n_chunks),
        in_specs=[pl.BlockSpec(memory_space=pltpu.SMEM)],
        out_specs=pl.BlockSpec((1, 1, HPG * TQ, TK), lambda g, t: (g, t, 0, 0)),
        out_shape=jax.ShapeDtypeStruct((N_KV, n_chunks, HPG * TQ, TK), F32),
        compiler_params=_cparams("arbitrary", "arbitrary"), name="bias_toeplitz")(rel_bias)


def _cmp_bias_kernel(tab_ref, o_ref):
    g = pl.program_id(0)
    qi = pl.program_id(1)
    row = lax.broadcasted_iota(jnp.int32, (LANES, TQ), 0)
    t_q = qi * TQ + lax.broadcasted_iota(jnp.int32, (LANES, TQ), 1)
    cblk = 2 * (row % HEAD_DIM) + row // HEAD_DIM
    dist = t_q - (cblk * CMP_BLOCK + (CMP_BLOCK - 1))
    for hh in range(HPG):
        head = g * HPG + hh
        val = _bias_of_distance(dist, lambda k, head=head: tab_ref[k, head])
        o_ref[0, 0, :, hh * TQ:(hh + 1) * TQ] = jnp.where(dist < 0, NEG_MASK, val)


def _cmp_bias_call(rel_bias, n_qt):
    return pl.pallas_call(
        _cmp_bias_kernel, grid=(N_KV, n_qt),
        in_specs=[pl.BlockSpec(memory_space=pltpu.SMEM)],
        out_specs=pl.BlockSpec((1, 1, LANES, HPG * TQ), lambda g, t: (g, t, 0, 0)),
        out_shape=jax.ShapeDtypeStruct((N_KV, n_qt, LANES, HPG * TQ), F32),
        compiler_params=_cparams("arbitrary", "arbitrary"), name="bias_cmp")(rel_bias)


def _bias_cols_kernel(dist_ref, tab_ref, o_ref):
    dist = dist_ref[...]
    val = _bias_of_distance(dist, lambda k: tab_ref[:, k:k + 1])
    o_ref[...] = jnp.where(dist < 0, NEG_MASK, val)


def _bias_cols_call(dist, tab_heads):
    r = dist.shape[1]
    return pl.pallas_call(
        _bias_cols_kernel, grid=(1,),
        in_specs=[pl.BlockSpec((N_HEADS, r), lambda i: (0, 0)), pl.BlockSpec((N_HEADS, N_BUCKETS), lambda i: (0, 0))],
        out_specs=pl.BlockSpec((N_HEADS, r), lambda i: (0, 0)),
        out_shape=jax.ShapeDtypeStruct((N_HEADS, r), F32),
        compiler_params=_cparams("arbitrary"), name="bias_cols")(dist, tab_heads)


def _nsa_seq_kernel(n_sb, q_ref, gate_ref, kaug_ref, v1_ref, kw_ref, vw1_ref, kcb_ref, vcb_ref, tcmp_ref,
                    tsel_ref, twin_ref, gexp_ref, o_ref, qaug, qw_sc, s_sc, mrun, mb, acc_sel, acc_win, score_sc):
    qi = pl.program_id(2)
    rows = HPG * TQ
    lane = lax.broadcasted_iota(jnp.int32, (TQ, LANES), 1)
    low = lane < HEAD_DIM

    qh = []
    for hh in range(HPG):
        qv = q_ref[0, :, (hh // 2) * LANES:(hh // 2 + 1) * LANES]
        if hh % 2 == 1:
            qv = pltpu.roll(qv, HEAD_DIM, 1)
        qh.append(jnp.where(low, qv, 0.0))
    qw = jnp.concatenate(qh, axis=0).astype(BF16)
    qw_sc[...] = qw

    s_c = _dot_nt(kcb_ref[0, 0].astype(BF16), qw) + tcmp_ref[0, 0]
    ok_c = s_c > 0.5 * NEG_MASK
    mx = jnp.max(s_c, axis=0, keepdims=True)
    p = jnp.where(ok_c, jnp.exp(s_c - mx), 0.0)
    den = jnp.sum(p, axis=0, keepdims=True)
    pn_t = p / jnp.where(den > 0, den, 1.0)
    o_c = _dot(pn_t.T.astype(BF16), vcb_ref[0, 0].astype(BF16))
    imp = pn_t[:, 0:TQ]
    for hh in range(1, HPG):
        imp = imp + pn_t[:, hh * TQ:(hh + 1) * TQ]
    n_blk = LANES // 2
    imp = imp[:n_blk] + imp[n_blk:]

    blk = lax.broadcasted_iota(jnp.int32, (n_blk, TQ), 0)
    cur = (qi * TQ + lax.broadcasted_iota(jnp.int32, (n_blk, TQ), 1)) // SEL_BLOCK
    valid = blk <= cur
    forced = (blk == 0) | (blk == cur) | (blk == cur - 1)
    score_sc[...] = jnp.where(valid & forced, BIG_SCORE, jnp.where(valid, imp, -BIG_SCORE))
    sub = 8
    groups = [score_sc[r0:r0 + sub, :] for r0 in range(0, n_blk, sub)]
    rowg = lax.broadcasted_iota(jnp.int32, (sub, TQ), 0)
    cnts = [jnp.zeros((sub, TQ), jnp.int32) for _ in groups]
    for bp in range(n_blk):
        r = score_sc[bp:bp + 1, :]
        for gi, s_g in enumerate(groups):
            if gi * sub > bp:
                ahead = r >= s_g
            elif (gi + 1) * sub - 1 < bp:
                ahead = r > s_g
            else:
                ahead = (r > s_g) | ((r == s_g) & (rowg + gi * sub > bp))
            cnts[gi] = cnts[gi] + jnp.where(ahead, 1, 0)
    cnt = jnp.concatenate(cnts, axis=0)
    selmask_t = jnp.where((cnt < min(TOP_N, n_sb)) & valid, 0.0, NEG_SEL)
    selmask = jnp.concatenate([jnp.zeros((n_blk, TQ), F32), selmask_t], axis=0).T
    for hh in range(HPG):
        qaug[hh * TQ:(hh + 1) * TQ, :] = jnp.where(low, qh[hh], selmask).astype(BF16)

    def branch(q_sc, k_ref, v_ref, t_ref, acc_sc, n_tiles, n_tbl, widths):
        mrun[...] = jnp.full(mrun.shape, NEG_MASK, F32)

        def tile_loop(step):
            done = 0
            for width in widths:
                def group(j, _, done=done, width=width):
                    step(tuple(done + width * j + u for u in range(width)))
                    return 0
                n_groups = (n_tiles - done) // width
                lax.fori_loop(0, n_groups, group, 0)
                done = done + n_groups * width

        def scores(tiles):
            m = mrun[...]
            for i in tiles:
                r0 = pl.multiple_of((qi - i) * TK, TK)
                s = _dot_nt(q_sc[...], k_ref[0, 0, pl.ds(r0, TK), :]) + t_ref[0, jnp.minimum(i, n_tbl)]
                s_sc[i] = s
                for c in range(TK // LANES):
                    m = jnp.maximum(m, s[:, c * LANES:(c + 1) * LANES])
            mrun[...] = m
        tile_loop(scores)
        mb[...] = jnp.broadcast_to(jnp.max(mrun[...], axis=1, keepdims=True), (rows, LANES))
        acc_sc[...] = jnp.zeros(acc_sc.shape, F32)

        def weigh(tiles):
            mbv = mb[...]
            mb2 = jnp.concatenate([mbv] * (TK // LANES), axis=1)
            acc = acc_sc[...]
            for i in tiles:
                r0 = pl.multiple_of((qi - i) * TK, TK)
                acc = acc + _dot(jnp.exp(s_sc[i] - mb2).astype(BF16), v_ref[0, 0, pl.ds(r0, TK), :])
            acc_sc[...] = acc
        tile_loop(weigh)
        return acc_sc[...]

    acc_s = branch(qaug, kaug_ref, v1_ref, tsel_ref, acc_sel, qi + 1, N_TBL, (8, 4, 2, 1))
    acc_w = branch(qw_sc, kw_ref, vw1_ref, twin_ref, acc_win, jnp.minimum(qi, N_WIN_TILES - 1) + 1, N_WIN_TILES - 1,
                   (N_WIN_TILES, 2, 1))

    gates = gate_ref[0]
    g_hi = gates.astype(BF16)
    g_lo = (gates - g_hi.astype(F32)).astype(BF16)
    g_all = _dot(jnp.concatenate([g_hi, g_lo], axis=1), gexp_ref[...])

    def normalized(acc, odd):
        swapped = pltpu.roll(acc, HEAD_DIM, 1)
        return swapped / acc if odd else acc / swapped

    outs = []
    for hh in range(HPG):
        gb = [g_all[:, (hh * 3 + br) * LANES:(hh * 3 + br + 1) * LANES] for br in range(3)]
        rs = slice(hh * TQ, (hh + 1) * TQ)
        outs.append(gb[0] * o_c[rs] + gb[1] * normalized(acc_s[rs], hh % 2) + gb[2] * normalized(acc_w[rs], hh % 2))
    for c in range(HPG // 2):
        o_ref[0, :, c * LANES:(c + 1) * LANES] = jnp.where(low, outs[2 * c], outs[2 * c + 1]).astype(o_ref.dtype)


def _nsa_seq_call(q, gates, kaug, v1, kw, vw1, kcb, vcb, tcmp, tsel, twin):
    b, s, _ = q.shape
    n_qt = s // TQ
    rows = HPG * TQ
    kv_spec = pl.BlockSpec((1, 1, s, LANES), lambda bb, g, i: (bb, g, 0, 0))
    cb_spec = pl.BlockSpec((1, 1, kcb.shape[2], LANES), lambda bb, g, i: (bb, g, 0, 0))
    tbl_spec = lambda t: pl.BlockSpec((1,) + t.shape[1:], lambda bb, g, i: (g, 0, 0, 0))
    n_gate = HPG * 3
    gexp = np.zeros((2 * LANES, n_gate * LANES), np.float32)
    for c in range(n_gate):
        gexp[c, c * LANES:(c + 1) * LANES] = 1.0
        gexp[LANES + c, c * LANES:(c + 1) * LANES] = 1.0
    gexp = jnp.asarray(gexp, BF16)
    return pl.pallas_call(
        functools.partial(_nsa_seq_kernel, s // SEL_BLOCK), grid=(b, N_KV, n_qt),
        in_specs=[pl.BlockSpec((1, TQ, HPG * HEAD_DIM), lambda bb, g, i: (bb, i, g)),
                  pl.BlockSpec((1, TQ, LANES), lambda bb, g, i: (bb, i, g)),
                  kv_spec, kv_spec, kv_spec, kv_spec, cb_spec, cb_spec,
                  pl.BlockSpec((1, 1, LANES, rows), lambda bb, g, i: (g, i, 0, 0)), tbl_spec(tsel), tbl_spec(twin),
                  pl.BlockSpec(gexp.shape, lambda bb, g, i: (0, 0))],
        out_specs=pl.BlockSpec((1, TQ, HPG * HEAD_DIM), lambda bb, g, i: (bb, i, g)),
        out_shape=jax.ShapeDtypeStruct((b, s, N_HEADS * HEAD_DIM), BF16),
        scratch_shapes=[pltpu.VMEM((rows, LANES), BF16), pltpu.VMEM((rows, LANES), BF16),
                        pltpu.VMEM((s // TK, rows, TK), F32), pltpu.VMEM((rows, LANES), F32),
                        pltpu.VMEM((rows, LANES), F32), pltpu.VMEM((rows, LANES), F32),
                        pltpu.VMEM((rows, LANES), F32), pltpu.VMEM((LANES // 2, TQ), F32)],
        compiler_params=_cparams("arbitrary", "arbitrary", "arbitrary"), name="nsa_seq")(
            q, gates, kaug, v1, kw, vw1, kcb, vcb, tcmp, tsel, twin, gexp)


def _softmax_lanes(s):
    p = jnp.exp(s - jnp.max(s, axis=1, keepdims=True))
    return p / jnp.sum(p, axis=1, keepdims=True)


def _rows_to_heads(rows):
    hg = lax.broadcasted_iota(jnp.int32, (N_HEADS, rows[0].shape[1]), 0) // HPG
    out = jnp.broadcast_to(rows[0], hg.shape)
    for g in range(1, N_KV):
        out = jnp.where(hg == g, rows[g], out)
    return out


def _nsa_stepT_kernel(n_pages, n_seq, pt_ref, *refs):
    del pt_ref
    n_pg = n_seq * n_pages
    q_ref = refs[0]
    cb_refs = refs[1:1 + n_pg]
    pg_refs = refs[1 + n_pg:1 + 2 * n_pg]
    (kvnew_ref, winnew_ref, wcol_ref, cwin_ref, bsel_ref, bnew_ref, bwin_ref, bcmp_ref) = refs[1 + 2 * n_pg:9 + 2 * n_pg]
    oc_ref, os_ref, ow_ref, nwin_ref = refs[9 + 2 * n_pg:13 + 2 * n_pg]
    (s_sc,) = refs[13 + 2 * n_pg:]
    for u in range(n_seq):
        one = lambda ref, u=u: ref.at[pl.ds(u, 1)]
        _nsa_step_one(n_pages, pl.program_id(0) * n_seq + u, one(q_ref), cb_refs[u * n_pages:(u + 1) * n_pages],
                      pg_refs[u * n_pages:(u + 1) * n_pages], one(kvnew_ref), one(winnew_ref), wcol_ref, one(cwin_ref),
                      bsel_ref, bnew_ref, bwin_ref, bcmp_ref, one(oc_ref), one(os_ref), one(ow_ref), one(nwin_ref),
                      s_sc.at[u])


def _nsa_step_one(n_pages, b, q_ref, cb_refs, pg_refs, kvnew_ref, winnew_ref, wcol_ref, cwin_ref, bsel_ref, bnew_ref,
                  bwin_ref, bcmp_ref, oc_ref, os_ref, ow_ref, nwin_ref, s_sc):
    gw = N_KV * HEAD_DIM
    n_past = n_pages * PAGE_SIZE
    q16 = q_ref[0].astype(BF16)
    qf = q16.astype(F32)
    lane = lax.broadcasted_iota(jnp.int32, (N_HEADS, LANES), 1)

    n_pad = LANES - CB_ROWS * n_pages
    cb = jnp.concatenate([jnp.concatenate([r[c] for c in range(N_KV)], axis=1) for r in cb_refs]
                         + ([jnp.zeros((n_pad, 2 * gw), F32)] if n_pad else []), axis=0)
    pn_c = _softmax_lanes(_dot_nt(q16, cb[:, :gw].astype(BF16)) + bcmp_ref[...])
    oc_ref[0] = _dot(pn_c.astype(BF16), cb[:, gw:].astype(BF16))

    grp = [pn_c[HPG * g:HPG * g + 1] + pn_c[HPG * g + 1:HPG * g + 2] + pn_c[HPG * g + 2:HPG * g + 3]
           + pn_c[HPG * g + 3:HPG * g + 4] for g in range(N_KV)]
    lane8 = lax.broadcasted_iota(jnp.int32, (8, LANES), 1)
    row8 = lax.broadcasted_iota(jnp.int32, (8, LANES), 0)
    imp = jnp.zeros((8, LANES), F32)
    for g in range(N_KV):
        imp = jnp.where(row8 == g, grp[g], imp)
    imp = imp + pltpu.roll(imp, LANES - 1, 1)
    n_past_blk = n_past // SEL_BLOCK
    cur_lane = LANES - 2
    is_blk = ((lane8 % CB_ROWS == 0) | (lane8 % CB_ROWS == 2)) & (lane8 < CB_ROWS * n_pages)
    last_lane = ((n_past_blk - 1) // 2) * CB_ROWS + 2 * ((n_past_blk - 1) % 2)
    forced = (lane8 == 0) | (lane8 == last_lane) | (lane8 == cur_lane)
    valid = is_blk | (lane8 == cur_lane)
    score = jnp.where(valid & forced, BIG_SCORE, jnp.where(valid, imp, -BIG_SCORE))
    cnt = jnp.zeros((8, LANES), jnp.int32)
    for k in range(2, LANES, 2):
        r = pltpu.roll(score, k, 1)
        cnt = cnt + jnp.where((r > score) | ((r == score) & (lane8 >= k)), 1, 0)
    selrows = jnp.where((cnt < min(TOP_N, n_past_blk + 1)) & valid, 0.0, NEG_SEL)
    selmask = _rows_to_heads([selrows[g:g + 1] for g in range(N_KV)])

    blocks_per_page = PAGE_SIZE // SEL_BLOCK
    for p in range(n_pages):
        kt = pg_refs[p][0, 0:gw, :].astype(BF16)
        msk = selmask[:, CB_ROWS * p:CB_ROWS * p + 1]
        for i in range(1, blocks_per_page):
            msk = jnp.where(lane >= i * SEL_BLOCK, selmask[:, CB_ROWS * p + 2 * i:CB_ROWS * p + 2 * i + 1], msk)
        s_sc[:, p * PAGE_SIZE:(p + 1) * PAGE_SIZE] = _dot(q16, kt) + bsel_ref[:, p * PAGE_SIZE:(p + 1) * PAGE_SIZE] + msk
    knew = kvnew_ref[0][:, 2 * gw:3 * gw].astype(BF16).astype(F32)
    s_new = jnp.sum(qf * knew, axis=1, keepdims=True) + bnew_ref[:, 0:1] + selmask[:, cur_lane:cur_lane + 1]
    s_sc[:, n_past:n_past + LANES] = jnp.where(lane == 0, s_new, NEG_MASK)
    s_all = s_sc[...]
    mx = jnp.max(s_all, axis=1, keepdims=True)
    den = jnp.sum(jnp.exp(s_all - mx), axis=1, keepdims=True)
    acc = jnp.zeros((N_HEADS, gw), F32)
    for p in range(n_pages):
        pn = jnp.exp(s_sc[:, p * PAGE_SIZE:(p + 1) * PAGE_SIZE] - mx) / den
        acc = acc + _dot_nt(pn.astype(BF16), pg_refs[p][0, gw:2 * gw, :].astype(BF16))
    pn_new = jnp.exp(s_new - mx) / den
    vnew = kvnew_ref[0][:, 3 * gw:4 * gw].astype(BF16).astype(F32)
    os_ref[0] = acc + pn_new.astype(BF16).astype(F32) * vnew

    cw = cwin_ref[0]
    w_len = cw.shape[1]
    s_w = _dot(q16, cw[0:gw].astype(BF16)) + bwin_ref[...]
    kwn = winnew_ref[0][:, 0:gw].astype(BF16).astype(F32)
    s_wn = jnp.sum(qf * kwn, axis=1, keepdims=True) + bnew_ref[:, 0:1]
    mxw = jnp.maximum(jnp.max(s_w, axis=1, keepdims=True), s_wn)
    pw = jnp.exp(s_w - mxw)
    pwn = jnp.exp(s_wn - mxw)
    denw = jnp.sum(pw, axis=1, keepdims=True) + pwn
    vwn = winnew_ref[0][:, gw:2 * gw].astype(BF16).astype(F32)
    ow_ref[0] = _dot_nt((pw / denw).astype(BF16), cw[gw:2 * gw].astype(BF16)) + (pwn / denw).astype(BF16).astype(F32) * vwn
    lane_w = lax.broadcasted_iota(jnp.int32, cw.shape, 1)
    nb_l = wcol_ref.shape[1]
    lane_b = lax.broadcasted_iota(jnp.int32, (cw.shape[0], nb_l), 1)
    col = jnp.sum(jnp.where(lane_b == b, wcol_ref[...], 0.0), axis=1, keepdims=True)
    nwin_ref[0] = jnp.where(lane_w == w_len - 1, col, pltpu.roll(cw, w_len - 1, 1))


def _nsa_step_call(page_table, qrows, cb_pool, cache_t, kvnew, winnew, wcol, cwin_t, bsel, bnew, bwin, bcmp):
    nb, n_pages = page_table.shape
    gw = N_KV * HEAD_DIM
    w_len = cwin_t.shape[2]
    n_seq = 2 if nb % 2 == 0 else 1
    slots = [(u, p) for u in range(n_seq) for p in range(n_pages)]
    cb_specs = [pl.BlockSpec((N_KV, None, CB_ROWS, LANES), lambda b, pt, _u=u, _p=p: (0, pt[b * n_seq + _u, _p], 0, 0))
                for u, p in slots]
    pg_specs = [pl.BlockSpec((1, 2 * gw, PAGE_SIZE), lambda b, pt, _u=u, _p=p: (pt[b * n_seq + _u, _p], 1, 0))
                for u, p in slots]
    const = lambda a: pl.BlockSpec(a.shape, lambda b, pt: (0, 0))
    seq = lambda *dims: pl.BlockSpec((n_seq,) + dims, lambda b, pt: (b, 0, 0))
    grid_spec = pltpu.PrefetchScalarGridSpec(
        num_scalar_prefetch=1, grid=(nb // n_seq,),
        in_specs=[seq(N_HEADS, gw)] + cb_specs + pg_specs + [
            seq(1, 4 * gw), seq(1, 2 * gw), const(wcol), seq(2 * gw, w_len),
            const(bsel), const(bnew), const(bwin), const(bcmp)],
        out_specs=[seq(N_HEADS, gw)] * 3 + [seq(2 * gw, w_len)],
        scratch_shapes=[pltpu.VMEM((n_seq, N_HEADS, n_pages * PAGE_SIZE + LANES), F32)])
    o_shape = jax.ShapeDtypeStruct((nb, N_HEADS, gw), F32)
    return pl.pallas_call(
        functools.partial(_nsa_stepT_kernel, n_pages, n_seq), grid_spec=grid_spec,
        out_shape=[o_shape, o_shape, o_shape, jax.ShapeDtypeStruct((nb, 2 * gw, w_len), F32)],
        compiler_params=_cparams("arbitrary"), name="nsa_step")(
            page_table, qrows, *([cb_pool] * len(slots)), *([cache_t] * len(slots)), kvnew, winnew, wcol, cwin_t,
            bsel, bnew, bwin, bcmp)


def _head_diag(o):
    b = o.shape[0]
    o5 = o.reshape(b, N_KV, HPG, N_KV, HEAD_DIM)
    return jnp.stack([o5[:, g, :, g, :] for g in range(N_KV)], axis=1).reshape(b, N_HEADS * HEAD_DIM)


def kernel(x_prompt, x_sample, c_prompt, c_sample, cache_kv, cache_win, state_conv_a, state_ffn_conv, page_table, mod_w, mod_b, norm_g, a_w_in, a_conv_w, a_conv_b, a_w_out, kv_mod_w, kv_mod_b, kv_norm_g, w_kv, cmp_pe, cmp_w1, cmp_w2, b_w_qg, b_w_out, rel_bias, ffn_w_up, ffn_conv_w, ffn_conv_b, ffn_w_down):
    bp, s, d = x_prompt.shape
    bs = x_sample.shape[0]
    depth = mod_w.shape[0]
    n_a = a_w_in.shape[0]
    assert depth == 2 and n_a == 1 and x_sample.shape[1] == 1
    dff = ffn_w_down.shape[1]
    n_pool = cache_kv.shape[0]
    n_pages = page_table.shape[1]
    past_len = n_pages * PAGE_SIZE
    gw = N_KV * HEAD_DIM
    nq = N_HEADS * HEAD_DIM

    n_c = bp + bs
    n_cp = -(-n_c // 8) * 8
    c_all = jnp.pad(jnp.concatenate([c_prompt, c_sample], 0), ((0, n_cp - n_c), (0, 0)))
    mods = _mod_call(c_all, mod_w.reshape(depth * 2, d, 3 * d), mod_b.reshape(depth * 2, 1, 3 * d))
    modkv = _mod_call(c_all, kv_mod_w[None], kv_mod_b[None, None])[0]
    mod_p = lambda i: mods[i, :bp][:, None, :]
    mod_s = lambda i: mods[i, bp:n_c][None]

    w_in = a_w_in[0].astype(BF16)
    w_out_a = a_w_out[0].astype(BF16)
    w_up = ffn_w_up.astype(BF16)
    w_dn = ffn_w_down.astype(BF16)
    w_kv_b = w_kv.astype(BF16)
    w_gate = jnp.pad(b_w_qg[0][:, nq:].reshape(d, N_KV, HPG * 3), ((0, 0), (0, 0), (0, LANES - HPG * 3)))
    w_qg = jnp.concatenate([b_w_qg[0][:, :nq], w_gate.reshape(d, N_KV * LANES)], axis=1).astype(BF16)
    w_out_b = b_w_out[0].astype(BF16)
    g = lambda l, i: norm_g[l, i][None]
    pe2, w1bd, w2bd = _compress_weights(cmp_pe, cmp_w1, cmp_w2)
    n_chunks = 2

    tm = min(512, s)
    tm_ffn = min(512, s)
    x1, st_a = _mixer_call(x_prompt, mod_p(0), g(0, 0), g(0, 1), w_in, a_conv_w[0], a_conv_b[0][None], w_out_a, None, tm)
    x2, st_f0 = _ffn_call(x1, mod_p(1), g(0, 2), g(0, 3), w_up[0], ffn_conv_w[0], ffn_conv_b[0][None], w_dn[0], None, tm_ffn, n_chunks)
    rows, win, q, gates, kaug, v1, kw, vw1 = _kvq_call(
        x2, modkv[:bp][:, None, :], mod_p(2), kv_norm_g[None], g(1, 0), w_kv_b, w_qg, tm, True)
    cb = _compress_call(rows.reshape(bp * s, 4 * gw), pe2, w1bd, w2bd)
    n_cbk = s // CMP_BLOCK
    assert n_cbk <= LANES
    cb = jnp.pad(cb.reshape(bp, n_cbk, 2, N_KV, HEAD_DIM), ((0, 0), (0, LANES - n_cbk), (0, 0), (0, 0), (0, 0)))
    cb = cb.reshape(bp, LANES // 2, 2, 2, N_KV, HEAD_DIM)
    cb = cb.transpose(3, 0, 4, 2, 1, 5).reshape(2, bp, N_KV, LANES, HEAD_DIM)
    kcb = jnp.pad(cb[0], ((0, 0),) * 3 + ((0, LANES - HEAD_DIM),))
    vcb = jnp.concatenate([cb[1], cb[1]], axis=-1)
    tsel = _toeplitz_call(rel_bias, N_TBL + 1, None)
    twin = _toeplitz_call(rel_bias, N_WIN_TILES, WINDOW)
    tcmp = _cmp_bias_call(rel_bias, s // TQ)
    o_att = _nsa_seq_call(q, gates, kaug, v1, kw, vw1, kcb, vcb, tcmp, tsel, twin)
    x3 = _outproj_call(x2, mod_p(2), g(1, 1), w_out_b, [o_att], tm)
    y_prompt, st_f1 = _ffn_call(x3, mod_p(3), g(1, 2), g(1, 3), w_up[1], ffn_conv_w[1], ffn_conv_b[1][None], w_dn[1], None, tm_ffn, n_chunks)
    kv_p = rows.reshape(bp, s, 4, N_KV, HEAD_DIM)
    keep = min(WINDOW, s)
    win_p = win[:, s - keep:].reshape(bp, keep, 2, N_KV, HEAD_DIM)
    conv_a_p = st_a[None, :, 6:8]
    ffn_p = jnp.stack([st_f0[:, 6:8], st_f1[:, 6:8]])

    xs = x_sample.reshape(1, bs, d)
    prev_a = (state_conv_a[0, :, 0][None], state_conv_a[0, :, 1][None])
    xs1, v_a = _mixer_call(xs, mod_s(0), g(0, 0), g(0, 1), w_in, a_conv_w[0], a_conv_b[0][None], w_out_a, prev_a, bs)
    prev_f = lambda l: (state_ffn_conv[l, :, 0][None], state_ffn_conv[l, :, 1][None])
    xs2, up0 = _ffn_call(xs1, mod_s(1), g(0, 2), g(0, 3), w_up[0], ffn_conv_w[0], ffn_conv_b[0][None], w_dn[0], prev_f(0), bs, n_chunks)
    rows_s, win_s, q_s, gates_s = _kvq_call(xs2, modkv[bp:n_c][None], mod_s(2), kv_norm_g[None], g(1, 0), w_kv_b, w_qg, bs, False)
    cache_t = cache_kv.transpose(0, 2, 3, 4, 1).reshape(n_pool, 4 * gw, PAGE_SIZE)
    w_len = cache_win.shape[1]
    cwin_t = cache_win.transpose(0, 2, 3, 4, 1).reshape(bs, 2 * gw, w_len)
    cb_pool = _compress_pool_call(cache_t.reshape(n_pool, 4, gw, PAGE_SIZE),
                                  *_compress_pool_weights(cmp_pe, cmp_w1, cmp_w2))
    cb_pool = cb_pool.reshape(N_KV, n_pool, CB_ROWS, LANES)
    assert n_pages * CB_ROWS <= LANES
    d_sel = past_len - np.arange(past_len)
    d_new = np.where(np.arange(LANES) == 0, 0, -1)
    d_win = w_len - np.arange(w_len)
    d_win = np.where(d_win < WINDOW, d_win, -1)
    cl = np.arange(LANES)
    d_cmp = past_len - (((cl // CB_ROWS) * N_PAGE_CB + cl % CB_ROWS) * CMP_BLOCK + CMP_BLOCK - 1)
    d_cmp = np.where((cl % CB_ROWS < N_PAGE_CB) & (cl // CB_ROWS < n_pages), d_cmp, -1)
    assert d_sel.min() >= 0 and d_cmp[d_cmp != -1].min() >= 0
    dist = np.concatenate([d_sel, d_new, d_win, d_cmp]).astype(np.int32)
    bias_cols = _bias_cols_call(jnp.asarray(np.repeat(dist[None, :], N_HEADS, 0)), rel_bias.T)
    bsel, bnew = bias_cols[:, :past_len], bias_cols[:, past_len:past_len + LANES]
    bwin = bias_cols[:, past_len + LANES:past_len + LANES + w_len]
    bcmp = bias_cols[:, past_len + LANES + w_len:]
    head_group = (np.arange(N_HEADS)[:, None] // HPG == np.arange(N_KV)[None, :])[None, :, :, None]
    qrows = jnp.where(head_group, q_s.reshape(bs, N_HEADS, 1, HEAD_DIM), 0.0).reshape(bs, N_HEADS, gw)
    oc, os_, ow, nwin_t = _nsa_step_call(
        page_table, qrows, cb_pool, cache_t, rows_s.reshape(bs, 1, 4 * gw),
        win_s.reshape(bs, 1, 2 * gw), win_s[0].T, cwin_t, bsel, bnew, bwin, bcmp)
    nwin = nwin_t.reshape(bs, 2, N_KV, HEAD_DIM, w_len).transpose(0, 4, 1, 2, 3)
    gts = gates_s[0].reshape(bs, N_KV, LANES)[:, :, :HPG * 3].reshape(bs, N_HEADS, 3)
    branches = []
    for br, o in enumerate((oc, os_, ow)):
        branches += [jnp.repeat(gts[:, :, br], HEAD_DIM, axis=1)[None], _head_diag(o)[None]]
    xs3 = _outproj_call(xs2, mod_s(2), g(1, 1), w_out_b, branches, bs)
    ys, up1 = _ffn_call(xs3, mod_s(3), g(1, 2), g(1, 3), w_up[1], ffn_conv_w[1], ffn_conv_b[1][None], w_dn[1], prev_f(1), bs, n_chunks)
    y_sample = ys.reshape(bs, 1, d)
    kv_s = rows_s.reshape(bs, 1, 4, N_KV, HEAD_DIM)
    win_state_s = nwin
    conv_a_s = jnp.stack([state_conv_a[0, :, 1], v_a[0]], axis=1)[None]
    ffn_s = jnp.stack([jnp.stack([state_ffn_conv[l, :, 1], u[0]], axis=1) for l, u in ((0, up0), (1, up1))])
    return (y_prompt, y_sample, kv_p, kv_s, win_p, win_state_s, conv_a_p, conv_a_s, ffn_p, ffn_s)
```

```python
import functools
import math

import numpy as np
import jax
import jax.numpy as jnp
from jax import lax
from jax.experimental import pallas as pl
from jax.experimental.pallas import tpu as pltpu

F32 = jnp.float32
BF16 = jnp.bfloat16

N_HEADS = 16
HEAD_DIM = 64
N_KV = 4
HPG = N_HEADS // N_KV
CMP_BLOCK = 32
SEL_BLOCK = 64
TOP_N = 16
WINDOW = 512
N_BUCKETS = 32
MAX_EXACT = N_BUCKETS // 2
MAX_DISTANCE = 1024
PAGE_SIZE = 128
EPS = 1e-6

LANES = 128
TQ = 256
TK = 256
NEG_MASK = -1e30
NEG_SEL = -1e9
BIG_SCORE = 1e30
VMEM_LIMIT_BYTES = 56 * 1024 * 1024


def _bucket_thresholds():
    d = np.arange(0, 4 * MAX_DISTANCE)
    nf = np.maximum(d, 1).astype(np.float32)
    large = MAX_EXACT + (np.log(nf / MAX_EXACT) / math.log(MAX_DISTANCE / MAX_EXACT)
                         * (N_BUCKETS - MAX_EXACT)).astype(np.int32)
    bucket = np.where(d < MAX_EXACT, d, np.minimum(large, N_BUCKETS - 1))
    assert np.all(np.diff(bucket) >= 0)
    return [int(np.argmax(bucket >= k)) for k in range(N_BUCKETS)]


BUCKET_THR = _bucket_thresholds()
assert TQ == TK
N_TBL = -(-(BUCKET_THR[-1] + TK - 1) // TQ)
N_WIN_TILES = WINDOW // TK + 1


def _cparams(*sem):
    return pltpu.CompilerParams(dimension_semantics=sem, vmem_limit_bytes=VMEM_LIMIT_BYTES)


def _dot(a, b):
    return jnp.dot(a, b, preferred_element_type=F32)


def _dot_nt(a, b):
    return lax.dot_general(a, b, (((1,), (1,)), ((), ())), preferred_element_type=F32)


def _rms(x, g):
    return (x * lax.rsqrt(jnp.mean(x * x, axis=-1, keepdims=True) + EPS)) * g


def _sigmoid(x):
    return 1.0 / (1.0 + jnp.exp(-x))


def _shift_rows(v, carry):
    r1 = pltpu.roll(v, 1, 0)
    r2 = pltpu.roll(v, 2, 0)
    if v.shape[0] > 8:
        row = lax.broadcasted_iota(jnp.int32, (8, v.shape[1]), 0)
        h1 = jnp.where(row == 0, carry[7:8], r1[:8])
        h2 = jnp.where(row == 0, carry[6:7], jnp.where(row == 1, carry[7:8], r2[:8]))
        return jnp.concatenate([h1, r1[8:]], axis=0), jnp.concatenate([h2, r2[8:]], axis=0)
    row = lax.broadcasted_iota(jnp.int32, v.shape, 0)
    return (jnp.where(row == 0, carry[7:8], r1),
            jnp.where(row == 0, carry[6:7], jnp.where(row == 1, carry[7:8], r2)))


def _conv3(v, r1, r2, cw, cb):
    return (cw[0:1] * r2 + cw[1:2] * r1) + cw[2:3] * v + cb


def _mod_kernel(c_ref, w_ref, b_ref, o_ref):
    o_ref[0] = _dot(c_ref[...].astype(BF16), w_ref[0].astype(BF16)) + b_ref[0]


def _mod_call(c_all, w, b):
    n, d, nn = w.shape
    r = c_all.shape[0]
    tn = 512
    return pl.pallas_call(
        _mod_kernel, grid=(n, nn // tn),
        in_specs=[pl.BlockSpec((r, d), lambda i, j: (0, 0)),
                  pl.BlockSpec((1, d, tn), lambda i, j: (i, 0, j)),
                  pl.BlockSpec((1, 1, tn), lambda i, j: (i, 0, j))],
        out_specs=pl.BlockSpec((1, r, tn), lambda i, j: (i, 0, j)),
        out_shape=jax.ShapeDtypeStruct((n, r, nn), F32),
        compiler_params=_cparams("arbitrary", "arbitrary"), name="mod")(c_all, w, b)


def _mixer_kernel(seq_mode, d, *refs):
    if seq_mode:
        x_ref, mod_ref, g0_ref, g1_ref, win_ref, cw_ref, cb_ref, wout_ref, xo_ref, st_ref, carry = refs
    else:
        x_ref, mod_ref, g0_ref, g1_ref, win_ref, cw_ref, cb_ref, wout_ref, p0_ref, p1_ref, xo_ref, st_ref = refs
    x = x_ref[0]
    m = mod_ref[0]
    h = _rms(x, g0_ref[...]) * (1.0 + m[:, d:2 * d]) + m[:, :d]
    z = _dot(h.astype(BF16), win_ref[...])
    bg, cg, u = z[:, :d], z[:, d:2 * d], z[:, 2 * d:]
    v = cg * u
    if seq_mode:
        @pl.when(pl.program_id(1) == 0)
        def _():
            carry[...] = jnp.zeros_like(carry)
        r1, r2 = _shift_rows(v, carry[...])
        carry[...] = v[-8:]
        st_ref[0] = v[-8:]
    else:
        r1, r2 = p1_ref[0], p0_ref[0]
        st_ref[0] = v
    y = _conv3(v, r1, r2, cw_ref[...], cb_ref[...])
    o = _dot((bg * y).astype(BF16), wout_ref[...])
    xo_ref[0] = x + m[:, 2 * d:] * _rms(o, g1_ref[...])


def _row_specs(bx, s, tm, mod):
    sm = mod.shape[1]
    if sm == 1:
        mod_spec = pl.BlockSpec((1, 1, mod.shape[2]), lambda b, i: (b, 0, 0))
    else:
        mod_spec = pl.BlockSpec((1, tm, mod.shape[2]), lambda b, i: (b, i, 0))
    return mod_spec


def _full(shape):
    nd = len(shape)
    return pl.BlockSpec(shape, lambda b, i, _nd=nd: (0,) * _nd, pipeline_mode=pl.Buffered(1))


def _mixer_call(x, mod, g0, g1, w_in, cw, cb, w_out, prev, tm):
    bx, s, d = x.shape
    seq_mode = prev is None
    row = pl.BlockSpec((1, tm, d), lambda b, i: (b, i, 0))
    in_specs = [row, _row_specs(bx, s, tm, mod), _full((1, d)), _full((1, d)), _full(w_in.shape),
                _full(cw.shape), _full((1, d)), _full(w_out.shape)]
    args = [x, mod, g0, g1, w_in, cw, cb, w_out]
    if seq_mode:
        st_shape, st_spec = (bx, 8, d), pl.BlockSpec((1, 8, d), lambda b, i: (b, 0, 0))
        scratch = [pltpu.VMEM((8, d), F32)]
    else:
        in_specs += [row, row]
        args += [prev[0], prev[1]]
        st_shape, st_spec = (bx, s, d), row
        scratch = []
    return pl.pallas_call(
        functools.partial(_mixer_kernel, seq_mode, d), grid=(bx, s // tm),
        in_specs=in_specs, out_specs=[row, st_spec],
        out_shape=[jax.ShapeDtypeStruct((bx, s, d), F32), jax.ShapeDtypeStruct(st_shape, F32)],
        scratch_shapes=scratch, compiler_params=_cparams("arbitrary", "arbitrary"), name="mixer_a")(*args)


def _ffn_kernel(seq_mode, with_attn, d, dff, n_chunks, *refs):
    if with_attn:
        (o_ref, moda_ref, g1_ref, wout_ref), refs = refs[:4], refs[4:]
    if seq_mode:
        x_ref, mod_ref, g2_ref, g3_ref, wup_ref, cw_ref, cb_ref, wdn_ref, xo_ref, st_ref, carry = refs
    else:
        x_ref, mod_ref, g2_ref, g3_ref, wup_ref, cw_ref, cb_ref, wdn_ref, p0_ref, p1_ref, xo_ref, st_ref = refs
    x = x_ref[0]
    if with_attn:
        x = x + moda_ref[0][:, 2 * d:] * _rms(_dot(o_ref[0].astype(BF16), wout_ref[...]), g1_ref[...])
    m = mod_ref[0]
    h = (_rms(x, g2_ref[...]) * (1.0 + m[:, d:2 * d]) + m[:, :d]).astype(BF16)
    if seq_mode:
        @pl.when(pl.program_id(1) == 0)
        def _():
            carry[...] = jnp.zeros_like(carry)
    cwid = dff // n_chunks
    acc = jnp.zeros((x.shape[0], d), F32)
    for k in range(n_chunks):
        halves = []
        for c0 in (k * cwid, dff + k * cwid):
            up = _dot(h, wup_ref[:, c0:c0 + cwid])
            if seq_mode:
                r1, r2 = _shift_rows(up, carry[:, c0:c0 + cwid])
                carry[:, c0:c0 + cwid] = up[-8:]
                st_ref[0, :, c0:c0 + cwid] = up[-8:]
            else:
                r1, r2 = p1_ref[0, :, c0:c0 + cwid], p0_ref[0, :, c0:c0 + cwid]
                st_ref[0, :, c0:c0 + cwid] = up
            halves.append(_conv3(up, r1, r2, cw_ref[:, c0:c0 + cwid], cb_ref[:, c0:c0 + cwid]))
        u, g = halves
        act = (g * _sigmoid(g)) * u
        acc = acc + _dot(act.astype(BF16), wdn_ref[k * cwid:(k + 1) * cwid, :])
    xo_ref[0] = x + m[:, 2 * d:] * _rms(acc, g3_ref[...])


def _ffn_call(x, mod, g2, g3, w_up, cw, cb, w_dn, prev, tm, n_chunks, attn=None):
    bx, s, d = x.shape
    dff = w_dn.shape[0]
    seq_mode = prev is None
    row = pl.BlockSpec((1, tm, d), lambda b, i: (b, i, 0))
    in_specs = [row, _row_specs(bx, s, tm, mod), _full((1, d)), _full((1, d)), _full(w_up.shape),
                _full(cw.shape), _full((1, 2 * dff)), _full(w_dn.shape)]
    args = [x, mod, g2, g3, w_up, cw, cb, w_dn]
    if attn is not None:
        in_specs = [row, _row_specs(bx, s, tm, attn[1]), _full((1, d)), _full(attn[3].shape)] + in_specs
        args = list(attn) + args
    if seq_mode:
        st_shape, st_spec = (bx, 8, 2 * dff), pl.BlockSpec((1, 8, 2 * dff), lambda b, i: (b, 0, 0))
        scratch = [pltpu.VMEM((8, 2 * dff), F32)]
    else:
        prow = pl.BlockSpec((1, tm, 2 * dff), lambda b, i: (b, i, 0))
        in_specs += [prow, prow]
        args += [prev[0], prev[1]]
        st_shape, st_spec = (bx, s, 2 * dff), prow
        scratch = []
    return pl.pallas_call(
        functools.partial(_ffn_kernel, seq_mode, attn is not None, d, dff, n_chunks), grid=(bx, s // tm),
        in_specs=in_specs, out_specs=[row, st_spec],
        out_shape=[jax.ShapeDtypeStruct((bx, s, d), F32), jax.ShapeDtypeStruct(st_shape, F32)],
        scratch_shapes=scratch, compiler_params=_cparams("arbitrary", "arbitrary"), name="conv_ffn")(*args)


def _kvq_kernel(d, n_rows_cols, emit_attn_kv, x_ref, modkv_ref, mod_ref, gkv_ref, g0_ref, wkv_ref, wqg_ref,
                rows_ref, win_ref, q_ref, gate_ref, *attn_refs):
    x = x_ref[0]
    r = x * lax.rsqrt(jnp.mean(x * x, axis=-1, keepdims=True) + EPS)
    mk = modkv_ref[0]
    hk = (r * gkv_ref[...]) * (1.0 + mk[:, d:]) + mk[:, :d]
    kv = _dot(hk.astype(BF16), wkv_ref[...])
    rows_ref[0] = kv[:, :n_rows_cols]
    win_ref[0] = kv[:, n_rows_cols:]
    if emit_attn_kv:
        kaug_ref, v1_ref, kw_ref, vw1_ref = attn_refs
        tm = x.shape[0]
        gw = N_KV * HEAD_DIM
        lane = lax.broadcasted_iota(jnp.int32, (tm, LANES), 1)
        pos = pl.program_id(1) * tm + lax.broadcasted_iota(jnp.int32, (tm, LANES), 0)
        low = lane < HEAD_DIM
        ones = jnp.where(low, 0.0, 1.0)
        onehot = jnp.where(lane - HEAD_DIM == pos // SEL_BLOCK, 1.0, 0.0)
        for ref, plane, fill in ((kaug_ref, 2, onehot), (v1_ref, 3, ones), (kw_ref, 4, 0.0), (vw1_ref, 5, ones)):
            for g in range(N_KV):
                c0 = plane * gw + (g // 2) * LANES
                blk = kv[:, c0:c0 + LANES]
                if g % 2 == 1:
                    blk = pltpu.roll(blk, HEAD_DIM, 1)
                ref[0, g] = jnp.where(low, blk, fill).astype(BF16)
    m = mod_ref[0]
    h1 = (r * g0_ref[...]) * (1.0 + m[:, d:2 * d]) + m[:, :d]
    qg = _dot(h1.astype(BF16), wqg_ref[...])
    nq = N_HEADS * HEAD_DIM
    q_ref[0] = qg[:, :nq] * (HEAD_DIM ** -0.5)
    gate_ref[0] = _sigmoid(qg[:, nq:])


def _kvq_call(x, modkv, mod, gkv, g0, w_kv, w_qg, tm, emit_attn_kv):
    bx, s, d = x.shape
    nkv = w_kv.shape[1]
    n_rows_cols = 4 * N_KV * HEAD_DIM
    n_win_cols = nkv - n_rows_cols
    nq = N_HEADS * HEAD_DIM
    ng = w_qg.shape[1] - nq
    row = lambda w: pl.BlockSpec((1, tm, w), lambda b, i: (b, i, 0))
    out_specs = [row(n_rows_cols), row(n_win_cols), row(nq), row(ng)]
    out_shape = [jax.ShapeDtypeStruct((bx, s, n_rows_cols), F32), jax.ShapeDtypeStruct((bx, s, n_win_cols), F32),
                 jax.ShapeDtypeStruct((bx, s, nq), F32), jax.ShapeDtypeStruct((bx, s, ng), F32)]
    if emit_attn_kv:
        out_specs += [pl.BlockSpec((1, N_KV, tm, LANES), lambda b, i: (b, 0, i, 0))] * 4
        out_shape += [jax.ShapeDtypeStruct((bx, N_KV, s, LANES), BF16)] * 4
    return pl.pallas_call(
        functools.partial(_kvq_kernel, d, n_rows_cols, emit_attn_kv), grid=(bx, s // tm),
        in_specs=[row(d), _row_specs(bx, s, tm, modkv), _row_specs(bx, s, tm, mod), _full((1, d)), _full((1, d)),
                  _full(w_kv.shape), _full(w_qg.shape)],
        out_specs=out_specs, out_shape=out_shape,
        compiler_params=_cparams("arbitrary", "arbitrary"), name="kv_q_proj")(x, modkv, mod, gkv, g0, w_kv, w_qg)


def _outproj_kernel(d, n_branch, *refs):
    x_ref, mod_ref, g1_ref, w_ref = refs[:4]
    xo_ref = refs[-1]
    if n_branch == 1:
        o = refs[4][0]
    else:
        o = sum(refs[4 + 2 * i][0] * refs[5 + 2 * i][0] for i in range(n_branch))
    y = _dot(o.astype(BF16), w_ref[...])
    xo_ref[0] = x_ref[0] + mod_ref[0][:, 2 * d:] * _rms(y, g1_ref[...])


def _outproj_call(x, mod, g1, w, branches, tm):
    bx, s, d = x.shape
    row = pl.BlockSpec((1, tm, d), lambda b, i: (b, i, 0))
    n_branch = 1 if len(branches) == 1 else len(branches) // 2
    return pl.pallas_call(
        functools.partial(_outproj_kernel, d, n_branch), grid=(bx, s // tm),
        in_specs=[row, _row_specs(bx, s, tm, mod), _full((1, d)), _full(w.shape)] + [row] * len(branches),
        out_specs=row, out_shape=jax.ShapeDtypeStruct((bx, s, d), F32),
        compiler_params=_cparams("arbitrary", "arbitrary"), name="attn_out_proj")(x, mod, g1, w, *branches)


def _gelu_tanh(x):
    return x * (0.5 * (1.0 + jnp.tanh(math.sqrt(2.0 / math.pi) * (x + 0.044715 * (x * x * x)))))


def _compress_kernel(tmb, x_ref, pe_ref, w1_ref, w2_ref, o_ref):
    acc = jnp.zeros((tmb, LANES), F32)
    for r in range(CMP_BLOCK):
        xr = x_ref[pl.ds(r, tmb, stride=CMP_BLOCK), :] + pe_ref[0, r]
        acc = acc + _dot(xr.astype(BF16), w1_ref[0, r])
    o_ref[...] = _dot(_gelu_tanh(acc).astype(BF16), w2_ref[0])


def _compress_call(rows2d, pe2, w1bd, w2bd):
    m = rows2d.shape[0] // CMP_BLOCK
    tmb = max(t for t in range(8, min(256, m) + 1, 8) if m % t == 0)
    return pl.pallas_call(
        functools.partial(_compress_kernel, tmb), grid=(m // tmb, 4),
        in_specs=[pl.BlockSpec((tmb * CMP_BLOCK, LANES), lambda i, j: (i, j)),
                  pl.BlockSpec((1, CMP_BLOCK, 1, LANES), lambda i, j: (j // 2, 0, 0, 0)),
                  pl.BlockSpec((1, CMP_BLOCK, LANES, LANES), lambda i, j: (j // 2, 0, 0, 0)),
                  pl.BlockSpec((1, LANES, LANES), lambda i, j: (j // 2, 0, 0))],
        out_specs=pl.BlockSpec((tmb, LANES), lambda i, j: (i, j)),
        out_shape=jax.ShapeDtypeStruct((m, 4 * LANES), F32),
        compiler_params=_cparams("arbitrary", "arbitrary"), name="compress")(rows2d, pe2, w1bd, w2bd)


def _compress_weights(cmp_pe, cmp_w1, cmp_w2):
    z = jnp.zeros((2, CMP_BLOCK, HEAD_DIM, HEAD_DIM), F32)
    w1 = cmp_w1.reshape(2, CMP_BLOCK, HEAD_DIM, HEAD_DIM)
    w1bd = jnp.concatenate([jnp.concatenate([w1, z], -1), jnp.concatenate([z, w1], -1)], -2).astype(BF16)
    z2 = jnp.zeros((2, HEAD_DIM, HEAD_DIM), F32)
    w2bd = jnp.concatenate([jnp.concatenate([cmp_w2, z2], -1), jnp.concatenate([z2, cmp_w2], -1)], -2).astype(BF16)
    pe2 = jnp.concatenate([cmp_pe, cmp_pe], -1)[:, :, None, :]
    return pe2, w1bd, w2bd


N_PAGE_CB = PAGE_SIZE // CMP_BLOCK
CB_ROWS = 8


PAGE_PITCH = N_KV * HEAD_DIM + 4


def _compress_pool_kernel(n_pg, n_i, x_hbm, pe_ref, mw_ref, w2_ref, o_ref, xbuf, sem):
    gw = N_KV * HEAD_DIM
    t = pl.program_id(0)
    n_t = pl.num_programs(0)
    slot_rows = n_pg * PAGE_PITCH

    def page_copy(step, slot, p):
        return pltpu.make_async_copy(x_hbm.at[(step % n_i) * n_pg + p, step // n_i],
                                     xbuf.at[pl.ds(slot * slot_rows + p * PAGE_PITCH, gw), :], sem.at[slot])

    def start_all(step, slot):
        for p in range(n_pg):
            page_copy(step, slot, p).start()

    slot = t % 2

    @pl.when(t == 0)
    def _():
        start_all(t, slot)

    @pl.when(t + 1 < n_t)
    def _():
        start_all(t + 1, 1 - slot)
    for p in range(n_pg):
        page_copy(t, slot, p).wait()

    def rows_of(g, d):
        return xbuf[pl.ds(slot * slot_rows + g * HEAD_DIM + d, n_pg, stride=PAGE_PITCH), :]
    lhs = jnp.concatenate(
        [(jnp.concatenate([rows_of(g, d) for g in range(N_KV)], axis=0) + pe_ref[0, d]).astype(BF16)
         for d in range(HEAD_DIM)], axis=1)
    z = _dot(_gelu_tanh(_dot(lhs, mw_ref[0])).astype(BF16), w2_ref[0])
    lane = lax.broadcasted_iota(jnp.int32, (n_pg, LANES), 1)
    low = lane < HEAD_DIM
    o_ref[...] = jnp.zeros(o_ref.shape, F32)
    for n in range(N_PAGE_CB):
        for c in range(N_KV // 2):
            a = z[(2 * c) * n_pg:(2 * c + 1) * n_pg, (n // 2) * LANES:(n // 2 + 1) * LANES]
            b = z[(2 * c + 1) * n_pg:(2 * c + 2) * n_pg, (n // 2) * LANES:(n // 2 + 1) * LANES]
            if n % 2 == 0:
                b = pltpu.roll(b, HEAD_DIM, 1)
            else:
                a = pltpu.roll(a, HEAD_DIM, 1)
            o_ref[c, pl.ds(n, n_pg, stride=CB_ROWS), :] = jnp.where(low, a, b)


def _compress_pool_call(cache_t, pe_t, mw, w2bd4):
    n_pool = cache_t.shape[0]
    n_pg = max(t for t in range(8, min(64, n_pool) + 1, 8) if n_pool % t == 0)
    n_i = n_pool // n_pg
    return pl.pallas_call(
        functools.partial(_compress_pool_kernel, n_pg, n_i), grid=(2 * n_i,),
        in_specs=[pl.BlockSpec(memory_space=pl.ANY),
                  pl.BlockSpec((1, HEAD_DIM, 1, PAGE_SIZE), lambda t: (t // n_i, 0, 0, 0)),
                  pl.BlockSpec((1, HEAD_DIM * PAGE_SIZE, N_PAGE_CB * HEAD_DIM), lambda t: (t // n_i, 0, 0)),
                  pl.BlockSpec((1, N_PAGE_CB * HEAD_DIM, N_PAGE_CB * HEAD_DIM), lambda t: (t // n_i, 0, 0))],
        out_specs=pl.BlockSpec((N_KV // 2, n_pg * CB_ROWS, LANES), lambda t: (t // n_i, t % n_i, 0)),
        out_shape=jax.ShapeDtypeStruct((N_KV, n_pool * CB_ROWS, LANES), F32),
        scratch_shapes=[pltpu.VMEM((2 * n_pg * PAGE_PITCH, PAGE_SIZE), F32), pltpu.SemaphoreType.DMA((2,))],
        compiler_params=_cparams("arbitrary"), name="compress_pool")(cache_t, pe_t, mw, w2bd4)


def _compress_pool_weights(cmp_pe, cmp_w1, cmp_w2):
    eye = np.eye(N_PAGE_CB, dtype=bool)
    w1t = cmp_w1.reshape(2, CMP_BLOCK, HEAD_DIM, HEAD_DIM).transpose(0, 2, 1, 3).astype(BF16)
    rows = jnp.broadcast_to(w1t[:, :, None], (2, HEAD_DIM, N_PAGE_CB, CMP_BLOCK, HEAD_DIM))
    rows = jnp.tile(rows.reshape(2, HEAD_DIM * PAGE_SIZE, HEAD_DIM), (1, 1, N_PAGE_CB))
    same_block = ((np.arange(HEAD_DIM * PAGE_SIZE) // CMP_BLOCK) % N_PAGE_CB)[:, None] == (
        np.arange(N_PAGE_CB * HEAD_DIM) // HEAD_DIM)[None, :]
    mw = jnp.where(same_block[None], rows, jnp.zeros((), BF16))
    w2bd4 = jnp.where(eye[None, :, None, :, None], cmp_w2[:, None, :, None, :], 0.0)
    w2bd4 = w2bd4.reshape(2, N_PAGE_CB * HEAD_DIM, N_PAGE_CB * HEAD_DIM).astype(BF16)
    pe_t = jnp.tile(cmp_pe.transpose(0, 2, 1), (1, 1, N_PAGE_CB))[:, :, None, :]
    return pe_t, mw, w2bd4


def _bias_of_distance(dist, tab):
    val = jnp.zeros(dist.shape, F32) + tab(0)
    for k in range(1, N_BUCKETS):
        val = jnp.where(dist >= BUCKET_THR[k], tab(k), val)
    return val


def _toeplitz_kernel(window, tab_ref, o_ref):
    g = pl.program_id(0)
    dt = pl.program_id(1)
    x = lax.broadcasted_iota(jnp.int32, (8, 2 * TK), 1)
    dist = dt * TQ - jnp.where(x < TK, x, x - 2 * TK)
    for hh in range(HPG):
        head = g * HPG + hh
        val = _bias_of_distance(dist, lambda k, head=head: tab_ref[k, head])
        val = jnp.where(dist < 0, NEG_MASK, val)
        if window is not None:
            val = jnp.where(dist >= window, NEG_MASK, val)
        tile = pltpu.roll(jnp.broadcast_to(val[0:1], (TQ, 2 * TK)), 0, 1, stride=1, stride_axis=0)
        o_ref[0, 0, hh * TQ:(hh + 1) * TQ, :] = tile[:, :TK]


def _toeplitz_call(rel_bias, n_chunks, window):
    return pl.pallas_call(
        functools.partial(_toeplitz_kernel, window), grid=(N_KV, n_chunks),
        in_specs=[pl.BlockSpec(memory_space=pltpu.SMEM)],
        out_specs=pl.BlockSpec((1, 1, HPG * TQ, TK), lambda g, t: (g, t, 0, 0)),
        out_shape=jax.ShapeDtypeStruct((N_KV, n_chunks, HPG * TQ, TK), F32),
        compiler_params=_cparams("arbitrary", "arbitrary"), name="bias_toeplitz")(rel_bias)


def _cmp_bias_kernel(tab_ref, o_ref):
    g = pl.program_id(0)
    qi = pl.program_id(1)
    sub = 8
    t_q = qi * TQ + lax.broadcasted_iota(jnp.int32, (sub, TQ), 1)
    for r0 in range(0, LANES, sub):
        rows = slice(r0, r0 + sub)
        cb_lo = 2 * (r0 % HEAD_DIM) + r0 // HEAD_DIM
        cb_hi = cb_lo + 2 * (sub - 1)
        d_min = qi * TQ - (cb_hi * CMP_BLOCK + CMP_BLOCK - 1)
        d_max = qi * TQ + TQ - 1 - (cb_lo * CMP_BLOCK + CMP_BLOCK - 1)

        @pl.when(d_max < 0)
        def _(rows=rows):
            o_ref[0, 0, rows, :] = jnp.full((sub, HPG * TQ), NEG_MASK, F32)

        @pl.when(d_min >= BUCKET_THR[-1])
        def _(rows=rows):
            for hh in range(HPG):
                o_ref[0, 0, rows, hh * TQ:(hh + 1) * TQ] = jnp.full((sub, TQ), tab_ref[N_BUCKETS - 1, g * HPG + hh], F32)

        @pl.when((d_max >= 0) & (d_min < BUCKET_THR[-1]))
        def _(rows=rows, cb_lo=cb_lo):
            cblk = cb_lo + 2 * lax.broadcasted_iota(jnp.int32, (sub, TQ), 0)
            dist = t_q - (cblk * CMP_BLOCK + (CMP_BLOCK - 1))
            for hh in range(HPG):
                head = g * HPG + hh
                val = _bias_of_distance(dist, lambda k, head=head: tab_ref[k, head])
                o_ref[0, 0, rows, hh * TQ:(hh + 1) * TQ] = jnp.where(dist < 0, NEG_MASK, val)


def _cmp_bias_call(rel_bias, n_qt):
    return pl.pallas_call(
        _cmp_bias_kernel, grid=(N_KV, n_qt),
        in_specs=[pl.BlockSpec(memory_space=pltpu.SMEM)],
        out_specs=pl.BlockSpec((1, 1, LANES, HPG * TQ), lambda g, t: (g, t, 0, 0)),
        out_shape=jax.ShapeDtypeStruct((N_KV, n_qt, LANES, HPG * TQ), F32),
        compiler_params=_cparams("arbitrary", "arbitrary"), name="bias_cmp")(rel_bias)


def _bias_cols_kernel(dist_ref, tab_ref, o_ref):
    dist = dist_ref[...]
    val = _bias_of_distance(dist, lambda k: tab_ref[:, k:k + 1])
    o_ref[...] = jnp.where(dist < 0, NEG_MASK, val)


def _bias_cols_call(dist, tab_heads):
    r = dist.shape[1]
    return pl.pallas_call(
        _bias_cols_kernel, grid=(1,),
        in_specs=[pl.BlockSpec((N_HEADS, r), lambda i: (0, 0)), pl.BlockSpec((N_HEADS, N_BUCKETS), lambda i: (0, 0))],
        out_specs=pl.BlockSpec((N_HEADS, r), lambda i: (0, 0)),
        out_shape=jax.ShapeDtypeStruct((N_HEADS, r), F32),
        compiler_params=_cparams("arbitrary"), name="bias_cols")(dist, tab_heads)


def _nsa_seq_kernel(n_sb, q_ref, gate_ref, kaug_ref, v1_ref, kw_ref, vw1_ref, kcb_ref, vcb_ref, tcmp_ref,
                    tsel_ref, twin_ref, gexp_ref, o_ref, qaug, qw_sc, s_sc, mrun, mb, acc_sel, acc_win, score_sc):
    qi = pl.program_id(2)
    rows = HPG * TQ
    lane = lax.broadcasted_iota(jnp.int32, (TQ, LANES), 1)
    low = lane < HEAD_DIM

    qh = []
    for hh in range(HPG):
        qv = q_ref[0, :, (hh // 2) * LANES:(hh // 2 + 1) * LANES]
        if hh % 2 == 1:
            qv = pltpu.roll(qv, HEAD_DIM, 1)
        qh.append(jnp.where(low, qv, 0.0))
    qw = jnp.concatenate(qh, axis=0).astype(BF16)
    qw_sc[...] = qw

    s_c = _dot_nt(kcb_ref[0, 0].astype(BF16), qw) + tcmp_ref[0, 0]
    ok_c = s_c > 0.5 * NEG_MASK
    mx = jnp.max(s_c, axis=0, keepdims=True)
    p = jnp.where(ok_c, jnp.exp(s_c - mx), 0.0)
    den = jnp.sum(p, axis=0, keepdims=True)
    pn_t = p / jnp.where(den > 0, den, 1.0)
    o_c = _dot(pn_t.T.astype(BF16), vcb_ref[0, 0].astype(BF16))
    imp = pn_t[:, 0:TQ]
    for hh in range(1, HPG):
        imp = imp + pn_t[:, hh * TQ:(hh + 1) * TQ]
    n_blk = LANES // 2
    imp = imp[:n_blk] + imp[n_blk:]

    blk = lax.broadcasted_iota(jnp.int32, (n_blk, TQ), 0)
    cur = (qi * TQ + lax.broadcasted_iota(jnp.int32, (n_blk, TQ), 1)) // SEL_BLOCK
    valid = blk <= cur
    forced = (blk == 0) | (blk == cur) | (blk == cur - 1)
    score_sc[...] = jnp.where(valid & forced, BIG_SCORE, jnp.where(valid, imp, -BIG_SCORE))
    sub = 8
    groups = [score_sc[r0:r0 + sub, :] for r0 in range(0, n_blk, sub)]
    rowg = lax.broadcasted_iota(jnp.int32, (sub, TQ), 0)
    cnts = [jnp.zeros((sub, TQ), jnp.int32) for _ in groups]
    for bp in range(n_blk):
        r = score_sc[bp:bp + 1, :]
        for gi, s_g in enumerate(groups):
            if gi * sub > bp:
                ahead = r >= s_g
            elif (gi + 1) * sub - 1 < bp:
                ahead = r > s_g
            else:
                ahead = (r > s_g) | ((r == s_g) & (rowg + gi * sub > bp))
            cnts[gi] = cnts[gi] + jnp.where(ahead, 1, 0)
    cnt = jnp.concatenate(cnts, axis=0)
    selmask_t = jnp.where((cnt < min(TOP_N, n_sb)) & valid, 0.0, NEG_SEL)
    selmask = jnp.concatenate([jnp.zeros((n_blk, TQ), F32), selmask_t], axis=0).T
    for hh in range(HPG):
        qaug[hh * TQ:(hh + 1) * TQ, :] = jnp.where(low, qh[hh], selmask).astype(BF16)

    def branch(q_sc, k_ref, v_ref, t_ref, acc_sc, n_tiles, n_tbl, widths):
        mrun[...] = jnp.full(mrun.shape, NEG_MASK, F32)

        def tile_loop(step):
            done = 0
            for width in widths:
                def group(j, _, done=done, width=width):
                    step(tuple(done + width * j + u for u in range(width)))
                    return 0
                n_groups = (n_tiles - done) // width
                lax.fori_loop(0, n_groups, group, 0)
                done = done + n_groups * width

        def scores(tiles):
            m = mrun[...]
            for i in tiles:
                r0 = pl.multiple_of((qi - i) * TK, TK)
                s = _dot_nt(q_sc[...], k_ref[0, 0, pl.ds(r0, TK), :]) + t_ref[0, jnp.minimum(i, n_tbl)]
                s_sc[i] = s
                for c in range(TK // LANES):
                    m = jnp.maximum(m, s[:, c * LANES:(c + 1) * LANES])
            mrun[...] = m
        tile_loop(scores)
        mb[...] = jnp.broadcast_to(jnp.max(mrun[...], axis=1, keepdims=True), (rows, LANES))
        acc_sc[...] = jnp.zeros(acc_sc.shape, F32)

        def weigh(tiles):
            mbv = mb[...]
            mb2 = jnp.concatenate([mbv] * (TK // LANES), axis=1)
            acc = acc_sc[...]
            for i in tiles:
                r0 = pl.multiple_of((qi - i) * TK, TK)
                acc = acc + _dot(jnp.exp(s_sc[i] - mb2).astype(BF16), v_ref[0, 0, pl.ds(r0, TK), :])
            acc_sc[...] = acc
        tile_loop(weigh)
        return acc_sc[...]

    acc_s = branch(qaug, kaug_ref, v1_ref, tsel_ref, acc_sel, qi + 1, N_TBL, (8, 4, 2, 1))
    acc_w = branch(qw_sc, kw_ref, vw1_ref, twin_ref, acc_win, jnp.minimum(qi, N_WIN_TILES - 1) + 1, N_WIN_TILES - 1,
                   (N_WIN_TILES, 2, 1))

    gates = gate_ref[0]
    g_hi = gates.astype(BF16)
    g_lo = (gates - g_hi.astype(F32)).astype(BF16)
    g_all = _dot(jnp.concatenate([g_hi, g_lo], axis=1), gexp_ref[...])

    def normalized(acc, odd):
        swapped = pltpu.roll(acc, HEAD_DIM, 1)
        return swapped / acc if odd else acc / swapped

    outs = []
    for hh in range(HPG):
        gb = [g_all[:, (hh * 3 + br) * LANES:(hh * 3 + br + 1) * LANES] for br in range(3)]
        rs = slice(hh * TQ, (hh + 1) * TQ)
        outs.append(gb[0] * o_c[rs] + gb[1] * normalized(acc_s[rs], hh % 2) + gb[2] * normalized(acc_w[rs], hh % 2))
    for c in range(HPG // 2):
        o_ref[0, :, c * LANES:(c + 1) * LANES] = jnp.where(low, outs[2 * c], outs[2 * c + 1]).astype(o_ref.dtype)


def _nsa_seq_call(q, gates, kaug, v1, kw, vw1, kcb, vcb, tcmp, tsel, twin):
    b, s, _ = q.shape
    n_qt = s // TQ
    rows = HPG * TQ
    kv_spec = pl.BlockSpec((1, 1, s, LANES), lambda bb, g, i: (bb, g, 0, 0))
    cb_spec = pl.BlockSpec((1, 1, kcb.shape[2], LANES), lambda bb, g, i: (bb, g, 0, 0))
    tbl_spec = lambda t: pl.BlockSpec((1,) + t.shape[1:], lambda bb, g, i: (g, 0, 0, 0))
    n_gate = HPG * 3
    gexp = np.zeros((2 * LANES, n_gate * LANES), np.float32)
    for c in range(n_gate):
        gexp[c, c * LANES:(c + 1) * LANES] = 1.0
        gexp[LANES + c, c * LANES:(c + 1) * LANES] = 1.0
    gexp = jnp.asarray(gexp, BF16)
    return pl.pallas_call(
        functools.partial(_nsa_seq_kernel, s // SEL_BLOCK), grid=(b, N_KV, n_qt),
        in_specs=[pl.BlockSpec((1, TQ, HPG * HEAD_DIM), lambda bb, g, i: (bb, i, g)),
                  pl.BlockSpec((1, TQ, LANES), lambda bb, g, i: (bb, i, g)),
                  kv_spec, kv_spec, kv_spec, kv_spec, cb_spec, cb_spec,
                  pl.BlockSpec((1, 1, LANES, rows), lambda bb, g, i: (g, i, 0, 0)), tbl_spec(tsel), tbl_spec(twin),
                  pl.BlockSpec(gexp.shape, lambda bb, g, i: (0, 0))],
        out_specs=pl.BlockSpec((1, TQ, HPG * HEAD_DIM), lambda bb, g, i: (bb, i, g)),
        out_shape=jax.ShapeDtypeStruct((b, s, N_HEADS * HEAD_DIM), BF16),
        scratch_shapes=[pltpu.VMEM((rows, LANES), BF16), pltpu.VMEM((rows, LANES), BF16),
                        pltpu.VMEM((s // TK, rows, TK), F32), pltpu.VMEM((rows, LANES), F32),
                        pltpu.VMEM((rows, LANES), F32), pltpu.VMEM((rows, LANES), F32),
                        pltpu.VMEM((rows, LANES), F32), pltpu.VMEM((LANES // 2, TQ), F32)],
        compiler_params=_cparams("arbitrary", "arbitrary", "arbitrary"), name="nsa_seq")(
            q, gates, kaug, v1, kw, vw1, kcb, vcb, tcmp, tsel, twin, gexp)


def _softmax_lanes(s):
    p = jnp.exp(s - jnp.max(s, axis=1, keepdims=True))
    return p / jnp.sum(p, axis=1, keepdims=True)


def _rows_to_heads(rows):
    hg = lax.broadcasted_iota(jnp.int32, (N_HEADS, rows[0].shape[1]), 0) // HPG
    out = jnp.broadcast_to(rows[0], hg.shape)
    for g in range(1, N_KV):
        out = jnp.where(hg == g, rows[g], out)
    return out


def _nsa_stepT_kernel(n_pages, n_seq, pt_ref, *refs):
    del pt_ref
    n_pg = n_seq * n_pages
    q_ref = refs[0]
    cb_refs = refs[1:1 + n_pg]
    pg_refs = refs[1 + n_pg:1 + 2 * n_pg]
    (kvnew_ref, winnew_ref, wcol_ref, cwin_ref, bsel_ref, bnew_ref, bwin_ref, bcmp_ref) = refs[1 + 2 * n_pg:9 + 2 * n_pg]
    oc_ref, os_ref, ow_ref, nwin_ref = refs[9 + 2 * n_pg:13 + 2 * n_pg]
    (s_sc,) = refs[13 + 2 * n_pg:]
    for u in range(n_seq):
        one = lambda ref, u=u: ref.at[pl.ds(u, 1)]
        _nsa_step_one(n_pages, pl.program_id(0) * n_seq + u, one(q_ref), cb_refs[u * n_pages:(u + 1) * n_pages],
                      pg_refs[u * n_pages:(u + 1) * n_pages], one(kvnew_ref), one(winnew_ref), wcol_ref, one(cwin_ref),
                      bsel_ref, bnew_ref, bwin_ref, bcmp_ref, one(oc_ref), one(os_ref), one(ow_ref), one(nwin_ref),
                      s_sc.at[u])


def _nsa_step_one(n_pages, b, q_ref, cb_refs, pg_refs, kvnew_ref, winnew_ref, wcol_ref, cwin_ref, bsel_ref, bnew_ref,
                  bwin_ref, bcmp_ref, oc_ref, os_ref, ow_ref, nwin_ref, s_sc):
    gw = N_KV * HEAD_DIM
    n_past = n_pages * PAGE_SIZE
    q16 = q_ref[0].astype(BF16)
    qf = q16.astype(F32)
    lane = lax.broadcasted_iota(jnp.int32, (N_HEADS, LANES), 1)

    n_pad = LANES - CB_ROWS * n_pages
    cb = jnp.concatenate([jnp.concatenate([r[c] for c in range(N_KV)], axis=1) for r in cb_refs]
                         + ([jnp.zeros((n_pad, 2 * gw), F32)] if n_pad else []), axis=0)
    pn_c = _softmax_lanes(_dot_nt(q16, cb[:, :gw].astype(BF16)) + bcmp_ref[...])
    oc_ref[0] = _dot(pn_c.astype(BF16), cb[:, gw:].astype(BF16))

    grp = [pn_c[HPG * g:HPG * g + 1] + pn_c[HPG * g + 1:HPG * g + 2] + pn_c[HPG * g + 2:HPG * g + 3]
           + pn_c[HPG * g + 3:HPG * g + 4] for g in range(N_KV)]
    lane8 = lax.broadcasted_iota(jnp.int32, (8, LANES), 1)
    row8 = lax.broadcasted_iota(jnp.int32, (8, LANES), 0)
    imp = jnp.zeros((8, LANES), F32)
    for g in range(N_KV):
        imp = jnp.where(row8 == g, grp[g], imp)
    imp = imp + pltpu.roll(imp, LANES - 1, 1)
    n_past_blk = n_past // SEL_BLOCK
    cur_lane = LANES - 2
    is_blk = ((lane8 % CB_ROWS == 0) | (lane8 % CB_ROWS == 2)) & (lane8 < CB_ROWS * n_pages)
    last_lane = ((n_past_blk - 1) // 2) * CB_ROWS + 2 * ((n_past_blk - 1) % 2)
    forced = (lane8 == 0) | (lane8 == last_lane) | (lane8 == cur_lane)
    valid = is_blk | (lane8 == cur_lane)
    score = jnp.where(valid & forced, BIG_SCORE, jnp.where(valid, imp, -BIG_SCORE))
    cnt = jnp.zeros((8, LANES), jnp.int32)
    for k in range(2, LANES, 2):
        r = pltpu.roll(score, k, 1)
        cnt = cnt + jnp.where((r > score) | ((r == score) & (lane8 >= k)), 1, 0)
    selrows = jnp.where((cnt < min(TOP_N, n_past_blk + 1)) & valid, 0.0, NEG_SEL)
    selmask = _rows_to_heads([selrows[g:g + 1] for g in range(N_KV)])

    blocks_per_page = PAGE_SIZE // SEL_BLOCK
    for p in range(n_pages):
        kt = pg_refs[p][0, 0:gw, :].astype(BF16)
        msk = selmask[:, CB_ROWS * p:CB_ROWS * p + 1]
        for i in range(1, blocks_per_page):
            msk = jnp.where(lane >= i * SEL_BLOCK, selmask[:, CB_ROWS * p + 2 * i:CB_ROWS * p + 2 * i + 1], msk)
        s_sc[:, p * PAGE_SIZE:(p + 1) * PAGE_SIZE] = _dot(q16, kt) + bsel_ref[:, p * PAGE_SIZE:(p + 1) * PAGE_SIZE] + msk
    knew = kvnew_ref[0][:, 2 * gw:3 * gw].astype(BF16).astype(F32)
    s_new = jnp.sum(qf * knew, axis=1, keepdims=True) + bnew_ref[:, 0:1] + selmask[:, cur_lane:cur_lane + 1]
    s_sc[:, n_past:n_past + LANES] = jnp.where(lane == 0, s_new, NEG_MASK)
    s_all = s_sc[...]
    mx = jnp.max(s_all, axis=1, keepdims=True)
    den = jnp.sum(jnp.exp(s_all - mx), axis=1, keepdims=True)
    acc = jnp.zeros((N_HEADS, gw), F32)
    for p in range(n_pages):
        pn = jnp.exp(s_sc[:, p * PAGE_SIZE:(p + 1) * PAGE_SIZE] - mx) / den
        acc = acc + _dot_nt(pn.astype(BF16), pg_refs[p][0, gw:2 * gw, :].astype(BF16))
    pn_new = jnp.exp(s_new - mx) / den
    vnew = kvnew_ref[0][:, 3 * gw:4 * gw].astype(BF16).astype(F32)
    os_ref[0] = acc + pn_new.astype(BF16).astype(F32) * vnew

    cw = cwin_ref[0]
    w_len = cw.shape[1]
    s_w = _dot(q16, cw[0:gw].astype(BF16)) + bwin_ref[...]
    kwn = winnew_ref[0][:, 0:gw].astype(BF16).astype(F32)
    s_wn = jnp.sum(qf * kwn, axis=1, keepdims=True) + bnew_ref[:, 0:1]
    mxw = jnp.maximum(jnp.max(s_w, axis=1, keepdims=True), s_wn)
    pw = jnp.exp(s_w - mxw)
    pwn = jnp.exp(s_wn - mxw)
    denw = jnp.sum(pw, axis=1, keepdims=True) + pwn
    vwn = winnew_ref[0][:, gw:2 * gw].astype(BF16).astype(F32)
    ow_ref[0] = _dot_nt((pw / denw).astype(BF16), cw[gw:2 * gw].astype(BF16)) + (pwn / denw).astype(BF16).astype(F32) * vwn
    lane_w = lax.broadcasted_iota(jnp.int32, cw.shape, 1)
    nb_l = wcol_ref.shape[1]
    lane_b = lax.broadcasted_iota(jnp.int32, (cw.shape[0], nb_l), 1)
    col = jnp.sum(jnp.where(lane_b == b, wcol_ref[...], 0.0), axis=1, keepdims=True)
    nwin_ref[0] = jnp.where(lane_w == w_len - 1, col, pltpu.roll(cw, w_len - 1, 1))


def _nsa_step_call(page_table, qrows, cb_pool, cache_t, kvnew, winnew, wcol, cwin_t, bsel, bnew, bwin, bcmp):
    nb, n_pages = page_table.shape
    gw = N_KV * HEAD_DIM
    w_len = cwin_t.shape[2]
    n_seq = 2 if nb % 2 == 0 else 1
    slots = [(u, p) for u in range(n_seq) for p in range(n_pages)]
    cb_specs = [pl.BlockSpec((N_KV, None, CB_ROWS, LANES), lambda b, pt, _u=u, _p=p: (0, pt[b * n_seq + _u, _p], 0, 0))
                for u, p in slots]
    pg_specs = [pl.BlockSpec((1, 2 * gw, PAGE_SIZE), lambda b, pt, _u=u, _p=p: (pt[b * n_seq + _u, _p], 1, 0))
                for u, p in slots]
    const = lambda a: pl.BlockSpec(a.shape, lambda b, pt: (0, 0))
    seq = lambda *dims: pl.BlockSpec((n_seq,) + dims, lambda b, pt: (b, 0, 0))
    grid_spec = pltpu.PrefetchScalarGridSpec(
        num_scalar_prefetch=1, grid=(nb // n_seq,),
        in_specs=[seq(N_HEADS, gw)] + cb_specs + pg_specs + [
            seq(1, 4 * gw), seq(1, 2 * gw), const(wcol), seq(2 * gw, w_len),
            const(bsel), const(bnew), const(bwin), const(bcmp)],
        out_specs=[seq(N_HEADS, gw)] * 3 + [seq(2 * gw, w_len)],
        scratch_shapes=[pltpu.VMEM((n_seq, N_HEADS, n_pages * PAGE_SIZE + LANES), F32)])
    o_shape = jax.ShapeDtypeStruct((nb, N_HEADS, gw), F32)
    return pl.pallas_call(
        functools.partial(_nsa_stepT_kernel, n_pages, n_seq), grid_spec=grid_spec,
        out_shape=[o_shape, o_shape, o_shape, jax.ShapeDtypeStruct((nb, 2 * gw, w_len), F32)],
        compiler_params=_cparams("arbitrary"), name="nsa_step")(
            page_table, qrows, *([cb_pool] * len(slots)), *([cache_t] * len(slots)), kvnew, winnew, wcol, cwin_t,
            bsel, bnew, bwin, bcmp)


def _head_diag(o):
    b = o.shape[0]
    o5 = o.reshape(b, N_KV, HPG, N_KV, HEAD_DIM)
    return jnp.stack([o5[:, g, :, g, :] for g in range(N_KV)], axis=1).reshape(b, N_HEADS * HEAD_DIM)


def kernel(x_prompt, x_sample, c_prompt, c_sample, cache_kv, cache_win, state_conv_a, state_ffn_conv, page_table, mod_w, mod_b, norm_g, a_w_in, a_conv_w, a_conv_b, a_w_out, kv_mod_w, kv_mod_b, kv_norm_g, w_kv, cmp_pe, cmp_w1, cmp_w2, b_w_qg, b_w_out, rel_bias, ffn_w_up, ffn_conv_w, ffn_conv_b, ffn_w_down):
    bp, s, d = x_prompt.shape
    bs = x_sample.shape[0]
    depth = mod_w.shape[0]
    n_a = a_w_in.shape[0]
    assert depth == 2 and n_a == 1 and x_sample.shape[1] == 1
    dff = ffn_w_down.shape[1]
    n_pool = cache_kv.shape[0]
    n_pages = page_table.shape[1]
    past_len = n_pages * PAGE_SIZE
    gw = N_KV * HEAD_DIM
    nq = N_HEADS * HEAD_DIM

    n_c = bp + bs
    n_cp = -(-n_c // 8) * 8
    c_all = jnp.pad(jnp.concatenate([c_prompt, c_sample], 0), ((0, n_cp - n_c), (0, 0)))
    mods = _mod_call(c_all, mod_w.reshape(depth * 2, d, 3 * d), mod_b.reshape(depth * 2, 1, 3 * d))
    modkv = _mod_call(c_all, kv_mod_w[None], kv_mod_b[None, None])[0]
    mod_p = lambda i: mods[i, :bp][:, None, :]
    mod_s = lambda i: mods[i, bp:n_c][None]

    w_in = a_w_in[0].astype(BF16)
    w_out_a = a_w_out[0].astype(BF16)
    w_up = ffn_w_up.astype(BF16)
    w_dn = ffn_w_down.astype(BF16)
    w_kv_b = w_kv.astype(BF16)
    w_gate = jnp.pad(b_w_qg[0][:, nq:].reshape(d, N_KV, HPG * 3), ((0, 0), (0, 0), (0, LANES - HPG * 3)))
    w_qg = jnp.concatenate([b_w_qg[0][:, :nq], w_gate.reshape(d, N_KV * LANES)], axis=1).astype(BF16)
    w_out_b = b_w_out[0].astype(BF16)
    g = lambda l, i: norm_g[l, i][None]
    pe2, w1bd, w2bd = _compress_weights(cmp_pe, cmp_w1, cmp_w2)
    n_chunks = 2

    tm = min(512, s)
    tm_ffn = min(512, s)
    x1, st_a = _mixer_call(x_prompt, mod_p(0), g(0, 0), g(0, 1), w_in, a_conv_w[0], a_conv_b[0][None], w_out_a, None, tm)
    x2, st_f0 = _ffn_call(x1, mod_p(1), g(0, 2), g(0, 3), w_up[0], ffn_conv_w[0], ffn_conv_b[0][None], w_dn[0], None, tm_ffn, n_chunks)
    rows, win, q, gates, kaug, v1, kw, vw1 = _kvq_call(
        x2, modkv[:bp][:, None, :], mod_p(2), kv_norm_g[None], g(1, 0), w_kv_b, w_qg, tm, True)
    cb = _compress_call(rows.reshape(bp * s, 4 * gw), pe2, w1bd, w2bd)
    n_cbk = s // CMP_BLOCK
    assert n_cbk <= LANES
    cb = jnp.pad(cb.reshape(bp, n_cbk, 2, N_KV, HEAD_DIM), ((0, 0), (0, LANES - n_cbk), (0, 0), (0, 0), (0, 0)))
    cb = cb.reshape(bp, LANES // 2, 2, 2, N_KV, HEAD_DIM)
    cb = cb.transpose(3, 0, 4, 2, 1, 5).reshape(2, bp, N_KV, LANES, HEAD_DIM)
    kcb = jnp.pad(cb[0], ((0, 0),) * 3 + ((0, LANES - HEAD_DIM),))
    vcb = jnp.concatenate([cb[1], cb[1]], axis=-1)
    tsel = _toeplitz_call(rel_bias, N_TBL + 1, None)
    twin = _toeplitz_call(rel_bias, N_WIN_TILES, WINDOW)
    tcmp = _cmp_bias_call(rel_bias, s // TQ)
    o_att = _nsa_seq_call(q, gates, kaug, v1, kw, vw1, kcb, vcb, tcmp, tsel, twin)
    y_prompt, st_f1 = _ffn_call(x2, mod_p(3), g(1, 2), g(1, 3), w_up[1], ffn_conv_w[1], ffn_conv_b[1][None], w_dn[1], None,
                                tm_ffn, n_chunks, attn=(o_att, mod_p(2), g(1, 1), w_out_b))
    kv_p = rows.reshape(bp, s, 4, N_KV, HEAD_DIM)
    keep = min(WINDOW, s)
    win_p = win[:, s - keep:].reshape(bp, keep, 2, N_KV, HEAD_DIM)
    conv_a_p = st_a[None, :, 6:8]
    ffn_p = jnp.stack([st_f0[:, 6:8], st_f1[:, 6:8]])

    xs = x_sample.reshape(1, bs, d)
    prev_a = (state_conv_a[0, :, 0][None], state_conv_a[0, :, 1][None])
    xs1, v_a = _mixer_call(xs, mod_s(0), g(0, 0), g(0, 1), w_in, a_conv_w[0], a_conv_b[0][None], w_out_a, prev_a, bs)
    prev_f = lambda l: (state_ffn_conv[l, :, 0][None], state_ffn_conv[l, :, 1][None])
    xs2, up0 = _ffn_call(xs1, mod_s(1), g(0, 2), g(0, 3), w_up[0], ffn_conv_w[0], ffn_conv_b[0][None], w_dn[0], prev_f(0), bs, n_chunks)
    rows_s, win_s, q_s, gates_s = _kvq_call(xs2, modkv[bp:n_c][None], mod_s(2), kv_norm_g[None], g(1, 0), w_kv_b, w_qg, bs, False)
    cache_t = cache_kv.transpose(0, 2, 3, 4, 1).reshape(n_pool, 4 * gw, PAGE_SIZE)
    w_len = cache_win.shape[1]
    cwin_t = cache_win.transpose(0, 2, 3, 4, 1).reshape(bs, 2 * gw, w_len)
    cb_pool = _compress_pool_call(cache_t.reshape(n_pool, 4, gw, PAGE_SIZE),
                                  *_compress_pool_weights(cmp_pe, cmp_w1, cmp_w2))
    cb_pool = cb_pool.reshape(N_KV, n_pool, CB_ROWS, LANES)
    assert n_pages * CB_ROWS <= LANES
    d_sel = past_len - np.arange(past_len)
    d_new = np.where(np.arange(LANES) == 0, 0, -1)
    d_win = w_len - np.arange(w_len)
    d_win = np.where(d_win < WINDOW, d_win, -1)
    cl = np.arange(LANES)
    d_cmp = past_len - (((cl // CB_ROWS) * N_PAGE_CB + cl % CB_ROWS) * CMP_BLOCK + CMP_BLOCK - 1)
    d_cmp = np.where((cl % CB_ROWS < N_PAGE_CB) & (cl // CB_ROWS < n_pages), d_cmp, -1)
    assert d_sel.min() >= 0 and d_cmp[d_cmp != -1].min() >= 0
    dist = np.concatenate([d_sel, d_new, d_win, d_cmp]).astype(np.int32)
    bias_cols = _bias_cols_call(jnp.asarray(np.repeat(dist[None, :], N_HEADS, 0)), rel_bias.T)
    bsel, bnew = bias_cols[:, :past_len], bias_cols[:, past_len:past_len + LANES]
    bwin = bias_cols[:, past_len + LANES:past_len + LANES + w_len]
    bcmp = bias_cols[:, past_len + LANES + w_len:]
    head_group = (np.arange(N_HEADS)[:, None] // HPG == np.arange(N_KV)[None, :])[None, :, :, None]
    qrows = jnp.where(head_group, q_s.reshape(bs, N_HEADS, 1, HEAD_DIM), 0.0).reshape(bs, N_HEADS, gw)
    oc, os_, ow, nwin_t = _nsa_step_call(
        page_table, qrows, cb_pool, cache_t, rows_s.reshape(bs, 1, 4 * gw),
        win_s.reshape(bs, 1, 2 * gw), win_s[0].T, cwin_t, bsel, bnew, bwin, bcmp)
    nwin = nwin_t.reshape(bs, 2, N_KV, HEAD_DIM, w_len).transpose(0, 4, 1, 2, 3)
    gts = gates_s[0].reshape(bs, N_KV, LANES)[:, :, :HPG * 3].reshape(bs, N_HEADS, 3)
    branches = []
    for br, o in enumerate((oc, os_, ow)):
        branches += [jnp.repeat(gts[:, :, br], HEAD_DIM, axis=1)[None], _head_diag(o)[None]]
    xs3 = _outproj_call(xs2, mod_s(2), g(1, 1), w_out_b, branches, bs)
    ys, up1 = _ffn_call(xs3, mod_s(3), g(1, 2), g(1, 3), w_up[1], ffn_conv_w[1], ffn_conv_b[1][None], w_dn[1], prev_f(1), bs, n_chunks)
    y_sample = ys.reshape(bs, 1, d)
    kv_s = rows_s.reshape(bs, 1, 4, N_KV, HEAD_DIM)
    win_state_s = nwin
    conv_a_s = jnp.stack([state_conv_a[0, :, 1], v_a[0]], axis=1)[None]
    ffn_s = jnp.stack([jnp.stack([state_ffn_conv[l, :, 1], u[0]], axis=1) for l, u in ((0, up0), (1, up1))])
    return (y_prompt, y_sample, kv_p, kv_s, win_p, win_state_s, conv_a_p, conv_a_s, ffn_p, ffn_s)
```

```python
import functools
import math

import numpy as np
import jax
import jax.numpy as jnp
from jax import lax
from jax.experimental import pallas as pl
from jax.experimental.pallas import tpu as pltpu

F32 = jnp.float32
BF16 = jnp.bfloat16

N_HEADS = 16
HEAD_DIM = 64
N_KV = 4
HPG = N_HEADS // N_KV
CMP_BLOCK = 32
SEL_BLOCK = 64
TOP_N = 16
WINDOW = 512
N_BUCKETS = 32
MAX_EXACT = N_BUCKETS // 2
MAX_DISTANCE = 1024
PAGE_SIZE = 128
EPS = 1e-6

LANES = 128
TQ = 256
TK = 256
NEG_MASK = -1e30
NEG_SEL = -1e9
BIG_SCORE = 1e30
VMEM_LIMIT_BYTES = 56 * 1024 * 1024


def _bucket_thresholds():
    d = np.arange(0, 4 * MAX_DISTANCE)
    nf = np.maximum(d, 1).astype(np.float32)
    large = MAX_EXACT + (np.log(nf / MAX_EXACT) / math.log(MAX_DISTANCE / MAX_EXACT)
                         * (N_BUCKETS - MAX_EXACT)).astype(np.int32)
    bucket = np.where(d < MAX_EXACT, d, np.minimum(large, N_BUCKETS - 1))
    assert np.all(np.diff(bucket) >= 0)
    return [int(np.argmax(bucket >= k)) for k in range(N_BUCKETS)]


BUCKET_THR = _bucket_thresholds()
assert TQ == TK
N_TBL = -(-(BUCKET_THR[-1] + TK - 1) // TQ)
N_WIN_TILES = WINDOW // TK + 1


def _cparams(*sem):
    return pltpu.CompilerParams(dimension_semantics=sem, vmem_limit_bytes=VMEM_LIMIT_BYTES)


def _dot(a, b):
    return jnp.dot(a, b, preferred_element_type=F32)


def _dot_nt(a, b):
    return lax.dot_general(a, b, (((1,), (1,)), ((), ())), preferred_element_type=F32)


def _rms(x, g):
    return (x * lax.rsqrt(jnp.mean(x * x, axis=-1, keepdims=True) + EPS)) * g


def _sigmoid(x):
    return 1.0 / (1.0 + jnp.exp(-x))


def _shift_rows(v, carry):
    r1 = pltpu.roll(v, 1, 0)
    r2 = pltpu.roll(v, 2, 0)
    if v.shape[0] > 8:
        row = lax.broadcasted_iota(jnp.int32, (8, v.shape[1]), 0)
        h1 = jnp.where(row == 0, carry[7:8], r1[:8])
        h2 = jnp.where(row == 0, carry[6:7], jnp.where(row == 1, carry[7:8], r2[:8]))
        return jnp.concatenate([h1, r1[8:]], axis=0), jnp.concatenate([h2, r2[8:]], axis=0)
    row = lax.broadcasted_iota(jnp.int32, v.shape, 0)
    return (jnp.where(row == 0, carry[7:8], r1),
            jnp.where(row == 0, carry[6:7], jnp.where(row == 1, carry[7:8], r2)))


def _conv3(v, r1, r2, cw, cb):
    return (cw[0:1] * r2 + cw[1:2] * r1) + cw[2:3] * v + cb


def _mod_kernel(c_ref, w_ref, b_ref, o_ref):
    o_ref[0] = _dot(c_ref[...].astype(BF16), w_ref[0].astype(BF16)) + b_ref[0]


def _mod_call(c_all, w, b):
    n, d, nn = w.shape
    r = c_all.shape[0]
    tn = 512
    return pl.pallas_call(
        _mod_kernel, grid=(n, nn // tn),
        in_specs=[pl.BlockSpec((r, d), lambda i, j: (0, 0)),
                  pl.BlockSpec((1, d, tn), lambda i, j: (i, 0, j)),
                  pl.BlockSpec((1, 1, tn), lambda i, j: (i, 0, j))],
        out_specs=pl.BlockSpec((1, r, tn), lambda i, j: (i, 0, j)),
        out_shape=jax.ShapeDtypeStruct((n, r, nn), F32),
        compiler_params=_cparams("arbitrary", "arbitrary"), name="mod")(c_all, w, b)


def _mixer_kernel(seq_mode, d, *refs):
    if seq_mode:
        x_ref, mod_ref, g0_ref, g1_ref, win_ref, cw_ref, cb_ref, wout_ref, xo_ref, st_ref, carry = refs
    else:
        x_ref, mod_ref, g0_ref, g1_ref, win_ref, cw_ref, cb_ref, wout_ref, p0_ref, p1_ref, xo_ref, st_ref = refs
    x = x_ref[0]
    m = mod_ref[0]
    h = _rms(x, g0_ref[...]) * (1.0 + m[:, d:2 * d]) + m[:, :d]
    z = _dot(h.astype(BF16), win_ref[...])
    bg, cg, u = z[:, :d], z[:, d:2 * d], z[:, 2 * d:]
    v = cg * u
    if seq_mode:
        @pl.when(pl.program_id(1) == 0)
        def _():
            carry[...] = jnp.zeros_like(carry)
        r1, r2 = _shift_rows(v, carry[...])
        carry[...] = v[-8:]
        st_ref[0] = v[-8:]
    else:
        r1, r2 = p1_ref[0], p0_ref[0]
        st_ref[0] = v
    y = _conv3(v, r1, r2, cw_ref[...], cb_ref[...])
    o = _dot((bg * y).astype(BF16), wout_ref[...])
    xo_ref[0] = x + m[:, 2 * d:] * _rms(o, g1_ref[...])


def _row_specs(bx, s, tm, mod):
    sm = mod.shape[1]
    if sm == 1:
        mod_spec = pl.BlockSpec((1, 1, mod.shape[2]), lambda b, i: (b, 0, 0))
    else:
        mod_spec = pl.BlockSpec((1, tm, mod.shape[2]), lambda b, i: (b, i, 0))
    return mod_spec


def _full(shape):
    nd = len(shape)
    return pl.BlockSpec(shape, lambda b, i, _nd=nd: (0,) * _nd, pipeline_mode=pl.Buffered(1))


def _mixer_call(x, mod, g0, g1, w_in, cw, cb, w_out, prev, tm):
    bx, s, d = x.shape
    seq_mode = prev is None
    row = pl.BlockSpec((1, tm, d), lambda b, i: (b, i, 0))
    in_specs = [row, _row_specs(bx, s, tm, mod), _full((1, d)), _full((1, d)), _full(w_in.shape),
                _full(cw.shape), _full((1, d)), _full(w_out.shape)]
    args = [x, mod, g0, g1, w_in, cw, cb, w_out]
    if seq_mode:
        st_shape, st_spec = (bx, 8, d), pl.BlockSpec((1, 8, d), lambda b, i: (b, 0, 0))
        scratch = [pltpu.VMEM((8, d), F32)]
    else:
        in_specs += [row, row]
        args += [prev[0], prev[1]]
        st_shape, st_spec = (bx, s, d), row
        scratch = []
    return pl.pallas_call(
        functools.partial(_mixer_kernel, seq_mode, d), grid=(bx, s // tm),
        in_specs=in_specs, out_specs=[row, st_spec],
        out_shape=[jax.ShapeDtypeStruct((bx, s, d), F32), jax.ShapeDtypeStruct(st_shape, F32)],
        scratch_shapes=scratch, compiler_params=_cparams("arbitrary", "arbitrary"), name="mixer_a")(*args)


def _ffn_kernel(seq_mode, with_attn, d, dff, n_chunks, *refs):
    if with_attn:
        (o_ref, moda_ref, g1_ref, wout_ref), refs = refs[:4], refs[4:]
    if seq_mode:
        x_ref, mod_ref, g2_ref, g3_ref, wup_ref, cw_ref, cb_ref, wdn_ref, xo_ref, st_ref, carry = refs
    else:
        x_ref, mod_ref, g2_ref, g3_ref, wup_ref, cw_ref, cb_ref, wdn_ref, p0_ref, p1_ref, xo_ref, st_ref = refs
    x = x_ref[0]
    if with_attn:
        x = x + moda_ref[0][:, 2 * d:] * _rms(_dot(o_ref[0].astype(BF16), wout_ref[...]), g1_ref[...])
    m = mod_ref[0]
    h = (_rms(x, g2_ref[...]) * (1.0 + m[:, d:2 * d]) + m[:, :d]).astype(BF16)
    if seq_mode:
        @pl.when(pl.program_id(1) == 0)
        def _():
            carry[...] = jnp.zeros_like(carry)
    cwid = dff // n_chunks
    acc = jnp.zeros((x.shape[0], d), F32)
    for k in range(n_chunks):
        halves = []
        for c0 in (k * cwid, dff + k * cwid):
            up = _dot(h, wup_ref[:, c0:c0 + cwid])
            if seq_mode:
                r1, r2 = _shift_rows(up, carry[:, c0:c0 + cwid])
                carry[:, c0:c0 + cwid] = up[-8:]
                st_ref[0, :, c0:c0 + cwid] = up[-8:]
            else:
                r1, r2 = p1_ref[0, :, c0:c0 + cwid], p0_ref[0, :, c0:c0 + cwid]
                st_ref[0, :, c0:c0 + cwid] = up
            halves.append(_conv3(up, r1, r2, cw_ref[:, c0:c0 + cwid], cb_ref[:, c0:c0 + cwid]))
        u, g = halves
        act = (g * _sigmoid(g)) * u
        acc = acc + _dot(act.astype(BF16), wdn_ref[k * cwid:(k + 1) * cwid, :])
    xo_ref[0] = x + m[:, 2 * d:] * _rms(acc, g3_ref[...])


def _ffn_call(x, mod, g2, g3, w_up, cw, cb, w_dn, prev, tm, n_chunks, attn=None):
    bx, s, d = x.shape
    dff = w_dn.shape[0]
    seq_mode = prev is None
    row = pl.BlockSpec((1, tm, d), lambda b, i: (b, i, 0))
    in_specs = [row, _row_specs(bx, s, tm, mod), _full((1, d)), _full((1, d)), _full(w_up.shape),
                _full(cw.shape), _full((1, 2 * dff)), _full(w_dn.shape)]
    args = [x, mod, g2, g3, w_up, cw, cb, w_dn]
    if attn is not None:
        in_specs = [row, _row_specs(bx, s, tm, attn[1]), _full((1, d)), _full(attn[3].shape)] + in_specs
        args = list(attn) + args
    if seq_mode:
        st_shape, st_spec = (bx, 8, 2 * dff), pl.BlockSpec((1, 8, 2 * dff), lambda b, i: (b, 0, 0))
        scratch = [pltpu.VMEM((8, 2 * dff), F32)]
    else:
        prow = pl.BlockSpec((1, tm, 2 * dff), lambda b, i: (b, i, 0))
        in_specs += [prow, prow]
        args += [prev[0], prev[1]]
        st_shape, st_spec = (bx, s, 2 * dff), prow
        scratch = []
    return pl.pallas_call(
        functools.partial(_ffn_kernel, seq_mode, attn is not None, d, dff, n_chunks), grid=(bx, s // tm),
        in_specs=in_specs, out_specs=[row, st_spec],
        out_shape=[jax.ShapeDtypeStruct((bx, s, d), F32), jax.ShapeDtypeStruct(st_shape, F32)],
        scratch_shapes=scratch, compiler_params=_cparams("arbitrary", "arbitrary"), name="conv_ffn")(*args)


def _kvq_kernel(d, n_rows_cols, emit_attn_kv, x_ref, modkv_ref, mod_ref, gkv_ref, g0_ref, wkv_ref, wqg_ref,
                rows_ref, win_ref, q_ref, gate_ref, *attn_refs):
    x = x_ref[0]
    r = x * lax.rsqrt(jnp.mean(x * x, axis=-1, keepdims=True) + EPS)
    mk = modkv_ref[0]
    hk = (r * gkv_ref[...]) * (1.0 + mk[:, d:]) + mk[:, :d]
    kv = _dot(hk.astype(BF16), wkv_ref[...])
    rows_ref[0] = kv[:, :n_rows_cols]
    win_ref[0] = kv[:, n_rows_cols:]
    if emit_attn_kv:
        kaug_ref, v1_ref, kw_ref, vw1_ref = attn_refs
        tm = x.shape[0]
        gw = N_KV * HEAD_DIM
        lane = lax.broadcasted_iota(jnp.int32, (tm, LANES), 1)
        pos = pl.program_id(1) * tm + lax.broadcasted_iota(jnp.int32, (tm, LANES), 0)
        low = lane < HEAD_DIM
        ones = jnp.where(low, 0.0, 1.0)
        onehot = jnp.where(lane - HEAD_DIM == pos // SEL_BLOCK, 1.0, 0.0)
        for ref, plane, fill in ((kaug_ref, 2, onehot), (v1_ref, 3, ones), (kw_ref, 4, 0.0), (vw1_ref, 5, ones)):
            for g in range(N_KV):
                c0 = plane * gw + (g // 2) * LANES
                blk = kv[:, c0:c0 + LANES]
                if g % 2 == 1:
                    blk = pltpu.roll(blk, HEAD_DIM, 1)
                ref[0, g] = jnp.where(low, blk, fill).astype(BF16)
    m = mod_ref[0]
    h1 = (r * g0_ref[...]) * (1.0 + m[:, d:2 * d]) + m[:, :d]
    qg = _dot(h1.astype(BF16), wqg_ref[...])
    nq = N_HEADS * HEAD_DIM
    q_ref[0] = qg[:, :nq] * (HEAD_DIM ** -0.5)
    gate_ref[0] = _sigmoid(qg[:, nq:])


def _kvq_call(x, modkv, mod, gkv, g0, w_kv, w_qg, tm, emit_attn_kv):
    bx, s, d = x.shape
    nkv = w_kv.shape[1]
    n_rows_cols = 4 * N_KV * HEAD_DIM
    n_win_cols = nkv - n_rows_cols
    nq = N_HEADS * HEAD_DIM
    ng = w_qg.shape[1] - nq
    row = lambda w: pl.BlockSpec((1, tm, w), lambda b, i: (b, i, 0))
    out_specs = [row(n_rows_cols), row(n_win_cols), row(nq), row(ng)]
    out_shape = [jax.ShapeDtypeStruct((bx, s, n_rows_cols), F32), jax.ShapeDtypeStruct((bx, s, n_win_cols), F32),
                 jax.ShapeDtypeStruct((bx, s, nq), F32), jax.ShapeDtypeStruct((bx, s, ng), F32)]
    if emit_attn_kv:
        out_specs += [pl.BlockSpec((1, N_KV, tm, LANES), lambda b, i: (b, 0, i, 0))] * 4
        out_shape += [jax.ShapeDtypeStruct((bx, N_KV, s, LANES), BF16)] * 4
    return pl.pallas_call(
        functools.partial(_kvq_kernel, d, n_rows_cols, emit_attn_kv), grid=(bx, s // tm),
        in_specs=[row(d), _row_specs(bx, s, tm, modkv), _row_specs(bx, s, tm, mod), _full((1, d)), _full((1, d)),
                  _full(w_kv.shape), _full(w_qg.shape)],
        out_specs=out_specs, out_shape=out_shape,
        compiler_params=_cparams("arbitrary", "arbitrary"), name="kv_q_proj")(x, modkv, mod, gkv, g0, w_kv, w_qg)


def _outproj_kernel(d, n_branch, *refs):
    x_ref, mod_ref, g1_ref, w_ref = refs[:4]
    xo_ref = refs[-1]
    if n_branch == 1:
        o = refs[4][0]
    else:
        o = sum(refs[4 + 2 * i][0] * refs[5 + 2 * i][0] for i in range(n_branch))
    y = _dot(o.astype(BF16), w_ref[...])
    xo_ref[0] = x_ref[0] + mod_ref[0][:, 2 * d:] * _rms(y, g1_ref[...])


def _outproj_call(x, mod, g1, w, branches, tm):
    bx, s, d = x.shape
    row = pl.BlockSpec((1, tm, d), lambda b, i: (b, i, 0))
    n_branch = 1 if len(branches) == 1 else len(branches) // 2
    return pl.pallas_call(
        functools.partial(_outproj_kernel, d, n_branch), grid=(bx, s // tm),
        in_specs=[row, _row_specs(bx, s, tm, mod), _full((1, d)), _full(w.shape)] + [row] * len(branches),
        out_specs=row, out_shape=jax.ShapeDtypeStruct((bx, s, d), F32),
        compiler_params=_cparams("arbitrary", "arbitrary"), name="attn_out_proj")(x, mod, g1, w, *branches)


def _gelu_tanh(x):
    return x * (0.5 * (1.0 + jnp.tanh(math.sqrt(2.0 / math.pi) * (x + 0.044715 * (x * x * x)))))


def _compress_kernel(tmb, x_ref, pe_ref, w1_ref, w2_ref, o_ref):
    acc = jnp.zeros((tmb, LANES), F32)
    for r in range(CMP_BLOCK):
        xr = x_ref[pl.ds(r, tmb, stride=CMP_BLOCK), :] + pe_ref[0, r]
        acc = acc + _dot(xr.astype(BF16), w1_ref[0, r])
    o_ref[...] = _dot(_gelu_tanh(acc).astype(BF16), w2_ref[0])


def _compress_call(rows2d, pe2, w1bd, w2bd):
    m = rows2d.shape[0] // CMP_BLOCK
    tmb = max(t for t in range(8, min(256, m) + 1, 8) if m % t == 0)
    return pl.pallas_call(
        functools.partial(_compress_kernel, tmb), grid=(m // tmb, 4),
        in_specs=[pl.BlockSpec((tmb * CMP_BLOCK, LANES), lambda i, j: (i, j)),
                  pl.BlockSpec((1, CMP_BLOCK, 1, LANES), lambda i, j: (j // 2, 0, 0, 0)),
                  pl.BlockSpec((1, CMP_BLOCK, LANES, LANES), lambda i, j: (j // 2, 0, 0, 0)),
                  pl.BlockSpec((1, LANES, LANES), lambda i, j: (j // 2, 0, 0))],
        out_specs=pl.BlockSpec((tmb, LANES), lambda i, j: (i, j)),
        out_shape=jax.ShapeDtypeStruct((m, 4 * LANES), F32),
        compiler_params=_cparams("arbitrary", "arbitrary"), name="compress")(rows2d, pe2, w1bd, w2bd)


def _compress_weights(cmp_pe, cmp_w1, cmp_w2):
    z = jnp.zeros((2, CMP_BLOCK, HEAD_DIM, HEAD_DIM), F32)
    w1 = cmp_w1.reshape(2, CMP_BLOCK, HEAD_DIM, HEAD_DIM)
    w1bd = jnp.concatenate([jnp.concatenate([w1, z], -1), jnp.concatenate([z, w1], -1)], -2).astype(BF16)
    z2 = jnp.zeros((2, HEAD_DIM, HEAD_DIM), F32)
    w2bd = jnp.concatenate([jnp.concatenate([cmp_w2, z2], -1), jnp.concatenate([z2, cmp_w2], -1)], -2).astype(BF16)
    pe2 = jnp.concatenate([cmp_pe, cmp_pe], -1)[:, :, None, :]
    return pe2, w1bd, w2bd


N_PAGE_CB = PAGE_SIZE // CMP_BLOCK
CB_ROWS = 8


PAGE_PITCH = N_KV * HEAD_DIM + 4


def _compress_pool_kernel(n_pg, n_i, x_hbm, pe_ref, mw_ref, w2_ref, o_ref, xbuf, sem):
    gw = N_KV * HEAD_DIM
    t = pl.program_id(0)
    n_t = pl.num_programs(0)
    slot_rows = n_pg * PAGE_PITCH

    def page_copy(step, slot, p):
        return pltpu.make_async_copy(x_hbm.at[(step % n_i) * n_pg + p, step // n_i],
                                     xbuf.at[pl.ds(slot * slot_rows + p * PAGE_PITCH, gw), :], sem.at[slot])

    def start_all(step, slot):
        for p in range(n_pg):
            page_copy(step, slot, p).start()

    slot = t % 2

    @pl.when(t == 0)
    def _():
        start_all(t, slot)

    @pl.when(t + 1 < n_t)
    def _():
        start_all(t + 1, 1 - slot)
    for p in range(n_pg):
        page_copy(t, slot, p).wait()

    def rows_of(g, d):
        return xbuf[pl.ds(slot * slot_rows + g * HEAD_DIM + d, n_pg, stride=PAGE_PITCH), :]
    lhs = jnp.concatenate(
        [(jnp.concatenate([rows_of(g, d) for g in range(N_KV)], axis=0) + pe_ref[0, d]).astype(BF16)
         for d in range(HEAD_DIM)], axis=1)
    z = _dot(_gelu_tanh(_dot(lhs, mw_ref[0])).astype(BF16), w2_ref[0])
    lane = lax.broadcasted_iota(jnp.int32, (n_pg, LANES), 1)
    low = lane < HEAD_DIM
    o_ref[...] = jnp.zeros(o_ref.shape, F32)
    for n in range(N_PAGE_CB):
        for c in range(N_KV // 2):
            a = z[(2 * c) * n_pg:(2 * c + 1) * n_pg, (n // 2) * LANES:(n // 2 + 1) * LANES]
            b = z[(2 * c + 1) * n_pg:(2 * c + 2) * n_pg, (n // 2) * LANES:(n // 2 + 1) * LANES]
            if n % 2 == 0:
                b = pltpu.roll(b, HEAD_DIM, 1)
            else:
                a = pltpu.roll(a, HEAD_DIM, 1)
            o_ref[c, pl.ds(n, n_pg, stride=CB_ROWS), :] = jnp.where(low, a, b)


def _compress_pool_call(cache_t, pe_t, mw, w2bd4):
    n_pool = cache_t.shape[0]
    n_pg = max(t for t in range(8, min(64, n_pool) + 1, 8) if n_pool % t == 0)
    n_i = n_pool // n_pg
    return pl.pallas_call(
        functools.partial(_compress_pool_kernel, n_pg, n_i), grid=(2 * n_i,),
        in_specs=[pl.BlockSpec(memory_space=pl.ANY),
                  pl.BlockSpec((1, HEAD_DIM, 1, PAGE_SIZE), lambda t: (t // n_i, 0, 0, 0)),
                  pl.BlockSpec((1, HEAD_DIM * PAGE_SIZE, N_PAGE_CB * HEAD_DIM), lambda t: (t // n_i, 0, 0)),
                  pl.BlockSpec((1, N_PAGE_CB * HEAD_DIM, N_PAGE_CB * HEAD_DIM), lambda t: (t // n_i, 0, 0))],
        out_specs=pl.BlockSpec((N_KV // 2, n_pg * CB_ROWS, LANES), lambda t: (t // n_i, t % n_i, 0)),
        out_shape=jax.ShapeDtypeStruct((N_KV, n_pool * CB_ROWS, LANES), F32),
        scratch_shapes=[pltpu.VMEM((2 * n_pg * PAGE_PITCH, PAGE_SIZE), F32), pltpu.SemaphoreType.DMA((2,))],
        compiler_params=_cparams("arbitrary"), name="compress_pool")(cache_t, pe_t, mw, w2bd4)


def _compress_pool_weights(cmp_pe, cmp_w1, cmp_w2):
    eye = np.eye(N_PAGE_CB, dtype=bool)
    w1t = cmp_w1.reshape(2, CMP_BLOCK, HEAD_DIM, HEAD_DIM).transpose(0, 2, 1, 3).astype(BF16)
    rows = jnp.broadcast_to(w1t[:, :, None], (2, HEAD_DIM, N_PAGE_CB, CMP_BLOCK, HEAD_DIM))
    rows = jnp.tile(rows.reshape(2, HEAD_DIM * PAGE_SIZE, HEAD_DIM), (1, 1, N_PAGE_CB))
    same_block = ((np.arange(HEAD_DIM * PAGE_SIZE) // CMP_BLOCK) % N_PAGE_CB)[:, None] == (
        np.arange(N_PAGE_CB * HEAD_DIM) // HEAD_DIM)[None, :]
    mw = jnp.where(same_block[None], rows, jnp.zeros((), BF16))
    w2bd4 = jnp.where(eye[None, :, None, :, None], cmp_w2[:, None, :, None, :], 0.0)
    w2bd4 = w2bd4.reshape(2, N_PAGE_CB * HEAD_DIM, N_PAGE_CB * HEAD_DIM).astype(BF16)
    pe_t = jnp.tile(cmp_pe.transpose(0, 2, 1), (1, 1, N_PAGE_CB))[:, :, None, :]
    return pe_t, mw, w2bd4


def _bias_of_distance(dist, tab):
    val = jnp.zeros(dist.shape, F32) + tab(0)
    for k in range(1, N_BUCKETS):
        val = jnp.where(dist >= BUCKET_THR[k], tab(k), val)
    return val


def _toeplitz_kernel(window, tab_ref, o_ref):
    g = pl.program_id(0)
    dt = pl.program_id(1)
    x = lax.broadcasted_iota(jnp.int32, (8, 2 * TK), 1)
    dist = dt * TQ - jnp.where(x < TK, x, x - 2 * TK)
    for hh in range(HPG):
        head = g * HPG + hh
        val = _bias_of_distance(dist, lambda k, head=head: tab_ref[k, head])
        val = jnp.where(dist < 0, NEG_MASK, val)
        if window is not None:
            val = jnp.where(dist >= window, NEG_MASK, val)
        tile = pltpu.roll(jnp.broadcast_to(val[0:1], (TQ, 2 * TK)), 0, 1, stride=1, stride_axis=0)
        o_ref[0, 0, hh * TQ:(hh + 1) * TQ, :] = tile[:, :TK]


def _toeplitz_call(rel_bias, n_chunks, window):
    return pl.pallas_call(
        functools.partial(_toeplitz_kernel, window), grid=(N_KV, n_chunks),
        in_specs=[pl.BlockSpec(memory_space=pltpu.SMEM)],
        out_specs=pl.BlockSpec((1, 1, HPG * TQ, TK), lambda g, t: (g, t, 0, 0)),
        out_shape=jax.ShapeDtypeStruct((N_KV, n_chunks, HPG * TQ, TK), F32),
        compiler_params=_cparams("arbitrary", "arbitrary"), name="bias_toeplitz")(rel_bias)


def _cmp_bias_kernel(tab_ref, o_ref):
    g = pl.program_id(0)
    qi = pl.program_id(1)
    sub = 8
    t_q = qi * TQ + lax.broadcasted_iota(jnp.int32, (sub, TQ), 1)
    for r0 in range(0, LANES, sub):
        rows = slice(r0, r0 + sub)
        cb_lo = 2 * (r0 % HEAD_DIM) + r0 // HEAD_DIM
        cb_hi = cb_lo + 2 * (sub - 1)
        d_min = qi * TQ - (cb_hi * CMP_BLOCK + CMP_BLOCK - 1)
        d_max = qi * TQ + TQ - 1 - (cb_lo * CMP_BLOCK + CMP_BLOCK - 1)

        @pl.when(d_max < 0)
        def _(rows=rows):
            o_ref[0, 0, rows, :] = jnp.full((sub, HPG * TQ), NEG_MASK, F32)

        @pl.when(d_min >= BUCKET_THR[-1])
        def _(rows=rows):
            for hh in range(HPG):
                o_ref[0, 0, rows, hh * TQ:(hh + 1) * TQ] = jnp.full((sub, TQ), tab_ref[N_BUCKETS - 1, g * HPG + hh], F32)

        @pl.when((d_max >= 0) & (d_min < BUCKET_THR[-1]))
        def _(rows=rows, cb_lo=cb_lo):
            cblk = cb_lo + 2 * lax.broadcasted_iota(jnp.int32, (sub, TQ), 0)
            dist = t_q - (cblk * CMP_BLOCK + (CMP_BLOCK - 1))
            for hh in range(HPG):
                head = g * HPG + hh
                val = _bias_of_distance(dist, lambda k, head=head: tab_ref[k, head])
                o_ref[0, 0, rows, hh * TQ:(hh + 1) * TQ] = jnp.where(dist < 0, NEG_MASK, val)


def _cmp_bias_call(rel_bias, n_qt):
    return pl.pallas_call(
        _cmp_bias_kernel, grid=(N_KV, n_qt),
        in_specs=[pl.BlockSpec(memory_space=pltpu.SMEM)],
        out_specs=pl.BlockSpec((1, 1, LANES, HPG * TQ), lambda g, t: (g, t, 0, 0)),
        out_shape=jax.ShapeDtypeStruct((N_KV, n_qt, LANES, HPG * TQ), F32),
        compiler_params=_cparams("arbitrary", "arbitrary"), name="bias_cmp")(rel_bias)


def _bias_cols_kernel(dist_ref, tab_ref, o_ref):
    dist = dist_ref[...]
    val = _bias_of_distance(dist, lambda k: tab_ref[:, k:k + 1])
    o_ref[...] = jnp.where(dist < 0, NEG_MASK, val)


def _bias_cols_call(dist, tab_heads):
    r = dist.shape[1]
    return pl.pallas_call(
        _bias_cols_kernel, grid=(1,),
        in_specs=[pl.BlockSpec((N_HEADS, r), lambda i: (0, 0)), pl.BlockSpec((N_HEADS, N_BUCKETS), lambda i: (0, 0))],
        out_specs=pl.BlockSpec((N_HEADS, r), lambda i: (0, 0)),
        out_shape=jax.ShapeDtypeStruct((N_HEADS, r), F32),
        compiler_params=_cparams("arbitrary"), name="bias_cols")(dist, tab_heads)


def _nsa_seq_kernel(n_sb, q_ref, gate_ref, kaug_ref, v1_ref, kw_ref, vw1_ref, kcb_ref, vcb_ref, tcmp_ref,
                    tsel_ref, twin_ref, gexp_ref, o_ref, qaug, s_win, s_sc, mrun, mb, acc_sel, acc_win, score_sc):
    qi = pl.program_id(2)
    rows = HPG * TQ
    lane = lax.broadcasted_iota(jnp.int32, (TQ, LANES), 1)
    low = lane < HEAD_DIM

    qh = []
    for hh in range(HPG):
        qv = q_ref[0, :, (hh // 2) * LANES:(hh // 2 + 1) * LANES]
        if hh % 2 == 1:
            qv = pltpu.roll(qv, HEAD_DIM, 1)
        qh.append(jnp.where(low, qv, 0.0))
    qw = jnp.concatenate(qh, axis=0).astype(BF16)

    m_w = jnp.full((rows, LANES), NEG_MASK, F32)
    for i in range(N_WIN_TILES):
        r0 = pl.multiple_of(jnp.maximum(qi - i, 0) * TK, TK)
        s = _dot_nt(qw, kw_ref[0, 0, pl.ds(r0, TK), :]) + twin_ref[0, jnp.where(qi >= i, i, N_WIN_TILES)]
        s_win[i] = s
        for c in range(TK // LANES):
            m_w = jnp.maximum(m_w, s[:, c * LANES:(c + 1) * LANES])
    mb_w = jnp.broadcast_to(jnp.max(m_w, axis=1, keepdims=True), (rows, LANES))
    mb_w2 = jnp.concatenate([mb_w] * (TK // LANES), axis=1)
    acc_w = jnp.zeros((rows, LANES), F32)
    for i in range(N_WIN_TILES):
        r0 = pl.multiple_of(jnp.maximum(qi - i, 0) * TK, TK)
        acc_w = acc_w + _dot(jnp.exp(s_win[i] - mb_w2).astype(BF16), vw1_ref[0, 0, pl.ds(r0, TK), :])
    acc_win[...] = acc_w

    s_c = _dot_nt(kcb_ref[0, 0].astype(BF16), qw) + tcmp_ref[0, 0]
    ok_c = s_c > 0.5 * NEG_MASK
    mx = jnp.max(s_c, axis=0, keepdims=True)
    p = jnp.where(ok_c, jnp.exp(s_c - mx), 0.0)
    den = jnp.sum(p, axis=0, keepdims=True)
    pn_t = p / jnp.where(den > 0, den, 1.0)
    o_c = _dot(pn_t.T.astype(BF16), vcb_ref[0, 0].astype(BF16))
    imp = pn_t[:, 0:TQ]
    for hh in range(1, HPG):
        imp = imp + pn_t[:, hh * TQ:(hh + 1) * TQ]
    n_blk = LANES // 2
    imp = imp[:n_blk] + imp[n_blk:]

    blk = lax.broadcasted_iota(jnp.int32, (n_blk, TQ), 0)
    cur = (qi * TQ + lax.broadcasted_iota(jnp.int32, (n_blk, TQ), 1)) // SEL_BLOCK
    valid = blk <= cur
    forced = (blk == 0) | (blk == cur) | (blk == cur - 1)
    score_sc[...] = jnp.where(valid & forced, BIG_SCORE, jnp.where(valid, imp, -BIG_SCORE))
    sub = 8
    groups = [score_sc[r0:r0 + sub, :] for r0 in range(0, n_blk, sub)]
    rowg = lax.broadcasted_iota(jnp.int32, (sub, TQ), 0)
    cnts = [jnp.zeros((sub, TQ), jnp.int32) for _ in groups]
    for bp in range(n_blk):
        r = score_sc[bp:bp + 1, :]
        for gi, s_g in enumerate(groups):
            if gi * sub > bp:
                ahead = r >= s_g
            elif (gi + 1) * sub - 1 < bp:
                ahead = r > s_g
            else:
                ahead = (r > s_g) | ((r == s_g) & (rowg + gi * sub > bp))
            cnts[gi] = cnts[gi] + jnp.where(ahead, 1, 0)
    cnt = jnp.concatenate(cnts, axis=0)
    selmask_t = jnp.where((cnt < min(TOP_N, n_sb)) & valid, 0.0, NEG_SEL)
    selmask = jnp.concatenate([jnp.zeros((n_blk, TQ), F32), selmask_t], axis=0).T
    for hh in range(HPG):
        qaug[hh * TQ:(hh + 1) * TQ, :] = jnp.where(low, qh[hh], selmask).astype(BF16)

    def branch(q_sc, k_ref, v_ref, t_ref, acc_sc, n_tiles, n_tbl, widths):
        mrun[...] = jnp.full(mrun.shape, NEG_MASK, F32)

        def tile_loop(step):
            done = 0
            for width in widths:
                def group(j, _, done=done, width=width):
                    step(tuple(done + width * j + u for u in range(width)))
                    return 0
                n_groups = (n_tiles - done) // width
                lax.fori_loop(0, n_groups, group, 0)
                done = done + n_groups * width

        def scores(tiles):
            m = mrun[...]
            for i in tiles:
                r0 = pl.multiple_of((qi - i) * TK, TK)
                s = _dot_nt(q_sc[...], k_ref[0, 0, pl.ds(r0, TK), :]) + t_ref[0, jnp.minimum(i, n_tbl)]
                s_sc[i] = s
                for c in range(TK // LANES):
                    m = jnp.maximum(m, s[:, c * LANES:(c + 1) * LANES])
            mrun[...] = m
        tile_loop(scores)
        mb[...] = jnp.broadcast_to(jnp.max(mrun[...], axis=1, keepdims=True), (rows, LANES))
        acc_sc[...] = jnp.zeros(acc_sc.shape, F32)

        def weigh(tiles):
            mbv = mb[...]
            mb2 = jnp.concatenate([mbv] * (TK // LANES), axis=1)
            acc = acc_sc[...]
            for i in tiles:
                r0 = pl.multiple_of((qi - i) * TK, TK)
                acc = acc + _dot(jnp.exp(s_sc[i] - mb2).astype(BF16), v_ref[0, 0, pl.ds(r0, TK), :])
            acc_sc[...] = acc
        tile_loop(weigh)
        return acc_sc[...]

    acc_s = branch(qaug, kaug_ref, v1_ref, tsel_ref, acc_sel, qi + 1, N_TBL, (8, 4, 2, 1))
    acc_w = acc_win[...]

    gates = gate_ref[0]
    g_hi = gates.astype(BF16)
    g_lo = (gates - g_hi.astype(F32)).astype(BF16)
    g_all = _dot(jnp.concatenate([g_hi, g_lo], axis=1), gexp_ref[...])

    def normalized(acc, odd):
        swapped = pltpu.roll(acc, HEAD_DIM, 1)
        return swapped / acc if odd else acc / swapped

    outs = []
    for hh in range(HPG):
        gb = [g_all[:, (hh * 3 + br) * LANES:(hh * 3 + br + 1) * LANES] for br in range(3)]
        rs = slice(hh * TQ, (hh + 1) * TQ)
        outs.append(gb[0] * o_c[rs] + gb[1] * normalized(acc_s[rs], hh % 2) + gb[2] * normalized(acc_w[rs], hh % 2))
    for c in range(HPG // 2):
        o_ref[0, :, c * LANES:(c + 1) * LANES] = jnp.where(low, outs[2 * c], outs[2 * c + 1]).astype(o_ref.dtype)


def _nsa_seq_call(q, gates, kaug, v1, kw, vw1, kcb, vcb, tcmp, tsel, twin):
    b, s, _ = q.shape
    n_qt = s // TQ
    rows = HPG * TQ
    kv_spec = pl.BlockSpec((1, 1, s, LANES), lambda bb, g, i: (bb, g, 0, 0))
    cb_spec = pl.BlockSpec((1, 1, kcb.shape[2], LANES), lambda bb, g, i: (bb, g, 0, 0))
    tbl_spec = lambda t: pl.BlockSpec((1,) + t.shape[1:], lambda bb, g, i: (g, 0, 0, 0))
    n_gate = HPG * 3
    gexp = np.zeros((2 * LANES, n_gate * LANES), np.float32)
    for c in range(n_gate):
        gexp[c, c * LANES:(c + 1) * LANES] = 1.0
        gexp[LANES + c, c * LANES:(c + 1) * LANES] = 1.0
    gexp = jnp.asarray(gexp, BF16)
    return pl.pallas_call(
        functools.partial(_nsa_seq_kernel, s // SEL_BLOCK), grid=(b, N_KV, n_qt),
        in_specs=[pl.BlockSpec((1, TQ, HPG * HEAD_DIM), lambda bb, g, i: (bb, i, g)),
                  pl.BlockSpec((1, TQ, LANES), lambda bb, g, i: (bb, i, g)),
                  kv_spec, kv_spec, kv_spec, kv_spec, cb_spec, cb_spec,
                  pl.BlockSpec((1, 1, LANES, rows), lambda bb, g, i: (g, i, 0, 0)), tbl_spec(tsel), tbl_spec(twin),
                  pl.BlockSpec(gexp.shape, lambda bb, g, i: (0, 0))],
        out_specs=pl.BlockSpec((1, TQ, HPG * HEAD_DIM), lambda bb, g, i: (bb, i, g)),
        out_shape=jax.ShapeDtypeStruct((b, s, N_HEADS * HEAD_DIM), BF16),
        scratch_shapes=[pltpu.VMEM((rows, LANES), BF16), pltpu.VMEM((N_WIN_TILES, rows, TK), F32),
                        pltpu.VMEM((s // TK, rows, TK), F32), pltpu.VMEM((rows, LANES), F32),
                        pltpu.VMEM((rows, LANES), F32), pltpu.VMEM((rows, LANES), F32),
                        pltpu.VMEM((rows, LANES), F32), pltpu.VMEM((LANES // 2, TQ), F32)],
        compiler_params=_cparams("arbitrary", "arbitrary", "arbitrary"), name="nsa_seq")(
            q, gates, kaug, v1, kw, vw1, kcb, vcb, tcmp, tsel, twin, gexp)


def _softmax_lanes(s):
    p = jnp.exp(s - jnp.max(s, axis=1, keepdims=True))
    return p / jnp.sum(p, axis=1, keepdims=True)


def _rows_to_heads(rows):
    hg = lax.broadcasted_iota(jnp.int32, (N_HEADS, rows[0].shape[1]), 0) // HPG
    out = jnp.broadcast_to(rows[0], hg.shape)
    for g in range(1, N_KV):
        out = jnp.where(hg == g, rows[g], out)
    return out


def _nsa_stepT_kernel(n_pages, n_seq, pt_ref, *refs):
    del pt_ref
    n_pg = n_seq * n_pages
    q_ref = refs[0]
    cb_refs = refs[1:1 + n_pg]
    pg_refs = refs[1 + n_pg:1 + 2 * n_pg]
    (kvnew_ref, winnew_ref, wcol_ref, cwin_ref, bsel_ref, bnew_ref, bwin_ref, bcmp_ref) = refs[1 + 2 * n_pg:9 + 2 * n_pg]
    oc_ref, os_ref, ow_ref, nwin_ref = refs[9 + 2 * n_pg:13 + 2 * n_pg]
    (s_sc,) = refs[13 + 2 * n_pg:]
    for u in range(n_seq):
        one = lambda ref, u=u: ref.at[pl.ds(u, 1)]
        _nsa_step_one(n_pages, pl.program_id(0) * n_seq + u, one(q_ref), cb_refs[u * n_pages:(u + 1) * n_pages],
                      pg_refs[u * n_pages:(u + 1) * n_pages], one(kvnew_ref), one(winnew_ref), wcol_ref, one(cwin_ref),
                      bsel_ref, bnew_ref, bwin_ref, bcmp_ref, one(oc_ref), one(os_ref), one(ow_ref), one(nwin_ref),
                      s_sc.at[u])


def _nsa_step_one(n_pages, b, q_ref, cb_refs, pg_refs, kvnew_ref, winnew_ref, wcol_ref, cwin_ref, bsel_ref, bnew_ref,
                  bwin_ref, bcmp_ref, oc_ref, os_ref, ow_ref, nwin_ref, s_sc):
    gw = N_KV * HEAD_DIM
    n_past = n_pages * PAGE_SIZE
    q16 = q_ref[0].astype(BF16)
    qf = q16.astype(F32)
    lane = lax.broadcasted_iota(jnp.int32, (N_HEADS, LANES), 1)

    n_pad = LANES - CB_ROWS * n_pages
    cb = jnp.concatenate([jnp.concatenate([r[c] for c in range(N_KV)], axis=1) for r in cb_refs]
                         + ([jnp.zeros((n_pad, 2 * gw), F32)] if n_pad else []), axis=0)
    pn_c = _softmax_lanes(_dot_nt(q16, cb[:, :gw].astype(BF16)) + bcmp_ref[...])
    oc_ref[0] = _dot(pn_c.astype(BF16), cb[:, gw:].astype(BF16))

    grp = [pn_c[HPG * g:HPG * g + 1] + pn_c[HPG * g + 1:HPG * g + 2] + pn_c[HPG * g + 2:HPG * g + 3]
           + pn_c[HPG * g + 3:HPG * g + 4] for g in range(N_KV)]
    lane8 = lax.broadcasted_iota(jnp.int32, (8, LANES), 1)
    row8 = lax.broadcasted_iota(jnp.int32, (8, LANES), 0)
    imp = jnp.zeros((8, LANES), F32)
    for g in range(N_KV):
        imp = jnp.where(row8 == g, grp[g], imp)
    imp = imp + pltpu.roll(imp, LANES - 1, 1)
    n_past_blk = n_past // SEL_BLOCK
    cur_lane = LANES - 2
    is_blk = ((lane8 % CB_ROWS == 0) | (lane8 % CB_ROWS == 2)) & (lane8 < CB_ROWS * n_pages)
    last_lane = ((n_past_blk - 1) // 2) * CB_ROWS + 2 * ((n_past_blk - 1) % 2)
    forced = (lane8 == 0) | (lane8 == last_lane) | (lane8 == cur_lane)
    valid = is_blk | (lane8 == cur_lane)
    score = jnp.where(valid & forced, BIG_SCORE, jnp.where(valid, imp, -BIG_SCORE))
    cnt = jnp.zeros((8, LANES), jnp.int32)
    for k in range(2, LANES, 2):
        r = pltpu.roll(score, k, 1)
        cnt = cnt + jnp.where((r > score) | ((r == score) & (lane8 >= k)), 1, 0)
    selrows = jnp.where((cnt < min(TOP_N, n_past_blk + 1)) & valid, 0.0, NEG_SEL)
    selmask = _rows_to_heads([selrows[g:g + 1] for g in range(N_KV)])

    blocks_per_page = PAGE_SIZE // SEL_BLOCK
    for p in range(n_pages):
        kt = pg_refs[p][0, 0:gw, :].astype(BF16)
        msk = selmask[:, CB_ROWS * p:CB_ROWS * p + 1]
        for i in range(1, blocks_per_page):
            msk = jnp.where(lane >= i * SEL_BLOCK, selmask[:, CB_ROWS * p + 2 * i:CB_ROWS * p + 2 * i + 1], msk)
        s_sc[:, p * PAGE_SIZE:(p + 1) * PAGE_SIZE] = _dot(q16, kt) + bsel_ref[:, p * PAGE_SIZE:(p + 1) * PAGE_SIZE] + msk
    knew = kvnew_ref[0][:, 2 * gw:3 * gw].astype(BF16).astype(F32)
    s_new = jnp.sum(qf * knew, axis=1, keepdims=True) + bnew_ref[:, 0:1] + selmask[:, cur_lane:cur_lane + 1]
    s_sc[:, n_past:n_past + LANES] = jnp.where(lane == 0, s_new, NEG_MASK)
    s_all = s_sc[...]
    mx = jnp.max(s_all, axis=1, keepdims=True)
    den = jnp.sum(jnp.exp(s_all - mx), axis=1, keepdims=True)
    acc = jnp.zeros((N_HEADS, gw), F32)
    for p in range(n_pages):
        pn = jnp.exp(s_sc[:, p * PAGE_SIZE:(p + 1) * PAGE_SIZE] - mx) / den
        acc = acc + _dot_nt(pn.astype(BF16), pg_refs[p][0, gw:2 * gw, :].astype(BF16))
    pn_new = jnp.exp(s_new - mx) / den
    vnew = kvnew_ref[0][:, 3 * gw:4 * gw].astype(BF16).astype(F32)
    os_ref[0] = acc + pn_new.astype(BF16).astype(F32) * vnew

    cw = cwin_ref[0]
    w_len = cw.shape[1]
    s_w = _dot(q16, cw[0:gw].astype(BF16)) + bwin_ref[...]
    kwn = winnew_ref[0][:, 0:gw].astype(BF16).astype(F32)
    s_wn = jnp.sum(qf * kwn, axis=1, keepdims=True) + bnew_ref[:, 0:1]
    mxw = jnp.maximum(jnp.max(s_w, axis=1, keepdims=True), s_wn)
    pw = jnp.exp(s_w - mxw)
    pwn = jnp.exp(s_wn - mxw)
    denw = jnp.sum(pw, axis=1, keepdims=True) + pwn
    vwn = winnew_ref[0][:, gw:2 * gw].astype(BF16).astype(F32)
    ow_ref[0] = _dot_nt((pw / denw).astype(BF16), cw[gw:2 * gw].astype(BF16)) + (pwn / denw).astype(BF16).astype(F32) * vwn
    lane_w = lax.broadcasted_iota(jnp.int32, cw.shape, 1)
    nb_l = wcol_ref.shape[1]
    lane_b = lax.broadcasted_iota(jnp.int32, (cw.shape[0], nb_l), 1)
    col = jnp.sum(jnp.where(lane_b == b, wcol_ref[...], 0.0), axis=1, keepdims=True)
    nwin_ref[0] = jnp.where(lane_w == w_len - 1, col, pltpu.roll(cw, w_len - 1, 1))


def _nsa_step_call(page_table, qrows, cb_pool, cache_t, kvnew, winnew, wcol, cwin_t, bsel, bnew, bwin, bcmp):
    nb, n_pages = page_table.shape
    gw = N_KV * HEAD_DIM
    w_len = cwin_t.shape[2]
    n_seq = 2 if nb % 2 == 0 else 1
    slots = [(u, p) for u in range(n_seq) for p in range(n_pages)]
    cb_specs = [pl.BlockSpec((N_KV, None, CB_ROWS, LANES), lambda b, pt, _u=u, _p=p: (0, pt[b * n_seq + _u, _p], 0, 0))
                for u, p in slots]
    pg_specs = [pl.BlockSpec((1, 2 * gw, PAGE_SIZE), lambda b, pt, _u=u, _p=p: (pt[b * n_seq + _u, _p], 1, 0))
                for u, p in slots]
    const = lambda a: pl.BlockSpec(a.shape, lambda b, pt: (0, 0))
    seq = lambda *dims: pl.BlockSpec((n_seq,) + dims, lambda b, pt: (b, 0, 0))
    grid_spec = pltpu.PrefetchScalarGridSpec(
        num_scalar_prefetch=1, grid=(nb // n_seq,),
        in_specs=[seq(N_HEADS, gw)] + cb_specs + pg_specs + [
            seq(1, 4 * gw), seq(1, 2 * gw), const(wcol), seq(2 * gw, w_len),
            const(bsel), const(bnew), const(bwin), const(bcmp)],
        out_specs=[seq(N_HEADS, gw)] * 3 + [seq(2 * gw, w_len)],
        scratch_shapes=[pltpu.VMEM((n_seq, N_HEADS, n_pages * PAGE_SIZE + LANES), F32)])
    o_shape = jax.ShapeDtypeStruct((nb, N_HEADS, gw), F32)
    return pl.pallas_call(
        functools.partial(_nsa_stepT_kernel, n_pages, n_seq), grid_spec=grid_spec,
        out_shape=[o_shape, o_shape, o_shape, jax.ShapeDtypeStruct((nb, 2 * gw, w_len), F32)],
        compiler_params=_cparams("arbitrary"), name="nsa_step")(
            page_table, qrows, *([cb_pool] * len(slots)), *([cache_t] * len(slots)), kvnew, winnew, wcol, cwin_t,
            bsel, bnew, bwin, bcmp)


def _head_diag(o):
    b = o.shape[0]
    o5 = o.reshape(b, N_KV, HPG, N_KV, HEAD_DIM)
    return jnp.stack([o5[:, g, :, g, :] for g in range(N_KV)], axis=1).reshape(b, N_HEADS * HEAD_DIM)


def kernel(x_prompt, x_sample, c_prompt, c_sample, cache_kv, cache_win, state_conv_a, state_ffn_conv, page_table, mod_w, mod_b, norm_g, a_w_in, a_conv_w, a_conv_b, a_w_out, kv_mod_w, kv_mod_b, kv_norm_g, w_kv, cmp_pe, cmp_w1, cmp_w2, b_w_qg, b_w_out, rel_bias, ffn_w_up, ffn_conv_w, ffn_conv_b, ffn_w_down):
    bp, s, d = x_prompt.shape
    bs = x_sample.shape[0]
    depth = mod_w.shape[0]
    n_a = a_w_in.shape[0]
    assert depth == 2 and n_a == 1 and x_sample.shape[1] == 1
    dff = ffn_w_down.shape[1]
    n_pool = cache_kv.shape[0]
    n_pages = page_table.shape[1]
    past_len = n_pages * PAGE_SIZE
    gw = N_KV * HEAD_DIM
    nq = N_HEADS * HEAD_DIM

    n_c = bp + bs
    n_cp = -(-n_c // 8) * 8
    c_all = jnp.pad(jnp.concatenate([c_prompt, c_sample], 0), ((0, n_cp - n_c), (0, 0)))
    mods = _mod_call(c_all, mod_w.reshape(depth * 2, d, 3 * d), mod_b.reshape(depth * 2, 1, 3 * d))
    modkv = _mod_call(c_all, kv_mod_w[None], kv_mod_b[None, None])[0]
    mod_p = lambda i: mods[i, :bp][:, None, :]
    mod_s = lambda i: mods[i, bp:n_c][None]

    w_in = a_w_in[0].astype(BF16)
    w_out_a = a_w_out[0].astype(BF16)
    w_up = ffn_w_up.astype(BF16)
    w_dn = ffn_w_down.astype(BF16)
    w_kv_b = w_kv.astype(BF16)
    w_gate = jnp.pad(b_w_qg[0][:, nq:].reshape(d, N_KV, HPG * 3), ((0, 0), (0, 0), (0, LANES - HPG * 3)))
    w_qg = jnp.concatenate([b_w_qg[0][:, :nq], w_gate.reshape(d, N_KV * LANES)], axis=1).astype(BF16)
    w_out_b = b_w_out[0].astype(BF16)
    g = lambda l, i: norm_g[l, i][None]
    pe2, w1bd, w2bd = _compress_weights(cmp_pe, cmp_w1, cmp_w2)
    n_chunks = 2

    tm = min(512, s)
    tm_ffn = min(512, s)
    x1, st_a = _mixer_call(x_prompt, mod_p(0), g(0, 0), g(0, 1), w_in, a_conv_w[0], a_conv_b[0][None], w_out_a, None, tm)
    x2, st_f0 = _ffn_call(x1, mod_p(1), g(0, 2), g(0, 3), w_up[0], ffn_conv_w[0], ffn_conv_b[0][None], w_dn[0], None, tm_ffn, n_chunks)
    rows, win, q, gates, kaug, v1, kw, vw1 = _kvq_call(
        x2, modkv[:bp][:, None, :], mod_p(2), kv_norm_g[None], g(1, 0), w_kv_b, w_qg, tm, True)
    cb = _compress_call(rows.reshape(bp * s, 4 * gw), pe2, w1bd, w2bd)
    n_cbk = s // CMP_BLOCK
    assert n_cbk <= LANES
    cb = jnp.pad(cb.reshape(bp, n_cbk, 2, N_KV, HEAD_DIM), ((0, 0), (0, LANES - n_cbk), (0, 0), (0, 0), (0, 0)))
    cb = cb.reshape(bp, LANES // 2, 2, 2, N_KV, HEAD_DIM)
    cb = cb.transpose(3, 0, 4, 2, 1, 5).reshape(2, bp, N_KV, LANES, HEAD_DIM)
    kcb = jnp.pad(cb[0], ((0, 0),) * 3 + ((0, LANES - HEAD_DIM),))
    vcb = jnp.concatenate([cb[1], cb[1]], axis=-1)
    tsel = _toeplitz_call(rel_bias, N_TBL + 1, None)
    twin = _toeplitz_call(rel_bias, N_WIN_TILES + 1, WINDOW)
    tcmp = _cmp_bias_call(rel_bias, s // TQ)
    o_att = _nsa_seq_call(q, gates, kaug, v1, kw, vw1, kcb, vcb, tcmp, tsel, twin)
    y_prompt, st_f1 = _ffn_call(x2, mod_p(3), g(1, 2), g(1, 3), w_up[1], ffn_conv_w[1], ffn_conv_b[1][None], w_dn[1], None,
                                tm_ffn, n_chunks, attn=(o_att, mod_p(2), g(1, 1), w_out_b))
    kv_p = rows.reshape(bp, s, 4, N_KV, HEAD_DIM)
    keep = min(WINDOW, s)
    win_p = win[:, s - keep:].reshape(bp, keep, 2, N_KV, HEAD_DIM)
    conv_a_p = st_a[None, :, 6:8]
    ffn_p = jnp.stack([st_f0[:, 6:8], st_f1[:, 6:8]])

    xs = x_sample.reshape(1, bs, d)
    prev_a = (state_conv_a[0, :, 0][None], state_conv_a[0, :, 1][None])
    xs1, v_a = _mixer_call(xs, mod_s(0), g(0, 0), g(0, 1), w_in, a_conv_w[0], a_conv_b[0][None], w_out_a, prev_a, bs)
    prev_f = lambda l: (state_ffn_conv[l, :, 0][None], state_ffn_conv[l, :, 1][None])
    xs2, up0 = _ffn_call(xs1, mod_s(1), g(0, 2), g(0, 3), w_up[0], ffn_conv_w[0], ffn_conv_b[0][None], w_dn[0], prev_f(0), bs, n_chunks)
    rows_s, win_s, q_s, gates_s = _kvq_call(xs2, modkv[bp:n_c][None], mod_s(2), kv_norm_g[None], g(1, 0), w_kv_b, w_qg, bs, False)
    cache_t = cache_kv.transpose(0, 2, 3, 4, 1).reshape(n_pool, 4 * gw, PAGE_SIZE)
    w_len = cache_win.shape[1]
    cwin_t = cache_win.transpose(0, 2, 3, 4, 1).reshape(bs, 2 * gw, w_len)
    cb_pool = _compress_pool_call(cache_t.reshape(n_pool, 4, gw, PAGE_SIZE),
                                  *_compress_pool_weights(cmp_pe, cmp_w1, cmp_w2))
    cb_pool = cb_pool.reshape(N_KV, n_pool, CB_ROWS, LANES)
    assert n_pages * CB_ROWS <= LANES
    d_sel = past_len - np.arange(past_len)
    d_new = np.where(np.arange(LANES) == 0, 0, -1)
    d_win = w_len - np.arange(w_len)
    d_win = np.where(d_win < WINDOW, d_win, -1)
    cl = np.arange(LANES)
    d_cmp = past_len - (((cl // CB_ROWS) * N_PAGE_CB + cl % CB_ROWS) * CMP_BLOCK + CMP_BLOCK - 1)
    d_cmp = np.where((cl % CB_ROWS < N_PAGE_CB) & (cl // CB_ROWS < n_pages), d_cmp, -1)
    assert d_sel.min() >= 0 and d_cmp[d_cmp != -1].min() >= 0
    dist = np.concatenate([d_sel, d_new, d_win, d_cmp]).astype(np.int32)
    bias_cols = _bias_cols_call(jnp.asarray(np.repeat(dist[None, :], N_HEADS, 0)), rel_bias.T)
    bsel, bnew = bias_cols[:, :past_len], bias_cols[:, past_len:past_len + LANES]
    bwin = bias_cols[:, past_len + LANES:past_len + LANES + w_len]
    bcmp = bias_cols[:, past_len + LANES + w_len:]
    head_group = (np.arange(N_HEADS)[:, None] // HPG == np.arange(N_KV)[None, :])[None, :, :, None]
    qrows = jnp.where(head_group, q_s.reshape(bs, N_HEADS, 1, HEAD_DIM), 0.0).reshape(bs, N_HEADS, gw)
    oc, os_, ow, nwin_t = _nsa_step_call(
        page_table, qrows, cb_pool, cache_t, rows_s.reshape(bs, 1, 4 * gw),
        win_s.reshape(bs, 1, 2 * gw), win_s[0].T, cwin_t, bsel, bnew, bwin, bcmp)
    nwin = nwin_t.reshape(bs, 2, N_KV, HEAD_DIM, w_len).transpose(0, 4, 1, 2, 3)
    gts = gates_s[0].reshape(bs, N_KV, LANES)[:, :, :HPG * 3].reshape(bs, N_HEADS, 3)
    branches = []
    for br, o in enumerate((oc, os_, ow)):
        branches += [jnp.repeat(gts[:, :, br], HEAD_DIM, axis=1)[None], _head_diag(o)[None]]
    xs3 = _outproj_call(xs2, mod_s(2), g(1, 1), w_out_b, branches, bs)
    ys, up1 = _ffn_call(xs3, mod_s(3), g(1, 2), g(1, 3), w_up[1], ffn_conv_w[1], ffn_conv_b[1][None], w_dn[1], prev_f(1), bs, n_chunks)
    y_sample = ys.reshape(bs, 1, d)
    kv_s = rows_s.reshape(bs, 1, 4, N_KV, HEAD_DIM)
    win_state_s = nwin
    conv_a_s = jnp.stack([state_conv_a[0, :, 1], v_a[0]], axis=1)[None]
    ffn_s = jnp.stack([jnp.stack([state_ffn_conv[l, :, 1], u[0]], axis=1) for l, u in ((0, up0), (1, up1))])
    return (y_prompt, y_sample, kv_p, kv_s, win_p, win_state_s, conv_a_p, conv_a_s, ffn_p, ffn_s)
```

```python
import functools
import math

import numpy as np
import jax
import jax.numpy as jnp
from jax import lax
from jax.experimental import pallas as pl
from jax.experimental.pallas import tpu as pltpu

F32 = jnp.float32
BF16 = jnp.bfloat16

N_HEADS = 16
HEAD_DIM = 64
N_KV = 4
HPG = N_HEADS // N_KV
CMP_BLOCK = 32
SEL_BLOCK = 64
TOP_N = 16
WINDOW = 512
N_BUCKETS = 32
MAX_EXACT = N_BUCKETS // 2
MAX_DISTANCE = 1024
PAGE_SIZE = 128
EPS = 1e-6

LANES = 128
TQ = 256
TK = 256
ROW_TILE = 512
FFN_COL_CHUNKS = 2
POOL_PAGES_PER_STEP = 64
STEP_SEQS = 2
NEG_MASK = -1e30
NEG_SEL = -1e9
BIG_SCORE = 1e30
VMEM_LIMIT_BYTES = 56 * 1024 * 1024


def _bucket_thresholds():
    d = np.arange(0, 4 * MAX_DISTANCE)
    nf = np.maximum(d, 1).astype(np.float32)
    large = MAX_EXACT + (np.log(nf / MAX_EXACT) / math.log(MAX_DISTANCE / MAX_EXACT)
                         * (N_BUCKETS - MAX_EXACT)).astype(np.int32)
    bucket = np.where(d < MAX_EXACT, d, np.minimum(large, N_BUCKETS - 1))
    assert np.all(np.diff(bucket) >= 0)
    return [int(np.argmax(bucket >= k)) for k in range(N_BUCKETS)]


BUCKET_THR = _bucket_thresholds()
assert TQ == TK
N_TBL = -(-(BUCKET_THR[-1] + TK - 1) // TQ)
N_WIN_TILES = WINDOW // TK + 1


def _cparams(*sem):
    return pltpu.CompilerParams(dimension_semantics=sem, vmem_limit_bytes=VMEM_LIMIT_BYTES)


def _dot(a, b):
    return jnp.dot(a, b, preferred_element_type=F32)


def _dot_nt(a, b):
    return lax.dot_general(a, b, (((1,), (1,)), ((), ())), preferred_element_type=F32)


def _rms(x, g):
    return (x * lax.rsqrt(jnp.mean(x * x, axis=-1, keepdims=True) + EPS)) * g


def _sigmoid(x):
    return 1.0 / (1.0 + jnp.exp(-x))


def _shift_rows(v, carry):
    r1 = pltpu.roll(v, 1, 0)
    r2 = pltpu.roll(v, 2, 0)
    if v.shape[0] > 8:
        row = lax.broadcasted_iota(jnp.int32, (8, v.shape[1]), 0)
        h1 = jnp.where(row == 0, carry[7:8], r1[:8])
        h2 = jnp.where(row == 0, carry[6:7], jnp.where(row == 1, carry[7:8], r2[:8]))
        return jnp.concatenate([h1, r1[8:]], axis=0), jnp.concatenate([h2, r2[8:]], axis=0)
    row = lax.broadcasted_iota(jnp.int32, v.shape, 0)
    return (jnp.where(row == 0, carry[7:8], r1),
            jnp.where(row == 0, carry[6:7], jnp.where(row == 1, carry[7:8], r2)))


def _conv3(v, r1, r2, cw, cb):
    return (cw[0:1] * r2 + cw[1:2] * r1) + cw[2:3] * v + cb


def _mod_kernel(c_ref, w_ref, b_ref, o_ref):
    o_ref[0] = _dot(c_ref[...].astype(BF16), w_ref[0].astype(BF16)) + b_ref[0]


def _mod_call(c_all, w, b):
    n, d, nn = w.shape
    r = c_all.shape[0]
    tn = 1024
    return pl.pallas_call(
        _mod_kernel, grid=(n, nn // tn),
        in_specs=[pl.BlockSpec((r, d), lambda i, j: (0, 0)),
                  pl.BlockSpec((1, d, tn), lambda i, j: (i, 0, j)),
                  pl.BlockSpec((1, 1, tn), lambda i, j: (i, 0, j))],
        out_specs=pl.BlockSpec((1, r, tn), lambda i, j: (i, 0, j)),
        out_shape=jax.ShapeDtypeStruct((n, r, nn), F32),
        compiler_params=_cparams("arbitrary", "arbitrary"), name="mod")(c_all, w, b)


def _mixer_kernel(seq_mode, d, *refs):
    if seq_mode:
        x_ref, mod_ref, g0_ref, g1_ref, win_ref, cw_ref, cb_ref, wout_ref, xo_ref, st_ref, carry = refs
    else:
        x_ref, mod_ref, g0_ref, g1_ref, win_ref, cw_ref, cb_ref, wout_ref, p0_ref, p1_ref, xo_ref, st_ref = refs
    x = x_ref[0]
    m = mod_ref[0]
    h = _rms(x, g0_ref[...]) * (1.0 + m[:, d:2 * d]) + m[:, :d]
    z = _dot(h.astype(BF16), win_ref[...])
    bg, cg, u = z[:, :d], z[:, d:2 * d], z[:, 2 * d:]
    v = cg * u
    if seq_mode:
        @pl.when(pl.program_id(1) == 0)
        def _():
            carry[...] = jnp.zeros_like(carry)
        r1, r2 = _shift_rows(v, carry[...])
        carry[...] = v[-8:]
        st_ref[0] = v[-8:]
    else:
        r1, r2 = p1_ref[0], p0_ref[0]
        st_ref[0] = v
    y = _conv3(v, r1, r2, cw_ref[...], cb_ref[...])
    o = _dot((bg * y).astype(BF16), wout_ref[...])
    xo_ref[0] = x + m[:, 2 * d:] * _rms(o, g1_ref[...])


def _row_specs(bx, s, tm, mod):
    sm = mod.shape[1]
    if sm == 1:
        mod_spec = pl.BlockSpec((1, 1, mod.shape[2]), lambda b, i: (b, 0, 0))
    else:
        mod_spec = pl.BlockSpec((1, tm, mod.shape[2]), lambda b, i: (b, i, 0))
    return mod_spec


def _full(shape):
    nd = len(shape)
    return pl.BlockSpec(shape, lambda b, i, _nd=nd: (0,) * _nd, pipeline_mode=pl.Buffered(1))


def _mixer_call(x, mod, g0, g1, w_in, cw, cb, w_out, prev, tm):
    bx, s, d = x.shape
    seq_mode = prev is None
    row = pl.BlockSpec((1, tm, d), lambda b, i: (b, i, 0))
    in_specs = [row, _row_specs(bx, s, tm, mod), _full((1, d)), _full((1, d)), _full(w_in.shape),
                _full(cw.shape), _full((1, d)), _full(w_out.shape)]
    args = [x, mod, g0, g1, w_in, cw, cb, w_out]
    if seq_mode:
        st_shape, st_spec = (bx, 8, d), pl.BlockSpec((1, 8, d), lambda b, i: (b, 0, 0))
        scratch = [pltpu.VMEM((8, d), F32)]
    else:
        in_specs += [row, row]
        args += [prev[0], prev[1]]
        st_shape, st_spec = (bx, s, d), row
        scratch = []
    return pl.pallas_call(
        functools.partial(_mixer_kernel, seq_mode, d), grid=(bx, s // tm),
        in_specs=in_specs, out_specs=[row, st_spec],
        out_shape=[jax.ShapeDtypeStruct((bx, s, d), F32), jax.ShapeDtypeStruct(st_shape, F32)],
        scratch_shapes=scratch, compiler_params=_cparams("arbitrary", "arbitrary"), name="mixer_a")(*args)


def _ffn_kernel(seq_mode, with_attn, d, dff, n_chunks, *refs):
    if with_attn:
        (o_ref, moda_ref, g1_ref, wout_ref), refs = refs[:4], refs[4:]
    if seq_mode:
        x_ref, mod_ref, g2_ref, g3_ref, wup_ref, cw_ref, cb_ref, wdn_ref, xo_ref, st_ref, carry = refs
    else:
        x_ref, mod_ref, g2_ref, g3_ref, wup_ref, cw_ref, cb_ref, wdn_ref, p0_ref, p1_ref, xo_ref, st_ref = refs
    x = x_ref[0]
    if with_attn:
        x = x + moda_ref[0][:, 2 * d:] * _rms(_dot(o_ref[0].astype(BF16), wout_ref[...]), g1_ref[...])
    m = mod_ref[0]
    h = (_rms(x, g2_ref[...]) * (1.0 + m[:, d:2 * d]) + m[:, :d]).astype(BF16)
    if seq_mode:
        @pl.when(pl.program_id(1) == 0)
        def _():
            carry[...] = jnp.zeros_like(carry)
    cwid = dff // n_chunks
    acc = jnp.zeros((x.shape[0], d), F32)
    for k in range(n_chunks):
        halves = []
        for c0 in (k * cwid, dff + k * cwid):
            up = _dot(h, wup_ref[:, c0:c0 + cwid])
            if seq_mode:
                r1, r2 = _shift_rows(up, carry[:, c0:c0 + cwid])
                carry[:, c0:c0 + cwid] = up[-8:]
                st_ref[0, :, c0:c0 + cwid] = up[-8:]
            else:
                r1, r2 = p1_ref[0, :, c0:c0 + cwid], p0_ref[0, :, c0:c0 + cwid]
                st_ref[0, :, c0:c0 + cwid] = up
            halves.append(_conv3(up, r1, r2, cw_ref[:, c0:c0 + cwid], cb_ref[:, c0:c0 + cwid]))
        u, g = halves
        act = (g * _sigmoid(g)) * u
        acc = acc + _dot(act.astype(BF16), wdn_ref[k * cwid:(k + 1) * cwid, :])
    xo_ref[0] = x + m[:, 2 * d:] * _rms(acc, g3_ref[...])


def _ffn_call(x, mod, g2, g3, w_up, cw, cb, w_dn, prev, tm, n_chunks, attn=None):
    bx, s, d = x.shape
    dff = w_dn.shape[0]
    seq_mode = prev is None
    row = pl.BlockSpec((1, tm, d), lambda b, i: (b, i, 0))
    in_specs = [row, _row_specs(bx, s, tm, mod), _full((1, d)), _full((1, d)), _full(w_up.shape),
                _full(cw.shape), _full((1, 2 * dff)), _full(w_dn.shape)]
    args = [x, mod, g2, g3, w_up, cw, cb, w_dn]
    if attn is not None:
        in_specs = [row, _row_specs(bx, s, tm, attn[1]), _full((1, d)), _full(attn[3].shape)] + in_specs
        args = list(attn) + args
    if seq_mode:
        st_shape, st_spec = (bx, 8, 2 * dff), pl.BlockSpec((1, 8, 2 * dff), lambda b, i: (b, 0, 0))
        scratch = [pltpu.VMEM((8, 2 * dff), F32)]
    else:
        prow = pl.BlockSpec((1, tm, 2 * dff), lambda b, i: (b, i, 0))
        in_specs += [prow, prow]
        args += [prev[0], prev[1]]
        st_shape, st_spec = (bx, s, 2 * dff), prow
        scratch = []
    return pl.pallas_call(
        functools.partial(_ffn_kernel, seq_mode, attn is not None, d, dff, n_chunks), grid=(bx, s // tm),
        in_specs=in_specs, out_specs=[row, st_spec],
        out_shape=[jax.ShapeDtypeStruct((bx, s, d), F32), jax.ShapeDtypeStruct(st_shape, F32)],
        scratch_shapes=scratch, compiler_params=_cparams("arbitrary", "arbitrary"), name="conv_ffn")(*args)


def _kvq_kernel(d, n_rows_cols, emit_attn_kv, x_ref, modkv_ref, mod_ref, gkv_ref, g0_ref, wkv_ref, wqg_ref,
                rows_ref, win_ref, q_ref, gate_ref, *attn_refs):
    x = x_ref[0]
    r = x * lax.rsqrt(jnp.mean(x * x, axis=-1, keepdims=True) + EPS)
    mk = modkv_ref[0]
    hk = (r * gkv_ref[...]) * (1.0 + mk[:, d:]) + mk[:, :d]
    kv = _dot(hk.astype(BF16), wkv_ref[...])
    rows_ref[0] = kv[:, :n_rows_cols]
    win_ref[0] = kv[:, n_rows_cols:]
    if emit_attn_kv:
        kaug_ref, v1_ref, kw_ref, vw1_ref = attn_refs
        tm = x.shape[0]
        gw = N_KV * HEAD_DIM
        lane = lax.broadcasted_iota(jnp.int32, (tm, LANES), 1)
        pos = pl.program_id(1) * tm + lax.broadcasted_iota(jnp.int32, (tm, LANES), 0)
        low = lane < HEAD_DIM
        ones = jnp.where(low, 0.0, 1.0)
        onehot = jnp.where(lane - HEAD_DIM == pos // SEL_BLOCK, 1.0, 0.0)
        for ref, plane, fill in ((kaug_ref, 2, onehot), (v1_ref, 3, ones), (kw_ref, 4, 0.0), (vw1_ref, 5, ones)):
            for g in range(N_KV):
                c0 = plane * gw + (g // 2) * LANES
                blk = kv[:, c0:c0 + LANES]
                if g % 2 == 1:
                    blk = pltpu.roll(blk, HEAD_DIM, 1)
                ref[0, g] = jnp.where(low, blk, fill).astype(BF16)
    m = mod_ref[0]
    h1 = (r * g0_ref[...]) * (1.0 + m[:, d:2 * d]) + m[:, :d]
    qg = _dot(h1.astype(BF16), wqg_ref[...])
    nq = N_HEADS * HEAD_DIM
    q_ref[0] = qg[:, :nq] * (HEAD_DIM ** -0.5)
    gate_ref[0] = _sigmoid(qg[:, nq:])


def _kvq_call(x, modkv, mod, gkv, g0, w_kv, w_qg, tm, emit_attn_kv):
    bx, s, d = x.shape
    nkv = w_kv.shape[1]
    n_rows_cols = 4 * N_KV * HEAD_DIM
    n_win_cols = nkv - n_rows_cols
    nq = N_HEADS * HEAD_DIM
    ng = w_qg.shape[1] - nq
    row = lambda w: pl.BlockSpec((1, tm, w), lambda b, i: (b, i, 0))
    out_specs = [row(n_rows_cols), row(n_win_cols), row(nq), row(ng)]
    out_shape = [jax.ShapeDtypeStruct((bx, s, n_rows_cols), F32), jax.ShapeDtypeStruct((bx, s, n_win_cols), F32),
                 jax.ShapeDtypeStruct((bx, s, nq), F32), jax.ShapeDtypeStruct((bx, s, ng), F32)]
    if emit_attn_kv:
        out_specs += [pl.BlockSpec((1, N_KV, tm, LANES), lambda b, i: (b, 0, i, 0))] * 4
        out_shape += [jax.ShapeDtypeStruct((bx, N_KV, s, LANES), BF16)] * 4
    return pl.pallas_call(
        functools.partial(_kvq_kernel, d, n_rows_cols, emit_attn_kv), grid=(bx, s // tm),
        in_specs=[row(d), _row_specs(bx, s, tm, modkv), _row_specs(bx, s, tm, mod), _full((1, d)), _full((1, d)),
                  _full(w_kv.shape), _full(w_qg.shape)],
        out_specs=out_specs, out_shape=out_shape,
        compiler_params=_cparams("arbitrary", "arbitrary"), name="kv_q_proj")(x, modkv, mod, gkv, g0, w_kv, w_qg)


def _outproj_kernel(d, n_branch, *refs):
    x_ref, mod_ref, g1_ref, w_ref = refs[:4]
    xo_ref = refs[-1]
    if n_branch == 1:
        o = refs[4][0]
    else:
        o = sum(refs[4 + 2 * i][0] * refs[5 + 2 * i][0] for i in range(n_branch))
    y = _dot(o.astype(BF16), w_ref[...])
    xo_ref[0] = x_ref[0] + mod_ref[0][:, 2 * d:] * _rms(y, g1_ref[...])


def _outproj_call(x, mod, g1, w, branches, tm):
    bx, s, d = x.shape
    row = pl.BlockSpec((1, tm, d), lambda b, i: (b, i, 0))
    n_branch = 1 if len(branches) == 1 else len(branches) // 2
    return pl.pallas_call(
        functools.partial(_outproj_kernel, d, n_branch), grid=(bx, s // tm),
        in_specs=[row, _row_specs(bx, s, tm, mod), _full((1, d)), _full(w.shape)] + [row] * len(branches),
        out_specs=row, out_shape=jax.ShapeDtypeStruct((bx, s, d), F32),
        compiler_params=_cparams("arbitrary", "arbitrary"), name="attn_out_proj")(x, mod, g1, w, *branches)


def _gelu_tanh(x):
    return x * (0.5 * (1.0 + jnp.tanh(math.sqrt(2.0 / math.pi) * (x + 0.044715 * (x * x * x)))))


def _compress_kernel(tmb, x_ref, pe_ref, w1_ref, w2_ref, o_ref):
    acc = jnp.zeros((tmb, LANES), F32)
    for r in range(CMP_BLOCK):
        xr = x_ref[pl.ds(r, tmb, stride=CMP_BLOCK), :] + pe_ref[0, r]
        acc = acc + _dot(xr.astype(BF16), w1_ref[0, r])
    o_ref[...] = _dot(_gelu_tanh(acc).astype(BF16), w2_ref[0])


def _compress_call(rows2d, pe2, w1bd, w2bd):
    m = rows2d.shape[0] // CMP_BLOCK
    tmb = max(t for t in range(8, min(256, m) + 1, 8) if m % t == 0)
    return pl.pallas_call(
        functools.partial(_compress_kernel, tmb), grid=(m // tmb, 4),
        in_specs=[pl.BlockSpec((tmb * CMP_BLOCK, LANES), lambda i, j: (i, j)),
                  pl.BlockSpec((1, CMP_BLOCK, 1, LANES), lambda i, j: (j // 2, 0, 0, 0)),
                  pl.BlockSpec((1, CMP_BLOCK, LANES, LANES), lambda i, j: (j // 2, 0, 0, 0)),
                  pl.BlockSpec((1, LANES, LANES), lambda i, j: (j // 2, 0, 0))],
        out_specs=pl.BlockSpec((tmb, LANES), lambda i, j: (i, j)),
        out_shape=jax.ShapeDtypeStruct((m, 4 * LANES), F32),
        compiler_params=_cparams("arbitrary", "arbitrary"), name="compress")(rows2d, pe2, w1bd, w2bd)


def _compress_weights(cmp_pe, cmp_w1, cmp_w2):
    z = jnp.zeros((2, CMP_BLOCK, HEAD_DIM, HEAD_DIM), F32)
    w1 = cmp_w1.reshape(2, CMP_BLOCK, HEAD_DIM, HEAD_DIM)
    w1bd = jnp.concatenate([jnp.concatenate([w1, z], -1), jnp.concatenate([z, w1], -1)], -2).astype(BF16)
    z2 = jnp.zeros((2, HEAD_DIM, HEAD_DIM), F32)
    w2bd = jnp.concatenate([jnp.concatenate([cmp_w2, z2], -1), jnp.concatenate([z2, cmp_w2], -1)], -2).astype(BF16)
    pe2 = jnp.concatenate([cmp_pe, cmp_pe], -1)[:, :, None, :]
    return pe2, w1bd, w2bd


N_PAGE_CB = PAGE_SIZE // CMP_BLOCK
CB_ROWS = 8


PAGE_PITCH = N_KV * HEAD_DIM + 4


def _compress_pool_kernel(n_pg, n_i, x_hbm, pe_ref, mw_ref, w2_ref, o_ref, xbuf, sem):
    gw = N_KV * HEAD_DIM
    t = pl.program_id(0)
    n_t = pl.num_programs(0)
    slot_rows = n_pg * PAGE_PITCH

    def page_copy(step, slot, p):
        return pltpu.make_async_copy(x_hbm.at[(step % n_i) * n_pg + p, step // n_i],
                                     xbuf.at[pl.ds(slot * slot_rows + p * PAGE_PITCH, gw), :], sem.at[slot])

    def start_all(step, slot):
        for p in range(n_pg):
            page_copy(step, slot, p).start()

    slot = t % 2

    @pl.when(t == 0)
    def _():
        start_all(t, slot)

    @pl.when(t + 1 < n_t)
    def _():
        start_all(t + 1, 1 - slot)
    for p in range(n_pg):
        page_copy(t, slot, p).wait()

    def rows_of(g, d):
        return xbuf[pl.ds(slot * slot_rows + g * HEAD_DIM + d, n_pg, stride=PAGE_PITCH), :]
    lhs = jnp.concatenate(
        [(jnp.concatenate([rows_of(g, d) for g in range(N_KV)], axis=0) + pe_ref[0, d]).astype(BF16)
         for d in range(HEAD_DIM)], axis=1)
    z = _dot(_gelu_tanh(_dot(lhs, mw_ref[0])).astype(BF16), w2_ref[0])
    lane = lax.broadcasted_iota(jnp.int32, (n_pg, LANES), 1)
    low = lane < HEAD_DIM
    o_ref[...] = jnp.zeros(o_ref.shape, F32)
    for n in range(N_PAGE_CB):
        for c in range(N_KV // 2):
            a = z[(2 * c) * n_pg:(2 * c + 1) * n_pg, (n // 2) * LANES:(n // 2 + 1) * LANES]
            b = z[(2 * c + 1) * n_pg:(2 * c + 2) * n_pg, (n // 2) * LANES:(n // 2 + 1) * LANES]
            if n % 2 == 0:
                b = pltpu.roll(b, HEAD_DIM, 1)
            else:
                a = pltpu.roll(a, HEAD_DIM, 1)
            o_ref[c, pl.ds(n, n_pg, stride=CB_ROWS), :] = jnp.where(low, a, b)


def _compress_pool_call(cache_t, pe_t, mw, w2bd4):
    n_pool = cache_t.shape[0]
    n_pg = max(t for t in range(8, min(POOL_PAGES_PER_STEP, n_pool) + 1, 8) if n_pool % t == 0)
    n_i = n_pool // n_pg
    return pl.pallas_call(
        functools.partial(_compress_pool_kernel, n_pg, n_i), grid=(2 * n_i,),
        in_specs=[pl.BlockSpec(memory_space=pl.ANY),
                  pl.BlockSpec((1, HEAD_DIM, 1, PAGE_SIZE), lambda t: (t // n_i, 0, 0, 0)),
                  pl.BlockSpec((1, HEAD_DIM * PAGE_SIZE, N_PAGE_CB * HEAD_DIM), lambda t: (t // n_i, 0, 0)),
                  pl.BlockSpec((1, N_PAGE_CB * HEAD_DIM, N_PAGE_CB * HEAD_DIM), lambda t: (t // n_i, 0, 0))],
        out_specs=pl.BlockSpec((N_KV // 2, n_pg * CB_ROWS, LANES), lambda t: (t // n_i, t % n_i, 0)),
        out_shape=jax.ShapeDtypeStruct((N_KV, n_pool * CB_ROWS, LANES), F32),
        scratch_shapes=[pltpu.VMEM((2 * n_pg * PAGE_PITCH, PAGE_SIZE), F32), pltpu.SemaphoreType.DMA((2,))],
        compiler_params=_cparams("arbitrary"), name="compress_pool")(cache_t, pe_t, mw, w2bd4)


def _compress_pool_weights(cmp_pe, cmp_w1, cmp_w2):
    eye = np.eye(N_PAGE_CB, dtype=bool)
    w1t = cmp_w1.reshape(2, CMP_BLOCK, HEAD_DIM, HEAD_DIM).transpose(0, 2, 1, 3).astype(BF16)
    rows = jnp.broadcast_to(w1t[:, :, None], (2, HEAD_DIM, N_PAGE_CB, CMP_BLOCK, HEAD_DIM))
    rows = jnp.tile(rows.reshape(2, HEAD_DIM * PAGE_SIZE, HEAD_DIM), (1, 1, N_PAGE_CB))
    same_block = ((np.arange(HEAD_DIM * PAGE_SIZE) // CMP_BLOCK) % N_PAGE_CB)[:, None] == (
        np.arange(N_PAGE_CB * HEAD_DIM) // HEAD_DIM)[None, :]
    mw = jnp.where(same_block[None], rows, jnp.zeros((), BF16))
    w2bd4 = jnp.where(eye[None, :, None, :, None], cmp_w2[:, None, :, None, :], 0.0)
    w2bd4 = w2bd4.reshape(2, N_PAGE_CB * HEAD_DIM, N_PAGE_CB * HEAD_DIM).astype(BF16)
    pe_t = jnp.tile(cmp_pe.transpose(0, 2, 1), (1, 1, N_PAGE_CB))[:, :, None, :]
    return pe_t, mw, w2bd4


def _bias_of_distance(dist, tab):
    val = jnp.zeros(dist.shape, F32) + tab(0)
    for k in range(1, N_BUCKETS):
        val = jnp.where(dist >= BUCKET_THR[k], tab(k), val)
    return val


def _toeplitz_kernel(window, tab_ref, o_ref):
    g = pl.program_id(0)
    dt = pl.program_id(1)
    x = lax.broadcasted_iota(jnp.int32, (8, 2 * TK), 1)
    dist = dt * TQ - jnp.where(x < TK, x, x - 2 * TK)
    for hh in range(HPG):
        head = g * HPG + hh
        val = _bias_of_distance(dist, lambda k, head=head: tab_ref[k, head])
        val = jnp.where(dist < 0, NEG_MASK, val)
        if window is not None:
            val = jnp.where(dist >= window, NEG_MASK, val)
        tile = pltpu.roll(jnp.broadcast_to(val[0:1], (TQ, 2 * TK)), 0, 1, stride=1, stride_axis=0)
        o_ref[0, 0, hh * TQ:(hh + 1) * TQ, :] = tile[:, :TK]


def _toeplitz_call(rel_bias, n_chunks, window):
    return pl.pallas_call(
        functools.partial(_toeplitz_kernel, window), grid=(N_KV, n_chunks),
        in_specs=[pl.BlockSpec(memory_space=pltpu.SMEM)],
        out_specs=pl.BlockSpec((1, 1, HPG * TQ, TK), lambda g, t: (g, t, 0, 0)),
        out_shape=jax.ShapeDtypeStruct((N_KV, n_chunks, HPG * TQ, TK), F32),
        compiler_params=_cparams("arbitrary", "arbitrary"), name="bias_toeplitz")(rel_bias)


def _cmp_bias_kernel(tab_ref, o_ref):
    g = pl.program_id(0)
    qi = pl.program_id(1)
    sub = 8
    t_q = qi * TQ + lax.broadcasted_iota(jnp.int32, (sub, TQ), 1)
    for r0 in range(0, LANES, sub):
        rows = slice(r0, r0 + sub)
        cb_lo = 2 * (r0 % HEAD_DIM) + r0 // HEAD_DIM
        cb_hi = cb_lo + 2 * (sub - 1)
        d_min = qi * TQ - (cb_hi * CMP_BLOCK + CMP_BLOCK - 1)
        d_max = qi * TQ + TQ - 1 - (cb_lo * CMP_BLOCK + CMP_BLOCK - 1)

        @pl.when(d_max < 0)
        def _(rows=rows):
            o_ref[0, 0, rows, :] = jnp.full((sub, HPG * TQ), NEG_MASK, F32)

        @pl.when(d_min >= BUCKET_THR[-1])
        def _(rows=rows):
            for hh in range(HPG):
                o_ref[0, 0, rows, hh * TQ:(hh + 1) * TQ] = jnp.full((sub, TQ), tab_ref[N_BUCKETS - 1, g * HPG + hh], F32)

        @pl.when((d_max >= 0) & (d_min < BUCKET_THR[-1]))
        def _(rows=rows, cb_lo=cb_lo):
            cblk = cb_lo + 2 * lax.broadcasted_iota(jnp.int32, (sub, TQ), 0)
            dist = t_q - (cblk * CMP_BLOCK + (CMP_BLOCK - 1))
            for hh in range(HPG):
                head = g * HPG + hh
                val = _bias_of_distance(dist, lambda k, head=head: tab_ref[k, head])
                o_ref[0, 0, rows, hh * TQ:(hh + 1) * TQ] = jnp.where(dist < 0, NEG_MASK, val)


def _cmp_bias_call(rel_bias, n_qt):
    return pl.pallas_call(
        _cmp_bias_kernel, grid=(N_KV, n_qt),
        in_specs=[pl.BlockSpec(memory_space=pltpu.SMEM)],
        out_specs=pl.BlockSpec((1, 1, LANES, HPG * TQ), lambda g, t: (g, t, 0, 0)),
        out_shape=jax.ShapeDtypeStruct((N_KV, n_qt, LANES, HPG * TQ), F32),
        compiler_params=_cparams("arbitrary", "arbitrary"), name="bias_cmp")(rel_bias)


def _bias_cols_kernel(dist_ref, tab_ref, o_ref):
    dist = dist_ref[...]
    val = _bias_of_distance(dist, lambda k: tab_ref[:, k:k + 1])
    o_ref[...] = jnp.where(dist < 0, NEG_MASK, val)


def _bias_cols_call(dist, tab_heads):
    r = dist.shape[1]
    return pl.pallas_call(
        _bias_cols_kernel, grid=(1,),
        in_specs=[pl.BlockSpec((N_HEADS, r), lambda i: (0, 0)), pl.BlockSpec((N_HEADS, N_BUCKETS), lambda i: (0, 0))],
        out_specs=pl.BlockSpec((N_HEADS, r), lambda i: (0, 0)),
        out_shape=jax.ShapeDtypeStruct((N_HEADS, r), F32),
        compiler_params=_cparams("arbitrary"), name="bias_cols")(dist, tab_heads)


def _nsa_seq_kernel(n_sb, q_ref, gate_ref, kaug_ref, v1_ref, kw_ref, vw1_ref, kcb_ref, vcb_ref, tcmp_ref,
                    tsel_ref, twin_ref, gexp_ref, o_ref, qaug, s_win, s_sc, mrun, mb, acc_sel, acc_win, score_sc):
    qi = pl.program_id(2)
    rows = HPG * TQ
    lane = lax.broadcasted_iota(jnp.int32, (TQ, LANES), 1)
    low = lane < HEAD_DIM

    qh = []
    for hh in range(HPG):
        qv = q_ref[0, :, (hh // 2) * LANES:(hh // 2 + 1) * LANES]
        if hh % 2 == 1:
            qv = pltpu.roll(qv, HEAD_DIM, 1)
        qh.append(jnp.where(low, qv, 0.0))
    qw = jnp.concatenate(qh, axis=0).astype(BF16)

    m_w = jnp.full((rows, LANES), NEG_MASK, F32)
    for i in range(N_WIN_TILES):
        r0 = pl.multiple_of(jnp.maximum(qi - i, 0) * TK, TK)
        s = _dot_nt(qw, kw_ref[0, 0, pl.ds(r0, TK), :]) + twin_ref[0, jnp.where(qi >= i, i, N_WIN_TILES)]
        s_win[i] = s
        for c in range(TK // LANES):
            m_w = jnp.maximum(m_w, s[:, c * LANES:(c + 1) * LANES])
    mb_w = jnp.broadcast_to(jnp.max(m_w, axis=1, keepdims=True), (rows, LANES))
    mb_w2 = jnp.concatenate([mb_w] * (TK // LANES), axis=1)
    acc_w = jnp.zeros((rows, LANES), F32)
    for i in range(N_WIN_TILES):
        r0 = pl.multiple_of(jnp.maximum(qi - i, 0) * TK, TK)
        acc_w = acc_w + _dot(jnp.exp(s_win[i] - mb_w2).astype(BF16), vw1_ref[0, 0, pl.ds(r0, TK), :])
    acc_win[...] = acc_w

    s_c = _dot_nt(kcb_ref[0, 0].astype(BF16), qw) + tcmp_ref[0, 0]
    ok_c = s_c > 0.5 * NEG_MASK
    mx = jnp.max(s_c, axis=0, keepdims=True)
    p = jnp.where(ok_c, jnp.exp(s_c - mx), 0.0)
    den = jnp.sum(p, axis=0, keepdims=True)
    pn_t = p / jnp.where(den > 0, den, 1.0)
    o_c = _dot(pn_t.T.astype(BF16), vcb_ref[0, 0].astype(BF16))
    imp = pn_t[:, 0:TQ]
    for hh in range(1, HPG):
        imp = imp + pn_t[:, hh * TQ:(hh + 1) * TQ]
    n_blk = LANES // 2
    imp = imp[:n_blk] + imp[n_blk:]

    blk = lax.broadcasted_iota(jnp.int32, (n_blk, TQ), 0)
    cur = (qi * TQ + lax.broadcasted_iota(jnp.int32, (n_blk, TQ), 1)) // SEL_BLOCK
    valid = blk <= cur
    forced = (blk == 0) | (blk == cur) | (blk == cur - 1)
    score_sc[...] = jnp.where(valid & forced, BIG_SCORE, jnp.where(valid, imp, -BIG_SCORE))
    sub = 8
    groups = [score_sc[r0:r0 + sub, :] for r0 in range(0, n_blk, sub)]
    rowg = lax.broadcasted_iota(jnp.int32, (sub, TQ), 0)
    cnts = [jnp.zeros((sub, TQ), jnp.int32) for _ in groups]
    for bp in range(n_blk):
        r = score_sc[bp:bp + 1, :]
        for gi, s_g in enumerate(groups):
            if gi * sub > bp:
                ahead = r >= s_g
            elif (gi + 1) * sub - 1 < bp:
                ahead = r > s_g
            else:
                ahead = (r > s_g) | ((r == s_g) & (rowg + gi * sub > bp))
            cnts[gi] = cnts[gi] + jnp.where(ahead, 1, 0)
    cnt = jnp.concatenate(cnts, axis=0)
    selmask_t = jnp.where((cnt < min(TOP_N, n_sb)) & valid, 0.0, NEG_SEL)
    selmask = jnp.concatenate([jnp.zeros((n_blk, TQ), F32), selmask_t], axis=0).T
    for hh in range(HPG):
        qaug[hh * TQ:(hh + 1) * TQ, :] = jnp.where(low, qh[hh], selmask).astype(BF16)

    def branch(q_sc, k_ref, v_ref, t_ref, acc_sc, n_tiles, n_tbl, widths):
        mrun[...] = jnp.full(mrun.shape, NEG_MASK, F32)

        def tile_loop(step):
            done = 0
            for width in widths:
                def group(j, _, done=done, width=width):
                    step(tuple(done + width * j + u for u in range(width)))
                    return 0
                n_groups = (n_tiles - done) // width
                lax.fori_loop(0, n_groups, group, 0)
                done = done + n_groups * width

        def scores(tiles):
            m = mrun[...]
            for i in tiles:
                r0 = pl.multiple_of((qi - i) * TK, TK)
                s = _dot_nt(q_sc[...], k_ref[0, 0, pl.ds(r0, TK), :]) + t_ref[0, jnp.minimum(i, n_tbl)]
                s_sc[i] = s
                for c in range(TK // LANES):
                    m = jnp.maximum(m, s[:, c * LANES:(c + 1) * LANES])
            mrun[...] = m
        tile_loop(scores)
        mb[...] = jnp.broadcast_to(jnp.max(mrun[...], axis=1, keepdims=True), (rows, LANES))
        acc_sc[...] = jnp.zeros(acc_sc.shape, F32)

        def weigh(tiles):
            mbv = mb[...]
            mb2 = jnp.concatenate([mbv] * (TK // LANES), axis=1)
            acc = acc_sc[...]
            for i in tiles:
                r0 = pl.multiple_of((qi - i) * TK, TK)
                acc = acc + _dot(jnp.exp(s_sc[i] - mb2).astype(BF16), v_ref[0, 0, pl.ds(r0, TK), :])
            acc_sc[...] = acc
        tile_loop(weigh)
        return acc_sc[...]

    acc_s = branch(qaug, kaug_ref, v1_ref, tsel_ref, acc_sel, qi + 1, N_TBL, (8, 4, 2, 1))
    acc_w = acc_win[...]

    gates = gate_ref[0]
    g_hi = gates.astype(BF16)
    g_lo = (gates - g_hi.astype(F32)).astype(BF16)
    g_all = _dot(jnp.concatenate([g_hi, g_lo], axis=1), gexp_ref[...])

    def normalized(acc, odd):
        swapped = pltpu.roll(acc, HEAD_DIM, 1)
        return swapped / acc if odd else acc / swapped

    outs = []
    for hh in range(HPG):
        gb = [g_all[:, (hh * 3 + br) * LANES:(hh * 3 + br + 1) * LANES] for br in range(3)]
        rs = slice(hh * TQ, (hh + 1) * TQ)
        outs.append(gb[0] * o_c[rs] + gb[1] * normalized(acc_s[rs], hh % 2) + gb[2] * normalized(acc_w[rs], hh % 2))
    for c in range(HPG // 2):
        o_ref[0, :, c * LANES:(c + 1) * LANES] = jnp.where(low, outs[2 * c], outs[2 * c + 1]).astype(o_ref.dtype)


def _nsa_seq_call(q, gates, kaug, v1, kw, vw1, kcb, vcb, tcmp, tsel, twin):
    b, s, _ = q.shape
    n_qt = s // TQ
    rows = HPG * TQ
    kv_spec = pl.BlockSpec((1, 1, s, LANES), lambda bb, g, i: (bb, g, 0, 0))
    cb_spec = pl.BlockSpec((1, 1, kcb.shape[2], LANES), lambda bb, g, i: (bb, g, 0, 0))
    tbl_spec = lambda t: pl.BlockSpec((1,) + t.shape[1:], lambda bb, g, i: (g, 0, 0, 0))
    n_gate = HPG * 3
    gexp = np.zeros((2 * LANES, n_gate * LANES), np.float32)
    for c in range(n_gate):
        gexp[c, c * LANES:(c + 1) * LANES] = 1.0
        gexp[LANES + c, c * LANES:(c + 1) * LANES] = 1.0
    gexp = jnp.asarray(gexp, BF16)
    return pl.pallas_call(
        functools.partial(_nsa_seq_kernel, s // SEL_BLOCK), grid=(b, N_KV, n_qt),
        in_specs=[pl.BlockSpec((1, TQ, HPG * HEAD_DIM), lambda bb, g, i: (bb, i, g)),
                  pl.BlockSpec((1, TQ, LANES), lambda bb, g, i: (bb, i, g)),
                  kv_spec, kv_spec, kv_spec, kv_spec, cb_spec, cb_spec,
                  pl.BlockSpec((1, 1, LANES, rows), lambda bb, g, i: (g, i, 0, 0)), tbl_spec(tsel), tbl_spec(twin),
                  pl.BlockSpec(gexp.shape, lambda bb, g, i: (0, 0))],
        out_specs=pl.BlockSpec((1, TQ, HPG * HEAD_DIM), lambda bb, g, i: (bb, i, g)),
        out_shape=jax.ShapeDtypeStruct((b, s, N_HEADS * HEAD_DIM), BF16),
        scratch_shapes=[pltpu.VMEM((rows, LANES), BF16), pltpu.VMEM((N_WIN_TILES, rows, TK), F32),
                        pltpu.VMEM((s // TK, rows, TK), F32), pltpu.VMEM((rows, LANES), F32),
                        pltpu.VMEM((rows, LANES), F32), pltpu.VMEM((rows, LANES), F32),
                        pltpu.VMEM((rows, LANES), F32), pltpu.VMEM((LANES // 2, TQ), F32)],
        compiler_params=_cparams("arbitrary", "arbitrary", "arbitrary"), name="nsa_seq")(
            q, gates, kaug, v1, kw, vw1, kcb, vcb, tcmp, tsel, twin, gexp)


def _softmax_lanes(s):
    p = jnp.exp(s - jnp.max(s, axis=1, keepdims=True))
    return p / jnp.sum(p, axis=1, keepdims=True)


def _rows_to_heads(rows):
    hg = lax.broadcasted_iota(jnp.int32, (N_HEADS, rows[0].shape[1]), 0) // HPG
    out = jnp.broadcast_to(rows[0], hg.shape)
    for g in range(1, N_KV):
        out = jnp.where(hg == g, rows[g], out)
    return out


def _nsa_stepT_kernel(n_pages, n_seq, pt_ref, *refs):
    del pt_ref
    n_pg = n_seq * n_pages
    q_ref = refs[0]
    cb_refs = refs[1:1 + n_pg]
    pg_refs = refs[1 + n_pg:1 + 2 * n_pg]
    (kvnew_ref, winnew_ref, wcol_ref, cwin_ref, bsel_ref, bnew_ref, bwin_ref, bcmp_ref) = refs[1 + 2 * n_pg:9 + 2 * n_pg]
    oc_ref, os_ref, ow_ref, nwin_ref = refs[9 + 2 * n_pg:13 + 2 * n_pg]
    (s_sc,) = refs[13 + 2 * n_pg:]
    for u in range(n_seq):
        one = lambda ref, u=u: ref.at[pl.ds(u, 1)]
        _nsa_step_one(n_pages, pl.program_id(0) * n_seq + u, one(q_ref), cb_refs[u * n_pages:(u + 1) * n_pages],
                      pg_refs[u * n_pages:(u + 1) * n_pages], one(kvnew_ref), one(winnew_ref), wcol_ref, one(cwin_ref),
                      bsel_ref, bnew_ref, bwin_ref, bcmp_ref, one(oc_ref), one(os_ref), one(ow_ref), one(nwin_ref),
                      s_sc.at[u])


def _nsa_step_one(n_pages, b, q_ref, cb_refs, pg_refs, kvnew_ref, winnew_ref, wcol_ref, cwin_ref, bsel_ref, bnew_ref,
                  bwin_ref, bcmp_ref, oc_ref, os_ref, ow_ref, nwin_ref, s_sc):
    gw = N_KV * HEAD_DIM
    n_past = n_pages * PAGE_SIZE
    q16 = q_ref[0].astype(BF16)
    qf = q16.astype(F32)
    lane = lax.broadcasted_iota(jnp.int32, (N_HEADS, LANES), 1)

    n_pad = LANES - CB_ROWS * n_pages
    cb = jnp.concatenate([jnp.concatenate([r[c] for c in range(N_KV)], axis=1) for r in cb_refs]
                         + ([jnp.zeros((n_pad, 2 * gw), F32)] if n_pad else []), axis=0)
    pn_c = _softmax_lanes(_dot_nt(q16, cb[:, :gw].astype(BF16)) + bcmp_ref[...])
    oc_ref[0] = _dot(pn_c.astype(BF16), cb[:, gw:].astype(BF16))

    grp = [pn_c[HPG * g:HPG * g + 1] + pn_c[HPG * g + 1:HPG * g + 2] + pn_c[HPG * g + 2:HPG * g + 3]
           + pn_c[HPG * g + 3:HPG * g + 4] for g in range(N_KV)]
    lane8 = lax.broadcasted_iota(jnp.int32, (8, LANES), 1)
    row8 = lax.broadcasted_iota(jnp.int32, (8, LANES), 0)
    imp = jnp.zeros((8, LANES), F32)
    for g in range(N_KV):
        imp = jnp.where(row8 == g, grp[g], imp)
    imp = imp + pltpu.roll(imp, LANES - 1, 1)
    n_past_blk = n_past // SEL_BLOCK
    cur_lane = LANES - 2
    is_blk = ((lane8 % CB_ROWS == 0) | (lane8 % CB_ROWS == 2)) & (lane8 < CB_ROWS * n_pages)
    last_lane = ((n_past_blk - 1) // 2) * CB_ROWS + 2 * ((n_past_blk - 1) % 2)
    forced = (lane8 == 0) | (lane8 == last_lane) | (lane8 == cur_lane)
    valid = is_blk | (lane8 == cur_lane)
    score = jnp.where(valid & forced, BIG_SCORE, jnp.where(valid, imp, -BIG_SCORE))
    cnt = jnp.zeros((8, LANES), jnp.int32)
    for k in range(2, LANES, 2):
        r = pltpu.roll(score, k, 1)
        cnt = cnt + jnp.where((r > score) | ((r == score) & (lane8 >= k)), 1, 0)
    selrows = jnp.where((cnt < min(TOP_N, n_past_blk + 1)) & valid, 0.0, NEG_SEL)
    selmask = _rows_to_heads([selrows[g:g + 1] for g in range(N_KV)])

    blocks_per_page = PAGE_SIZE // SEL_BLOCK
    for p in range(n_pages):
        kt = pg_refs[p][0, 0:gw, :].astype(BF16)
        msk = selmask[:, CB_ROWS * p:CB_ROWS * p + 1]
        for i in range(1, blocks_per_page):
            msk = jnp.where(lane >= i * SEL_BLOCK, selmask[:, CB_ROWS * p + 2 * i:CB_ROWS * p + 2 * i + 1], msk)
        s_sc[:, p * PAGE_SIZE:(p + 1) * PAGE_SIZE] = _dot(q16, kt) + bsel_ref[:, p * PAGE_SIZE:(p + 1) * PAGE_SIZE] + msk
    knew = kvnew_ref[0][:, 2 * gw:3 * gw].astype(BF16).astype(F32)
    s_new = jnp.sum(qf * knew, axis=1, keepdims=True) + bnew_ref[:, 0:1] + selmask[:, cur_lane:cur_lane + 1]
    s_sc[:, n_past:n_past + LANES] = jnp.where(lane == 0, s_new, NEG_MASK)
    s_all = s_sc[...]
    mx = jnp.max(s_all, axis=1, keepdims=True)
    den = jnp.sum(jnp.exp(s_all - mx), axis=1, keepdims=True)
    acc = jnp.zeros((N_HEADS, gw), F32)
    for p in range(n_pages):
        pn = jnp.exp(s_sc[:, p * PAGE_SIZE:(p + 1) * PAGE_SIZE] - mx) / den
        acc = acc + _dot_nt(pn.astype(BF16), pg_refs[p][0, gw:2 * gw, :].astype(BF16))
    pn_new = jnp.exp(s_new - mx) / den
    vnew = kvnew_ref[0][:, 3 * gw:4 * gw].astype(BF16).astype(F32)
    os_ref[0] = acc + pn_new.astype(BF16).astype(F32) * vnew

    cw = cwin_ref[0]
    w_len = cw.shape[1]
    s_w = _dot(q16, cw[0:gw].astype(BF16)) + bwin_ref[...]
    kwn = winnew_ref[0][:, 0:gw].astype(BF16).astype(F32)
    s_wn = jnp.sum(qf * kwn, axis=1, keepdims=True) + bnew_ref[:, 0:1]
    mxw = jnp.maximum(jnp.max(s_w, axis=1, keepdims=True), s_wn)
    pw = jnp.exp(s_w - mxw)
    pwn = jnp.exp(s_wn - mxw)
    denw = jnp.sum(pw, axis=1, keepdims=True) + pwn
    vwn = winnew_ref[0][:, gw:2 * gw].astype(BF16).astype(F32)
    ow_ref[0] = _dot_nt((pw / denw).astype(BF16), cw[gw:2 * gw].astype(BF16)) + (pwn / denw).astype(BF16).astype(F32) * vwn
    lane_w = lax.broadcasted_iota(jnp.int32, cw.shape, 1)
    nb_l = wcol_ref.shape[1]
    lane_b = lax.broadcasted_iota(jnp.int32, (cw.shape[0], nb_l), 1)
    col = jnp.sum(jnp.where(lane_b == b, wcol_ref[...], 0.0), axis=1, keepdims=True)
    nwin_ref[0] = jnp.where(lane_w == w_len - 1, col, pltpu.roll(cw, w_len - 1, 1))


def _nsa_step_call(page_table, qrows, cb_pool, cache_t, kvnew, winnew, wcol, cwin_t, bsel, bnew, bwin, bcmp):
    nb, n_pages = page_table.shape
    gw = N_KV * HEAD_DIM
    w_len = cwin_t.shape[2]
    n_seq = STEP_SEQS if nb % STEP_SEQS == 0 else 1
    slots = [(u, p) for u in range(n_seq) for p in range(n_pages)]
    cb_specs = [pl.BlockSpec((N_KV, None, CB_ROWS, LANES), lambda b, pt, _u=u, _p=p: (0, pt[b * n_seq + _u, _p], 0, 0))
                for u, p in slots]
    pg_specs = [pl.BlockSpec((1, 2 * gw, PAGE_SIZE), lambda b, pt, _u=u, _p=p: (pt[b * n_seq + _u, _p], 1, 0))
                for u, p in slots]
    const = lambda a: pl.BlockSpec(a.shape, lambda b, pt: (0, 0))
    seq = lambda *dims: pl.BlockSpec((n_seq,) + dims, lambda b, pt: (b, 0, 0))
    grid_spec = pltpu.PrefetchScalarGridSpec(
        num_scalar_prefetch=1, grid=(nb // n_seq,),
        in_specs=[seq(N_HEADS, gw)] + cb_specs + pg_specs + [
            seq(1, 4 * gw), seq(1, 2 * gw), const(wcol), seq(2 * gw, w_len),
            const(bsel), const(bnew), const(bwin), const(bcmp)],
        out_specs=[seq(N_HEADS, gw)] * 3 + [seq(2 * gw, w_len)],
        scratch_shapes=[pltpu.VMEM((n_seq, N_HEADS, n_pages * PAGE_SIZE + LANES), F32)])
    o_shape = jax.ShapeDtypeStruct((nb, N_HEADS, gw), F32)
    return pl.pallas_call(
        functools.partial(_nsa_stepT_kernel, n_pages, n_seq), grid_spec=grid_spec,
        out_shape=[o_shape, o_shape, o_shape, jax.ShapeDtypeStruct((nb, 2 * gw, w_len), F32)],
        compiler_params=_cparams("arbitrary"), name="nsa_step")(
            page_table, qrows, *([cb_pool] * len(slots)), *([cache_t] * len(slots)), kvnew, winnew, wcol, cwin_t,
            bsel, bnew, bwin, bcmp)


def _head_diag(o):
    b = o.shape[0]
    o5 = o.reshape(b, N_KV, HPG, N_KV, HEAD_DIM)
    return jnp.stack([o5[:, g, :, g, :] for g in range(N_KV)], axis=1).reshape(b, N_HEADS * HEAD_DIM)


def kernel(x_prompt, x_sample, c_prompt, c_sample, cache_kv, cache_win, state_conv_a, state_ffn_conv, page_table, mod_w, mod_b, norm_g, a_w_in, a_conv_w, a_conv_b, a_w_out, kv_mod_w, kv_mod_b, kv_norm_g, w_kv, cmp_pe, cmp_w1, cmp_w2, b_w_qg, b_w_out, rel_bias, ffn_w_up, ffn_conv_w, ffn_conv_b, ffn_w_down):
    bp, s, d = x_prompt.shape
    bs = x_sample.shape[0]
    depth = mod_w.shape[0]
    n_a = a_w_in.shape[0]
    assert depth == 2 and n_a == 1 and x_sample.shape[1] == 1
    dff = ffn_w_down.shape[1]
    n_pool = cache_kv.shape[0]
    n_pages = page_table.shape[1]
    past_len = n_pages * PAGE_SIZE
    gw = N_KV * HEAD_DIM
    nq = N_HEADS * HEAD_DIM

    n_c = bp + bs
    n_cp = -(-n_c // 8) * 8
    c_all = jnp.pad(jnp.concatenate([c_prompt, c_sample], 0), ((0, n_cp - n_c), (0, 0)))
    mods = _mod_call(c_all, mod_w.reshape(depth * 2, d, 3 * d), mod_b.reshape(depth * 2, 1, 3 * d))
    modkv = _mod_call(c_all, kv_mod_w[None], kv_mod_b[None, None])[0]
    mod_p = lambda i: mods[i, :bp][:, None, :]
    mod_s = lambda i: mods[i, bp:n_c][None]

    w_in = a_w_in[0].astype(BF16)
    w_out_a = a_w_out[0].astype(BF16)
    w_up = ffn_w_up.astype(BF16)
    w_dn = ffn_w_down.astype(BF16)
    w_kv_b = w_kv.astype(BF16)
    w_gate = jnp.pad(b_w_qg[0][:, nq:].reshape(d, N_KV, HPG * 3), ((0, 0), (0, 0), (0, LANES - HPG * 3)))
    w_qg = jnp.concatenate([b_w_qg[0][:, :nq], w_gate.reshape(d, N_KV * LANES)], axis=1).astype(BF16)
    w_out_b = b_w_out[0].astype(BF16)
    g = lambda l, i: norm_g[l, i][None]
    pe2, w1bd, w2bd = _compress_weights(cmp_pe, cmp_w1, cmp_w2)
    n_chunks = FFN_COL_CHUNKS

    tm = min(ROW_TILE, s)
    x1, st_a = _mixer_call(x_prompt, mod_p(0), g(0, 0), g(0, 1), w_in, a_conv_w[0], a_conv_b[0][None], w_out_a, None, tm)
    x2, st_f0 = _ffn_call(x1, mod_p(1), g(0, 2), g(0, 3), w_up[0], ffn_conv_w[0], ffn_conv_b[0][None], w_dn[0], None, tm, n_chunks)
    rows, win, q, gates, kaug, v1, kw, vw1 = _kvq_call(
        x2, modkv[:bp][:, None, :], mod_p(2), kv_norm_g[None], g(1, 0), w_kv_b, w_qg, tm, True)
    cb = _compress_call(rows.reshape(bp * s, 4 * gw), pe2, w1bd, w2bd)
    n_cbk = s // CMP_BLOCK
    assert n_cbk <= LANES
    cb = jnp.pad(cb.reshape(bp, n_cbk, 2, N_KV, HEAD_DIM), ((0, 0), (0, LANES - n_cbk), (0, 0), (0, 0), (0, 0)))
    cb = cb.reshape(bp, LANES // 2, 2, 2, N_KV, HEAD_DIM)
    cb = cb.transpose(3, 0, 4, 2, 1, 5).reshape(2, bp, N_KV, LANES, HEAD_DIM)
    kcb = jnp.pad(cb[0], ((0, 0),) * 3 + ((0, LANES - HEAD_DIM),))
    vcb = jnp.concatenate([cb[1], cb[1]], axis=-1)
    tsel = _toeplitz_call(rel_bias, N_TBL + 1, None)
    twin = _toeplitz_call(rel_bias, N_WIN_TILES + 1, WINDOW)
    tcmp = _cmp_bias_call(rel_bias, s // TQ)
    o_att = _nsa_seq_call(q, gates, kaug, v1, kw, vw1, kcb, vcb, tcmp, tsel, twin)
    y_prompt, st_f1 = _ffn_call(x2, mod_p(3), g(1, 2), g(1, 3), w_up[1], ffn_conv_w[1], ffn_conv_b[1][None], w_dn[1], None,
                                tm, n_chunks, attn=(o_att, mod_p(2), g(1, 1), w_out_b))
    kv_p = rows.reshape(bp, s, 4, N_KV, HEAD_DIM)
    keep = min(WINDOW, s)
    win_p = win[:, s - keep:].reshape(bp, keep, 2, N_KV, HEAD_DIM)
    conv_a_p = st_a[None, :, 6:8]
    ffn_p = jnp.stack([st_f0[:, 6:8], st_f1[:, 6:8]])

    xs = x_sample.reshape(1, bs, d)
    prev_a = (state_conv_a[0, :, 0][None], state_conv_a[0, :, 1][None])
    xs1, v_a = _mixer_call(xs, mod_s(0), g(0, 0), g(0, 1), w_in, a_conv_w[0], a_conv_b[0][None], w_out_a, prev_a, bs)
    prev_f = lambda l: (state_ffn_conv[l, :, 0][None], state_ffn_conv[l, :, 1][None])
    xs2, up0 = _ffn_call(xs1, mod_s(1), g(0, 2), g(0, 3), w_up[0], ffn_conv_w[0], ffn_conv_b[0][None], w_dn[0], prev_f(0), bs, n_chunks)
    rows_s, win_s, q_s, gates_s = _kvq_call(xs2, modkv[bp:n_c][None], mod_s(2), kv_norm_g[None], g(1, 0), w_kv_b, w_qg, bs, False)
    cache_t = cache_kv.transpose(0, 2, 3, 4, 1).reshape(n_pool, 4 * gw, PAGE_SIZE)
    w_len = cache_win.shape[1]
    cwin_t = cache_win.transpose(0, 2, 3, 4, 1).reshape(bs, 2 * gw, w_len)
    cb_pool = _compress_pool_call(cache_t.reshape(n_pool, 4, gw, PAGE_SIZE),
                                  *_compress_pool_weights(cmp_pe, cmp_w1, cmp_w2))
    cb_pool = cb_pool.reshape(N_KV, n_pool, CB_ROWS, LANES)
    assert n_pages * CB_ROWS <= LANES
    d_sel = past_len - np.arange(past_len)
    d_new = np.where(np.arange(LANES) == 0, 0, -1)
    d_win = w_len - np.arange(w_len)
    d_win = np.where(d_win < WINDOW, d_win, -1)
    cl = np.arange(LANES)
    d_cmp = past_len - (((cl // CB_ROWS) * N_PAGE_CB + cl % CB_ROWS) * CMP_BLOCK + CMP_BLOCK - 1)
    d_cmp = np.where((cl % CB_ROWS < N_PAGE_CB) & (cl // CB_ROWS < n_pages), d_cmp, -1)
    assert d_sel.min() >= 0 and d_cmp[d_cmp != -1].min() >= 0
    dist = np.concatenate([d_sel, d_new, d_win, d_cmp]).astype(np.int32)
    bias_cols = _bias_cols_call(jnp.asarray(np.repeat(dist[None, :], N_HEADS, 0)), rel_bias.T)
    bsel, bnew = bias_cols[:, :past_len], bias_cols[:, past_len:past_len + LANES]
    bwin = bias_cols[:, past_len + LANES:past_len + LANES + w_len]
    bcmp = bias_cols[:, past_len + LANES + w_len:]
    head_group = (np.arange(N_HEADS)[:, None] // HPG == np.arange(N_KV)[None, :])[None, :, :, None]
    qrows = jnp.where(head_group, q_s.reshape(bs, N_HEADS, 1, HEAD_DIM), 0.0).reshape(bs, N_HEADS, gw)
    oc, os_, ow, nwin_t = _nsa_step_call(
        page_table, qrows, cb_pool, cache_t, rows_s.reshape(bs, 1, 4 * gw),
        win_s.reshape(bs, 1, 2 * gw), win_s[0].T, cwin_t, bsel, bnew, bwin, bcmp)
    nwin = nwin_t.reshape(bs, 2, N_KV, HEAD_DIM, w_len).transpose(0, 4, 1, 2, 3)
    gts = gates_s[0].reshape(bs, N_KV, LANES)[:, :, :HPG * 3].reshape(bs, N_HEADS, 3)
    branches = []
    for br, o in enumerate((oc, os_, ow)):
        branches += [jnp.repeat(gts[:, :, br], HEAD_DIM, axis=1)[None], _head_diag(o)[None]]
    xs3 = _outproj_call(xs2, mod_s(2), g(1, 1), w_out_b, branches, bs)
    ys, up1 = _ffn_call(xs3, mod_s(3), g(1, 2), g(1, 3), w_up[1], ffn_conv_w[1], ffn_conv_b[1][None], w_dn[1], prev_f(1), bs, n_chunks)
    y_sample = ys.reshape(bs, 1, d)
    kv_s = rows_s.reshape(bs, 1, 4, N_KV, HEAD_DIM)
    win_state_s = nwin
    conv_a_s = jnp.stack([state_conv_a[0, :, 1], v_a[0]], axis=1)[None]
    ffn_s = jnp.stack([jnp.stack([state_ffn_conv[l, :, 1], u[0]], axis=1) for l, u in ((0, up0), (1, up1))])
    return (y_prompt, y_sample, kv_p, kv_s, win_p, win_state_s, conv_a_p, conv_a_s, ffn_p, ffn_s)
```

```python
import functools
import math

import numpy as np
import jax
import jax.numpy as jnp
from jax import lax
from jax.experimental import pallas as pl
from jax.experimental.pallas import tpu as pltpu

F32 = jnp.float32
BF16 = jnp.bfloat16

N_HEADS = 16
HEAD_DIM = 64
N_KV = 4
HPG = N_HEADS // N_KV
CMP_BLOCK = 32
SEL_BLOCK = 64
TOP_N = 16
WINDOW = 512
N_BUCKETS = 32
MAX_EXACT = N_BUCKETS // 2
MAX_DISTANCE = 1024
PAGE_SIZE = 128
EPS = 1e-6

LANES = 128
TQ = 256
TK = 256
ROW_TILE = 512
FFN_COL_CHUNKS = 2
POOL_PAGES_PER_STEP = 64
STEP_SEQS = 2
NEG_MASK = -1e30
NEG_SEL = -1e9
BIG_SCORE = 1e30
VMEM_LIMIT_BYTES = 56 * 1024 * 1024


def _bucket_thresholds():
    d = np.arange(0, 4 * MAX_DISTANCE)
    nf = np.maximum(d, 1).astype(np.float32)
    large = MAX_EXACT + (np.log(nf / MAX_EXACT) / math.log(MAX_DISTANCE / MAX_EXACT)
                         * (N_BUCKETS - MAX_EXACT)).astype(np.int32)
    bucket = np.where(d < MAX_EXACT, d, np.minimum(large, N_BUCKETS - 1))
    assert np.all(np.diff(bucket) >= 0)
    return [int(np.argmax(bucket >= k)) for k in range(N_BUCKETS)]


BUCKET_THR = _bucket_thresholds()
assert TQ == TK
N_TBL = -(-(BUCKET_THR[-1] + TK - 1) // TQ)
N_WIN_TILES = WINDOW // TK + 1


def _cparams(*sem):
    return pltpu.CompilerParams(dimension_semantics=sem, vmem_limit_bytes=VMEM_LIMIT_BYTES)


def _dot(a, b):
    return jnp.dot(a, b, preferred_element_type=F32)


def _dot_nt(a, b):
    return lax.dot_general(a, b, (((1,), (1,)), ((), ())), preferred_element_type=F32)


def _rms(x, g):
    return (x * lax.rsqrt(jnp.mean(x * x, axis=-1, keepdims=True) + EPS)) * g


def _sigmoid(x):
    return 1.0 / (1.0 + jnp.exp(-x))


def _shift_rows(v, carry):
    r1 = pltpu.roll(v, 1, 0)
    r2 = pltpu.roll(v, 2, 0)
    if v.shape[0] > 8:
        row = lax.broadcasted_iota(jnp.int32, (8, v.shape[1]), 0)
        h1 = jnp.where(row == 0, carry[7:8], r1[:8])
        h2 = jnp.where(row == 0, carry[6:7], jnp.where(row == 1, carry[7:8], r2[:8]))
        return jnp.concatenate([h1, r1[8:]], axis=0), jnp.concatenate([h2, r2[8:]], axis=0)
    row = lax.broadcasted_iota(jnp.int32, v.shape, 0)
    return (jnp.where(row == 0, carry[7:8], r1),
            jnp.where(row == 0, carry[6:7], jnp.where(row == 1, carry[7:8], r2)))


def _conv3(v, r1, r2, cw, cb):
    return (cw[0:1] * r2 + cw[1:2] * r1) + cw[2:3] * v + cb


def _mod_kernel(c_ref, w_ref, b_ref, o_ref):
    o_ref[0] = _dot(c_ref[...].astype(BF16), w_ref[0].astype(BF16)) + b_ref[0]


def _mod_call(c_all, w, b):
    n, d, nn = w.shape
    r = c_all.shape[0]
    tn = 1024
    return pl.pallas_call(
        _mod_kernel, grid=(n, nn // tn),
        in_specs=[pl.BlockSpec((r, d), lambda i, j: (0, 0)),
                  pl.BlockSpec((1, d, tn), lambda i, j: (i, 0, j)),
                  pl.BlockSpec((1, 1, tn), lambda i, j: (i, 0, j))],
        out_specs=pl.BlockSpec((1, r, tn), lambda i, j: (i, 0, j)),
        out_shape=jax.ShapeDtypeStruct((n, r, nn), F32),
        compiler_params=_cparams("arbitrary", "arbitrary"), name="mod")(c_all, w, b)


def _mixer_kernel(seq_mode, d, *refs):
    if seq_mode:
        x_ref, mod_ref, g0_ref, g1_ref, win_ref, cw_ref, cb_ref, wout_ref, xo_ref, st_ref, carry = refs
    else:
        x_ref, mod_ref, g0_ref, g1_ref, win_ref, cw_ref, cb_ref, wout_ref, p0_ref, p1_ref, xo_ref, st_ref = refs
    x = x_ref[0]
    m = mod_ref[0]
    h = _rms(x, g0_ref[...]) * (1.0 + m[:, d:2 * d]) + m[:, :d]
    z = _dot(h.astype(BF16), win_ref[...])
    bg, cg, u = z[:, :d], z[:, d:2 * d], z[:, 2 * d:]
    v = cg * u
    if seq_mode:
        @pl.when(pl.program_id(1) == 0)
        def _():
            carry[...] = jnp.zeros_like(carry)
        r1, r2 = _shift_rows(v, carry[...])
        carry[...] = v[-8:]
        st_ref[0] = v[-8:]
    else:
        r1, r2 = p1_ref[0], p0_ref[0]
        st_ref[0] = v
    y = _conv3(v, r1, r2, cw_ref[...], cb_ref[...])
    o = _dot((bg * y).astype(BF16), wout_ref[...])
    xo_ref[0] = x + m[:, 2 * d:] * _rms(o, g1_ref[...])


def _row_specs(bx, s, tm, mod):
    sm = mod.shape[1]
    if sm == 1:
        mod_spec = pl.BlockSpec((1, 1, mod.shape[2]), lambda b, i: (b, 0, 0))
    else:
        mod_spec = pl.BlockSpec((1, tm, mod.shape[2]), lambda b, i: (b, i, 0))
    return mod_spec


def _full(shape):
    nd = len(shape)
    return pl.BlockSpec(shape, lambda b, i, _nd=nd: (0,) * _nd, pipeline_mode=pl.Buffered(1))


def _mixer_call(x, mod, g0, g1, w_in, cw, cb, w_out, prev, tm):
    bx, s, d = x.shape
    seq_mode = prev is None
    row = pl.BlockSpec((1, tm, d), lambda b, i: (b, i, 0))
    in_specs = [row, _row_specs(bx, s, tm, mod), _full((1, d)), _full((1, d)), _full(w_in.shape),
                _full(cw.shape), _full((1, d)), _full(w_out.shape)]
    args = [x, mod, g0, g1, w_in, cw, cb, w_out]
    if seq_mode:
        st_shape, st_spec = (bx, 8, d), pl.BlockSpec((1, 8, d), lambda b, i: (b, 0, 0))
        scratch = [pltpu.VMEM((8, d), F32)]
    else:
        in_specs += [row, row]
        args += [prev[0], prev[1]]
        st_shape, st_spec = (bx, s, d), row
        scratch = []
    return pl.pallas_call(
        functools.partial(_mixer_kernel, seq_mode, d), grid=(bx, s // tm),
        in_specs=in_specs, out_specs=[row, st_spec],
        out_shape=[jax.ShapeDtypeStruct((bx, s, d), F32), jax.ShapeDtypeStruct(st_shape, F32)],
        scratch_shapes=scratch, compiler_params=_cparams("arbitrary", "arbitrary"), name="mixer_a")(*args)


def _ffn_kernel(seq_mode, with_attn, d, dff, n_chunks, *refs):
    if with_attn:
        (o_ref, moda_ref, g1_ref, wout_ref), refs = refs[:4], refs[4:]
    if seq_mode:
        x_ref, mod_ref, g2_ref, g3_ref, wup_ref, cw_ref, cb_ref, wdn_ref, xo_ref, st_ref, carry = refs
    else:
        x_ref, mod_ref, g2_ref, g3_ref, wup_ref, cw_ref, cb_ref, wdn_ref, p0_ref, p1_ref, xo_ref, st_ref = refs
    x = x_ref[0]
    if with_attn:
        x = x + moda_ref[0][:, 2 * d:] * _rms(_dot(o_ref[0].astype(BF16), wout_ref[...]), g1_ref[...])
    m = mod_ref[0]
    h = (_rms(x, g2_ref[...]) * (1.0 + m[:, d:2 * d]) + m[:, :d]).astype(BF16)
    if seq_mode:
        @pl.when(pl.program_id(1) == 0)
        def _():
            carry[...] = jnp.zeros_like(carry)
    cwid = dff // n_chunks
    acc = jnp.zeros((x.shape[0], d), F32)
    for k in range(n_chunks):
        halves = []
        for c0 in (k * cwid, dff + k * cwid):
            up = _dot(h, wup_ref[:, c0:c0 + cwid])
            if seq_mode:
                r1, r2 = _shift_rows(up, carry[:, c0:c0 + cwid])
                carry[:, c0:c0 + cwid] = up[-8:]
                st_ref[0, :, c0:c0 + cwid] = up[-8:]
            else:
                r1, r2 = p1_ref[0, :, c0:c0 + cwid], p0_ref[0, :, c0:c0 + cwid]
                st_ref[0, :, c0:c0 + cwid] = up
            halves.append(_conv3(up, r1, r2, cw_ref[:, c0:c0 + cwid], cb_ref[:, c0:c0 + cwid]))
        u, g = halves
        act = (g * _sigmoid(g)) * u
        acc = acc + _dot(act.astype(BF16), wdn_ref[k * cwid:(k + 1) * cwid, :])
    xo_ref[0] = x + m[:, 2 * d:] * _rms(acc, g3_ref[...])


def _ffn_call(x, mod, g2, g3, w_up, cw, cb, w_dn, prev, tm, n_chunks, attn=None):
    bx, s, d = x.shape
    dff = w_dn.shape[0]
    seq_mode = prev is None
    row = pl.BlockSpec((1, tm, d), lambda b, i: (b, i, 0))
    in_specs = [row, _row_specs(bx, s, tm, mod), _full((1, d)), _full((1, d)), _full(w_up.shape),
                _full(cw.shape), _full((1, 2 * dff)), _full(w_dn.shape)]
    args = [x, mod, g2, g3, w_up, cw, cb, w_dn]
    if attn is not None:
        in_specs = [row, _row_specs(bx, s, tm, attn[1]), _full((1, d)), _full(attn[3].shape)] + in_specs
        args = list(attn) + args
    if seq_mode:
        st_shape, st_spec = (bx, 8, 2 * dff), pl.BlockSpec((1, 8, 2 * dff), lambda b, i: (b, 0, 0))
        scratch = [pltpu.VMEM((8, 2 * dff), F32)]
    else:
        prow = pl.BlockSpec((1, tm, 2 * dff), lambda b, i: (b, i, 0))
        in_specs += [prow, prow]
        args += [prev[0], prev[1]]
        st_shape, st_spec = (bx, s, 2 * dff), prow
        scratch = []
    return pl.pallas_call(
        functools.partial(_ffn_kernel, seq_mode, attn is not None, d, dff, n_chunks), grid=(bx, s // tm),
        in_specs=in_specs, out_specs=[row, st_spec],
        out_shape=[jax.ShapeDtypeStruct((bx, s, d), F32), jax.ShapeDtypeStruct(st_shape, F32)],
        scratch_shapes=scratch, compiler_params=_cparams("arbitrary", "arbitrary"), name="conv_ffn")(*args)


def _kvq_kernel(d, n_rows_cols, emit_attn_kv, x_ref, modkv_ref, mod_ref, gkv_ref, g0_ref, wkv_ref, wqg_ref,
                rows_ref, win_ref, q_ref, gate_ref, *attn_refs):
    x = x_ref[0]
    r = x * lax.rsqrt(jnp.mean(x * x, axis=-1, keepdims=True) + EPS)
    mk = modkv_ref[0]
    hk = (r * gkv_ref[...]) * (1.0 + mk[:, d:]) + mk[:, :d]
    kv = _dot(hk.astype(BF16), wkv_ref[...])
    rows_ref[0] = kv[:, :n_rows_cols]
    win_ref[0] = kv[:, n_rows_cols:]
    if emit_attn_kv:
        kaug_ref, v1_ref, kw_ref, vw1_ref = attn_refs
        tm = x.shape[0]
        gw = N_KV * HEAD_DIM
        lane = lax.broadcasted_iota(jnp.int32, (tm, LANES), 1)
        pos = pl.program_id(1) * tm + lax.broadcasted_iota(jnp.int32, (tm, LANES), 0)
        low = lane < HEAD_DIM
        ones = jnp.where(low, 0.0, 1.0)
        onehot = jnp.where(lane - HEAD_DIM == pos // SEL_BLOCK, 1.0, 0.0)
        for ref, plane, fill in ((kaug_ref, 2, onehot), (v1_ref, 3, ones), (kw_ref, 4, 0.0), (vw1_ref, 5, ones)):
            for g in range(N_KV):
                c0 = plane * gw + (g // 2) * LANES
                blk = kv[:, c0:c0 + LANES]
                if g % 2 == 1:
                    blk = pltpu.roll(blk, HEAD_DIM, 1)
                ref[0, g] = jnp.where(low, blk, fill).astype(BF16)
    m = mod_ref[0]
    h1 = (r * g0_ref[...]) * (1.0 + m[:, d:2 * d]) + m[:, :d]
    qg = _dot(h1.astype(BF16), wqg_ref[...])
    nq = N_HEADS * HEAD_DIM
    q_ref[0] = qg[:, :nq] * (HEAD_DIM ** -0.5)
    gate_ref[0] = _sigmoid(qg[:, nq:])


def _kvq_call(x, modkv, mod, gkv, g0, w_kv, w_qg, tm, emit_attn_kv):
    bx, s, d = x.shape
    nkv = w_kv.shape[1]
    n_rows_cols = 4 * N_KV * HEAD_DIM
    n_win_cols = nkv - n_rows_cols
    nq = N_HEADS * HEAD_DIM
    ng = w_qg.shape[1] - nq
    row = lambda w: pl.BlockSpec((1, tm, w), lambda b, i: (b, i, 0))
    out_specs = [row(n_rows_cols), row(n_win_cols), row(nq), row(ng)]
    out_shape = [jax.ShapeDtypeStruct((bx, s, n_rows_cols), F32), jax.ShapeDtypeStruct((bx, s, n_win_cols), F32),
                 jax.ShapeDtypeStruct((bx, s, nq), F32), jax.ShapeDtypeStruct((bx, s, ng), F32)]
    if emit_attn_kv:
        out_specs += [pl.BlockSpec((1, N_KV, tm, LANES), lambda b, i: (b, 0, i, 0))] * 4
        out_shape += [jax.ShapeDtypeStruct((bx, N_KV, s, LANES), BF16)] * 4
    return pl.pallas_call(
        functools.partial(_kvq_kernel, d, n_rows_cols, emit_attn_kv), grid=(bx, s // tm),
        in_specs=[row(d), _row_specs(bx, s, tm, modkv), _row_specs(bx, s, tm, mod), _full((1, d)), _full((1, d)),
                  _full(w_kv.shape), _full(w_qg.shape)],
        out_specs=out_specs, out_shape=out_shape,
        compiler_params=_cparams("arbitrary", "arbitrary"), name="kv_q_proj")(x, modkv, mod, gkv, g0, w_kv, w_qg)


def _outproj_kernel(d, n_branch, *refs):
    x_ref, mod_ref, g1_ref, w_ref = refs[:4]
    xo_ref = refs[-1]
    if n_branch == 1:
        o = refs[4][0]
    else:
        o = sum(refs[4 + 2 * i][0] * refs[5 + 2 * i][0] for i in range(n_branch))
    y = _dot(o.astype(BF16), w_ref[...])
    xo_ref[0] = x_ref[0] + mod_ref[0][:, 2 * d:] * _rms(y, g1_ref[...])


def _outproj_call(x, mod, g1, w, branches, tm):
    bx, s, d = x.shape
    row = pl.BlockSpec((1, tm, d), lambda b, i: (b, i, 0))
    n_branch = 1 if len(branches) == 1 else len(branches) // 2
    return pl.pallas_call(
        functools.partial(_outproj_kernel, d, n_branch), grid=(bx, s // tm),
        in_specs=[row, _row_specs(bx, s, tm, mod), _full((1, d)), _full(w.shape)] + [row] * len(branches),
        out_specs=row, out_shape=jax.ShapeDtypeStruct((bx, s, d), F32),
        compiler_params=_cparams("arbitrary", "arbitrary"), name="attn_out_proj")(x, mod, g1, w, *branches)


def _gelu_tanh(x):
    return x * (0.5 * (1.0 + jnp.tanh(math.sqrt(2.0 / math.pi) * (x + 0.044715 * (x * x * x)))))


def _compress_kernel(tmb, x_ref, pe_ref, w1_ref, w2_ref, o_ref):
    acc = jnp.zeros((tmb, LANES), F32)
    for r in range(CMP_BLOCK):
        xr = x_ref[pl.ds(r, tmb, stride=CMP_BLOCK), :] + pe_ref[0, r]
        acc = acc + _dot(xr.astype(BF16), w1_ref[0, r])
    o_ref[...] = _dot(_gelu_tanh(acc).astype(BF16), w2_ref[0])


def _compress_call(rows2d, pe2, w1bd, w2bd):
    m = rows2d.shape[0] // CMP_BLOCK
    tmb = max(t for t in range(8, min(256, m) + 1, 8) if m % t == 0)
    return pl.pallas_call(
        functools.partial(_compress_kernel, tmb), grid=(m // tmb, 4),
        in_specs=[pl.BlockSpec((tmb * CMP_BLOCK, LANES), lambda i, j: (i, j)),
                  pl.BlockSpec((1, CMP_BLOCK, 1, LANES), lambda i, j: (j // 2, 0, 0, 0)),
                  pl.BlockSpec((1, CMP_BLOCK, LANES, LANES), lambda i, j: (j // 2, 0, 0, 0)),
                  pl.BlockSpec((1, LANES, LANES), lambda i, j: (j // 2, 0, 0))],
        out_specs=pl.BlockSpec((tmb, LANES), lambda i, j: (i, j)),
        out_shape=jax.ShapeDtypeStruct((m, 4 * LANES), F32),
        compiler_params=_cparams("arbitrary", "arbitrary"), name="compress")(rows2d, pe2, w1bd, w2bd)


def _compress_weights(cmp_pe, cmp_w1, cmp_w2):
    z = jnp.zeros((2, CMP_BLOCK, HEAD_DIM, HEAD_DIM), F32)
    w1 = cmp_w1.reshape(2, CMP_BLOCK, HEAD_DIM, HEAD_DIM)
    w1bd = jnp.concatenate([jnp.concatenate([w1, z], -1), jnp.concatenate([z, w1], -1)], -2).astype(BF16)
    z2 = jnp.zeros((2, HEAD_DIM, HEAD_DIM), F32)
    w2bd = jnp.concatenate([jnp.concatenate([cmp_w2, z2], -1), jnp.concatenate([z2, cmp_w2], -1)], -2).astype(BF16)
    pe2 = jnp.concatenate([cmp_pe, cmp_pe], -1)[:, :, None, :]
    return pe2, w1bd, w2bd


N_PAGE_CB = PAGE_SIZE // CMP_BLOCK
CB_ROWS = 8


PAGE_PITCH = N_KV * HEAD_DIM + 4


def _compress_pool_kernel(n_pg, n_i, x_hbm, pe_ref, mw_ref, w2_ref, o_ref, xbuf, sem):
    gw = N_KV * HEAD_DIM
    t = pl.program_id(0)
    n_t = pl.num_programs(0)
    slot_rows = n_pg * PAGE_PITCH

    def page_copy(step, slot, p):
        return pltpu.make_async_copy(x_hbm.at[(step % n_i) * n_pg + p, step // n_i],
                                     xbuf.at[pl.ds(slot * slot_rows + p * PAGE_PITCH, gw), :], sem.at[slot])

    def start_all(step, slot):
        for p in range(n_pg):
            page_copy(step, slot, p).start(priority=p % 2)

    slot = t % 2

    @pl.when(t == 0)
    def _():
        start_all(t, slot)

    @pl.when(t + 1 < n_t)
    def _():
        start_all(t + 1, 1 - slot)
    for p in range(n_pg):
        page_copy(t, slot, p).wait()

    def rows_of(g, d):
        return xbuf[pl.ds(slot * slot_rows + g * HEAD_DIM + d, n_pg, stride=PAGE_PITCH), :]
    lhs = jnp.concatenate(
        [(jnp.concatenate([rows_of(g, d) for g in range(N_KV)], axis=0) + pe_ref[0, d]).astype(BF16)
         for d in range(HEAD_DIM)], axis=1)
    z = _dot(_gelu_tanh(_dot(lhs, mw_ref[0])).astype(BF16), w2_ref[0])
    lane = lax.broadcasted_iota(jnp.int32, (n_pg, LANES), 1)
    low = lane < HEAD_DIM
    o_ref[...] = jnp.zeros(o_ref.shape, F32)
    for n in range(N_PAGE_CB):
        for c in range(N_KV // 2):
            a = z[(2 * c) * n_pg:(2 * c + 1) * n_pg, (n // 2) * LANES:(n // 2 + 1) * LANES]
            b = z[(2 * c + 1) * n_pg:(2 * c + 2) * n_pg, (n // 2) * LANES:(n // 2 + 1) * LANES]
            if n % 2 == 0:
                b = pltpu.roll(b, HEAD_DIM, 1)
            else:
                a = pltpu.roll(a, HEAD_DIM, 1)
            o_ref[c, pl.ds(n, n_pg, stride=CB_ROWS), :] = jnp.where(low, a, b)


def _compress_pool_call(cache_t, pe_t, mw, w2bd4):
    n_pool = cache_t.shape[0]
    n_pg = max(t for t in range(8, min(POOL_PAGES_PER_STEP, n_pool) + 1, 8) if n_pool % t == 0)
    n_i = n_pool // n_pg
    return pl.pallas_call(
        functools.partial(_compress_pool_kernel, n_pg, n_i), grid=(2 * n_i,),
        in_specs=[pl.BlockSpec(memory_space=pl.ANY),
                  pl.BlockSpec((1, HEAD_DIM, 1, PAGE_SIZE), lambda t: (t // n_i, 0, 0, 0)),
                  pl.BlockSpec((1, HEAD_DIM * PAGE_SIZE, N_PAGE_CB * HEAD_DIM), lambda t: (t // n_i, 0, 0)),
                  pl.BlockSpec((1, N_PAGE_CB * HEAD_DIM, N_PAGE_CB * HEAD_DIM), lambda t: (t // n_i, 0, 0))],
        out_specs=pl.BlockSpec((N_KV // 2, n_pg * CB_ROWS, LANES), lambda t: (t // n_i, t % n_i, 0)),
        out_shape=jax.ShapeDtypeStruct((N_KV, n_pool * CB_ROWS, LANES), F32),
        scratch_shapes=[pltpu.VMEM((2 * n_pg * PAGE_PITCH, PAGE_SIZE), F32), pltpu.SemaphoreType.DMA((2,))],
        compiler_params=_cparams("arbitrary"), name="compress_pool")(cache_t, pe_t, mw, w2bd4)


def _compress_pool_weights(cmp_pe, cmp_w1, cmp_w2):
    eye = np.eye(N_PAGE_CB, dtype=bool)
    w1t = cmp_w1.reshape(2, CMP_BLOCK, HEAD_DIM, HEAD_DIM).transpose(0, 2, 1, 3).astype(BF16)
    rows = jnp.broadcast_to(w1t[:, :, None], (2, HEAD_DIM, N_PAGE_CB, CMP_BLOCK, HEAD_DIM))
    rows = jnp.tile(rows.reshape(2, HEAD_DIM * PAGE_SIZE, HEAD_DIM), (1, 1, N_PAGE_CB))
    same_block = ((np.arange(HEAD_DIM * PAGE_SIZE) // CMP_BLOCK) % N_PAGE_CB)[:, None] == (
        np.arange(N_PAGE_CB * HEAD_DIM) // HEAD_DIM)[None, :]
    mw = jnp.where(same_block[None], rows, jnp.zeros((), BF16))
    w2bd4 = jnp.where(eye[None, :, None, :, None], cmp_w2[:, None, :, None, :], 0.0)
    w2bd4 = w2bd4.reshape(2, N_PAGE_CB * HEAD_DIM, N_PAGE_CB * HEAD_DIM).astype(BF16)
    pe_t = jnp.tile(cmp_pe.transpose(0, 2, 1), (1, 1, N_PAGE_CB))[:, :, None, :]
    return pe_t, mw, w2bd4


def _bias_of_distance(dist, tab):
    val = jnp.zeros(dist.shape, F32) + tab(0)
    for k in range(1, N_BUCKETS):
        val = jnp.where(dist >= BUCKET_THR[k], tab(k), val)
    return val


def _toeplitz_kernel(window, tab_ref, o_ref):
    g = pl.program_id(0)
    dt = pl.program_id(1)
    x = lax.broadcasted_iota(jnp.int32, (8, 2 * TK), 1)
    dist = dt * TQ - jnp.where(x < TK, x, x - 2 * TK)
    for hh in range(HPG):
        head = g * HPG + hh
        val = _bias_of_distance(dist, lambda k, head=head: tab_ref[k, head])
        val = jnp.where(dist < 0, NEG_MASK, val)
        if window is not None:
            val = jnp.where(dist >= window, NEG_MASK, val)
        tile = pltpu.roll(jnp.broadcast_to(val[0:1], (TQ, 2 * TK)), 0, 1, stride=1, stride_axis=0)
        o_ref[0, 0, hh * TQ:(hh + 1) * TQ, :] = tile[:, :TK]


def _toeplitz_call(rel_bias, n_chunks, window):
    return pl.pallas_call(
        functools.partial(_toeplitz_kernel, window), grid=(N_KV, n_chunks),
        in_specs=[pl.BlockSpec(memory_space=pltpu.SMEM)],
        out_specs=pl.BlockSpec((1, 1, HPG * TQ, TK), lambda g, t: (g, t, 0, 0)),
        out_shape=jax.ShapeDtypeStruct((N_KV, n_chunks, HPG * TQ, TK), F32),
        compiler_params=_cparams("arbitrary", "arbitrary"), name="bias_toeplitz")(rel_bias)


def _cmp_bias_kernel(tab_ref, o_ref):
    g = pl.program_id(0)
    qi = pl.program_id(1)
    sub = 8
    t_q = qi * TQ + lax.broadcasted_iota(jnp.int32, (sub, TQ), 1)
    for r0 in range(0, LANES, sub):
        rows = slice(r0, r0 + sub)
        cb_lo = 2 * (r0 % HEAD_DIM) + r0 // HEAD_DIM
        cb_hi = cb_lo + 2 * (sub - 1)
        d_min = qi * TQ - (cb_hi * CMP_BLOCK + CMP_BLOCK - 1)
        d_max = qi * TQ + TQ - 1 - (cb_lo * CMP_BLOCK + CMP_BLOCK - 1)

        @pl.when(d_max < 0)
        def _(rows=rows):
            o_ref[0, 0, rows, :] = jnp.full((sub, HPG * TQ), NEG_MASK, F32)

        @pl.when(d_min >= BUCKET_THR[-1])
        def _(rows=rows):
            for hh in range(HPG):
                o_ref[0, 0, rows, hh * TQ:(hh + 1) * TQ] = jnp.full((sub, TQ), tab_ref[N_BUCKETS - 1, g * HPG + hh], F32)

        @pl.when((d_max >= 0) & (d_min < BUCKET_THR[-1]))
        def _(rows=rows, cb_lo=cb_lo):
            cblk = cb_lo + 2 * lax.broadcasted_iota(jnp.int32, (sub, TQ), 0)
            dist = t_q - (cblk * CMP_BLOCK + (CMP_BLOCK - 1))
            for hh in range(HPG):
                head = g * HPG + hh
                val = _bias_of_distance(dist, lambda k, head=head: tab_ref[k, head])
                o_ref[0, 0, rows, hh * TQ:(hh + 1) * TQ] = jnp.where(dist < 0, NEG_MASK, val)


def _cmp_bias_call(rel_bias, n_qt):
    return pl.pallas_call(
        _cmp_bias_kernel, grid=(N_KV, n_qt),
        in_specs=[pl.BlockSpec(memory_space=pltpu.SMEM)],
        out_specs=pl.BlockSpec((1, 1, LANES, HPG * TQ), lambda g, t: (g, t, 0, 0)),
        out_shape=jax.ShapeDtypeStruct((N_KV, n_qt, LANES, HPG * TQ), F32),
        compiler_params=_cparams("arbitrary", "arbitrary"), name="bias_cmp")(rel_bias)


def _bias_cols_kernel(dist_ref, tab_ref, o_ref):
    dist = dist_ref[...]
    val = _bias_of_distance(dist, lambda k: tab_ref[:, k:k + 1])
    o_ref[...] = jnp.where(dist < 0, NEG_MASK, val)


def _bias_cols_call(dist, tab_heads):
    r = dist.shape[1]
    return pl.pallas_call(
        _bias_cols_kernel, grid=(1,),
        in_specs=[pl.BlockSpec((N_HEADS, r), lambda i: (0, 0)), pl.BlockSpec((N_HEADS, N_BUCKETS), lambda i: (0, 0))],
        out_specs=pl.BlockSpec((N_HEADS, r), lambda i: (0, 0)),
        out_shape=jax.ShapeDtypeStruct((N_HEADS, r), F32),
        compiler_params=_cparams("arbitrary"), name="bias_cols")(dist, tab_heads)


def _nsa_seq_kernel(n_sb, q_ref, gate_ref, kaug_ref, v1_ref, kw_ref, vw1_ref, kcb_ref, vcb_ref, tcmp_ref,
                    tsel_ref, twin_ref, gexp_ref, o_ref, qaug, s_win, s_sc, mrun, mb, acc_sel, acc_win, score_sc):
    qi = pl.program_id(2)
    rows = HPG * TQ
    lane = lax.broadcasted_iota(jnp.int32, (TQ, LANES), 1)
    low = lane < HEAD_DIM

    qh = []
    for hh in range(HPG):
        qv = q_ref[0, :, (hh // 2) * LANES:(hh // 2 + 1) * LANES]
        if hh % 2 == 1:
            qv = pltpu.roll(qv, HEAD_DIM, 1)
        qh.append(jnp.where(low, qv, 0.0))
    qw = jnp.concatenate(qh, axis=0).astype(BF16)

    m_w = jnp.full((rows, LANES), NEG_MASK, F32)
    for i in range(N_WIN_TILES):
        r0 = pl.multiple_of(jnp.maximum(qi - i, 0) * TK, TK)
        s = _dot_nt(qw, kw_ref[0, 0, pl.ds(r0, TK), :]) + twin_ref[0, jnp.where(qi >= i, i, N_WIN_TILES)]
        s_win[i] = s
        for c in range(TK // LANES):
            m_w = jnp.maximum(m_w, s[:, c * LANES:(c + 1) * LANES])
    mb_w = jnp.broadcast_to(jnp.max(m_w, axis=1, keepdims=True), (rows, LANES))
    mb_w2 = jnp.concatenate([mb_w] * (TK // LANES), axis=1)
    acc_w = jnp.zeros((rows, LANES), F32)
    for i in range(N_WIN_TILES):
        r0 = pl.multiple_of(jnp.maximum(qi - i, 0) * TK, TK)
        acc_w = acc_w + _dot(jnp.exp(s_win[i] - mb_w2).astype(BF16), vw1_ref[0, 0, pl.ds(r0, TK), :])
    acc_win[...] = acc_w

    s_c = _dot_nt(kcb_ref[0, 0].astype(BF16), qw) + tcmp_ref[0, 0]
    ok_c = s_c > 0.5 * NEG_MASK
    mx = jnp.max(s_c, axis=0, keepdims=True)
    p = jnp.where(ok_c, jnp.exp(s_c - mx), 0.0)
    den = jnp.sum(p, axis=0, keepdims=True)
    pn_t = p / jnp.where(den > 0, den, 1.0)
    o_c = _dot(pn_t.T.astype(BF16), vcb_ref[0, 0].astype(BF16))
    imp = pn_t[:, 0:TQ]
    for hh in range(1, HPG):
        imp = imp + pn_t[:, hh * TQ:(hh + 1) * TQ]
    n_blk = LANES // 2
    imp = imp[:n_blk] + imp[n_blk:]

    blk = lax.broadcasted_iota(jnp.int32, (n_blk, TQ), 0)
    cur = (qi * TQ + lax.broadcasted_iota(jnp.int32, (n_blk, TQ), 1)) // SEL_BLOCK
    valid = blk <= cur
    forced = (blk == 0) | (blk == cur) | (blk == cur - 1)
    score_sc[...] = jnp.where(valid & forced, BIG_SCORE, jnp.where(valid, imp, -BIG_SCORE))
    sub = 8
    groups = [score_sc[r0:r0 + sub, :] for r0 in range(0, n_blk, sub)]
    rowg = lax.broadcasted_iota(jnp.int32, (sub, TQ), 0)
    cnts = [jnp.zeros((sub, TQ), jnp.int32) for _ in groups]
    for bp in range(n_blk):
        r = score_sc[bp:bp + 1, :]
        for gi, s_g in enumerate(groups):
            if gi * sub > bp:
                ahead = r >= s_g
            elif (gi + 1) * sub - 1 < bp:
                ahead = r > s_g
            else:
                ahead = (r > s_g) | ((r == s_g) & (rowg + gi * sub > bp))
            cnts[gi] = cnts[gi] + jnp.where(ahead, 1, 0)
    cnt = jnp.concatenate(cnts, axis=0)
    selmask_t = jnp.where((cnt < min(TOP_N, n_sb)) & valid, 0.0, NEG_SEL)
    selmask = jnp.concatenate([jnp.zeros((n_blk, TQ), F32), selmask_t], axis=0).T
    for hh in range(HPG):
        qaug[hh * TQ:(hh + 1) * TQ, :] = jnp.where(low, qh[hh], selmask).astype(BF16)

    def branch(q_sc, k_ref, v_ref, t_ref, acc_sc, n_tiles, n_tbl, widths):
        mrun[...] = jnp.full(mrun.shape, NEG_MASK, F32)

        def tile_loop(step):
            done = 0
            for width in widths:
                def group(j, _, done=done, width=width):
                    step(tuple(done + width * j + u for u in range(width)))
                    return 0
                n_groups = (n_tiles - done) // width
                lax.fori_loop(0, n_groups, group, 0)
                done = done + n_groups * width

        def scores(tiles):
            m = mrun[...]
            for i in tiles:
                r0 = pl.multiple_of((qi - i) * TK, TK)
                s = _dot_nt(q_sc[...], k_ref[0, 0, pl.ds(r0, TK), :]) + t_ref[0, jnp.minimum(i, n_tbl)]
                s_sc[i] = s
                for c in range(TK // LANES):
                    m = jnp.maximum(m, s[:, c * LANES:(c + 1) * LANES])
            mrun[...] = m
        tile_loop(scores)
        mb[...] = jnp.broadcast_to(jnp.max(mrun[...], axis=1, keepdims=True), (rows, LANES))
        acc_sc[...] = jnp.zeros(acc_sc.shape, F32)

        def weigh(tiles):
            mbv = mb[...]
            mb2 = jnp.concatenate([mbv] * (TK // LANES), axis=1)
            acc = acc_sc[...]
            for i in tiles:
                r0 = pl.multiple_of((qi - i) * TK, TK)
                acc = acc + _dot(jnp.exp(s_sc[i] - mb2).astype(BF16), v_ref[0, 0, pl.ds(r0, TK), :])
            acc_sc[...] = acc
        tile_loop(weigh)
        return acc_sc[...]

    acc_s = branch(qaug, kaug_ref, v1_ref, tsel_ref, acc_sel, qi + 1, N_TBL, (8, 4, 2, 1))
    acc_w = acc_win[...]

    gates = gate_ref[0]
    g_hi = gates.astype(BF16)
    g_lo = (gates - g_hi.astype(F32)).astype(BF16)
    g_all = _dot(jnp.concatenate([g_hi, g_lo], axis=1), gexp_ref[...])

    def normalized(acc, odd):
        swapped = pltpu.roll(acc, HEAD_DIM, 1)
        return swapped / acc if odd else acc / swapped

    outs = []
    for hh in range(HPG):
        gb = [g_all[:, (hh * 3 + br) * LANES:(hh * 3 + br + 1) * LANES] for br in range(3)]
        rs = slice(hh * TQ, (hh + 1) * TQ)
        outs.append(gb[0] * o_c[rs] + gb[1] * normalized(acc_s[rs], hh % 2) + gb[2] * normalized(acc_w[rs], hh % 2))
    for c in range(HPG // 2):
        o_ref[0, :, c * LANES:(c + 1) * LANES] = jnp.where(low, outs[2 * c], outs[2 * c + 1]).astype(o_ref.dtype)


def _nsa_seq_call(q, gates, kaug, v1, kw, vw1, kcb, vcb, tcmp, tsel, twin):
    b, s, _ = q.shape
    n_qt = s // TQ
    rows = HPG * TQ
    kv_spec = pl.BlockSpec((1, 1, s, LANES), lambda bb, g, i: (bb, g, 0, 0))
    cb_spec = pl.BlockSpec((1, 1, kcb.shape[2], LANES), lambda bb, g, i: (bb, g, 0, 0))
    tbl_spec = lambda t: pl.BlockSpec((1,) + t.shape[1:], lambda bb, g, i: (g, 0, 0, 0))
    n_gate = HPG * 3
    gexp = np.zeros((2 * LANES, n_gate * LANES), np.float32)
    for c in range(n_gate):
        gexp[c, c * LANES:(c + 1) * LANES] = 1.0
        gexp[LANES + c, c * LANES:(c + 1) * LANES] = 1.0
    gexp = jnp.asarray(gexp, BF16)
    return pl.pallas_call(
        functools.partial(_nsa_seq_kernel, s // SEL_BLOCK), grid=(b, N_KV, n_qt),
        in_specs=[pl.BlockSpec((1, TQ, HPG * HEAD_DIM), lambda bb, g, i: (bb, i, g)),
                  pl.BlockSpec((1, TQ, LANES), lambda bb, g, i: (bb, i, g)),
                  kv_spec, kv_spec, kv_spec, kv_spec, cb_spec, cb_spec,
                  pl.BlockSpec((1, 1, LANES, rows), lambda bb, g, i: (g, i, 0, 0)), tbl_spec(tsel), tbl_spec(twin),
                  pl.BlockSpec(gexp.shape, lambda bb, g, i: (0, 0))],
        out_specs=pl.BlockSpec((1, TQ, HPG * HEAD_DIM), lambda bb, g, i: (bb, i, g)),
        out_shape=jax.ShapeDtypeStruct((b, s, N_HEADS * HEAD_DIM), BF16),
        scratch_shapes=[pltpu.VMEM((rows, LANES), BF16), pltpu.VMEM((N_WIN_TILES, rows, TK), F32),
                        pltpu.VMEM((s // TK, rows, TK), F32), pltpu.VMEM((rows, LANES), F32),
                        pltpu.VMEM((rows, LANES), F32), pltpu.VMEM((rows, LANES), F32),
                        pltpu.VMEM((rows, LANES), F32), pltpu.VMEM((LANES // 2, TQ), F32)],
        compiler_params=_cparams("arbitrary", "arbitrary", "arbitrary"), name="nsa_seq")(
            q, gates, kaug, v1, kw, vw1, kcb, vcb, tcmp, tsel, twin, gexp)


def _softmax_lanes(s):
    p = jnp.exp(s - jnp.max(s, axis=1, keepdims=True))
    return p / jnp.sum(p, axis=1, keepdims=True)


def _rows_to_heads(rows):
    hg = lax.broadcasted_iota(jnp.int32, (N_HEADS, rows[0].shape[1]), 0) // HPG
    out = jnp.broadcast_to(rows[0], hg.shape)
    for g in range(1, N_KV):
        out = jnp.where(hg == g, rows[g], out)
    return out


def _nsa_stepT_kernel(n_pages, n_seq, pt_ref, *refs):
    del pt_ref
    n_pg = n_seq * n_pages
    q_ref = refs[0]
    cb_refs = refs[1:1 + n_pg]
    pg_refs = refs[1 + n_pg:1 + 2 * n_pg]
    (kvnew_ref, winnew_ref, wcol_ref, cwin_ref, bsel_ref, bnew_ref, bwin_ref, bcmp_ref) = refs[1 + 2 * n_pg:9 + 2 * n_pg]
    oc_ref, os_ref, ow_ref, nwin_ref = refs[9 + 2 * n_pg:13 + 2 * n_pg]
    (s_sc,) = refs[13 + 2 * n_pg:]
    for u in range(n_seq):
        one = lambda ref, u=u: ref.at[pl.ds(u, 1)]
        _nsa_step_one(n_pages, pl.program_id(0) * n_seq + u, one(q_ref), cb_refs[u * n_pages:(u + 1) * n_pages],
                      pg_refs[u * n_pages:(u + 1) * n_pages], one(kvnew_ref), one(winnew_ref), wcol_ref, one(cwin_ref),
                      bsel_ref, bnew_ref, bwin_ref, bcmp_ref, one(oc_ref), one(os_ref), one(ow_ref), one(nwin_ref),
                      s_sc.at[u])


def _nsa_step_one(n_pages, b, q_ref, cb_refs, pg_refs, kvnew_ref, winnew_ref, wcol_ref, cwin_ref, bsel_ref, bnew_ref,
                  bwin_ref, bcmp_ref, oc_ref, os_ref, ow_ref, nwin_ref, s_sc):
    gw = N_KV * HEAD_DIM
    n_past = n_pages * PAGE_SIZE
    q16 = q_ref[0].astype(BF16)
    qf = q16.astype(F32)
    lane = lax.broadcasted_iota(jnp.int32, (N_HEADS, LANES), 1)

    n_pad = LANES - CB_ROWS * n_pages
    cb = jnp.concatenate([jnp.concatenate([r[c] for c in range(N_KV)], axis=1) for r in cb_refs]
                         + ([jnp.zeros((n_pad, 2 * gw), F32)] if n_pad else []), axis=0)
    pn_c = _softmax_lanes(_dot_nt(q16, cb[:, :gw].astype(BF16)) + bcmp_ref[...])
    oc_ref[0] = _dot(pn_c.astype(BF16), cb[:, gw:].astype(BF16))

    grp = [pn_c[HPG * g:HPG * g + 1] + pn_c[HPG * g + 1:HPG * g + 2] + pn_c[HPG * g + 2:HPG * g + 3]
           + pn_c[HPG * g + 3:HPG * g + 4] for g in range(N_KV)]
    lane8 = lax.broadcasted_iota(jnp.int32, (8, LANES), 1)
    row8 = lax.broadcasted_iota(jnp.int32, (8, LANES), 0)
    imp = jnp.zeros((8, LANES), F32)
    for g in range(N_KV):
        imp = jnp.where(row8 == g, grp[g], imp)
    imp = imp + pltpu.roll(imp, LANES - 1, 1)
    n_past_blk = n_past // SEL_BLOCK
    cur_lane = LANES - 2
    is_blk = ((lane8 % CB_ROWS == 0) | (lane8 % CB_ROWS == 2)) & (lane8 < CB_ROWS * n_pages)
    last_lane = ((n_past_blk - 1) // 2) * CB_ROWS + 2 * ((n_past_blk - 1) % 2)
    forced = (lane8 == 0) | (lane8 == last_lane) | (lane8 == cur_lane)
    valid = is_blk | (lane8 == cur_lane)
    score = jnp.where(valid & forced, BIG_SCORE, jnp.where(valid, imp, -BIG_SCORE))
    cnt = jnp.zeros((8, LANES), jnp.int32)
    for k in range(2, LANES, 2):
        r = pltpu.roll(score, k, 1)
        cnt = cnt + jnp.where((r > score) | ((r == score) & (lane8 >= k)), 1, 0)
    selrows = jnp.where((cnt < min(TOP_N, n_past_blk + 1)) & valid, 0.0, NEG_SEL)
    selmask = _rows_to_heads([selrows[g:g + 1] for g in range(N_KV)])

    blocks_per_page = PAGE_SIZE // SEL_BLOCK
    for p in range(n_pages):
        kt = pg_refs[p][0, 0:gw, :].astype(BF16)
        msk = selmask[:, CB_ROWS * p:CB_ROWS * p + 1]
        for i in range(1, blocks_per_page):
            msk = jnp.where(lane >= i * SEL_BLOCK, selmask[:, CB_ROWS * p + 2 * i:CB_ROWS * p + 2 * i + 1], msk)
        s_sc[:, p * PAGE_SIZE:(p + 1) * PAGE_SIZE] = _dot(q16, kt) + bsel_ref[:, p * PAGE_SIZE:(p + 1) * PAGE_SIZE] + msk
    knew = kvnew_ref[0][:, 2 * gw:3 * gw].astype(BF16).astype(F32)
    s_new = jnp.sum(qf * knew, axis=1, keepdims=True) + bnew_ref[:, 0:1] + selmask[:, cur_lane:cur_lane + 1]
    s_sc[:, n_past:n_past + LANES] = jnp.where(lane == 0, s_new, NEG_MASK)
    s_all = s_sc[...]
    mx = jnp.max(s_all, axis=1, keepdims=True)
    den = jnp.sum(jnp.exp(s_all - mx), axis=1, keepdims=True)
    acc = jnp.zeros((N_HEADS, gw), F32)
    for p in range(n_pages):
        pn = jnp.exp(s_sc[:, p * PAGE_SIZE:(p + 1) * PAGE_SIZE] - mx) / den
        acc = acc + _dot_nt(pn.astype(BF16), pg_refs[p][0, gw:2 * gw, :].astype(BF16))
    pn_new = jnp.exp(s_new - mx) / den
    vnew = kvnew_ref[0][:, 3 * gw:4 * gw].astype(BF16).astype(F32)
    os_ref[0] = acc + pn_new.astype(BF16).astype(F32) * vnew

    cw = cwin_ref[0]
    w_len = cw.shape[1]
    s_w = _dot(q16, cw[0:gw].astype(BF16)) + bwin_ref[...]
    kwn = winnew_ref[0][:, 0:gw].astype(BF16).astype(F32)
    s_wn = jnp.sum(qf * kwn, axis=1, keepdims=True) + bnew_ref[:, 0:1]
    mxw = jnp.maximum(jnp.max(s_w, axis=1, keepdims=True), s_wn)
    pw = jnp.exp(s_w - mxw)
    pwn = jnp.exp(s_wn - mxw)
    denw = jnp.sum(pw, axis=1, keepdims=True) + pwn
    vwn = winnew_ref[0][:, gw:2 * gw].astype(BF16).astype(F32)
    ow_ref[0] = _dot_nt((pw / denw).astype(BF16), cw[gw:2 * gw].astype(BF16)) + (pwn / denw).astype(BF16).astype(F32) * vwn
    lane_w = lax.broadcasted_iota(jnp.int32, cw.shape, 1)
    nb_l = wcol_ref.shape[1]
    lane_b = lax.broadcasted_iota(jnp.int32, (cw.shape[0], nb_l), 1)
    col = jnp.sum(jnp.where(lane_b == b, wcol_ref[...], 0.0), axis=1, keepdims=True)
    nwin_ref[0] = jnp.where(lane_w == w_len - 1, col, pltpu.roll(cw, w_len - 1, 1))


def _nsa_step_call(page_table, qrows, cb_pool, cache_t, kvnew, winnew, wcol, cwin_t, bsel, bnew, bwin, bcmp):
    nb, n_pages = page_table.shape
    gw = N_KV * HEAD_DIM
    w_len = cwin_t.shape[2]
    n_seq = STEP_SEQS if nb % STEP_SEQS == 0 else 1
    slots = [(u, p) for u in range(n_seq) for p in range(n_pages)]
    cb_specs = [pl.BlockSpec((N_KV, None, CB_ROWS, LANES), lambda b, pt, _u=u, _p=p: (0, pt[b * n_seq + _u, _p], 0, 0))
                for u, p in slots]
    pg_specs = [pl.BlockSpec((1, 2 * gw, PAGE_SIZE), lambda b, pt, _u=u, _p=p: (pt[b * n_seq + _u, _p], 1, 0))
                for u, p in slots]
    const = lambda a: pl.BlockSpec(a.shape, lambda b, pt: (0, 0))
    seq = lambda *dims: pl.BlockSpec((n_seq,) + dims, lambda b, pt: (b, 0, 0))
    grid_spec = pltpu.PrefetchScalarGridSpec(
        num_scalar_prefetch=1, grid=(nb // n_seq,),
        in_specs=[seq(N_HEADS, gw)] + cb_specs + pg_specs + [
            seq(1, 4 * gw), seq(1, 2 * gw), const(wcol), seq(2 * gw, w_len),
            const(bsel), const(bnew), const(bwin), const(bcmp)],
        out_specs=[seq(N_HEADS, gw)] * 3 + [seq(2 * gw, w_len)],
        scratch_shapes=[pltpu.VMEM((n_seq, N_HEADS, n_pages * PAGE_SIZE + LANES), F32)])
    o_shape = jax.ShapeDtypeStruct((nb, N_HEADS, gw), F32)
    return pl.pallas_call(
        functools.partial(_nsa_stepT_kernel, n_pages, n_seq), grid_spec=grid_spec,
        out_shape=[o_shape, o_shape, o_shape, jax.ShapeDtypeStruct((nb, 2 * gw, w_len), F32)],
        compiler_params=_cparams("arbitrary"), name="nsa_step")(
            page_table, qrows, *([cb_pool] * len(slots)), *([cache_t] * len(slots)), kvnew, winnew, wcol, cwin_t,
            bsel, bnew, bwin, bcmp)


def _head_diag(o):
    b = o.shape[0]
    o5 = o.reshape(b, N_KV, HPG, N_KV, HEAD_DIM)
    return jnp.stack([o5[:, g, :, g, :] for g in range(N_KV)], axis=1).reshape(b, N_HEADS * HEAD_DIM)


def kernel(x_prompt, x_sample, c_prompt, c_sample, cache_kv, cache_win, state_conv_a, state_ffn_conv, page_table, mod_w, mod_b, norm_g, a_w_in, a_conv_w, a_conv_b, a_w_out, kv_mod_w, kv_mod_b, kv_norm_g, w_kv, cmp_pe, cmp_w1, cmp_w2, b_w_qg, b_w_out, rel_bias, ffn_w_up, ffn_conv_w, ffn_conv_b, ffn_w_down):
    bp, s, d = x_prompt.shape
    bs = x_sample.shape[0]
    depth = mod_w.shape[0]
    n_a = a_w_in.shape[0]
    assert depth == 2 and n_a == 1 and x_sample.shape[1] == 1
    dff = ffn_w_down.shape[1]
    n_pool = cache_kv.shape[0]
    n_pages = page_table.shape[1]
    past_len = n_pages * PAGE_SIZE
    gw = N_KV * HEAD_DIM
    nq = N_HEADS * HEAD_DIM

    n_c = bp + bs
    n_cp = -(-n_c // 8) * 8
    c_all = jnp.pad(jnp.concatenate([c_prompt, c_sample], 0), ((0, n_cp - n_c), (0, 0)))
    mods = _mod_call(c_all, mod_w.reshape(depth * 2, d, 3 * d), mod_b.reshape(depth * 2, 1, 3 * d))
    modkv = _mod_call(c_all, kv_mod_w[None], kv_mod_b[None, None])[0]
    mod_p = lambda i: mods[i, :bp][:, None, :]
    mod_s = lambda i: mods[i, bp:n_c][None]

    w_in = a_w_in[0].astype(BF16)
    w_out_a = a_w_out[0].astype(BF16)
    w_up = ffn_w_up.astype(BF16)
    w_dn = ffn_w_down.astype(BF16)
    w_kv_b = w_kv.astype(BF16)
    w_gate = jnp.pad(b_w_qg[0][:, nq:].reshape(d, N_KV, HPG * 3), ((0, 0), (0, 0), (0, LANES - HPG * 3)))
    w_qg = jnp.concatenate([b_w_qg[0][:, :nq], w_gate.reshape(d, N_KV * LANES)], axis=1).astype(BF16)
    w_out_b = b_w_out[0].astype(BF16)
    g = lambda l, i: norm_g[l, i][None]
    pe2, w1bd, w2bd = _compress_weights(cmp_pe, cmp_w1, cmp_w2)
    n_chunks = FFN_COL_CHUNKS

    tm = min(ROW_TILE, s)
    x1, st_a = _mixer_call(x_prompt, mod_p(0), g(0, 0), g(0, 1), w_in, a_conv_w[0], a_conv_b[0][None], w_out_a, None, tm)
    x2, st_f0 = _ffn_call(x1, mod_p(1), g(0, 2), g(0, 3), w_up[0], ffn_conv_w[0], ffn_conv_b[0][None], w_dn[0], None, tm, n_chunks)
    rows, win, q, gates, kaug, v1, kw, vw1 = _kvq_call(
        x2, modkv[:bp][:, None, :], mod_p(2), kv_norm_g[None], g(1, 0), w_kv_b, w_qg, tm, True)
    cb = _compress_call(rows.reshape(bp * s, 4 * gw), pe2, w1bd, w2bd)
    n_cbk = s // CMP_BLOCK
    assert n_cbk <= LANES
    cb = jnp.pad(cb.reshape(bp, n_cbk, 2, N_KV, HEAD_DIM), ((0, 0), (0, LANES - n_cbk), (0, 0), (0, 0), (0, 0)))
    cb = cb.reshape(bp, LANES // 2, 2, 2, N_KV, HEAD_DIM)
    cb = cb.transpose(3, 0, 4, 2, 1, 5).reshape(2, bp, N_KV, LANES, HEAD_DIM)
    kcb = jnp.pad(cb[0], ((0, 0),) * 3 + ((0, LANES - HEAD_DIM),))
    vcb = jnp.concatenate([cb[1], cb[1]], axis=-1)
    tsel = _toeplitz_call(rel_bias, N_TBL + 1, None)
    twin = _toeplitz_call(rel_bias, N_WIN_TILES + 1, WINDOW)
    tcmp = _cmp_bias_call(rel_bias, s // TQ)
    o_att = _nsa_seq_call(q, gates, kaug, v1, kw, vw1, kcb, vcb, tcmp, tsel, twin)
    y_prompt, st_f1 = _ffn_call(x2, mod_p(3), g(1, 2), g(1, 3), w_up[1], ffn_conv_w[1], ffn_conv_b[1][None], w_dn[1], None,
                                tm, n_chunks, attn=(o_att, mod_p(2), g(1, 1), w_out_b))
    kv_p = rows.reshape(bp, s, 4, N_KV, HEAD_DIM)
    keep = min(WINDOW, s)
    win_p = win[:, s - keep:].reshape(bp, keep, 2, N_KV, HEAD_DIM)
    conv_a_p = st_a[None, :, 6:8]
    ffn_p = jnp.stack([st_f0[:, 6:8], st_f1[:, 6:8]])

    xs = x_sample.reshape(1, bs, d)
    prev_a = (state_conv_a[0, :, 0][None], state_conv_a[0, :, 1][None])
    xs1, v_a = _mixer_call(xs, mod_s(0), g(0, 0), g(0, 1), w_in, a_conv_w[0], a_conv_b[0][None], w_out_a, prev_a, bs)
    prev_f = lambda l: (state_ffn_conv[l, :, 0][None], state_ffn_conv[l, :, 1][None])
    xs2, up0 = _ffn_call(xs1, mod_s(1), g(0, 2), g(0, 3), w_up[0], ffn_conv_w[0], ffn_conv_b[0][None], w_dn[0], prev_f(0), bs, n_chunks)
    rows_s, win_s, q_s, gates_s = _kvq_call(xs2, modkv[bp:n_c][None], mod_s(2), kv_norm_g[None], g(1, 0), w_kv_b, w_qg, bs, False)
    cache_t = cache_kv.transpose(0, 2, 3, 4, 1).reshape(n_pool, 4 * gw, PAGE_SIZE)
    w_len = cache_win.shape[1]
    cwin_t = cache_win.transpose(0, 2, 3, 4, 1).reshape(bs, 2 * gw, w_len)
    cb_pool = _compress_pool_call(cache_t.reshape(n_pool, 4, gw, PAGE_SIZE),
                                  *_compress_pool_weights(cmp_pe, cmp_w1, cmp_w2))
    cb_pool = cb_pool.reshape(N_KV, n_pool, CB_ROWS, LANES)
    assert n_pages * CB_ROWS <= LANES
    d_sel = past_len - np.arange(past_len)
    d_new = np.where(np.arange(LANES) == 0, 0, -1)
    d_win = w_len - np.arange(w_len)
    d_win = np.where(d_win < WINDOW, d_win, -1)
    cl = np.arange(LANES)
    d_cmp = past_len - (((cl // CB_ROWS) * N_PAGE_CB + cl % CB_ROWS) * CMP_BLOCK + CMP_BLOCK - 1)
    d_cmp = np.where((cl % CB_ROWS < N_PAGE_CB) & (cl // CB_ROWS < n_pages), d_cmp, -1)
    assert d_sel.min() >= 0 and d_cmp[d_cmp != -1].min() >= 0
    dist = np.concatenate([d_sel, d_new, d_win, d_cmp]).astype(np.int32)
    bias_cols = _bias_cols_call(jnp.asarray(np.repeat(dist[None, :], N_HEADS, 0)), rel_bias.T)
    bsel, bnew = bias_cols[:, :past_len], bias_cols[:, past_len:past_len + LANES]
    bwin = bias_cols[:, past_len + LANES:past_len + LANES + w_len]
    bcmp = bias_cols[:, past_len + LANES + w_len:]
    head_group = (np.arange(N_HEADS)[:, None] // HPG == np.arange(N_KV)[None, :])[None, :, :, None]
    qrows = jnp.where(head_group, q_s.reshape(bs, N_HEADS, 1, HEAD_DIM), 0.0).reshape(bs, N_HEADS, gw)
    oc, os_, ow, nwin_t = _nsa_step_call(
        page_table, qrows, cb_pool, cache_t, rows_s.reshape(bs, 1, 4 * gw),
        win_s.reshape(bs, 1, 2 * gw), win_s[0].T, cwin_t, bsel, bnew, bwin, bcmp)
    nwin = nwin_t.reshape(bs, 2, N_KV, HEAD_DIM, w_len).transpose(0, 4, 1, 2, 3)
    gts = gates_s[0].reshape(bs, N_KV, LANES)[:, :, :HPG * 3].reshape(bs, N_HEADS, 3)
    branches = []
    for br, o in enumerate((oc, os_, ow)):
        branches += [jnp.repeat(gts[:, :, br], HEAD_DIM, axis=1)[None], _head_diag(o)[None]]
    xs3 = _outproj_call(xs2, mod_s(2), g(1, 1), w_out_b, branches, bs)
    ys, up1 = _ffn_call(xs3, mod_s(3), g(1, 2), g(1, 3), w_up[1], ffn_conv_w[1], ffn_conv_b[1][None], w_dn[1], prev_f(1), bs, n_chunks)
    y_sample = ys.reshape(bs, 1, d)
    kv_s = rows_s.reshape(bs, 1, 4, N_KV, HEAD_DIM)
    win_state_s = nwin
    conv_a_s = jnp.stack([state_conv_a[0, :, 1], v_a[0]], axis=1)[None]
    ffn_s = jnp.stack([jnp.stack([state_ffn_conv[l, :, 1], u[0]], axis=1) for l, u in ((0, up0), (1, up1))])
    return (y_prompt, y_sample, kv_p, kv_s, win_p, win_state_s, conv_a_p, conv_a_s, ffn_p, ffn_s)
```
